```python
import math
import jax, jax.numpy as jnp
from jax import lax
import numpy as np

D_MODEL = 1024
BATCH = 1
SEQ = 16384
DEPTH = 2

GRID_W = 64
CTX_LEN = 256
BLOCK_Q = 128
RET_CHUNK = 128
ROPE_THETA = 10000.0
EPS = 1e-6

RET_HEADS = 8
RET_HEAD_DIM = 64
GQA_HEADS = 8
GQA_KV_HEADS = 2
GQA_HEAD_DIM = 64
GQA_GROUP = GQA_HEADS // GQA_KV_HEADS
MLA_HEADS = 8
MLA_Q_RANK = 384
MLA_KV_RANK = 256
MLA_NOPE = 64
MLA_ROPE = 32
MLA_V = 64
FFN_DIM = 2816
N_EXPERTS = 8
TOP_K = 2
EXPERT_DIM = 3584

RET_W = RET_HEADS * RET_HEAD_DIM
GQA_QW = GQA_HEADS * GQA_HEAD_DIM
GQA_KW = GQA_KV_HEADS * GQA_HEAD_DIM
L0_WIDTHS = (RET_W, RET_W, RET_W, RET_W, GQA_QW, GQA_KW, GQA_KW)
L0_IN = sum(L0_WIDTHS)
L0_SPLITS = tuple(int(v) for v in np.cumsum(L0_WIDTHS)[:-1])
L1_WIDTHS = (MLA_Q_RANK, MLA_KV_RANK, MLA_ROPE)
L1_IN = sum(L1_WIDTHS)
L1_SPLITS = tuple(int(v) for v in np.cumsum(L1_WIDTHS)[:-1])

kernel_name = "hybrid_retention_gqa_mla_moe_dit"


def rms_norm(x, g):
    xf = x.astype(jnp.float32)
    y = xf * lax.rsqrt(jnp.mean(xf * xf, axis=-1, keepdims=True) + EPS)
    return (y * g.astype(jnp.float32)).astype(x.dtype)


def modulate(x, g, shift, scale):
    return rms_norm(x, g) * (1 + scale) + shift


def ada_params(cvec, w, b):
    mod = jax.nn.silu(cvec) @ w + b
    return jnp.split(mod[..., None, :], 6, axis=-1)


def axial_rope_tables(n, rot_dim):
    rows = n // GRID_W
    row = jnp.broadcast_to(jnp.arange(rows)[:, None], (rows, GRID_W)).reshape(n).astype(jnp.float32)
    col = jnp.broadcast_to(jnp.arange(GRID_W)[None, :], (rows, GRID_W)).reshape(n).astype(jnp.float32)
    axis_dim = rot_dim // 2
    inv_freq = ROPE_THETA ** (-jnp.arange(0, axis_dim, 2, dtype=jnp.float32) / axis_dim)
    ang = jnp.concatenate([row[:, None] * inv_freq, col[:, None] * inv_freq], axis=-1)
    return jnp.cos(ang), jnp.sin(ang)


def apply_rope(x, cos, sin):
    xf = x.astype(jnp.float32).reshape(x.shape[:-1] + (x.shape[-1] // 2, 2))
    x0, x1 = xf[..., 0], xf[..., 1]
    c = cos[None, :, None, :]
    s = sin[None, :, None, :]
    out = jnp.stack([x0 * c - x1 * s, x0 * s + x1 * c], axis=-1).reshape(x.shape)
    return out.astype(x.dtype)


def block_attention(q, k, v):
    b, n, hk, g, dq = q.shape
    nb = n // BLOCK_Q
    scale = dq ** -0.5
    qb = q.reshape(b, nb, BLOCK_Q, hk, g, dq).transpose(1, 0, 2, 3, 4, 5)

    def one_block(qblk):
        s = jnp.einsum("bqkgd,bskd->bkgqs", qblk, k, preferred_element_type=jnp.float32) * scale
        p = jax.nn.softmax(s, axis=-1).astype(v.dtype)
        return jnp.einsum("bkgqs,bskd->bqkgd", p, v)

    out = lax.map(one_block, qb)
    return out.transpose(1, 0, 2, 3, 4, 5).reshape(b, n, hk, g, v.shape[-1])


def retention_scan(q, k, v, log_gamma, s0):
    b, n, h, d = q.shape
    C = RET_CHUNK
    nc = n // C
    f32 = jnp.float32
    qf = q.astype(f32).reshape(b, nc, C, h, d)
    kf = k.astype(f32).reshape(b, nc, C, h, d)
    vf = v.astype(f32).reshape(b, nc, C, h, d)
    lg = log_gamma.astype(f32)
    idx = jnp.arange(C, dtype=f32)
    rel = idx[:, None] - idx[None, :]
    decay = jnp.where(rel[None] >= 0, jnp.exp(jnp.maximum(rel, 0.0)[None] * lg[:, None, None]), 0.0)
    scores = jnp.einsum("bnchd,bnmhd->bnhcm", qf, kf) * decay[None, None]
    o_inner = jnp.einsum("bnhcm,bnmhe->bnche", scores, vf)
    zeta = jnp.exp((C - 1 - idx)[:, None] * lg[None, :])
    xi = jnp.exp((idx + 1)[:, None] * lg[None, :])
    u = jnp.einsum("bnmhd,bnmhe->nbhde", kf * zeta[None, None, :, :, None], vf)
    g_chunk = jnp.exp(C * lg)[None, :, None, None]

    def step(s, u_n):
        return g_chunk * s + u_n, s

    s_final, s_prev = lax.scan(step, s0, u)
    o_cross = jnp.einsum("bnchd,nbhde->bnche", qf * xi[None, None, :, :, None], s_prev)
    return (o_inner + o_cross).reshape(b, n, h, d), s_final


def bidir_retention(q, k, v, log_decay, s0_fwd, s0_bwd):
    o_f, s_f = retention_scan(q, k, v, log_decay[0], s0_fwd)
    o_b, s_b = retention_scan(q[:, ::-1], k[:, ::-1], v[:, ::-1], log_decay[1], s0_bwd)
    return o_f + o_b[:, ::-1], s_f, s_b


def head_group_norm(o):
    mu = jnp.mean(o, axis=-1, keepdims=True)
    var = jnp.mean(jnp.square(o - mu), axis=-1, keepdims=True)
    return (o - mu) * lax.rsqrt(var + EPS)


def swiglu(h, w1, w3, w2):
    return (jax.nn.silu(h @ w1) * (h @ w3)) @ w2


def even_project(h, w_in, qn_g, kn_g):
    b, n, _ = h.shape
    a_q, a_k, a_v, a_g, b_q, b_k, b_v = jnp.split(h @ w_in, L0_SPLITS, axis=-1)
    rq = a_q.reshape(b, n, RET_HEADS, RET_HEAD_DIM)
    rk = a_k.reshape(b, n, RET_HEADS, RET_HEAD_DIM) * (RET_HEAD_DIM ** -0.5)
    rv = a_v.reshape(b, n, RET_HEADS, RET_HEAD_DIM)
    aq = rms_norm(b_q.reshape(b, n, GQA_HEADS, GQA_HEAD_DIM), qn_g)
    ak = rms_norm(b_k.reshape(b, n, GQA_KV_HEADS, GQA_HEAD_DIM), kn_g)
    av = b_v.reshape(b, n, GQA_KV_HEADS, GQA_HEAD_DIM)
    return rq, rk, rv, a_g, aq, ak, av


def retention_out(o, gate, dtype):
    b, n = o.shape[:2]
    return (head_group_norm(o).reshape(b, n, RET_W) * jax.nn.silu(gate.astype(jnp.float32))).astype(dtype)


def even_layer(x, ctx, c, c_ctx, ada_w, ada_b, n1_g, n2_g, w_in, log_decay, qn_g, kn_g, w_out,
               w1, w3, w2, cos, sin):
    b, n, _ = x.shape
    lc = ctx.shape[1]
    mx = ada_params(c, ada_w, ada_b)
    mc = ada_params(c_ctx, ada_w, ada_b)
    rq_c, rk_c, rv_c, rg_c, aq_c, ak_c, av_c = even_project(modulate(ctx, n1_g, mc[0], mc[1]), w_in, qn_g, kn_g)
    rq_x, rk_x, rv_x, rg_x, aq_x, ak_x, av_x = even_project(modulate(x, n1_g, mx[0], mx[1]), w_in, qn_g, kn_g)
    zero = jnp.zeros((b, RET_HEADS, RET_HEAD_DIM, RET_HEAD_DIM), jnp.float32)
    ro_c, s_f, s_b = bidir_retention(rq_c, rk_c, rv_c, log_decay, zero, zero)
    ro_x, _, _ = bidir_retention(rq_x, rk_x, rv_x, log_decay, s_f, s_b)
    ra_c = retention_out(ro_c, rg_c, ctx.dtype)
    ra_x = retention_out(ro_x, rg_x, x.dtype)
    aq_x = apply_rope(aq_x, cos, sin)
    ak_x = apply_rope(ak_x, cos, sin)
    k_all = jnp.concatenate([ak_x, ak_c], axis=1)
    v_all = jnp.concatenate([av_x, av_c], axis=1)
    ao_x = block_attention(aq_x.reshape(b, n, GQA_KV_HEADS, GQA_GROUP, GQA_HEAD_DIM), k_all, v_all).reshape(b, n, GQA_QW)
    ao_c = block_attention(aq_c.reshape(b, lc, GQA_KV_HEADS, GQA_GROUP, GQA_HEAD_DIM), ak_c, av_c).reshape(b, lc, GQA_QW)
    x = x + mx[2] * (jnp.concatenate([ra_x, ao_x], axis=-1) @ w_out)
    ctx = ctx + mc[2] * (jnp.concatenate([ra_c, ao_c], axis=-1) @ w_out)
    x = x + mx[5] * swiglu(modulate(x, n2_g, mx[3], mx[4]), w1, w3, w2)
    ctx = ctx + mc[5] * swiglu(modulate(ctx, n2_g, mc[3], mc[4]), w1, w3, w2)
    return x, ctx


def mla_keys(ckv, kr, kv_lora_g, w_ukv, kn_g, kr_g, rope):
    b, n, _ = ckv.shape
    kv = (rms_norm(ckv, kv_lora_g) @ w_ukv).reshape(b, n, MLA_HEADS, MLA_NOPE + MLA_V)
    k_nope = rms_norm(kv[..., :MLA_NOPE], kn_g)
    k_rope = rms_norm(kr.reshape(b, n, 1, MLA_ROPE), kr_g)
    if rope is not None:
        k_rope = apply_rope(k_rope, rope[0], rope[1])
    k = jnp.concatenate([k_nope, jnp.broadcast_to(k_rope, (b, n, MLA_HEADS, MLA_ROPE))], axis=-1)
    return k, kv[..., MLA_NOPE:]


def moe_swiglu(h, router, e_w1, e_w3, e_w2):
    b, n, d = h.shape
    t = h.reshape(b * n, d)
    logits = jnp.einsum("td,de->te", t, router, preferred_element_type=jnp.float32)
    top_v, top_i = lax.top_k(logits, TOP_K)
    top_w = jax.nn.softmax(top_v, axis=-1)
    gates = jnp.sum(jax.nn.one_hot(top_i, N_EXPERTS, dtype=jnp.float32) * top_w[..., None], axis=1)
    out = jnp.zeros((b * n, d), jnp.float32)
    for e in range(N_EXPERTS):
        out = out + gates[:, e:e + 1] * swiglu(t, e_w1[e], e_w3[e], e_w2[e]).astype(jnp.float32)
    return out.reshape(b, n, d).astype(h.dtype)


def odd_layer(x, ctx, c, c_ctx, ada_w, ada_b, n1_g, n2_g, w_in, q_lora_g, kv_lora_g, w_uq, w_ukv,
              qn_g, qr_g, kn_g, kr_g, w_out, router, e_w1, e_w3, e_w2, cos, sin):
    b, n, _ = x.shape
    mx = ada_params(c, ada_w, ada_b)
    mc = ada_params(c_ctx, ada_w, ada_b)
    cq_x, ckv_x, kr_x = jnp.split(modulate(x, n1_g, mx[0], mx[1]) @ w_in, L1_SPLITS, axis=-1)
    ckv_c, kr_c = jnp.split(modulate(ctx, n1_g, mc[0], mc[1]) @ w_in[:, MLA_Q_RANK:], (MLA_KV_RANK,), axis=-1)
    q = (rms_norm(cq_x, q_lora_g) @ w_uq).reshape(b, n, MLA_HEADS, MLA_NOPE + MLA_ROPE)
    q = jnp.concatenate([rms_norm(q[..., :MLA_NOPE], qn_g),
                         apply_rope(rms_norm(q[..., MLA_NOPE:], qr_g), cos, sin)], axis=-1)
    k_x, v_x = mla_keys(ckv_x, kr_x, kv_lora_g, w_ukv, kn_g, kr_g, (cos, sin))
    k_c, v_c = mla_keys(ckv_c, kr_c, kv_lora_g, w_ukv, kn_g, kr_g, None)
    o = block_attention(q[:, :, :, None, :], jnp.concatenate([k_x, k_c], axis=1),
                        jnp.concatenate([v_x, v_c], axis=1)).reshape(b, n, MLA_HEADS * MLA_V)
    x = x + mx[2] * (o @ w_out)
    x = x + mx[5] * moe_swiglu(modulate(x, n2_g, mx[3], mx[4]), router, e_w1, e_w3, e_w2)
    return x


def setup_inputs(seed: int = 0) -> dict:
    key = jax.random.key(seed)
    ks = iter(jax.random.split(key, 48))
    D = D_MODEL

    def nrm(shape, scale):
        return jax.random.normal(next(ks), shape, jnp.float32) * scale

    def gain(m):
        return 1.0 + nrm((m,), 0.02)

    base_log_decay = jnp.log1p(-(2.0 ** (-5.0 - jnp.arange(RET_HEADS, dtype=jnp.float32))))
    return {
        "x": nrm((BATCH, SEQ, D), 1.0),
        "c": nrm((BATCH, D), 1.0),
        "ctx": nrm((BATCH, CTX_LEN, D), 1.0),
        "c_ctx": nrm((D,), 1.0),
        "l0_ada_w": nrm((D, 6 * D), 0.5 * D ** -0.5),
        "l0_ada_b": nrm((6 * D,), 0.01),
        "l0_norm1_g": gain(D),
        "l0_norm2_g": gain(D),
        "l0_w_in": nrm((D, L0_IN), D ** -0.5),
        "l0_ret_log_decay": base_log_decay[None, :] * jnp.exp(nrm((2, RET_HEADS), 0.05)),
        "l0_q_norm_g": gain(GQA_HEAD_DIM),
        "l0_k_norm_g": gain(GQA_HEAD_DIM),
        "l0_w_out": nrm((RET_W + GQA_QW, D), (RET_W + GQA_QW) ** -0.5),
        "l0_ffn_w1": nrm((D, FFN_DIM), D ** -0.5),
        "l0_ffn_w3": nrm((D, FFN_DIM), D ** -0.5),
        "l0_ffn_w2": nrm((FFN_DIM, D), FFN_DIM ** -0.5),
        "l1_ada_w": nrm((D, 6 * D), 0.5 * D ** -0.5),
        "l1_ada_b": nrm((6 * D,), 0.01),
        "l1_norm1_g": gain(D),
        "l1_norm2_g": gain(D),
        "l1_w_in": nrm((D, L1_IN), D ** -0.5),
        "l1_q_lora_g": gain(MLA_Q_RANK),
        "l1_kv_lora_g": gain(MLA_KV_RANK),
        "l1_w_uq": nrm((MLA_Q_RANK, MLA_HEADS * (MLA_NOPE + MLA_ROPE)), MLA_Q_RANK ** -0.5),
        "l1_w_ukv": nrm((MLA_KV_RANK, MLA_HEADS * (MLA_NOPE + MLA_V)), MLA_KV_RANK ** -0.5),
        "l1_q_nope_g": gain(MLA_NOPE),
        "l1_q_rope_g": gain(MLA_ROPE),
        "l1_k_nope_g": gain(MLA_NOPE),
        "l1_k_rope_g": gain(MLA_ROPE),
        "l1_w_out": nrm((MLA_HEADS * MLA_V, D), (MLA_HEADS * MLA_V) ** -0.5),
        "l1_router": nrm((D, N_EXPERTS), D ** -0.5),
        "l1_exp_w1": nrm((N_EXPERTS, D, EXPERT_DIM), D ** -0.5),
        "l1_exp_w3": nrm((N_EXPERTS, D, EXPERT_DIM), D ** -0.5),
        "l1_exp_w2": nrm((N_EXPERTS, EXPERT_DIM, D), EXPERT_DIM ** -0.5),
    }


def reference(x, c, ctx, c_ctx, l0_ada_w, l0_ada_b, l0_norm1_g, l0_norm2_g, l0_w_in, l0_ret_log_decay,
              l0_q_norm_g, l0_k_norm_g, l0_w_out, l0_ffn_w1, l0_ffn_w3, l0_ffn_w2,
              l1_ada_w, l1_ada_b, l1_norm1_g, l1_norm2_g, l1_w_in, l1_q_lora_g, l1_kv_lora_g,
              l1_w_uq, l1_w_ukv, l1_q_nope_g, l1_q_rope_g, l1_k_nope_g, l1_k_rope_g, l1_w_out,
              l1_router, l1_exp_w1, l1_exp_w3, l1_exp_w2):
    n = x.shape[1]
    cos_b, sin_b = axial_rope_tables(n, GQA_HEAD_DIM)
    cos_m, sin_m = axial_rope_tables(n, MLA_ROPE)
    for layer in range(DEPTH):
        if layer % 2 == 0:
            x, ctx = even_layer(x, ctx, c, c_ctx, l0_ada_w, l0_ada_b, l0_norm1_g, l0_norm2_g, l0_w_in,
                                l0_ret_log_decay, l0_q_norm_g, l0_k_norm_g, l0_w_out,
                                l0_ffn_w1, l0_ffn_w3, l0_ffn_w2, cos_b, sin_b)
        else:
            x = odd_layer(x, ctx, c, c_ctx, l1_ada_w, l1_ada_b, l1_norm1_g, l1_norm2_g, l1_w_in,
                          l1_q_lora_g, l1_kv_lora_g, l1_w_uq, l1_w_ukv, l1_q_nope_g, l1_q_rope_g,
                          l1_k_nope_g, l1_k_rope_g, l1_w_out, l1_router, l1_exp_w1, l1_exp_w3,
                          l1_exp_w2, cos_m, sin_m)
    return x
```

```python
import functools
import math

import numpy as np
import jax
import jax.numpy as jnp
from jax import lax
from jax.experimental import pallas as pl
from jax.experimental.pallas import tpu as pltpu

F32 = jnp.float32
BF16 = jnp.bfloat16

EPS = 1e-6
ROPE_THETA = 10000.0
GRID_W = 64
LANES = 128
HEAD = 64
RET_CHUNK = 128
RET_HEADS = 8
GQA_HEADS = 8
GQA_KV_HEADS = 2
MLA_HEADS = 8
MLA_Q_RANK = 384
MLA_KV_RANK = 256
MLA_NOPE = 64
MLA_ROPE = 32
N_EXPERTS = 8
TOP_K = 2
LOG2E = math.log2(math.e)
VMEM_LIMIT = 56 * 1024 * 1024


def _cparams(sem, vmem=VMEM_LIMIT):
    return pltpu.CompilerParams(dimension_semantics=sem, vmem_limit_bytes=vmem)


def _resident(shape):
    nd = len(shape)
    return pl.BlockSpec(shape, lambda *_: (0,) * nd, pipeline_mode=pl.Buffered(1))


def _dot(a, b):
    return jnp.dot(a, b, preferred_element_type=F32)


def _dot_nt(a, b):
    return lax.dot_general(a, b, (((1,), (1,)), ((), ())), preferred_element_type=F32)


def _seg_mean(v, seg):
    hi = v.astype(BF16)
    lo = (v - hi.astype(F32)).astype(BF16)
    return _dot(hi, seg) + _dot(lo, seg)


def _silu(x):
    return x * jax.nn.sigmoid(x)


def _modulated(x, mod_ref, g_ref, which, tile, tm, n_x, d):
    ms = jnp.mean(x * x, axis=-1, keepdims=True)
    xn = x * lax.rsqrt(ms + EPS)
    g = g_ref[...]
    sh, sc = 3 * which, 3 * which + 1
    a_x = g * (1.0 + mod_ref[0:1, sc * d:(sc + 1) * d])
    a_c = g * (1.0 + mod_ref[1:2, sc * d:(sc + 1) * d])
    b_x = mod_ref[0:1, sh * d:(sh + 1) * d]
    b_c = mod_ref[1:2, sh * d:(sh + 1) * d]
    row = tile * tm + lax.broadcasted_iota(jnp.int32, (tm, 1), 0)
    is_ctx = row >= n_x
    return xn * jnp.where(is_ctx, a_c, a_x) + jnp.where(is_ctx, b_c, b_x)


def _row_gate(mod_ref, idx, tile, tm, n_x, d):
    row = tile * tm + lax.broadcasted_iota(jnp.int32, (tm, 1), 0)
    return jnp.where(row >= n_x, mod_ref[1:2, idx * d:(idx + 1) * d], mod_ref[0:1, idx * d:(idx + 1) * d])


def _lane(shape):
    return lax.broadcasted_iota(jnp.int32, shape, len(shape) - 1)


def _ada_kernel(c_ref, w_ref, b_ref, o_ref):
    c = c_ref[...]
    o_ref[...] = jnp.dot(_silu(c), w_ref[...], preferred_element_type=F32,
                         precision=lax.Precision.HIGHEST) + b_ref[...]


def _ada(cvec8, w, b):
    d, n = w.shape
    tn = n // 4
    return pl.pallas_call(
        _ada_kernel,
        grid=(n // tn,),
        in_specs=[pl.BlockSpec((8, d), lambda j: (0, 0)),
                  pl.BlockSpec((d, tn), lambda j: (0, j)),
                  pl.BlockSpec((1, tn), lambda j: (0, j))],
        out_specs=pl.BlockSpec((8, tn), lambda j: (0, j)),
        out_shape=jax.ShapeDtypeStruct((8, n), F32),
        compiler_params=_cparams(("arbitrary",)),
    )(cvec8, w, b.reshape(1, n))


def _rope128(v, c, s, half):
    lane = _lane(v.shape)
    swapped = jnp.where(lane % (2 * half) < half, pltpu.roll(v, LANES - half, 1), pltpu.roll(v, half, 1))
    return v * c + swapped * s


def _l0_proj_kernel(x_ref, mod_ref, g_ref, w_ref, seg_ref, gq_ref, gk_ref, c_ref, s_ref,
                    rq_ref, rk_ref, rv_ref, rg_ref, q_ref, k_ref, v_ref, *, tm, n_x, d):
    i = pl.program_id(0)
    h = _modulated(x_ref[...], mod_ref, g_ref, 0, i, tm, n_x, d).astype(BF16)
    rw = RET_HEADS * HEAD
    for idx, ref in enumerate((rq_ref, rk_ref, rv_ref, rg_ref)):
        ref[...] = _dot(h, w_ref[:, idx * rw:(idx + 1) * rw]).astype(BF16)
    seg = seg_ref[...]
    cos, sin = c_ref[...], s_ref[...]
    base = 4 * rw
    qw = GQA_HEADS * HEAD
    qa = _dot(h, w_ref[:, base:base + qw])
    for g in range(qw // LANES):
        v = qa[:, g * LANES:(g + 1) * LANES]
        vn = v * lax.rsqrt(_seg_mean(v * v, seg) + EPS) * gq_ref[...]
        q_ref[:, g * LANES:(g + 1) * LANES] = _rope128(vn, cos, sin, HEAD // 2).astype(BF16)
    kv = _dot(h, w_ref[:, base + qw:base + qw + 2 * LANES])
    kk = kv[:, :LANES]
    kk = kk * lax.rsqrt(_seg_mean(kk * kk, seg) + EPS) * gk_ref[...]
    kk = _rope128(kk, cos, sin, HEAD // 2)
    vv = kv[:, LANES:]
    low = _lane(kk.shape) < HEAD
    for src, ref in ((kk, k_ref), (vv, v_ref)):
        sw = pltpu.roll(src, HEAD, 1)
        ref[:, 0 * LANES:1 * LANES] = jnp.where(low, src, 0.0).astype(BF16)
        ref[:, 1 * LANES:2 * LANES] = jnp.where(low, 0.0, sw).astype(BF16)
        ref[:, 2 * LANES:3 * LANES] = jnp.where(low, sw, 0.0).astype(BF16)
        ref[:, 3 * LANES:4 * LANES] = jnp.where(low, 0.0, src).astype(BF16)


def _l0_proj(xa, mod, g, w, seg, gq, gk, cos, sin, *, tm, n_x):
    t, d = xa.shape
    rw = RET_HEADS * HEAD
    row = lambda i: (i, 0)
    outs = [jax.ShapeDtypeStruct((t, rw), BF16)] * 7
    return pl.pallas_call(
        functools.partial(_l0_proj_kernel, tm=tm, n_x=n_x, d=d),
        grid=(t // tm,),
        in_specs=[pl.BlockSpec((tm, d), row), _resident(mod.shape), _resident(g.shape), _resident(w.shape),
                  _resident(seg.shape), _resident(gq.shape), _resident(gk.shape),
                  pl.BlockSpec((tm, LANES), row), pl.BlockSpec((tm, LANES), row)],
        out_specs=[pl.BlockSpec((tm, rw), row)] * 7,
        out_shape=outs,
        compiler_params=_cparams(("parallel",)),
    )(xa, mod, g, w, seg, gq, gk, cos, sin)


def _retention_kernel(lg_ref, qf_ref, kf_ref, vf_ref, qb_ref, kb_ref, vb_ref, of_ref, ob_ref,
                      state_ref, decay_ref, xi_ref, zeta_ref, gl_ref):
    c = RET_CHUNK
    npairs = RET_HEADS * HEAD // LANES
    step = pl.program_id(0)

    @pl.when(step == 0)
    def _init():
        state_ref[...] = jnp.zeros_like(state_ref)
        ci = lax.broadcasted_iota(jnp.int32, (c, c), 0).astype(F32)
        mi = lax.broadcasted_iota(jnp.int32, (c, c), 1).astype(F32)
        pos = lax.broadcasted_iota(jnp.int32, (c, RET_HEADS * HEAD), 0).astype(F32)
        lane_head = _lane((1, RET_HEADS * HEAD)) // HEAD
        for dr in range(2):
            lgv = jnp.zeros((1, RET_HEADS * HEAD), F32)
            for hd in range(RET_HEADS):
                lg = lg_ref[dr, hd]
                rel = (ci - mi) if dr == 0 else (mi - ci)
                decay_ref[dr, hd] = jnp.where(rel >= 0, jnp.exp(jnp.maximum(rel, 0.0) * lg), 0.0)
                lgv = jnp.where(lane_head == hd, lg, lgv)
            p = pos if dr == 0 else (c - 1.0 - pos)
            xi_ref[dr] = jnp.exp((p + 1.0) * lgv)
            zeta_ref[dr] = jnp.exp((c - 1.0 - p) * lgv)
            gl_ref[dr] = jnp.exp(float(c) * lgv)

    low = _lane((c, LANES)) < HEAD
    r_i = lax.broadcasted_iota(jnp.int32, (LANES, LANES), 0) // HEAD
    c_i = lax.broadcasted_iota(jnp.int32, (LANES, LANES), 1) // HEAD
    blockdiag = r_i == c_i
    for dr, (q_ref, k_ref, v_ref, o_ref) in enumerate(((qf_ref, kf_ref, vf_ref, of_ref),
                                                       (qb_ref, kb_ref, vb_ref, ob_ref))):
        for j in range(npairs):
            sl = slice(j * LANES, (j + 1) * LANES)
            q, k, v = q_ref[:, sl], k_ref[:, sl], v_ref[:, sl]
            zero = jnp.zeros_like(q)
            s0 = _dot_nt(jnp.where(low, q, zero), k) * decay_ref[dr, 2 * j]
            s1 = _dot_nt(jnp.where(low, zero, q), k) * decay_ref[dr, 2 * j + 1]
            o = _dot(s0.astype(BF16), jnp.where(low, v, zero)) + _dot(s1.astype(BF16), jnp.where(low, zero, v))
            st = state_ref[dr, j]
            qx = (q.astype(F32) * xi_ref[dr, :, sl]).astype(BF16)
            o = o + _dot(qx, st.astype(BF16))
            o_ref[:, sl] = o.astype(BF16)
            kz = (k.astype(F32) * zeta_ref[dr, :, sl]).T.astype(BF16)
            u = _dot(kz, v)
            state_ref[dr, j] = st * gl_ref[dr, :, sl] + jnp.where(blockdiag, u, 0.0)


def _retention(lg, rq, rk, rv, *, n_x):
    t, w = rq.shape
    c = RET_CHUNK
    nc, ncx = t // c, n_x // c
    fwd = lambda i: ((i + ncx) % nc, 0)
    bwd = lambda i: (nc - 1 - i, 0)
    blk = lambda m: pl.BlockSpec((c, w), m)
    npairs = w // LANES
    return pl.pallas_call(
        _retention_kernel,
        grid=(nc,),
        in_specs=[pl.BlockSpec(memory_space=pltpu.SMEM)] + [blk(fwd)] * 3 + [blk(bwd)] * 3,
        out_specs=[blk(fwd), blk(bwd)],
        out_shape=[jax.ShapeDtypeStruct((t, w), BF16)] * 2,
        scratch_shapes=[pltpu.VMEM((2, npairs, LANES, LANES), F32),
                        pltpu.VMEM((2, RET_HEADS, c, c), F32),
                        pltpu.VMEM((2, c, w), F32), pltpu.VMEM((2, c, w), F32),
                        pltpu.VMEM((2, 1, w), F32)],
        compiler_params=_cparams(("arbitrary",)),
    )(lg, rq, rk, rv, rq, rk, rv)


def _flash_kernel(q0_ref, q1_ref, k0_ref, k1_ref, v0_ref, v1_ref, o_ref, *, bk, nkv):
    q0, q1 = q0_ref[...], q1_ref[...]
    bq = q0.shape[0]
    low = _lane((bq, LANES)) < HEAD

    def body(t, carry):
        m0, l0, m1, l1, acc = carry
        ks = pl.ds(pl.multiple_of(t * bk, bk), bk)
        s0 = _dot_nt(q0, k0_ref[ks, :])
        s1 = _dot_nt(q1, k1_ref[ks, :])
        m0n = jnp.maximum(m0, jnp.max(s0, axis=-1, keepdims=True))
        m1n = jnp.maximum(m1, jnp.max(s1, axis=-1, keepdims=True))
        a0, a1 = jnp.exp2(m0 - m0n), jnp.exp2(m1 - m1n)
        p0, p1 = jnp.exp2(s0 - m0n), jnp.exp2(s1 - m1n)
        l0 = a0 * l0 + jnp.sum(p0, axis=-1, keepdims=True)
        l1 = a1 * l1 + jnp.sum(p1, axis=-1, keepdims=True)
        pv = _dot(p0.astype(BF16), v0_ref[ks, :]) + _dot(p1.astype(BF16), v1_ref[ks, :])
        acc = acc * jnp.where(low, a0, a1) + pv
        return m0n, l0, m1n, l1, acc

    neg = jnp.full((bq, 1), -jnp.inf, F32)
    zero = jnp.zeros((bq, 1), F32)
    _, l0, _, l1, acc = lax.fori_loop(0, nkv, body, (neg, zero, neg, zero, jnp.zeros((bq, LANES), F32)))
    o_ref[...] = (acc / jnp.where(low, l0, l1)).astype(o_ref.dtype)


def _flash(q, kmat, vmat, *, q_maps, k_maps, v_maps, n_q, q_row0, kv_row0, n_kv, n_pairs, bq, bk):
    assert q_row0 % bq == 0 and n_q % bq == 0 and n_kv % bk == 0 and (kv_row0 % n_kv == 0)
    qb0, kb0 = q_row0 // bq, kv_row0 // n_kv
    qspec = lambda m: pl.BlockSpec((bq, LANES), lambda j, i: (i + qb0, m(j)))
    kspec = lambda m: pl.BlockSpec((n_kv, LANES), lambda j, i: (kb0, m(j)))
    return pl.pallas_call(
        functools.partial(_flash_kernel, bk=bk, nkv=n_kv // bk),
        grid=(n_pairs, n_q // bq),
        in_specs=[qspec(q_maps[0]), qspec(q_maps[1]), kspec(k_maps[0]), kspec(k_maps[1]),
                  kspec(v_maps[0]), kspec(v_maps[1])],
        out_specs=pl.BlockSpec((bq, LANES), lambda j, i: (i, j)),
        out_shape=jax.ShapeDtypeStruct((n_q, n_pairs * LANES), BF16),
        compiler_params=_cparams(("parallel", "parallel")),
    )(q, q, kmat, kmat, vmat[0], vmat[1])


def _l0_out_kernel(x_ref, mod_ref, of_ref, ob_ref, rg_ref, ao_ref, seg_ref, wo_ref, o_ref, *, tm, n_x, d):
    i = pl.program_id(0)
    seg = seg_ref[...]
    rw = RET_HEADS * HEAD
    acc = _dot(ao_ref[...], wo_ref[rw:, :])
    for g in range(rw // LANES):
        sl = slice(g * LANES, (g + 1) * LANES)
        o = of_ref[:, sl].astype(F32) + ob_ref[:, sl].astype(F32)
        dv = o - _seg_mean(o, seg)
        nrm = dv * lax.rsqrt(_seg_mean(dv * dv, seg) + EPS)
        ra = (nrm * _silu(rg_ref[:, sl].astype(F32))).astype(BF16)
        acc = acc + _dot(ra, wo_ref[g * LANES:(g + 1) * LANES, :])
    o_ref[...] = x_ref[...] + _row_gate(mod_ref, 2, i, tm, n_x, d) * acc


def _l0_out(xa, mod, o_f, o_b, rg, ao, seg, wo, *, tm, n_x):
    t, d = xa.shape
    rw = o_f.shape[1]
    row = lambda i: (i, 0)
    return pl.pallas_call(
        functools.partial(_l0_out_kernel, tm=tm, n_x=n_x, d=d),
        grid=(t // tm,),
        in_specs=[pl.BlockSpec((tm, d), row), _resident(mod.shape)] + [pl.BlockSpec((tm, rw), row)] * 4
                 + [_resident(seg.shape), _resident(wo.shape)],
        out_specs=pl.BlockSpec((tm, d), row),
        out_shape=jax.ShapeDtypeStruct((t, d), F32),
        compiler_params=_cparams(("parallel",)),
    )(xa, mod, o_f, o_b, rg, ao, seg, wo)


def _ffn_kernel(x_ref, mod_ref, g_ref, w1_ref, w3_ref, w2_ref, o_ref, *, tm, n_x, d):
    i = pl.program_id(0)
    x = x_ref[...]
    h = _modulated(x, mod_ref, g_ref, 1, i, tm, n_x, d).astype(BF16)
    a = _dot(h, w1_ref[...])
    u = (_silu(a) * _dot(h, w3_ref[...])).astype(BF16)
    o_ref[...] = x + _row_gate(mod_ref, 5, i, tm, n_x, d) * _dot(u, w2_ref[...])


def _ffn(xa, mod, g, w1, w3, w2, *, tm, n_x):
    t, d = xa.shape
    row = lambda i: (i, 0)
    return pl.pallas_call(
        functools.partial(_ffn_kernel, tm=tm, n_x=n_x, d=d),
        grid=(t // tm,),
        in_specs=[pl.BlockSpec((tm, d), row), _resident(mod.shape), _resident(g.shape),
                  _resident(w1.shape), _resident(w3.shape), _resident(w2.shape)],
        out_specs=pl.BlockSpec((tm, d), row),
        out_shape=jax.ShapeDtypeStruct((t, d), F32),
        compiler_params=_cparams(("parallel",)),
    )(xa, mod, g, w1, w3, w2)


def _l1_proj_kernel(x_ref, mod_ref, g_ref, wq_ref, wkv_ref, wkr_ref, gql_ref, gkvl_ref, wuq_ref, wuk_ref,
                    wuv_ref, seg_ref, gq_ref, gk_ref, gkr_ref, c_ref, s_ref,
                    q_ref, k_ref, vlo_ref, vhi_ref, *, tm, n_x, d):
    i = pl.program_id(0)
    h = _modulated(x_ref[...], mod_ref, g_ref, 0, i, tm, n_x, d).astype(BF16)
    seg = seg_ref[...]
    cos, sin = c_ref[...], s_ref[...]

    def lora_norm(v, g):
        return (v * lax.rsqrt(jnp.mean(v * v, axis=-1, keepdims=True) + EPS) * g).astype(BF16)

    cq = lora_norm(_dot(h, wq_ref[...]), gql_ref[...])
    ckv = lora_norm(_dot(h, wkv_ref[...]), gkvl_ref[...])
    kr = _dot(h, wkr_ref[...])
    kr = kr * lax.rsqrt(_seg_mean(kr * kr, seg) + EPS) * gkr_ref[...]
    kr = _rope128(kr, cos, sin, MLA_ROPE // 2)
    qa = _dot(cq, wuq_ref[...])
    ka = _dot(ckv, wuk_ref[...])
    for hd in range(MLA_HEADS):
        sl = slice(hd * LANES, (hd + 1) * LANES)
        v = qa[:, sl]
        vn = v * lax.rsqrt(_seg_mean(v * v, seg) + EPS) * gq_ref[...]
        q_ref[:, sl] = _rope128(vn, cos, sin, MLA_ROPE // 2).astype(BF16)
        v = ka[:, sl]
        k_ref[:, sl] = (v * lax.rsqrt(_seg_mean(v * v, seg) + EPS) * gk_ref[...] + kr).astype(BF16)
    va = _dot(ckv, wuv_ref[...])
    low = _lane(va.shape) % LANES < HEAD
    vlo_ref[...] = jnp.where(low, va, 0.0).astype(BF16)
    vhi_ref[...] = jnp.where(low, 0.0, va).astype(BF16)


def _l1_proj(xa, mod, g, wq, wkv, wkr, gql, gkvl, wuq, wuk, wuv, seg, gq, gk, gkr, cos, sin, *, tm, n_x):
    t, d = xa.shape
    row = lambda i: (i, 0)
    hw = MLA_HEADS * LANES
    vw = MLA_HEADS * HEAD
    consts = (mod, g, wq, wkv, wkr, gql, gkvl, wuq, wuk, wuv, seg, gq, gk, gkr)
    return pl.pallas_call(
        functools.partial(_l1_proj_kernel, tm=tm, n_x=n_x, d=d),
        grid=(t // tm,),
        in_specs=[pl.BlockSpec((tm, d), row)] + [_resident(a.shape) for a in consts]
                 + [pl.BlockSpec((tm, LANES), row)] * 2,
        out_specs=[pl.BlockSpec((tm, hw), row), pl.BlockSpec((tm, hw), row),
                   pl.BlockSpec((tm, vw), row), pl.BlockSpec((tm, vw), row)],
        out_shape=[jax.ShapeDtypeStruct((t, hw), BF16), jax.ShapeDtypeStruct((t, hw), BF16),
                   jax.ShapeDtypeStruct((t, vw), BF16), jax.ShapeDtypeStruct((t, vw), BF16)],
        compiler_params=_cparams(("parallel",)),
    )(xa, *consts, cos, sin)


def _l1_out_kernel(x_ref, mod_ref, g_ref, o_ref, wo_ref, rhi_ref, rlo_ref, x3_ref, h_ref, ei_ref, ew_ref, *, d):
    x3 = x_ref[...] + mod_ref[0:1, 2 * d:3 * d] * _dot(o_ref[...], wo_ref[...])
    x3_ref[...] = x3
    ms = jnp.mean(x3 * x3, axis=-1, keepdims=True)
    h = x3 * lax.rsqrt(ms + EPS) * (g_ref[...] * (1.0 + mod_ref[0:1, 4 * d:5 * d])) + mod_ref[0:1, 3 * d:4 * d]
    h_ref[...] = h.astype(BF16)
    hi = h.astype(BF16)
    lo = (h - hi.astype(F32)).astype(BF16)
    logits = _dot(hi, rhi_ref[...]) + (_dot(hi, rlo_ref[...]) + _dot(lo, rhi_ref[...]))
    lane_i = _lane(logits.shape)
    lane = lane_i.astype(F32)
    logits = jnp.where(lane_i < N_EXPERTS, logits, -jnp.inf)
    v1 = jnp.max(logits, axis=-1, keepdims=True)
    i1 = jnp.min(jnp.where(logits == v1, lane, float(LANES)), axis=-1, keepdims=True)
    rest = jnp.where(lane == i1, -jnp.inf, logits)
    v2 = jnp.max(rest, axis=-1, keepdims=True)
    i2 = jnp.min(jnp.where(rest == v2, lane, float(LANES)), axis=-1, keepdims=True)
    e2 = jnp.exp(v2 - v1)
    den = 1.0 + e2
    ei_ref[...] = jnp.where(lane_i == 0, i1, jnp.where(lane_i == 1, i2, 0.0)).astype(jnp.int32)
    ew_ref[...] = jnp.where(lane_i == 0, 1.0 / den, jnp.where(lane_i == 1, e2 / den, 0.0))


def _l1_out(xa, mod, g, o, wo, rhi, rlo, *, tm):
    n, d = o.shape[0], xa.shape[1]
    row = lambda i: (i, 0)
    return pl.pallas_call(
        functools.partial(_l1_out_kernel, d=d),
        grid=(n // tm,),
        in_specs=[pl.BlockSpec((tm, d), row), _resident(mod.shape), _resident(g.shape),
                  pl.BlockSpec((tm, o.shape[1]), row), _resident(wo.shape), _resident(rhi.shape),
                  _resident(rlo.shape)],
        out_specs=[pl.BlockSpec((tm, d), row), pl.BlockSpec((tm, d), row),
                   pl.BlockSpec((tm, LANES), row), pl.BlockSpec((tm, LANES), row)],
        out_shape=[jax.ShapeDtypeStruct((n, d), F32), jax.ShapeDtypeStruct((n, d), BF16),
                   jax.ShapeDtypeStruct((n, LANES), jnp.int32), jax.ShapeDtypeStruct((n, LANES), F32)],
        compiler_params=_cparams(("parallel",)),
    )(xa, mod, g, o, wo, rhi, rlo)


def _moe_kernel(te_ref, nv_ref, x_ref, w1_ref, w3_ref, w2_ref, rw_ref, y_ref, acc_ref, *, nf):
    i, f = pl.program_id(0), pl.program_id(1)

    @pl.when(f == 0)
    def _zero():
        acc_ref[...] = jnp.zeros_like(acc_ref)

    @pl.when(i < nv_ref[0])
    def _compute():
        x = x_ref[...]
        a = _dot(x, w1_ref[0])
        u = (_silu(a) * _dot(x, w3_ref[0])).astype(BF16)
        acc_ref[...] += _dot(u, w2_ref[0])

    @pl.when(f == nf - 1)
    def _store():
        y_ref[...] = (acc_ref[...] * rw_ref[...]).astype(y_ref.dtype)


def _moe(tile_expert, n_valid, xs, w1, w3, w2, rw, *, tm, tf):
    p, d = xs.shape
    fdim = w1.shape[2]
    nf = fdim // tf
    fi = lambda i, f, te, nv: jnp.where(i < nv[0], f, nf - 1)
    grid_spec = pltpu.PrefetchScalarGridSpec(
        num_scalar_prefetch=2,
        grid=(p // tm, nf),
        in_specs=[pl.BlockSpec((tm, d), lambda i, f, te, nv: (i, 0)),
                  pl.BlockSpec((1, d, tf), lambda i, f, te, nv: (te[i], 0, fi(i, f, te, nv))),
                  pl.BlockSpec((1, d, tf), lambda i, f, te, nv: (te[i], 0, fi(i, f, te, nv))),
                  pl.BlockSpec((1, tf, d), lambda i, f, te, nv: (te[i], fi(i, f, te, nv), 0)),
                  pl.BlockSpec((tm, 1), lambda i, f, te, nv: (i, 0))],
        out_specs=pl.BlockSpec((tm, d), lambda i, f, te, nv: (i, 0)),
        scratch_shapes=[pltpu.VMEM((tm, d), F32)],
    )
    return pl.pallas_call(
        functools.partial(_moe_kernel, nf=nf),
        grid_spec=grid_spec,
        out_shape=jax.ShapeDtypeStruct((p, d), BF16),
        compiler_params=_cparams(("arbitrary", "arbitrary")),
    )(tile_expert, n_valid, xs, w1, w3, w2, rw)


def _combine_kernel(x_ref, mod_ref, ya_ref, yb_ref, o_ref, *, d):
    y = ya_ref[...].astype(F32) + yb_ref[...].astype(F32)
    o_ref[...] = x_ref[...] + mod_ref[0:1, 5 * d:6 * d] * y


def _combine(x3, mod, ya, yb, *, tm):
    n, d = x3.shape
    row = lambda i: (i, 0)
    return pl.pallas_call(
        functools.partial(_combine_kernel, d=d),
        grid=(n // tm,),
        in_specs=[pl.BlockSpec((tm, d), row), _resident(mod.shape), pl.BlockSpec((tm, d), row),
                  pl.BlockSpec((tm, d), row)],
        out_specs=pl.BlockSpec((tm, d), row),
        out_shape=jax.ShapeDtypeStruct((n, d), F32),
        compiler_params=_cparams(("parallel",)),
    )(x3, mod, ya, yb)


def _deinterleave(width):
    return np.concatenate([np.arange(0, width, 2), np.arange(1, width, 2)])


def _rope_tables(n_x, n_ctx, rot_dim, seg_start, seg_repeat):
    rows = n_x // GRID_W
    row = jnp.broadcast_to(jnp.arange(rows)[:, None], (rows, GRID_W)).reshape(n_x).astype(F32)
    col = jnp.broadcast_to(jnp.arange(GRID_W)[None, :], (rows, GRID_W)).reshape(n_x).astype(F32)
    axis_dim = rot_dim // 2
    inv_freq = ROPE_THETA ** (-jnp.arange(0, axis_dim, 2, dtype=F32) / axis_dim)
    ang = jnp.concatenate([row[:, None] * inv_freq, col[:, None] * inv_freq], axis=-1)
    cos, sin = jnp.cos(ang), jnp.sin(ang)
    cseg = jnp.concatenate([cos, cos], axis=-1)
    sseg = jnp.concatenate([-sin, sin], axis=-1)
    c = jnp.ones((n_x, LANES), F32)
    s = jnp.zeros((n_x, LANES), F32)
    for r in range(seg_repeat):
        lo = seg_start + r * rot_dim
        c = c.at[:, lo:lo + rot_dim].set(cseg)
        s = s.at[:, lo:lo + rot_dim].set(sseg)
    c = jnp.concatenate([c, jnp.ones((n_ctx, LANES), F32)], axis=0)
    s = jnp.concatenate([s, jnp.zeros((n_ctx, LANES), F32)], axis=0)
    return c, s


def _segment_matrix(bounds):
    m = np.zeros((LANES, LANES), np.float32)
    for lo, hi in bounds:
        m[lo:hi, lo:hi] = 1.0 / (hi - lo)
    return jnp.asarray(m, BF16)


def _token_tile(t):
    for tm in (640, 512, 256, 128):
        if t % tm == 0:
            return tm
    raise ValueError(f"token count {t} has no supported tile")


def kernel(x, c, ctx, c_ctx, l0_ada_w, l0_ada_b, l0_norm1_g, l0_norm2_g, l0_w_in, l0_ret_log_decay, l0_q_norm_g, l0_k_norm_g, l0_w_out, l0_ffn_w1, l0_ffn_w3, l0_ffn_w2, l1_ada_w, l1_ada_b, l1_norm1_g, l1_norm2_g, l1_w_in, l1_q_lora_g, l1_kv_lora_g, l1_w_uq, l1_w_ukv, l1_q_nope_g, l1_q_rope_g, l1_k_nope_g, l1_k_rope_g, l1_w_out, l1_router, l1_exp_w1, l1_exp_w3, l1_exp_w2):
    b, n_x, d = x.shape
    n_ctx = ctx.shape[1]
    assert b == 1 and n_x % 256 == 0 and n_ctx % 256 == 0 and n_x % GRID_W == 0
    t = n_x + n_ctx
    tm = _token_tile(t)
    tmx = _token_tile(n_x)
    xa = jnp.concatenate([x[0], ctx[0]], axis=0)
    row1 = lambda v: v.reshape(1, -1).astype(F32)

    cvec = jnp.zeros((8, d), F32).at[0].set(c[0]).at[1].set(c_ctx)
    mod0 = _ada(cvec, l0_ada_w, l0_ada_b)
    mod1 = _ada(cvec, l1_ada_w, l1_ada_b)

    rw = RET_HEADS * HEAD
    perm = _deinterleave(HEAD)
    cols = np.arange(l0_w_in.shape[1])
    for hd in range(GQA_HEADS + GQA_KV_HEADS):
        lo = 4 * rw + hd * HEAD
        cols[lo:lo + HEAD] = lo + perm
    colscale = np.ones((l0_w_in.shape[1],), np.float32)
    colscale[rw:2 * rw] = HEAD ** -0.5
    w_in0 = (l0_w_in[:, cols] * colscale).astype(BF16)
    seg64 = _segment_matrix([(0, HEAD), (HEAD, 2 * HEAD)])
    gq0 = row1(jnp.tile(l0_q_norm_g[perm], 2) * (HEAD ** -0.5 * LOG2E))
    gk0 = row1(jnp.tile(l0_k_norm_g[perm], 2))
    cos0, sin0 = _rope_tables(n_x, n_ctx, HEAD, 0, 2)

    rq, rk, rv, rg, gq, gkx, gvx = _l0_proj(xa, mod0, row1(l0_norm1_g), w_in0, seg64, gq0, gk0, cos0, sin0,
                                             tm=tm, n_x=n_x)
    o_f, o_b = _retention(l0_ret_log_decay.astype(F32), rq, rk, rv, n_x=n_x)

    gqa_maps = dict(q_maps=(lambda j: j, lambda j: j),
                    k_maps=(lambda j: 2 * (j // 2), lambda j: 2 * (j // 2) + 1),
                    v_maps=(lambda j: 2 * (j // 2), lambda j: 2 * (j // 2) + 1), n_pairs=GQA_HEADS // 2)
    bq = 512 if n_x % 512 == 0 else 256
    ao_x = _flash(gq, gkx, (gvx, gvx), n_q=n_x, q_row0=0, kv_row0=0, n_kv=t, bq=bq, bk=256, **gqa_maps)
    ao_c = _flash(gq, gkx, (gvx, gvx), n_q=n_ctx, q_row0=n_x, kv_row0=n_x, n_kv=n_ctx, bq=n_ctx, bk=256,
                  **gqa_maps)
    ao = jnp.concatenate([ao_x, ao_c], axis=0)

    xa = _l0_out(xa, mod0, o_f, o_b, rg, ao, seg64, l0_w_out.astype(BF16), tm=tm, n_x=n_x)
    xa = _ffn(xa, mod0, row1(l0_norm2_g), l0_ffn_w1.astype(BF16), l0_ffn_w3.astype(BF16),
              l0_ffn_w2.astype(BF16), tm=tm, n_x=n_x)

    rperm = _deinterleave(MLA_ROPE)
    qk_w = MLA_NOPE + MLA_ROPE
    wuq = jnp.zeros((MLA_Q_RANK, MLA_HEADS * LANES), F32)
    wuk = jnp.zeros((MLA_KV_RANK, MLA_HEADS * LANES), F32)
    wuv = []
    for hd in range(MLA_HEADS):
        src = l1_w_uq[:, hd * qk_w:(hd + 1) * qk_w]
        wuq = wuq.at[:, hd * LANES:hd * LANES + MLA_NOPE].set(src[:, :MLA_NOPE])
        wuq = wuq.at[:, hd * LANES + MLA_NOPE:hd * LANES + qk_w].set(src[:, MLA_NOPE:][:, rperm])
        kvsrc = l1_w_ukv[:, hd * 2 * HEAD:(hd + 1) * 2 * HEAD]
        wuk = wuk.at[:, hd * LANES:hd * LANES + MLA_NOPE].set(kvsrc[:, :MLA_NOPE])
        wuv.append(kvsrc[:, MLA_NOPE:])
    wuv = jnp.concatenate(wuv, axis=1)
    wkr = jnp.zeros((d, LANES), F32).at[:, MLA_NOPE:qk_w].set(l1_w_in[:, MLA_Q_RANK + MLA_KV_RANK:][:, rperm])
    pad = jnp.zeros((LANES - qk_w,), F32)
    zn = jnp.zeros((MLA_NOPE,), F32)
    gq1 = row1(jnp.concatenate([l1_q_nope_g, l1_q_rope_g[rperm], pad]) * (qk_w ** -0.5 * LOG2E))
    gk1 = row1(jnp.concatenate([l1_k_nope_g, jnp.zeros((LANES - MLA_NOPE,), F32)]))
    gkr1 = row1(jnp.concatenate([zn, l1_k_rope_g[rperm], pad]))
    seg_mla = _segment_matrix([(0, MLA_NOPE), (MLA_NOPE, qk_w)])
    cos1, sin1 = _rope_tables(n_x, n_ctx, MLA_ROPE, MLA_NOPE, 1)

    mq, mk, mvlo, mvhi = _l1_proj(
        xa, mod1, row1(l1_norm1_g), l1_w_in[:, :MLA_Q_RANK].astype(BF16),
        l1_w_in[:, MLA_Q_RANK:MLA_Q_RANK + MLA_KV_RANK].astype(BF16), wkr.astype(BF16),
        row1(l1_q_lora_g), row1(l1_kv_lora_g), wuq.astype(BF16), wuk.astype(BF16), wuv.astype(BF16),
        seg_mla, gq1, gk1, gkr1, cos1, sin1, tm=tm, n_x=n_x)
    mo = _flash(mq, mk, (mvlo, mvhi), q_maps=(lambda j: 2 * j, lambda j: 2 * j + 1),
                k_maps=(lambda j: 2 * j, lambda j: 2 * j + 1), v_maps=(lambda j: j, lambda j: j),
                n_pairs=MLA_HEADS // 2, n_q=n_x, q_row0=0, kv_row0=0, n_kv=t, bq=bq, bk=256)

    router = jnp.zeros((d, LANES), F32).at[:, :N_EXPERTS].set(l1_router)
    r_hi = router.astype(BF16)
    r_lo = (router - r_hi.astype(F32)).astype(BF16)
    x3, hmoe, ei, ew = _l1_out(xa, mod1, row1(l1_norm2_g), mo, l1_w_out.astype(BF16), r_hi, r_lo, tm=tmx)

    tme = 512 if n_x >= 4096 else 128
    e_flat = ei[:, :TOP_K].reshape(-1)
    w_flat = ew[:, :TOP_K].reshape(-1)
    onehot = (e_flat[:, None] == jnp.arange(N_EXPERTS)[None, :]).astype(jnp.int32)
    csum = jnp.cumsum(onehot, axis=0)
    rank = jnp.sum((csum - onehot) * onehot, axis=1)
    counts = csum[-1]
    padded = ((counts + tme - 1) // tme) * tme
    ends = jnp.cumsum(padded)
    pos = (ends - padded)[e_flat] + rank
    n_tiles = (TOP_K * n_x) // tme + N_EXPERTS
    p_rows = n_tiles * tme
    src = jnp.zeros((p_rows,), jnp.int32).at[pos].set(jnp.arange(TOP_K * n_x, dtype=jnp.int32) // TOP_K)
    roww = jnp.zeros((p_rows,), F32).at[pos].set(w_flat)
    n_valid = (ends[-1] // tme).astype(jnp.int32)
    tile_ids = jnp.minimum(jnp.arange(n_tiles, dtype=jnp.int32), n_valid - 1)
    tile_expert = jnp.minimum(jnp.searchsorted(ends, tile_ids * tme, side="right"), N_EXPERTS - 1).astype(jnp.int32)
    xs = jnp.take(hmoe, src, axis=0)
    fdim = l1_exp_w1.shape[2]
    tf = fdim // 2 if (fdim // 2) % LANES == 0 else fdim
    ys = _moe(tile_expert, n_valid.reshape(1), xs, l1_exp_w1.astype(BF16), l1_exp_w3.astype(BF16),
              l1_exp_w2.astype(BF16), roww.reshape(p_rows, 1), tm=tme, tf=tf)
    pos2 = pos.reshape(n_x, TOP_K)
    out = _combine(x3, mod1, jnp.take(ys, pos2[:, 0], axis=0), jnp.take(ys, pos2[:, 1], axis=0), tm=tmx)
    return out[None]
```

```python
import functools
import math

import numpy as np
import jax
import jax.numpy as jnp
from jax import lax
from jax.experimental import pallas as pl
from jax.experimental.pallas import tpu as pltpu

F32 = jnp.float32
BF16 = jnp.bfloat16

EPS = 1e-6
ROPE_THETA = 10000.0
GRID_W = 64
LANES = 128
HEAD = 64
RET_CHUNK = 128
RET_HEADS = 8
GQA_HEADS = 8
GQA_KV_HEADS = 2
MLA_HEADS = 8
MLA_Q_RANK = 384
MLA_KV_RANK = 256
MLA_NOPE = 64
MLA_ROPE = 32
N_EXPERTS = 8
TOP_K = 2
LOW_ONE = HEAD
HIGH_ONE = 0
LOG2E = math.log2(math.e)
VMEM_LIMIT = 56 * 1024 * 1024


def _cparams(sem, vmem=VMEM_LIMIT):
    return pltpu.CompilerParams(dimension_semantics=sem, vmem_limit_bytes=vmem)


def _resident(shape):
    nd = len(shape)
    return pl.BlockSpec(shape, lambda *_: (0,) * nd, pipeline_mode=pl.Buffered(1))


def _dot(a, b):
    return jnp.dot(a, b, preferred_element_type=F32)


def _dot_nt(a, b):
    return lax.dot_general(a, b, (((1,), (1,)), ((), ())), preferred_element_type=F32)


def _seg_mean(v, seg):
    hi = v.astype(BF16)
    lo = (v - hi.astype(F32)).astype(BF16)
    return _dot(hi, seg) + _dot(lo, seg)


def _silu(x):
    return x * jax.nn.sigmoid(x)


def _modulated(x, mod_ref, g_ref, which, tile, tm, n_x, d):
    ms = jnp.mean(x * x, axis=-1, keepdims=True)
    xn = x * lax.rsqrt(ms + EPS)
    g = g_ref[...]
    sh, sc = 3 * which, 3 * which + 1
    a_x = g * (1.0 + mod_ref[0:1, sc * d:(sc + 1) * d])
    a_c = g * (1.0 + mod_ref[1:2, sc * d:(sc + 1) * d])
    b_x = mod_ref[0:1, sh * d:(sh + 1) * d]
    b_c = mod_ref[1:2, sh * d:(sh + 1) * d]
    row = tile * tm + lax.broadcasted_iota(jnp.int32, (tm, 1), 0)
    is_ctx = row >= n_x
    return xn * jnp.where(is_ctx, a_c, a_x) + jnp.where(is_ctx, b_c, b_x)


def _row_gate(mod_ref, idx, tile, tm, n_x, d):
    row = tile * tm + lax.broadcasted_iota(jnp.int32, (tm, 1), 0)
    return jnp.where(row >= n_x, mod_ref[1:2, idx * d:(idx + 1) * d], mod_ref[0:1, idx * d:(idx + 1) * d])


def _lane(shape):
    return lax.broadcasted_iota(jnp.int32, shape, len(shape) - 1)


def _ada_kernel(c_ref, w_ref, b_ref, o_ref):
    c = c_ref[...]
    o_ref[...] = jnp.dot(_silu(c), w_ref[...], preferred_element_type=F32,
                         precision=lax.Precision.HIGHEST) + b_ref[...]


def _ada(cvec8, w, b):
    d, n = w.shape
    tn = n // 4
    return pl.pallas_call(
        _ada_kernel,
        grid=(n // tn,),
        in_specs=[pl.BlockSpec((8, d), lambda j: (0, 0)),
                  pl.BlockSpec((d, tn), lambda j: (0, j)),
                  pl.BlockSpec((1, tn), lambda j: (0, j))],
        out_specs=pl.BlockSpec((8, tn), lambda j: (0, j)),
        out_shape=jax.ShapeDtypeStruct((8, n), F32),
        compiler_params=_cparams(("arbitrary",)),
    )(cvec8, w, b.reshape(1, n))


def _rope128(v, c, s, half):
    lane = _lane(v.shape)
    swapped = jnp.where(lane % (2 * half) < half, pltpu.roll(v, LANES - half, 1), pltpu.roll(v, half, 1))
    return v * c + swapped * s


def _l0_proj_kernel(x_ref, mod_ref, g_ref, w_ref, seg_ref, gq_ref, gk_ref, c_ref, s_ref,
                    rq_ref, rk_ref, rv_ref, rg_ref, q_ref, k_ref, v_ref, *, tm, n_x, d):
    i = pl.program_id(0)
    h = _modulated(x_ref[...], mod_ref, g_ref, 0, i, tm, n_x, d).astype(BF16)
    rw = RET_HEADS * HEAD
    for idx, ref in enumerate((rq_ref, rk_ref, rv_ref, rg_ref)):
        ref[...] = _dot(h, w_ref[:, idx * rw:(idx + 1) * rw]).astype(BF16)
    seg = seg_ref[...]
    cos, sin = c_ref[...], s_ref[...]
    base = 4 * rw
    qw = GQA_HEADS * HEAD
    qa = _dot(h, w_ref[:, base:base + qw])
    for g in range(qw // LANES):
        v = qa[:, g * LANES:(g + 1) * LANES]
        vn = v * lax.rsqrt(_seg_mean(v * v, seg) + EPS) * gq_ref[...]
        q_ref[:, g * LANES:(g + 1) * LANES] = _rope128(vn, cos, sin, HEAD // 2).astype(BF16)
    kv = _dot(h, w_ref[:, base + qw:base + qw + 2 * LANES])
    kk = kv[:, :LANES]
    kk = kk * lax.rsqrt(_seg_mean(kk * kk, seg) + EPS) * gk_ref[...]
    kk = _rope128(kk, cos, sin, HEAD // 2)
    vv = kv[:, LANES:]
    lane = _lane(kk.shape)
    low = lane < HEAD
    for src, ref, one in ((kk, k_ref, 0.0), (vv, v_ref, 1.0)):
        sw = pltpu.roll(src, HEAD, 1)
        lo_fill = jnp.where(lane == LOW_ONE, one, 0.0)
        hi_fill = jnp.where(lane == HIGH_ONE, one, 0.0)
        ref[:, 0 * LANES:1 * LANES] = jnp.where(low, src, lo_fill).astype(BF16)
        ref[:, 1 * LANES:2 * LANES] = jnp.where(low, hi_fill, sw).astype(BF16)
        ref[:, 2 * LANES:3 * LANES] = jnp.where(low, sw, lo_fill).astype(BF16)
        ref[:, 3 * LANES:4 * LANES] = jnp.where(low, hi_fill, src).astype(BF16)


def _l0_proj(xa, mod, g, w, seg, gq, gk, cos, sin, *, tm, n_x):
    t, d = xa.shape
    rw = RET_HEADS * HEAD
    row = lambda i: (i, 0)
    outs = [jax.ShapeDtypeStruct((t, rw), BF16)] * 7
    return pl.pallas_call(
        functools.partial(_l0_proj_kernel, tm=tm, n_x=n_x, d=d),
        grid=(t // tm,),
        in_specs=[pl.BlockSpec((tm, d), row), _resident(mod.shape), _resident(g.shape), _resident(w.shape),
                  _resident(seg.shape), _resident(gq.shape), _resident(gk.shape),
                  pl.BlockSpec((tm, LANES), row), pl.BlockSpec((tm, LANES), row)],
        out_specs=[pl.BlockSpec((tm, rw), row)] * 7,
        out_shape=outs,
        compiler_params=_cparams(("parallel",)),
    )(xa, mod, g, w, seg, gq, gk, cos, sin)


def _retention_kernel(lg_ref, qf_ref, kf_ref, vf_ref, qb_ref, kb_ref, vb_ref, of_ref, ob_ref,
                      state_ref, decay_ref, xi_ref, zeta_ref, gl_ref):
    c = RET_CHUNK
    npairs = RET_HEADS * HEAD // LANES
    step = pl.program_id(0)

    @pl.when(step == 0)
    def _init():
        state_ref[...] = jnp.zeros_like(state_ref)
        ci = lax.broadcasted_iota(jnp.int32, (c, c), 0).astype(F32)
        mi = lax.broadcasted_iota(jnp.int32, (c, c), 1).astype(F32)
        pos = lax.broadcasted_iota(jnp.int32, (c, RET_HEADS * HEAD), 0).astype(F32)
        lane_head = _lane((1, RET_HEADS * HEAD)) // HEAD
        for dr in range(2):
            lgv = jnp.zeros((1, RET_HEADS * HEAD), F32)
            for hd in range(RET_HEADS):
                lg = lg_ref[dr, hd]
                rel = (ci - mi) if dr == 0 else (mi - ci)
                decay_ref[dr, hd] = jnp.where(rel >= 0, jnp.exp(jnp.maximum(rel, 0.0) * lg), 0.0)
                lgv = jnp.where(lane_head == hd, lg, lgv)
            p = pos if dr == 0 else (c - 1.0 - pos)
            xi_ref[dr] = jnp.exp((p + 1.0) * lgv)
            zeta_ref[dr] = jnp.exp((c - 1.0 - p) * lgv)
            gl_ref[dr] = jnp.exp(float(c) * lgv)

    low = _lane((c, LANES)) < HEAD
    r_i = lax.broadcasted_iota(jnp.int32, (LANES, LANES), 0) // HEAD
    c_i = lax.broadcasted_iota(jnp.int32, (LANES, LANES), 1) // HEAD
    blockdiag = r_i == c_i
    for dr, (q_ref, k_ref, v_ref, o_ref) in enumerate(((qf_ref, kf_ref, vf_ref, of_ref),
                                                       (qb_ref, kb_ref, vb_ref, ob_ref))):
        for j in range(npairs):
            sl = slice(j * LANES, (j + 1) * LANES)
            q, k, v = q_ref[:, sl], k_ref[:, sl], v_ref[:, sl]
            zero = jnp.zeros_like(q)
            s0 = _dot_nt(jnp.where(low, q, zero), k) * decay_ref[dr, 2 * j]
            s1 = _dot_nt(jnp.where(low, zero, q), k) * decay_ref[dr, 2 * j + 1]
            o = _dot(s0.astype(BF16), jnp.where(low, v, zero)) + _dot(s1.astype(BF16), jnp.where(low, zero, v))
            st = state_ref[dr, j]
            qx = (q.astype(F32) * xi_ref[dr, :, sl]).astype(BF16)
            o = o + _dot(qx, st.astype(BF16))
            o_ref[:, sl] = o.astype(BF16)
            kz = (k.astype(F32) * zeta_ref[dr, :, sl]).T.astype(BF16)
            u = _dot(kz, v)
            state_ref[dr, j] = st * gl_ref[dr, :, sl] + jnp.where(blockdiag, u, 0.0)


def _retention(lg, rq, rk, rv, *, n_x):
    t, w = rq.shape
    c = RET_CHUNK
    nc, ncx = t // c, n_x // c
    fwd = lambda i: ((i + ncx) % nc, 0)
    bwd = lambda i: (nc - 1 - i, 0)
    blk = lambda m: pl.BlockSpec((c, w), m)
    npairs = w // LANES
    return pl.pallas_call(
        _retention_kernel,
        grid=(nc,),
        in_specs=[pl.BlockSpec(memory_space=pltpu.SMEM)] + [blk(fwd)] * 3 + [blk(bwd)] * 3,
        out_specs=[blk(fwd), blk(bwd)],
        out_shape=[jax.ShapeDtypeStruct((t, w), BF16)] * 2,
        scratch_shapes=[pltpu.VMEM((2, npairs, LANES, LANES), F32),
                        pltpu.VMEM((2, RET_HEADS, c, c), F32),
                        pltpu.VMEM((2, c, w), F32), pltpu.VMEM((2, c, w), F32),
                        pltpu.VMEM((2, 1, w), F32)],
        compiler_params=_cparams(("arbitrary",)),
    )(lg, rq, rk, rv, rq, rk, rv)


def _flash_kernel(q0_ref, q1_ref, k0_ref, k1_ref, v0_ref, v1_ref, o_ref,
                  s0_ref, s1_ref, p0_ref, p1_ref, m0_ref, m1_ref, a0_ref, a1_ref, acc0_ref, acc1_ref,
                  *, bk, nkv, rs):
    bq = q0_ref.shape[0]
    heads = ((q0_ref, k0_ref, v0_ref, s0_ref, p0_ref, m0_ref, a0_ref, acc0_ref),
             (q1_ref, k1_ref, v1_ref, s1_ref, p1_ref, m1_ref, a1_ref, acc1_ref))
    for _, _, _, _, _, m_ref, _, acc_ref in heads:
        m_ref[...] = jnp.full(m_ref.shape, -jnp.inf, F32)
        acc_ref[...] = jnp.zeros(acc_ref.shape, F32)

    def body(t, carry):
        ks = pl.ds(pl.multiple_of(t * bk, bk), bk)
        for q_ref, k_ref, _, s_ref, _, _, _, _ in heads:
            s_ref[...] = _dot_nt(q_ref[...], k_ref[ks, :])
        for r in range(bq // rs):
            rows = slice(r * rs, (r + 1) * rs)
            for _, _, _, s_ref, p_ref, m_ref, a_ref, _ in heads:
                s = s_ref[rows, :]
                m_old = m_ref[rows, :]
                m_new = jnp.maximum(m_old, jnp.max(s, axis=-1, keepdims=True))
                p_ref[rows, :] = jnp.exp2(s - m_new[:, 0:1]).astype(BF16)
                a_ref[rows, :] = jnp.exp2(m_old - m_new)
                m_ref[rows, :] = m_new
        for _, _, v_ref, _, p_ref, _, a_ref, acc_ref in heads:
            acc_ref[...] = acc_ref[...] * a_ref[...] + _dot(p_ref[...], v_ref[ks, :])
        return carry

    lax.fori_loop(0, nkv, body, 0)
    acc0, acc1 = acc0_ref[...], acc1_ref[...]
    low = _lane((bq, LANES)) < HEAD
    out = jnp.where(low, acc0 / acc0[:, LOW_ONE:LOW_ONE + 1], acc1 / acc1[:, HIGH_ONE:HIGH_ONE + 1])
    o_ref[...] = out.astype(o_ref.dtype)


def _flash(q, kmat, vmat, *, q_maps, k_maps, v_maps, n_q, q_row0, kv_row0, n_kv, n_pairs, bq, bk, rs):
    assert q_row0 % bq == 0 and n_q % bq == 0 and n_kv % bk == 0 and kv_row0 % n_kv == 0 and bq % rs == 0
    qb0, kb0 = q_row0 // bq, kv_row0 // n_kv
    qspec = lambda m: pl.BlockSpec((bq, LANES), lambda j, i: (i + qb0, m(j)))
    kspec = lambda m: pl.BlockSpec((n_kv, LANES), lambda j, i: (kb0, m(j)))
    stat = pltpu.VMEM((bq, LANES), F32)
    return pl.pallas_call(
        functools.partial(_flash_kernel, bk=bk, nkv=n_kv // bk, rs=rs),
        grid=(n_pairs, n_q // bq),
        in_specs=[qspec(q_maps[0]), qspec(q_maps[1]), kspec(k_maps[0]), kspec(k_maps[1]),
                  kspec(v_maps[0]), kspec(v_maps[1])],
        out_specs=pl.BlockSpec((bq, LANES), lambda j, i: (i, j)),
        out_shape=jax.ShapeDtypeStruct((n_q, n_pairs * LANES), BF16),
        scratch_shapes=[pltpu.VMEM((bq, bk), F32)] * 2 + [pltpu.VMEM((bq, bk), BF16)] * 2 + [stat] * 6,
        compiler_params=_cparams(("parallel", "parallel")),
    )(q, q, kmat, kmat, vmat[0], vmat[1])


def _kv_block(n_kv):
    for bk in (1280, 1024, 512, 256):
        if n_kv % bk == 0:
            return bk
    raise ValueError(f"key count {n_kv} has no supported block")


def _l0_out_kernel(x_ref, mod_ref, of_ref, ob_ref, rg_ref, ao_ref, seg_ref, wo_ref, o_ref, *, tm, n_x, d):
    i = pl.program_id(0)
    seg = seg_ref[...]
    rw = RET_HEADS * HEAD
    acc = _dot(ao_ref[...], wo_ref[rw:, :])
    for g in range(rw // LANES):
        sl = slice(g * LANES, (g + 1) * LANES)
        o = of_ref[:, sl].astype(F32) + ob_ref[:, sl].astype(F32)
        dv = o - _seg_mean(o, seg)
        nrm = dv * lax.rsqrt(_seg_mean(dv * dv, seg) + EPS)
        ra = (nrm * _silu(rg_ref[:, sl].astype(F32))).astype(BF16)
        acc = acc + _dot(ra, wo_ref[g * LANES:(g + 1) * LANES, :])
    o_ref[...] = x_ref[...] + _row_gate(mod_ref, 2, i, tm, n_x, d) * acc


def _l0_out(xa, mod, o_f, o_b, rg, ao, seg, wo, *, tm, n_x):
    t, d = xa.shape
    rw = o_f.shape[1]
    row = lambda i: (i, 0)
    return pl.pallas_call(
        functools.partial(_l0_out_kernel, tm=tm, n_x=n_x, d=d),
        grid=(t // tm,),
        in_specs=[pl.BlockSpec((tm, d), row), _resident(mod.shape)] + [pl.BlockSpec((tm, rw), row)] * 4
                 + [_resident(seg.shape), _resident(wo.shape)],
        out_specs=pl.BlockSpec((tm, d), row),
        out_shape=jax.ShapeDtypeStruct((t, d), F32),
        compiler_params=_cparams(("parallel",)),
    )(xa, mod, o_f, o_b, rg, ao, seg, wo)


def _ffn_kernel(x_ref, mod_ref, g_ref, w1_ref, w3_ref, w2_ref, o_ref, *, tm, n_x, d):
    i = pl.program_id(0)
    x = x_ref[...]
    h = _modulated(x, mod_ref, g_ref, 1, i, tm, n_x, d).astype(BF16)
    a = _dot(h, w1_ref[...])
    u = (_silu(a) * _dot(h, w3_ref[...])).astype(BF16)
    o_ref[...] = x + _row_gate(mod_ref, 5, i, tm, n_x, d) * _dot(u, w2_ref[...])


def _ffn(xa, mod, g, w1, w3, w2, *, tm, n_x):
    t, d = xa.shape
    row = lambda i: (i, 0)
    return pl.pallas_call(
        functools.partial(_ffn_kernel, tm=tm, n_x=n_x, d=d),
        grid=(t // tm,),
        in_specs=[pl.BlockSpec((tm, d), row), _resident(mod.shape), _resident(g.shape),
                  _resident(w1.shape), _resident(w3.shape), _resident(w2.shape)],
        out_specs=pl.BlockSpec((tm, d), row),
        out_shape=jax.ShapeDtypeStruct((t, d), F32),
        compiler_params=_cparams(("parallel",)),
    )(xa, mod, g, w1, w3, w2)


def _l1_proj_kernel(x_ref, mod_ref, g_ref, wq_ref, wkv_ref, wkr_ref, gql_ref, gkvl_ref, wuq_ref, wuk_ref,
                    wuv_ref, seg_ref, gq_ref, gk_ref, gkr_ref, c_ref, s_ref,
                    q_ref, k_ref, vlo_ref, vhi_ref, *, tm, n_x, d):
    i = pl.program_id(0)
    h = _modulated(x_ref[...], mod_ref, g_ref, 0, i, tm, n_x, d).astype(BF16)
    seg = seg_ref[...]
    cos, sin = c_ref[...], s_ref[...]

    def lora_norm(v, g):
        return (v * lax.rsqrt(jnp.mean(v * v, axis=-1, keepdims=True) + EPS) * g).astype(BF16)

    cq = lora_norm(_dot(h, wq_ref[...]), gql_ref[...])
    ckv = lora_norm(_dot(h, wkv_ref[...]), gkvl_ref[...])
    kr = _dot(h, wkr_ref[...])
    kr = kr * lax.rsqrt(_seg_mean(kr * kr, seg) + EPS) * gkr_ref[...]
    kr = _rope128(kr, cos, sin, MLA_ROPE // 2)
    qa = _dot(cq, wuq_ref[...])
    ka = _dot(ckv, wuk_ref[...])
    for hd in range(MLA_HEADS):
        sl = slice(hd * LANES, (hd + 1) * LANES)
        v = qa[:, sl]
        vn = v * lax.rsqrt(_seg_mean(v * v, seg) + EPS) * gq_ref[...]
        q_ref[:, sl] = _rope128(vn, cos, sin, MLA_ROPE // 2).astype(BF16)
        v = ka[:, sl]
        k_ref[:, sl] = (v * lax.rsqrt(_seg_mean(v * v, seg) + EPS) * gk_ref[...] + kr).astype(BF16)
    va = _dot(ckv, wuv_ref[...])
    lane = _lane(va.shape) % LANES
    low = lane < HEAD
    vlo_ref[...] = jnp.where(low, va, jnp.where(lane == LOW_ONE, 1.0, 0.0)).astype(BF16)
    vhi_ref[...] = jnp.where(low, jnp.where(lane == HIGH_ONE, 1.0, 0.0), va).astype(BF16)


def _l1_proj(xa, mod, g, wq, wkv, wkr, gql, gkvl, wuq, wuk, wuv, seg, gq, gk, gkr, cos, sin, *, tm, n_x):
    t, d = xa.shape
    row = lambda i: (i, 0)
    hw = MLA_HEADS * LANES
    vw = MLA_HEADS * HEAD
    consts = (mod, g, wq, wkv, wkr, gql, gkvl, wuq, wuk, wuv, seg, gq, gk, gkr)
    return pl.pallas_call(
        functools.partial(_l1_proj_kernel, tm=tm, n_x=n_x, d=d),
        grid=(t // tm,),
        in_specs=[pl.BlockSpec((tm, d), row)] + [_resident(a.shape) for a in consts]
                 + [pl.BlockSpec((tm, LANES), row)] * 2,
        out_specs=[pl.BlockSpec((tm, hw), row), pl.BlockSpec((tm, hw), row),
                   pl.BlockSpec((tm, vw), row), pl.BlockSpec((tm, vw), row)],
        out_shape=[jax.ShapeDtypeStruct((t, hw), BF16), jax.ShapeDtypeStruct((t, hw), BF16),
                   jax.ShapeDtypeStruct((t, vw), BF16), jax.ShapeDtypeStruct((t, vw), BF16)],
        compiler_params=_cparams(("parallel",)),
    )(xa, *consts, cos, sin)


def _l1_out_kernel(x_ref, mod_ref, g_ref, o_ref, wo_ref, rhi_ref, rlo_ref, x3_ref, h_ref, ei_ref, ew_ref, *, d):
    x3 = x_ref[...] + mod_ref[0:1, 2 * d:3 * d] * _dot(o_ref[...], wo_ref[...])
    x3_ref[...] = x3
    ms = jnp.mean(x3 * x3, axis=-1, keepdims=True)
    h = x3 * lax.rsqrt(ms + EPS) * (g_ref[...] * (1.0 + mod_ref[0:1, 4 * d:5 * d])) + mod_ref[0:1, 3 * d:4 * d]
    h_ref[...] = h.astype(BF16)
    hi = h.astype(BF16)
    lo = (h - hi.astype(F32)).astype(BF16)
    logits = _dot(hi, rhi_ref[...]) + (_dot(hi, rlo_ref[...]) + _dot(lo, rhi_ref[...]))
    lane_i = _lane(logits.shape)
    lane = lane_i.astype(F32)
    logits = jnp.where(lane_i < N_EXPERTS, logits, -jnp.inf)
    v1 = jnp.max(logits, axis=-1, keepdims=True)
    i1 = jnp.min(jnp.where(logits == v1, lane, float(LANES)), axis=-1, keepdims=True)
    rest = jnp.where(lane == i1, -jnp.inf, logits)
    v2 = jnp.max(rest, axis=-1, keepdims=True)
    i2 = jnp.min(jnp.where(rest == v2, lane, float(LANES)), axis=-1, keepdims=True)
    e2 = jnp.exp(v2 - v1)
    den = 1.0 + e2
    ei_ref[...] = jnp.where(lane_i == 0, i1, jnp.where(lane_i == 1, i2, 0.0)).astype(jnp.int32)
    ew_ref[...] = jnp.where(lane_i == 0, 1.0 / den, jnp.where(lane_i == 1, e2 / den, 0.0))


def _l1_out(xa, mod, g, o, wo, rhi, rlo, *, tm):
    n, d = o.shape[0], xa.shape[1]
    row = lambda i: (i, 0)
    return pl.pallas_call(
        functools.partial(_l1_out_kernel, d=d),
        grid=(n // tm,),
        in_specs=[pl.BlockSpec((tm, d), row), _resident(mod.shape), _resident(g.shape),
                  pl.BlockSpec((tm, o.shape[1]), row), _resident(wo.shape), _resident(rhi.shape),
                  _resident(rlo.shape)],
        out_specs=[pl.BlockSpec((tm, d), row), pl.BlockSpec((tm, d), row),
                   pl.BlockSpec((tm, LANES), row), pl.BlockSpec((tm, LANES), row)],
        out_shape=[jax.ShapeDtypeStruct((n, d), F32), jax.ShapeDtypeStruct((n, d), BF16),
                   jax.ShapeDtypeStruct((n, LANES), jnp.int32), jax.ShapeDtypeStruct((n, LANES), F32)],
        compiler_params=_cparams(("parallel",)),
    )(xa, mod, g, o, wo, rhi, rlo)


def _moe_kernel(te_ref, nv_ref, x_ref, w1_ref, w3_ref, w2_ref, y_ref, acc_ref, *, nf):
    i, f = pl.program_id(0), pl.program_id(1)

    @pl.when(f == 0)
    def _zero():
        acc_ref[...] = jnp.zeros_like(acc_ref)

    @pl.when(i < nv_ref[0])
    def _compute():
        x = x_ref[...]
        a = _dot(x, w1_ref[0])
        u = (_silu(a) * _dot(x, w3_ref[0])).astype(BF16)
        acc_ref[...] += _dot(u, w2_ref[0])

    @pl.when(f == nf - 1)
    def _store():
        y_ref[...] = acc_ref[...].astype(y_ref.dtype)


def _moe(tile_expert, n_valid, xs, w1, w3, w2, *, tm, tf):
    p, d = xs.shape
    fdim = w1.shape[2]
    nf = fdim // tf
    fi = lambda i, f, te, nv: jnp.where(i < nv[0], f, nf - 1)
    grid_spec = pltpu.PrefetchScalarGridSpec(
        num_scalar_prefetch=2,
        grid=(p // tm, nf),
        in_specs=[pl.BlockSpec((tm, d), lambda i, f, te, nv: (i, 0)),
                  pl.BlockSpec((1, d, tf), lambda i, f, te, nv: (te[i], 0, fi(i, f, te, nv))),
                  pl.BlockSpec((1, d, tf), lambda i, f, te, nv: (te[i], 0, fi(i, f, te, nv))),
                  pl.BlockSpec((1, tf, d), lambda i, f, te, nv: (te[i], fi(i, f, te, nv), 0))],
        out_specs=pl.BlockSpec((tm, d), lambda i, f, te, nv: (i, 0)),
        scratch_shapes=[pltpu.VMEM((tm, d), F32)],
    )
    return pl.pallas_call(
        functools.partial(_moe_kernel, nf=nf),
        grid_spec=grid_spec,
        out_shape=jax.ShapeDtypeStruct((p, d), BF16),
        compiler_params=_cparams(("arbitrary", "arbitrary")),
    )(tile_expert, n_valid, xs, w1, w3, w2)


def _combine_kernel(x_ref, mod_ref, ew_ref, ya_ref, yb_ref, o_ref, *, d):
    ew = ew_ref[...]
    y = ew[:, 0:1] * ya_ref[...].astype(F32) + ew[:, 1:2] * yb_ref[...].astype(F32)
    o_ref[...] = x_ref[...] + mod_ref[0:1, 5 * d:6 * d] * y


def _combine(x3, mod, ew, ya, yb, *, tm):
    n, d = x3.shape
    row = lambda i: (i, 0)
    return pl.pallas_call(
        functools.partial(_combine_kernel, d=d),
        grid=(n // tm,),
        in_specs=[pl.BlockSpec((tm, d), row), _resident(mod.shape), pl.BlockSpec((tm, LANES), row),
                  pl.BlockSpec((tm, d), row), pl.BlockSpec((tm, d), row)],
        out_specs=pl.BlockSpec((tm, d), row),
        out_shape=jax.ShapeDtypeStruct((n, d), F32),
        compiler_params=_cparams(("parallel",)),
    )(x3, mod, ew, ya, yb)


def _deinterleave(width):
    return np.concatenate([np.arange(0, width, 2), np.arange(1, width, 2)])


def _rope_tables(n_x, n_ctx, rot_dim, seg_start, seg_repeat):
    rows = n_x // GRID_W
    row = jnp.broadcast_to(jnp.arange(rows)[:, None], (rows, GRID_W)).reshape(n_x).astype(F32)
    col = jnp.broadcast_to(jnp.arange(GRID_W)[None, :], (rows, GRID_W)).reshape(n_x).astype(F32)
    axis_dim = rot_dim // 2
    inv_freq = ROPE_THETA ** (-jnp.arange(0, axis_dim, 2, dtype=F32) / axis_dim)
    ang = jnp.concatenate([row[:, None] * inv_freq, col[:, None] * inv_freq], axis=-1)
    cos, sin = jnp.cos(ang), jnp.sin(ang)
    cseg = jnp.concatenate([cos, cos], axis=-1)
    sseg = jnp.concatenate([-sin, sin], axis=-1)
    c = jnp.ones((n_x, LANES), F32)
    s = jnp.zeros((n_x, LANES), F32)
    for r in range(seg_repeat):
        lo = seg_start + r * rot_dim
        c = c.at[:, lo:lo + rot_dim].set(cseg)
        s = s.at[:, lo:lo + rot_dim].set(sseg)
    c = jnp.concatenate([c, jnp.ones((n_ctx, LANES), F32)], axis=0)
    s = jnp.concatenate([s, jnp.zeros((n_ctx, LANES), F32)], axis=0)
    return c, s


def _segment_matrix(bounds):
    m = np.zeros((LANES, LANES), np.float32)
    for lo, hi in bounds:
        m[lo:hi, lo:hi] = 1.0 / (hi - lo)
    return jnp.asarray(m, BF16)


def _token_tile(t):
    for tm in (640, 512, 256, 128):
        if t % tm == 0:
            return tm
    raise ValueError(f"token count {t} has no supported tile")


def kernel(x, c, ctx, c_ctx, l0_ada_w, l0_ada_b, l0_norm1_g, l0_norm2_g, l0_w_in, l0_ret_log_decay, l0_q_norm_g, l0_k_norm_g, l0_w_out, l0_ffn_w1, l0_ffn_w3, l0_ffn_w2, l1_ada_w, l1_ada_b, l1_norm1_g, l1_norm2_g, l1_w_in, l1_q_lora_g, l1_kv_lora_g, l1_w_uq, l1_w_ukv, l1_q_nope_g, l1_q_rope_g, l1_k_nope_g, l1_k_rope_g, l1_w_out, l1_router, l1_exp_w1, l1_exp_w3, l1_exp_w2):
    b, n_x, d = x.shape
    n_ctx = ctx.shape[1]
    assert b == 1 and n_x % 256 == 0 and n_ctx % 256 == 0 and n_x % GRID_W == 0
    t = n_x + n_ctx
    tm = _token_tile(t)
    tmx = _token_tile(n_x)
    xa = jnp.concatenate([x[0], ctx[0]], axis=0)
    row1 = lambda v: v.reshape(1, -1).astype(F32)

    cvec = jnp.zeros((8, d), F32).at[0].set(c[0]).at[1].set(c_ctx)
    mod0 = _ada(cvec, l0_ada_w, l0_ada_b)
    mod1 = _ada(cvec, l1_ada_w, l1_ada_b)

    rw = RET_HEADS * HEAD
    perm = _deinterleave(HEAD)
    cols = np.arange(l0_w_in.shape[1])
    for hd in range(GQA_HEADS + GQA_KV_HEADS):
        lo = 4 * rw + hd * HEAD
        cols[lo:lo + HEAD] = lo + perm
    colscale = np.ones((l0_w_in.shape[1],), np.float32)
    colscale[rw:2 * rw] = HEAD ** -0.5
    w_in0 = (l0_w_in[:, cols] * colscale).astype(BF16)
    seg64 = _segment_matrix([(0, HEAD), (HEAD, 2 * HEAD)])
    gq0 = row1(jnp.tile(l0_q_norm_g[perm], 2) * (HEAD ** -0.5 * LOG2E))
    gk0 = row1(jnp.tile(l0_k_norm_g[perm], 2))
    cos0, sin0 = _rope_tables(n_x, n_ctx, HEAD, 0, 2)

    rq, rk, rv, rg, gq, gkx, gvx = _l0_proj(xa, mod0, row1(l0_norm1_g), w_in0, seg64, gq0, gk0, cos0, sin0,
                                             tm=tm, n_x=n_x)
    o_f, o_b = _retention(l0_ret_log_decay.astype(F32), rq, rk, rv, n_x=n_x)

    gqa_maps = dict(q_maps=(lambda j: j, lambda j: j),
                    k_maps=(lambda j: 2 * (j // 2), lambda j: 2 * (j // 2) + 1),
                    v_maps=(lambda j: 2 * (j // 2), lambda j: 2 * (j // 2) + 1), n_pairs=GQA_HEADS // 2)
    bq = 512 if n_x % 512 == 0 else 256
    ao_x = _flash(gq, gkx, (gvx, gvx), n_q=n_x, q_row0=0, kv_row0=0, n_kv=t, bq=bq, bk=_kv_block(t), rs=32,
                  **gqa_maps)
    ao_c = _flash(gq, gkx, (gvx, gvx), n_q=n_ctx, q_row0=n_x, kv_row0=n_x, n_kv=n_ctx, bq=n_ctx,
                  bk=_kv_block(n_ctx), rs=32, **gqa_maps)
    ao = jnp.concatenate([ao_x, ao_c], axis=0)

    xa = _l0_out(xa, mod0, o_f, o_b, rg, ao, seg64, l0_w_out.astype(BF16), tm=tm, n_x=n_x)
    xa = _ffn(xa, mod0, row1(l0_norm2_g), l0_ffn_w1.astype(BF16), l0_ffn_w3.astype(BF16),
              l0_ffn_w2.astype(BF16), tm=tm, n_x=n_x)

    rperm = _deinterleave(MLA_ROPE)
    qk_w = MLA_NOPE + MLA_ROPE
    wuq = jnp.zeros((MLA_Q_RANK, MLA_HEADS * LANES), F32)
    wuk = jnp.zeros((MLA_KV_RANK, MLA_HEADS * LANES), F32)
    wuv = []
    for hd in range(MLA_HEADS):
        src = l1_w_uq[:, hd * qk_w:(hd + 1) * qk_w]
        wuq = wuq.at[:, hd * LANES:hd * LANES + MLA_NOPE].set(src[:, :MLA_NOPE])
        wuq = wuq.at[:, hd * LANES + MLA_NOPE:hd * LANES + qk_w].set(src[:, MLA_NOPE:][:, rperm])
        kvsrc = l1_w_ukv[:, hd * 2 * HEAD:(hd + 1) * 2 * HEAD]
        wuk = wuk.at[:, hd * LANES:hd * LANES + MLA_NOPE].set(kvsrc[:, :MLA_NOPE])
        wuv.append(kvsrc[:, MLA_NOPE:])
    wuv = jnp.concatenate(wuv, axis=1)
    wkr = jnp.zeros((d, LANES), F32).at[:, MLA_NOPE:qk_w].set(l1_w_in[:, MLA_Q_RANK + MLA_KV_RANK:][:, rperm])
    pad = jnp.zeros((LANES - qk_w,), F32)
    zn = jnp.zeros((MLA_NOPE,), F32)
    gq1 = row1(jnp.concatenate([l1_q_nope_g, l1_q_rope_g[rperm], pad]) * (qk_w ** -0.5 * LOG2E))
    gk1 = row1(jnp.concatenate([l1_k_nope_g, jnp.zeros((LANES - MLA_NOPE,), F32)]))
    gkr1 = row1(jnp.concatenate([zn, l1_k_rope_g[rperm], pad]))
    seg_mla = _segment_matrix([(0, MLA_NOPE), (MLA_NOPE, qk_w)])
    cos1, sin1 = _rope_tables(n_x, n_ctx, MLA_ROPE, MLA_NOPE, 1)

    mq, mk, mvlo, mvhi = _l1_proj(
        xa, mod1, row1(l1_norm1_g), l1_w_in[:, :MLA_Q_RANK].astype(BF16),
        l1_w_in[:, MLA_Q_RANK:MLA_Q_RANK + MLA_KV_RANK].astype(BF16), wkr.astype(BF16),
        row1(l1_q_lora_g), row1(l1_kv_lora_g), wuq.astype(BF16), wuk.astype(BF16), wuv.astype(BF16),
        seg_mla, gq1, gk1, gkr1, cos1, sin1, tm=tm, n_x=n_x)
    mo = _flash(mq, mk, (mvlo, mvhi), q_maps=(lambda j: 2 * j, lambda j: 2 * j + 1),
                k_maps=(lambda j: 2 * j, lambda j: 2 * j + 1), v_maps=(lambda j: j, lambda j: j),
                n_pairs=MLA_HEADS // 2, n_q=n_x, q_row0=0, kv_row0=0, n_kv=t, bq=bq, bk=_kv_block(t), rs=32)

    router = jnp.zeros((d, LANES), F32).at[:, :N_EXPERTS].set(l1_router)
    r_hi = router.astype(BF16)
    r_lo = (router - r_hi.astype(F32)).astype(BF16)
    x3, hmoe, ei, ew = _l1_out(xa, mod1, row1(l1_norm2_g), mo, l1_w_out.astype(BF16), r_hi, r_lo, tm=tmx)

    tme = 512 if n_x >= 4096 else 128
    e_flat = ei[:, :TOP_K].reshape(-1)
    onehot = (e_flat[:, None] == jnp.arange(N_EXPERTS)[None, :]).astype(jnp.int32)
    csum = jnp.cumsum(onehot, axis=0)
    rank = jnp.sum((csum - onehot) * onehot, axis=1)
    counts = csum[-1]
    padded = ((counts + tme - 1) // tme) * tme
    ends = jnp.cumsum(padded)
    pos = (ends - padded)[e_flat] + rank
    n_tiles = (TOP_K * n_x) // tme + N_EXPERTS
    p_rows = n_tiles * tme
    src = jnp.zeros((p_rows,), jnp.int32).at[pos].set(jnp.arange(TOP_K * n_x, dtype=jnp.int32) // TOP_K)
    n_valid = (ends[-1] // tme).astype(jnp.int32)
    tile_ids = jnp.minimum(jnp.arange(n_tiles, dtype=jnp.int32), n_valid - 1)
    tile_expert = jnp.sum((ends[None, :] <= (tile_ids * tme)[:, None]).astype(jnp.int32), axis=1)
    tile_expert = jnp.minimum(tile_expert, N_EXPERTS - 1)
    xs = jnp.take(hmoe, src, axis=0)
    fdim = l1_exp_w1.shape[2]
    tf = fdim // 2 if (fdim // 2) % LANES == 0 else fdim
    ys = _moe(tile_expert, n_valid.reshape(1), xs, l1_exp_w1.astype(BF16), l1_exp_w3.astype(BF16),
              l1_exp_w2.astype(BF16), tm=tme, tf=tf)
    pos2 = pos.reshape(n_x, TOP_K)
    out = _combine(x3, mod1, ew, jnp.take(ys, pos2[:, 0], axis=0), jnp.take(ys, pos2[:, 1], axis=0), tm=tmx)
    return out[None]
```

```python
import functools
import math

import numpy as np
import jax
import jax.numpy as jnp
from jax import lax
from jax.experimental import pallas as pl
from jax.experimental.pallas import tpu as pltpu

F32 = jnp.float32
BF16 = jnp.bfloat16

EPS = 1e-6
ROPE_THETA = 10000.0
GRID_W = 64
LANES = 128
HEAD = 64
RET_CHUNK = 128
RET_HEADS = 8
GQA_HEADS = 8
GQA_KV_HEADS = 2
MLA_HEADS = 8
MLA_Q_RANK = 384
MLA_KV_RANK = 256
MLA_NOPE = 64
MLA_ROPE = 32
N_EXPERTS = 8
TOP_K = 2
LOW_ONE = HEAD
HIGH_ONE = 0
LOG2E = math.log2(math.e)
VMEM_LIMIT = 56 * 1024 * 1024


def _cparams(sem, vmem=VMEM_LIMIT):
    return pltpu.CompilerParams(dimension_semantics=sem, vmem_limit_bytes=vmem)


def _resident(shape):
    nd = len(shape)
    return pl.BlockSpec(shape, lambda *_: (0,) * nd, pipeline_mode=pl.Buffered(1))


def _dot(a, b):
    return jnp.dot(a, b, preferred_element_type=F32)


def _dot_nt(a, b):
    return lax.dot_general(a, b, (((1,), (1,)), ((), ())), preferred_element_type=F32)


def _seg_mean(v, seg):
    hi = v.astype(BF16)
    lo = (v - hi.astype(F32)).astype(BF16)
    return _dot(hi, seg) + _dot(lo, seg)


def _silu(x):
    return x * jax.nn.sigmoid(x)


def _modulated(x, mod_ref, g_ref, which, tile, tm, n_x, d):
    ms = jnp.mean(x * x, axis=-1, keepdims=True)
    xn = x * lax.rsqrt(ms + EPS)
    g = g_ref[...]
    sh, sc = 3 * which, 3 * which + 1
    a_x = g * (1.0 + mod_ref[0:1, sc * d:(sc + 1) * d])
    a_c = g * (1.0 + mod_ref[1:2, sc * d:(sc + 1) * d])
    b_x = mod_ref[0:1, sh * d:(sh + 1) * d]
    b_c = mod_ref[1:2, sh * d:(sh + 1) * d]
    row = tile * tm + lax.broadcasted_iota(jnp.int32, (tm, 1), 0)
    is_ctx = row >= n_x
    return xn * jnp.where(is_ctx, a_c, a_x) + jnp.where(is_ctx, b_c, b_x)


def _row_gate(mod_ref, idx, tile, tm, n_x, d):
    row = tile * tm + lax.broadcasted_iota(jnp.int32, (tm, 1), 0)
    return jnp.where(row >= n_x, mod_ref[1:2, idx * d:(idx + 1) * d], mod_ref[0:1, idx * d:(idx + 1) * d])


def _lane(shape):
    return lax.broadcasted_iota(jnp.int32, shape, len(shape) - 1)


def _ada_kernel(c_ref, w_ref, b_ref, o_ref):
    c = c_ref[...]
    o_ref[...] = jnp.dot(_silu(c), w_ref[...], preferred_element_type=F32,
                         precision=lax.Precision.HIGHEST) + b_ref[...]


def _ada(cvec8, w, b):
    d, n = w.shape
    tn = n // 4
    return pl.pallas_call(
        _ada_kernel,
        grid=(n // tn,),
        in_specs=[pl.BlockSpec((8, d), lambda j: (0, 0)),
                  pl.BlockSpec((d, tn), lambda j: (0, j)),
                  pl.BlockSpec((1, tn), lambda j: (0, j))],
        out_specs=pl.BlockSpec((8, tn), lambda j: (0, j)),
        out_shape=jax.ShapeDtypeStruct((8, n), F32),
        compiler_params=_cparams(("arbitrary",)),
    )(cvec8, w, b.reshape(1, n))


def _rope128(v, c, s, half):
    lane = _lane(v.shape)
    swapped = jnp.where(lane % (2 * half) < half, pltpu.roll(v, LANES - half, 1), pltpu.roll(v, half, 1))
    return v * c + swapped * s


def _l0_proj_kernel(x_ref, mod_ref, g_ref, w_ref, seg_ref, gq_ref, gk_ref, c_ref, s_ref,
                    rq_ref, rk_ref, rv_ref, rg_ref, q_ref, k_ref, v_ref, *, tm, n_x, d):
    i = pl.program_id(0)
    h = _modulated(x_ref[...], mod_ref, g_ref, 0, i, tm, n_x, d).astype(BF16)
    rw = RET_HEADS * HEAD
    for idx, ref in enumerate((rq_ref, rk_ref, rv_ref, rg_ref)):
        ref[...] = _dot(h, w_ref[:, idx * rw:(idx + 1) * rw]).astype(BF16)
    seg = seg_ref[...]
    cos, sin = c_ref[...], s_ref[...]
    base = 4 * rw
    qw = GQA_HEADS * HEAD
    qa = _dot(h, w_ref[:, base:base + qw])
    for g in range(qw // LANES):
        v = qa[:, g * LANES:(g + 1) * LANES]
        vn = v * lax.rsqrt(_seg_mean(v * v, seg) + EPS) * gq_ref[...]
        q_ref[:, g * LANES:(g + 1) * LANES] = _rope128(vn, cos, sin, HEAD // 2).astype(BF16)
    kv = _dot(h, w_ref[:, base + qw:base + qw + 2 * LANES])
    kk = kv[:, :LANES]
    kk = kk * lax.rsqrt(_seg_mean(kk * kk, seg) + EPS) * gk_ref[...]
    kk = _rope128(kk, cos, sin, HEAD // 2)
    vv = kv[:, LANES:]
    lane = _lane(kk.shape)
    low = lane < HEAD
    for src, ref, one in ((kk, k_ref, 0.0), (vv, v_ref, 1.0)):
        sw = pltpu.roll(src, HEAD, 1)
        lo_fill = jnp.where(lane == LOW_ONE, one, 0.0)
        hi_fill = jnp.where(lane == HIGH_ONE, one, 0.0)
        ref[:, 0 * LANES:1 * LANES] = jnp.where(low, src, lo_fill).astype(BF16)
        ref[:, 1 * LANES:2 * LANES] = jnp.where(low, hi_fill, sw).astype(BF16)
        ref[:, 2 * LANES:3 * LANES] = jnp.where(low, sw, lo_fill).astype(BF16)
        ref[:, 3 * LANES:4 * LANES] = jnp.where(low, hi_fill, src).astype(BF16)


def _l0_proj(xa, mod, g, w, seg, gq, gk, cos, sin, *, tm, n_x):
    t, d = xa.shape
    rw = RET_HEADS * HEAD
    row = lambda i: (i, 0)
    outs = [jax.ShapeDtypeStruct((t, rw), BF16)] * 7
    return pl.pallas_call(
        functools.partial(_l0_proj_kernel, tm=tm, n_x=n_x, d=d),
        grid=(t // tm,),
        in_specs=[pl.BlockSpec((tm, d), row), _resident(mod.shape), _resident(g.shape), _resident(w.shape),
                  _resident(seg.shape), _resident(gq.shape), _resident(gk.shape),
                  pl.BlockSpec((tm, LANES), row), pl.BlockSpec((tm, LANES), row)],
        out_specs=[pl.BlockSpec((tm, rw), row)] * 7,
        out_shape=outs,
        compiler_params=_cparams(("parallel",)),
    )(xa, mod, g, w, seg, gq, gk, cos, sin)


def _retention_kernel(lg_ref, qf_ref, kf_ref, vf_ref, qb_ref, kb_ref, vb_ref, of_ref, ob_ref,
                      state_ref, decay_ref, xi_ref, zeta_ref, gl_ref):
    c = RET_CHUNK
    npairs = RET_HEADS * HEAD // LANES
    step = pl.program_id(0)

    @pl.when(step == 0)
    def _init():
        state_ref[...] = jnp.zeros_like(state_ref)
        ci = lax.broadcasted_iota(jnp.int32, (c, c), 0).astype(F32)
        mi = lax.broadcasted_iota(jnp.int32, (c, c), 1).astype(F32)
        pos = lax.broadcasted_iota(jnp.int32, (c, RET_HEADS * HEAD), 0).astype(F32)
        lane_head = _lane((1, RET_HEADS * HEAD)) // HEAD
        for dr in range(2):
            lgv = jnp.zeros((1, RET_HEADS * HEAD), F32)
            for hd in range(RET_HEADS):
                lg = lg_ref[dr, hd]
                rel = (ci - mi) if dr == 0 else (mi - ci)
                decay_ref[dr, hd] = jnp.where(rel >= 0, jnp.exp(jnp.maximum(rel, 0.0) * lg), 0.0)
                lgv = jnp.where(lane_head == hd, lg, lgv)
            p = pos if dr == 0 else (c - 1.0 - pos)
            xi_ref[dr] = jnp.exp((p + 1.0) * lgv)
            zeta_ref[dr] = jnp.exp((c - 1.0 - p) * lgv)
            gl_ref[dr] = jnp.exp(float(c) * lgv)

    low = _lane((c, LANES)) < HEAD
    r_i = lax.broadcasted_iota(jnp.int32, (LANES, LANES), 0) // HEAD
    c_i = lax.broadcasted_iota(jnp.int32, (LANES, LANES), 1) // HEAD
    blockdiag = r_i == c_i
    for dr, (q_ref, k_ref, v_ref, o_ref) in enumerate(((qf_ref, kf_ref, vf_ref, of_ref),
                                                       (qb_ref, kb_ref, vb_ref, ob_ref))):
        for j in range(npairs):
            sl = slice(j * LANES, (j + 1) * LANES)
            q, k, v = q_ref[:, sl], k_ref[:, sl], v_ref[:, sl]
            zero = jnp.zeros_like(q)
            s0 = _dot_nt(jnp.where(low, q, zero), k) * decay_ref[dr, 2 * j]
            s1 = _dot_nt(jnp.where(low, zero, q), k) * decay_ref[dr, 2 * j + 1]
            o = _dot(s0.astype(BF16), jnp.where(low, v, zero)) + _dot(s1.astype(BF16), jnp.where(low, zero, v))
            st = state_ref[dr, j]
            qx = (q.astype(F32) * xi_ref[dr, :, sl]).astype(BF16)
            o = o + _dot(qx, st.astype(BF16))
            o_ref[:, sl] = o.astype(BF16)
            kz = (k.astype(F32) * zeta_ref[dr, :, sl]).T.astype(BF16)
            u = _dot(kz, v)
            state_ref[dr, j] = st * gl_ref[dr, :, sl] + jnp.where(blockdiag, u, 0.0)


def _retention(lg, rq, rk, rv, *, n_x):
    t, w = rq.shape
    c = RET_CHUNK
    nc, ncx = t // c, n_x // c
    fwd = lambda i: ((i + ncx) % nc, 0)
    bwd = lambda i: (nc - 1 - i, 0)
    blk = lambda m: pl.BlockSpec((c, w), m)
    npairs = w // LANES
    return pl.pallas_call(
        _retention_kernel,
        grid=(nc,),
        in_specs=[pl.BlockSpec(memory_space=pltpu.SMEM)] + [blk(fwd)] * 3 + [blk(bwd)] * 3,
        out_specs=[blk(fwd), blk(bwd)],
        out_shape=[jax.ShapeDtypeStruct((t, w), BF16)] * 2,
        scratch_shapes=[pltpu.VMEM((2, npairs, LANES, LANES), F32),
                        pltpu.VMEM((2, RET_HEADS, c, c), F32),
                        pltpu.VMEM((2, c, w), F32), pltpu.VMEM((2, c, w), F32),
                        pltpu.VMEM((2, 1, w), F32)],
        compiler_params=_cparams(("arbitrary",)),
    )(lg, rq, rk, rv, rq, rk, rv)


def _flash_kernel(q0_ref, q1_ref, k0_ref, k1_ref, v0_ref, v1_ref, o_ref,
                  s_ref, p_ref, a_ref, m_ref, acc_ref, *, bk, nkv, rs):
    bq = q0_ref.shape[0]
    q_refs, k_refs, v_refs = (q0_ref, q1_ref), (k0_ref, k1_ref), (v0_ref, v1_ref)
    m_ref[...] = jnp.full(m_ref.shape, -jnp.inf, F32)
    acc_ref[...] = jnp.zeros(acc_ref.shape, F32)

    def keys(t):
        return pl.ds(t * bk if isinstance(t, int) else pl.multiple_of(t * bk, bk), bk)

    def scores(t, slot):
        for h in range(2):
            s_ref[slot, h] = _dot_nt(q_refs[h][...], k_refs[h][keys(t), :])

    def softmax(slot):
        col = lambda c: slice(c * LANES, (c + 1) * LANES)
        for r in range(bq // rs):
            rows = slice(r * rs, (r + 1) * rs)
            for h in range(2):
                mx = s_ref[slot, h, rows, col(0)]
                for c in range(1, bk // LANES):
                    mx = jnp.maximum(mx, s_ref[slot, h, rows, col(c)])
                m_old = m_ref[h, rows, :]
                m_new = jnp.maximum(m_old, jnp.max(mx, axis=-1, keepdims=True))
                a_ref[slot, h, rows, :] = jnp.exp2(m_old - m_new)
                m_ref[h, rows, :] = m_new
                for c in range(bk // LANES):
                    p_ref[slot, h, rows, col(c)] = jnp.exp2(s_ref[slot, h, rows, col(c)] - m_new).astype(BF16)

    def values(t, slot):
        for h in range(2):
            acc_ref[h] = acc_ref[h] * a_ref[slot, h] + _dot(p_ref[slot, h], v_refs[h][keys(t), :])

    scores(0, 0)

    def body(i, carry):
        t = 2 * i
        scores(t + 1, 1)
        softmax(0)
        values(t, 0)
        scores(t + 2, 0)
        softmax(1)
        values(t + 1, 1)
        return carry

    n_loop = (nkv - 1) // 2
    lax.fori_loop(0, n_loop, body, 0)
    last = 2 * n_loop
    if last + 1 < nkv:
        scores(last + 1, 1)
    softmax(0)
    values(last, 0)
    if last + 1 < nkv:
        softmax(1)
        values(last + 1, 1)
    low = _lane((bq, LANES)) < HEAD
    acc0, acc1 = acc_ref[0], acc_ref[1]
    out = jnp.where(low, acc0 / acc0[:, LOW_ONE:LOW_ONE + 1], acc1 / acc1[:, HIGH_ONE:HIGH_ONE + 1])
    o_ref[...] = out.astype(o_ref.dtype)


def _flash(q, kmat, vmat, *, q_maps, k_maps, v_maps, n_q, q_row0, kv_row0, n_kv, n_pairs, bq, bk, rs):
    assert q_row0 % bq == 0 and n_q % bq == 0 and n_kv % bk == 0 and kv_row0 % n_kv == 0 and bq % rs == 0
    qb0, kb0 = q_row0 // bq, kv_row0 // n_kv
    qspec = lambda m: pl.BlockSpec((bq, LANES), lambda j, i: (i + qb0, m(j)))
    kspec = lambda m: pl.BlockSpec((n_kv, LANES), lambda j, i: (kb0, m(j)), pipeline_mode=pl.Buffered(1))
    return pl.pallas_call(
        functools.partial(_flash_kernel, bk=bk, nkv=n_kv // bk, rs=rs),
        grid=(n_pairs, n_q // bq),
        in_specs=[qspec(q_maps[0]), qspec(q_maps[1]), kspec(k_maps[0]), kspec(k_maps[1]),
                  kspec(v_maps[0]), kspec(v_maps[1])],
        out_specs=pl.BlockSpec((bq, LANES), lambda j, i: (i, j)),
        out_shape=jax.ShapeDtypeStruct((n_q, n_pairs * LANES), BF16),
        scratch_shapes=[pltpu.VMEM((2, 2, bq, bk), F32), pltpu.VMEM((2, 2, bq, bk), BF16),
                        pltpu.VMEM((2, 2, bq, LANES), F32), pltpu.VMEM((2, bq, LANES), F32),
                        pltpu.VMEM((2, bq, LANES), F32)],
        compiler_params=_cparams(("parallel", "parallel")),
    )(q, q, kmat, kmat, vmat[0], vmat[1])


def _kv_block(n_kv):
    for bk in (1280, 1024, 512, 256):
        if n_kv % bk == 0:
            return bk
    raise ValueError(f"key count {n_kv} has no supported block")


def _l0_out_kernel(x_ref, mod_ref, of_ref, ob_ref, rg_ref, ao_ref, seg_ref, wo_ref, o_ref, *, tm, n_x, d):
    i = pl.program_id(0)
    seg = seg_ref[...]
    rw = RET_HEADS * HEAD
    acc = _dot(ao_ref[...], wo_ref[rw:, :])
    for g in range(rw // LANES):
        sl = slice(g * LANES, (g + 1) * LANES)
        o = of_ref[:, sl].astype(F32) + ob_ref[:, sl].astype(F32)
        dv = o - _seg_mean(o, seg)
        nrm = dv * lax.rsqrt(_seg_mean(dv * dv, seg) + EPS)
        ra = (nrm * _silu(rg_ref[:, sl].astype(F32))).astype(BF16)
        acc = acc + _dot(ra, wo_ref[g * LANES:(g + 1) * LANES, :])
    o_ref[...] = x_ref[...] + _row_gate(mod_ref, 2, i, tm, n_x, d) * acc


def _l0_out(xa, mod, o_f, o_b, rg, ao, seg, wo, *, tm, n_x):
    t, d = xa.shape
    rw = o_f.shape[1]
    row = lambda i: (i, 0)
    return pl.pallas_call(
        functools.partial(_l0_out_kernel, tm=tm, n_x=n_x, d=d),
        grid=(t // tm,),
        in_specs=[pl.BlockSpec((tm, d), row), _resident(mod.shape)] + [pl.BlockSpec((tm, rw), row)] * 4
                 + [_resident(seg.shape), _resident(wo.shape)],
        out_specs=pl.BlockSpec((tm, d), row),
        out_shape=jax.ShapeDtypeStruct((t, d), F32),
        compiler_params=_cparams(("parallel",)),
    )(xa, mod, o_f, o_b, rg, ao, seg, wo)


def _ffn_kernel(x_ref, mod_ref, g_ref, w1_ref, w3_ref, w2_ref, o_ref, *, tm, n_x, d):
    i = pl.program_id(0)
    x = x_ref[...]
    h = _modulated(x, mod_ref, g_ref, 1, i, tm, n_x, d).astype(BF16)
    a = _dot(h, w1_ref[...])
    u = (_silu(a) * _dot(h, w3_ref[...])).astype(BF16)
    o_ref[...] = x + _row_gate(mod_ref, 5, i, tm, n_x, d) * _dot(u, w2_ref[...])


def _ffn(xa, mod, g, w1, w3, w2, *, tm, n_x):
    t, d = xa.shape
    row = lambda i: (i, 0)
    return pl.pallas_call(
        functools.partial(_ffn_kernel, tm=tm, n_x=n_x, d=d),
        grid=(t // tm,),
        in_specs=[pl.BlockSpec((tm, d), row), _resident(mod.shape), _resident(g.shape),
                  _resident(w1.shape), _resident(w3.shape), _resident(w2.shape)],
        out_specs=pl.BlockSpec((tm, d), row),
        out_shape=jax.ShapeDtypeStruct((t, d), F32),
        compiler_params=_cparams(("parallel",)),
    )(xa, mod, g, w1, w3, w2)


def _l1_proj_kernel(x_ref, mod_ref, g_ref, wq_ref, wkv_ref, wkr_ref, gql_ref, gkvl_ref, wuq_ref, wuk_ref,
                    wuv_ref, seg_ref, gq_ref, gk_ref, gkr_ref, c_ref, s_ref,
                    q_ref, k_ref, vlo_ref, vhi_ref, *, tm, n_x, d):
    i = pl.program_id(0)
    h = _modulated(x_ref[...], mod_ref, g_ref, 0, i, tm, n_x, d).astype(BF16)
    seg = seg_ref[...]
    cos, sin = c_ref[...], s_ref[...]

    def lora_norm(v, g):
        return (v * lax.rsqrt(jnp.mean(v * v, axis=-1, keepdims=True) + EPS) * g).astype(BF16)

    cq = lora_norm(_dot(h, wq_ref[...]), gql_ref[...])
    ckv = lora_norm(_dot(h, wkv_ref[...]), gkvl_ref[...])
    kr = _dot(h, wkr_ref[...])
    kr = kr * lax.rsqrt(_seg_mean(kr * kr, seg) + EPS) * gkr_ref[...]
    kr = _rope128(kr, cos, sin, MLA_ROPE // 2)
    qa = _dot(cq, wuq_ref[...])
    ka = _dot(ckv, wuk_ref[...])
    for hd in range(MLA_HEADS):
        sl = slice(hd * LANES, (hd + 1) * LANES)
        v = qa[:, sl]
        vn = v * lax.rsqrt(_seg_mean(v * v, seg) + EPS) * gq_ref[...]
        q_ref[:, sl] = _rope128(vn, cos, sin, MLA_ROPE // 2).astype(BF16)
        v = ka[:, sl]
        k_ref[:, sl] = (v * lax.rsqrt(_seg_mean(v * v, seg) + EPS) * gk_ref[...] + kr).astype(BF16)
    va = _dot(ckv, wuv_ref[...])
    lane = _lane(va.shape) % LANES
    low = lane < HEAD
    vlo_ref[...] = jnp.where(low, va, jnp.where(lane == LOW_ONE, 1.0, 0.0)).astype(BF16)
    vhi_ref[...] = jnp.where(low, jnp.where(lane == HIGH_ONE, 1.0, 0.0), va).astype(BF16)


def _l1_proj(xa, mod, g, wq, wkv, wkr, gql, gkvl, wuq, wuk, wuv, seg, gq, gk, gkr, cos, sin, *, tm, n_x):
    t, d = xa.shape
    row = lambda i: (i, 0)
    hw = MLA_HEADS * LANES
    vw = MLA_HEADS * HEAD
    consts = (mod, g, wq, wkv, wkr, gql, gkvl, wuq, wuk, wuv, seg, gq, gk, gkr)
    return pl.pallas_call(
        functools.partial(_l1_proj_kernel, tm=tm, n_x=n_x, d=d),
        grid=(t // tm,),
        in_specs=[pl.BlockSpec((tm, d), row)] + [_resident(a.shape) for a in consts]
                 + [pl.BlockSpec((tm, LANES), row)] * 2,
        out_specs=[pl.BlockSpec((tm, hw), row), pl.BlockSpec((tm, hw), row),
                   pl.BlockSpec((tm, vw), row), pl.BlockSpec((tm, vw), row)],
        out_shape=[jax.ShapeDtypeStruct((t, hw), BF16), jax.ShapeDtypeStruct((t, hw), BF16),
                   jax.ShapeDtypeStruct((t, vw), BF16), jax.ShapeDtypeStruct((t, vw), BF16)],
        compiler_params=_cparams(("parallel",)),
    )(xa, *consts, cos, sin)


def _l1_out_kernel(x_ref, mod_ref, g_ref, o_ref, wo_ref, rhi_ref, rlo_ref, x3_ref, h_ref, ei_ref, ew_ref, *, d):
    x3 = x_ref[...] + mod_ref[0:1, 2 * d:3 * d] * _dot(o_ref[...], wo_ref[...])
    x3_ref[...] = x3
    ms = jnp.mean(x3 * x3, axis=-1, keepdims=True)
    h = x3 * lax.rsqrt(ms + EPS) * (g_ref[...] * (1.0 + mod_ref[0:1, 4 * d:5 * d])) + mod_ref[0:1, 3 * d:4 * d]
    h_ref[...] = h.astype(BF16)
    hi = h.astype(BF16)
    lo = (h - hi.astype(F32)).astype(BF16)
    logits = _dot(hi, rhi_ref[...]) + (_dot(hi, rlo_ref[...]) + _dot(lo, rhi_ref[...]))
    lane_i = _lane(logits.shape)
    lane = lane_i.astype(F32)
    logits = jnp.where(lane_i < N_EXPERTS, logits, -jnp.inf)
    v1 = jnp.max(logits, axis=-1, keepdims=True)
    i1 = jnp.min(jnp.where(logits == v1, lane, float(LANES)), axis=-1, keepdims=True)
    rest = jnp.where(lane == i1, -jnp.inf, logits)
    v2 = jnp.max(rest, axis=-1, keepdims=True)
    i2 = jnp.min(jnp.where(rest == v2, lane, float(LANES)), axis=-1, keepdims=True)
    e2 = jnp.exp(v2 - v1)
    den = 1.0 + e2
    ei_ref[...] = jnp.where(lane_i == 0, i1, jnp.where(lane_i == 1, i2, 0.0)).astype(jnp.int32)
    ew_ref[...] = jnp.where(lane_i == 0, 1.0 / den, jnp.where(lane_i == 1, e2 / den, 0.0))


def _l1_out(xa, mod, g, o, wo, rhi, rlo, *, tm):
    n, d = o.shape[0], xa.shape[1]
    row = lambda i: (i, 0)
    return pl.pallas_call(
        functools.partial(_l1_out_kernel, d=d),
        grid=(n // tm,),
        in_specs=[pl.BlockSpec((tm, d), row), _resident(mod.shape), _resident(g.shape),
                  pl.BlockSpec((tm, o.shape[1]), row), _resident(wo.shape), _resident(rhi.shape),
                  _resident(rlo.shape)],
        out_specs=[pl.BlockSpec((tm, d), row), pl.BlockSpec((tm, d), row),
                   pl.BlockSpec((tm, LANES), row), pl.BlockSpec((tm, LANES), row)],
        out_shape=[jax.ShapeDtypeStruct((n, d), F32), jax.ShapeDtypeStruct((n, d), BF16),
                   jax.ShapeDtypeStruct((n, LANES), jnp.int32), jax.ShapeDtypeStruct((n, LANES), F32)],
        compiler_params=_cparams(("parallel",)),
    )(xa, mod, g, o, wo, rhi, rlo)


def _moe_kernel(te_ref, nv_ref, x_ref, w1_ref, w3_ref, w2_ref, y_ref, acc_ref, *, nf):
    i, f = pl.program_id(0), pl.program_id(1)

    @pl.when(f == 0)
    def _zero():
        acc_ref[...] = jnp.zeros_like(acc_ref)

    @pl.when(i < nv_ref[0])
    def _compute():
        x = x_ref[...]
        a = _dot(x, w1_ref[0])
        u = (_silu(a) * _dot(x, w3_ref[0])).astype(BF16)
        acc_ref[...] += _dot(u, w2_ref[0])

    @pl.when(f == nf - 1)
    def _store():
        y_ref[...] = acc_ref[...].astype(y_ref.dtype)


def _moe(tile_expert, n_valid, xs, w1, w3, w2, *, tm, tf):
    p, d = xs.shape
    fdim = w1.shape[2]
    nf = fdim // tf
    fi = lambda i, f, te, nv: jnp.where(i < nv[0], f, nf - 1)
    grid_spec = pltpu.PrefetchScalarGridSpec(
        num_scalar_prefetch=2,
        grid=(p // tm, nf),
        in_specs=[pl.BlockSpec((tm, d), lambda i, f, te, nv: (i, 0)),
                  pl.BlockSpec((1, d, tf), lambda i, f, te, nv: (te[i], 0, fi(i, f, te, nv))),
                  pl.BlockSpec((1, d, tf), lambda i, f, te, nv: (te[i], 0, fi(i, f, te, nv))),
                  pl.BlockSpec((1, tf, d), lambda i, f, te, nv: (te[i], fi(i, f, te, nv), 0))],
        out_specs=pl.BlockSpec((tm, d), lambda i, f, te, nv: (i, 0)),
        scratch_shapes=[pltpu.VMEM((tm, d), F32)],
    )
    return pl.pallas_call(
        functools.partial(_moe_kernel, nf=nf),
        grid_spec=grid_spec,
        out_shape=jax.ShapeDtypeStruct((p, d), BF16),
        compiler_params=_cparams(("arbitrary", "arbitrary")),
    )(tile_expert, n_valid, xs, w1, w3, w2)


def _combine_kernel(x_ref, mod_ref, ew_ref, ya_ref, yb_ref, o_ref, *, d):
    ew = ew_ref[...]
    y = ew[:, 0:1] * ya_ref[...].astype(F32) + ew[:, 1:2] * yb_ref[...].astype(F32)
    o_ref[...] = x_ref[...] + mod_ref[0:1, 5 * d:6 * d] * y


def _combine(x3, mod, ew, ya, yb, *, tm):
    n, d = x3.shape
    row = lambda i: (i, 0)
    return pl.pallas_call(
        functools.partial(_combine_kernel, d=d),
        grid=(n // tm,),
        in_specs=[pl.BlockSpec((tm, d), row), _resident(mod.shape), pl.BlockSpec((tm, LANES), row),
                  pl.BlockSpec((tm, d), row), pl.BlockSpec((tm, d), row)],
        out_specs=pl.BlockSpec((tm, d), row),
        out_shape=jax.ShapeDtypeStruct((n, d), F32),
        compiler_params=_cparams(("parallel",)),
    )(x3, mod, ew, ya, yb)


def _deinterleave(width):
    return np.concatenate([np.arange(0, width, 2), np.arange(1, width, 2)])


def _rope_tables(n_x, n_ctx, rot_dim, seg_start, seg_repeat):
    rows = n_x // GRID_W
    row = jnp.broadcast_to(jnp.arange(rows)[:, None], (rows, GRID_W)).reshape(n_x).astype(F32)
    col = jnp.broadcast_to(jnp.arange(GRID_W)[None, :], (rows, GRID_W)).reshape(n_x).astype(F32)
    axis_dim = rot_dim // 2
    inv_freq = ROPE_THETA ** (-jnp.arange(0, axis_dim, 2, dtype=F32) / axis_dim)
    ang = jnp.concatenate([row[:, None] * inv_freq, col[:, None] * inv_freq], axis=-1)
    cos, sin = jnp.cos(ang), jnp.sin(ang)
    cseg = jnp.concatenate([cos, cos], axis=-1)
    sseg = jnp.concatenate([-sin, sin], axis=-1)
    c = jnp.ones((n_x, LANES), F32)
    s = jnp.zeros((n_x, LANES), F32)
    for r in range(seg_repeat):
        lo = seg_start + r * rot_dim
        c = c.at[:, lo:lo + rot_dim].set(cseg)
        s = s.at[:, lo:lo + rot_dim].set(sseg)
    c = jnp.concatenate([c, jnp.ones((n_ctx, LANES), F32)], axis=0)
    s = jnp.concatenate([s, jnp.zeros((n_ctx, LANES), F32)], axis=0)
    return c, s


def _segment_matrix(bounds):
    m = np.zeros((LANES, LANES), np.float32)
    for lo, hi in bounds:
        m[lo:hi, lo:hi] = 1.0 / (hi - lo)
    return jnp.asarray(m, BF16)


def _token_tile(t):
    for tm in (640, 512, 256, 128):
        if t % tm == 0:
            return tm
    raise ValueError(f"token count {t} has no supported tile")


def kernel(x, c, ctx, c_ctx, l0_ada_w, l0_ada_b, l0_norm1_g, l0_norm2_g, l0_w_in, l0_ret_log_decay, l0_q_norm_g, l0_k_norm_g, l0_w_out, l0_ffn_w1, l0_ffn_w3, l0_ffn_w2, l1_ada_w, l1_ada_b, l1_norm1_g, l1_norm2_g, l1_w_in, l1_q_lora_g, l1_kv_lora_g, l1_w_uq, l1_w_ukv, l1_q_nope_g, l1_q_rope_g, l1_k_nope_g, l1_k_rope_g, l1_w_out, l1_router, l1_exp_w1, l1_exp_w3, l1_exp_w2):
    b, n_x, d = x.shape
    n_ctx = ctx.shape[1]
    assert b == 1 and n_x % 256 == 0 and n_ctx % 256 == 0 and n_x % GRID_W == 0
    t = n_x + n_ctx
    tm = _token_tile(t)
    tmx = _token_tile(n_x)
    xa = jnp.concatenate([x[0], ctx[0]], axis=0)
    row1 = lambda v: v.reshape(1, -1).astype(F32)

    cvec = jnp.zeros((8, d), F32).at[0].set(c[0]).at[1].set(c_ctx)
    mod0 = _ada(cvec, l0_ada_w, l0_ada_b)
    mod1 = _ada(cvec, l1_ada_w, l1_ada_b)

    rw = RET_HEADS * HEAD
    perm = _deinterleave(HEAD)
    cols = np.arange(l0_w_in.shape[1])
    for hd in range(GQA_HEADS + GQA_KV_HEADS):
        lo = 4 * rw + hd * HEAD
        cols[lo:lo + HEAD] = lo + perm
    colscale = np.ones((l0_w_in.shape[1],), np.float32)
    colscale[rw:2 * rw] = HEAD ** -0.5
    w_in0 = (l0_w_in[:, cols] * colscale).astype(BF16)
    seg64 = _segment_matrix([(0, HEAD), (HEAD, 2 * HEAD)])
    gq0 = row1(jnp.tile(l0_q_norm_g[perm], 2) * (HEAD ** -0.5 * LOG2E))
    gk0 = row1(jnp.tile(l0_k_norm_g[perm], 2))
    cos0, sin0 = _rope_tables(n_x, n_ctx, HEAD, 0, 2)

    rq, rk, rv, rg, gq, gkx, gvx = _l0_proj(xa, mod0, row1(l0_norm1_g), w_in0, seg64, gq0, gk0, cos0, sin0,
                                             tm=tm, n_x=n_x)
    o_f, o_b = _retention(l0_ret_log_decay.astype(F32), rq, rk, rv, n_x=n_x)

    gqa_maps = dict(q_maps=(lambda j: j, lambda j: j),
                    k_maps=(lambda j: 2 * (j // 2), lambda j: 2 * (j // 2) + 1),
                    v_maps=(lambda j: 2 * (j // 2), lambda j: 2 * (j // 2) + 1), n_pairs=GQA_HEADS // 2)
    bq = 512 if n_x % 512 == 0 else 256
    ao_x = _flash(gq, gkx, (gvx, gvx), n_q=n_x, q_row0=0, kv_row0=0, n_kv=t, bq=bq, bk=_kv_block(t), rs=32,
                  **gqa_maps)
    ao_c = _flash(gq, gkx, (gvx, gvx), n_q=n_ctx, q_row0=n_x, kv_row0=n_x, n_kv=n_ctx, bq=n_ctx,
                  bk=_kv_block(n_ctx), rs=32, **gqa_maps)
    ao = jnp.concatenate([ao_x, ao_c], axis=0)

    xa = _l0_out(xa, mod0, o_f, o_b, rg, ao, seg64, l0_w_out.astype(BF16), tm=tm, n_x=n_x)
    xa = _ffn(xa, mod0, row1(l0_norm2_g), l0_ffn_w1.astype(BF16), l0_ffn_w3.astype(BF16),
              l0_ffn_w2.astype(BF16), tm=tm, n_x=n_x)

    rperm = _deinterleave(MLA_ROPE)
    qk_w = MLA_NOPE + MLA_ROPE
    wuq = jnp.zeros((MLA_Q_RANK, MLA_HEADS * LANES), F32)
    wuk = jnp.zeros((MLA_KV_RANK, MLA_HEADS * LANES), F32)
    wuv = []
    for hd in range(MLA_HEADS):
        src = l1_w_uq[:, hd * qk_w:(hd + 1) * qk_w]
        wuq = wuq.at[:, hd * LANES:hd * LANES + MLA_NOPE].set(src[:, :MLA_NOPE])
        wuq = wuq.at[:, hd * LANES + MLA_NOPE:hd * LANES + qk_w].set(src[:, MLA_NOPE:][:, rperm])
        kvsrc = l1_w_ukv[:, hd * 2 * HEAD:(hd + 1) * 2 * HEAD]
        wuk = wuk.at[:, hd * LANES:hd * LANES + MLA_NOPE].set(kvsrc[:, :MLA_NOPE])
        wuv.append(kvsrc[:, MLA_NOPE:])
    wuv = jnp.concatenate(wuv, axis=1)
    wkr = jnp.zeros((d, LANES), F32).at[:, MLA_NOPE:qk_w].set(l1_w_in[:, MLA_Q_RANK + MLA_KV_RANK:][:, rperm])
    pad = jnp.zeros((LANES - qk_w,), F32)
    zn = jnp.zeros((MLA_NOPE,), F32)
    gq1 = row1(jnp.concatenate([l1_q_nope_g, l1_q_rope_g[rperm], pad]) * (qk_w ** -0.5 * LOG2E))
    gk1 = row1(jnp.concatenate([l1_k_nope_g, jnp.zeros((LANES - MLA_NOPE,), F32)]))
    gkr1 = row1(jnp.concatenate([zn, l1_k_rope_g[rperm], pad]))
    seg_mla = _segment_matrix([(0, MLA_NOPE), (MLA_NOPE, qk_w)])
    cos1, sin1 = _rope_tables(n_x, n_ctx, MLA_ROPE, MLA_NOPE, 1)

    mq, mk, mvlo, mvhi = _l1_proj(
        xa, mod1, row1(l1_norm1_g), l1_w_in[:, :MLA_Q_RANK].astype(BF16),
        l1_w_in[:, MLA_Q_RANK:MLA_Q_RANK + MLA_KV_RANK].astype(BF16), wkr.astype(BF16),
        row1(l1_q_lora_g), row1(l1_kv_lora_g), wuq.astype(BF16), wuk.astype(BF16), wuv.astype(BF16),
        seg_mla, gq1, gk1, gkr1, cos1, sin1, tm=tm, n_x=n_x)
    mo = _flash(mq, mk, (mvlo, mvhi), q_maps=(lambda j: 2 * j, lambda j: 2 * j + 1),
                k_maps=(lambda j: 2 * j, lambda j: 2 * j + 1), v_maps=(lambda j: j, lambda j: j),
                n_pairs=MLA_HEADS // 2, n_q=n_x, q_row0=0, kv_row0=0, n_kv=t, bq=bq, bk=_kv_block(t), rs=32)

    router = jnp.zeros((d, LANES), F32).at[:, :N_EXPERTS].set(l1_router)
    r_hi = router.astype(BF16)
    r_lo = (router - r_hi.astype(F32)).astype(BF16)
    x3, hmoe, ei, ew = _l1_out(xa, mod1, row1(l1_norm2_g), mo, l1_w_out.astype(BF16), r_hi, r_lo, tm=tmx)

    tme = 512 if n_x >= 4096 else 128
    e_flat = ei[:, :TOP_K].reshape(-1)
    onehot = (e_flat[:, None] == jnp.arange(N_EXPERTS)[None, :]).astype(jnp.int32)
    csum = jnp.cumsum(onehot, axis=0)
    rank = jnp.sum((csum - onehot) * onehot, axis=1)
    counts = csum[-1]
    padded = ((counts + tme - 1) // tme) * tme
    ends = jnp.cumsum(padded)
    pos = (ends - padded)[e_flat] + rank
    n_tiles = (TOP_K * n_x) // tme + N_EXPERTS
    p_rows = n_tiles * tme
    src = jnp.zeros((p_rows,), jnp.int32).at[pos].set(jnp.arange(TOP_K * n_x, dtype=jnp.int32) // TOP_K)
    n_valid = (ends[-1] // tme).astype(jnp.int32)
    tile_ids = jnp.minimum(jnp.arange(n_tiles, dtype=jnp.int32), n_valid - 1)
    tile_expert = jnp.sum((ends[None, :] <= (tile_ids * tme)[:, None]).astype(jnp.int32), axis=1)
    tile_expert = jnp.minimum(tile_expert, N_EXPERTS - 1)
    xs = jnp.take(hmoe, src, axis=0, mode="clip")
    fdim = l1_exp_w1.shape[2]
    tf = fdim // 2 if (fdim // 2) % LANES == 0 else fdim
    ys = _moe(tile_expert, n_valid.reshape(1), xs, l1_exp_w1.astype(BF16), l1_exp_w3.astype(BF16),
              l1_exp_w2.astype(BF16), tm=tme, tf=tf)
    pos2 = pos.reshape(n_x, TOP_K)
    out = _combine(x3, mod1, ew, jnp.take(ys, pos2[:, 0], axis=0, mode="clip"),
                   jnp.take(ys, pos2[:, 1], axis=0, mode="clip"), tm=tmx)
    return out[None]
```

```python
import functools
import math

import numpy as np
import jax
import jax.numpy as jnp
from jax import lax
from jax.experimental import pallas as pl
from jax.experimental.pallas import tpu as pltpu
from jax.experimental.pallas import tpu_sc as plsc

F32 = jnp.float32
BF16 = jnp.bfloat16

EPS = 1e-6
ROPE_THETA = 10000.0
GRID_W = 64
LANES = 128
HEAD = 64
RET_CHUNK = 128
RET_HEADS = 8
GQA_HEADS = 8
GQA_KV_HEADS = 2
MLA_HEADS = 8
MLA_Q_RANK = 384
MLA_KV_RANK = 256
MLA_NOPE = 64
MLA_ROPE = 32
N_EXPERTS = 8
TOP_K = 2
LOW_ONE = HEAD
HIGH_ONE = 0
LOG2E = math.log2(math.e)
VMEM_LIMIT = 56 * 1024 * 1024


def _cparams(sem, vmem=VMEM_LIMIT):
    return pltpu.CompilerParams(dimension_semantics=sem, vmem_limit_bytes=vmem)


def _resident(shape):
    nd = len(shape)
    return pl.BlockSpec(shape, lambda *_: (0,) * nd, pipeline_mode=pl.Buffered(1))


def _dot(a, b):
    return jnp.dot(a, b, preferred_element_type=F32)


def _dot_nt(a, b):
    return lax.dot_general(a, b, (((1,), (1,)), ((), ())), preferred_element_type=F32)


def _seg_mean(v, seg):
    hi = v.astype(BF16)
    lo = (v - hi.astype(F32)).astype(BF16)
    return _dot(hi, seg) + _dot(lo, seg)


def _silu(x):
    return x * jax.nn.sigmoid(x)


def _modulated(x, mod_ref, g_ref, which, tile, tm, n_x, d):
    ms = jnp.mean(x * x, axis=-1, keepdims=True)
    xn = x * lax.rsqrt(ms + EPS)
    g = g_ref[...]
    sh, sc = 3 * which, 3 * which + 1
    a_x = g * (1.0 + mod_ref[0:1, sc * d:(sc + 1) * d])
    a_c = g * (1.0 + mod_ref[1:2, sc * d:(sc + 1) * d])
    b_x = mod_ref[0:1, sh * d:(sh + 1) * d]
    b_c = mod_ref[1:2, sh * d:(sh + 1) * d]
    row = tile * tm + lax.broadcasted_iota(jnp.int32, (tm, 1), 0)
    is_ctx = row >= n_x
    return xn * jnp.where(is_ctx, a_c, a_x) + jnp.where(is_ctx, b_c, b_x)


def _row_gate(mod_ref, idx, tile, tm, n_x, d):
    row = tile * tm + lax.broadcasted_iota(jnp.int32, (tm, 1), 0)
    return jnp.where(row >= n_x, mod_ref[1:2, idx * d:(idx + 1) * d], mod_ref[0:1, idx * d:(idx + 1) * d])


def _lane(shape):
    return lax.broadcasted_iota(jnp.int32, shape, len(shape) - 1)


def _ada_kernel(c_ref, w_ref, b_ref, o_ref):
    c = c_ref[...]
    o_ref[...] = jnp.dot(_silu(c), w_ref[...], preferred_element_type=F32,
                         precision=lax.Precision.HIGHEST) + b_ref[...]


def _ada(cvec8, w, b):
    d, n = w.shape
    tn = n // 4
    return pl.pallas_call(
        _ada_kernel,
        grid=(n // tn,),
        in_specs=[pl.BlockSpec((8, d), lambda j: (0, 0)),
                  pl.BlockSpec((d, tn), lambda j: (0, j)),
                  pl.BlockSpec((1, tn), lambda j: (0, j))],
        out_specs=pl.BlockSpec((8, tn), lambda j: (0, j)),
        out_shape=jax.ShapeDtypeStruct((8, n), F32),
        compiler_params=_cparams(("arbitrary",)),
    )(cvec8, w, b.reshape(1, n))


def _rope128(v, c, s, half):
    lane = _lane(v.shape)
    swapped = jnp.where(lane % (2 * half) < half, pltpu.roll(v, LANES - half, 1), pltpu.roll(v, half, 1))
    return v * c + swapped * s


def _l0_proj_kernel(x_ref, mod_ref, g_ref, w_ref, seg_ref, gq_ref, gk_ref, c_ref, s_ref,
                    rq_ref, rk_ref, rv_ref, rg_ref, q_ref, k_ref, v_ref, *, tm, n_x, d):
    i = pl.program_id(0)
    h = _modulated(x_ref[...], mod_ref, g_ref, 0, i, tm, n_x, d).astype(BF16)
    rw = RET_HEADS * HEAD
    for idx, ref in enumerate((rq_ref, rk_ref, rv_ref, rg_ref)):
        ref[...] = _dot(h, w_ref[:, idx * rw:(idx + 1) * rw]).astype(BF16)
    seg = seg_ref[...]
    cos, sin = c_ref[...], s_ref[...]
    base = 4 * rw
    qw = GQA_HEADS * HEAD
    qa = _dot(h, w_ref[:, base:base + qw])
    for g in range(qw // LANES):
        v = qa[:, g * LANES:(g + 1) * LANES]
        vn = v * lax.rsqrt(_seg_mean(v * v, seg) + EPS) * gq_ref[...]
        q_ref[:, g * LANES:(g + 1) * LANES] = _rope128(vn, cos, sin, HEAD // 2).astype(BF16)
    kv = _dot(h, w_ref[:, base + qw:base + qw + 2 * LANES])
    kk = kv[:, :LANES]
    kk = kk * lax.rsqrt(_seg_mean(kk * kk, seg) + EPS) * gk_ref[...]
    kk = _rope128(kk, cos, sin, HEAD // 2)
    vv = kv[:, LANES:]
    lane = _lane(kk.shape)
    low = lane < HEAD
    for src, ref, one in ((kk, k_ref, 0.0), (vv, v_ref, 1.0)):
        sw = pltpu.roll(src, HEAD, 1)
        lo_fill = jnp.where(lane == LOW_ONE, one, 0.0)
        hi_fill = jnp.where(lane == HIGH_ONE, one, 0.0)
        ref[:, 0 * LANES:1 * LANES] = jnp.where(low, src, lo_fill).astype(BF16)
        ref[:, 1 * LANES:2 * LANES] = jnp.where(low, hi_fill, sw).astype(BF16)
        ref[:, 2 * LANES:3 * LANES] = jnp.where(low, sw, lo_fill).astype(BF16)
        ref[:, 3 * LANES:4 * LANES] = jnp.where(low, hi_fill, src).astype(BF16)


def _l0_proj(xa, mod, g, w, seg, gq, gk, cos, sin, *, tm, n_x):
    t, d = xa.shape
    rw = RET_HEADS * HEAD
    row = lambda i: (i, 0)
    outs = [jax.ShapeDtypeStruct((t, rw), BF16)] * 7
    return pl.pallas_call(
        functools.partial(_l0_proj_kernel, tm=tm, n_x=n_x, d=d),
        grid=(t // tm,),
        in_specs=[pl.BlockSpec((tm, d), row), _resident(mod.shape), _resident(g.shape), _resident(w.shape),
                  _resident(seg.shape), _resident(gq.shape), _resident(gk.shape),
                  pl.BlockSpec((tm, LANES), row), pl.BlockSpec((tm, LANES), row)],
        out_specs=[pl.BlockSpec((tm, rw), row)] * 7,
        out_shape=outs,
        compiler_params=_cparams(("parallel",)),
    )(xa, mod, g, w, seg, gq, gk, cos, sin)


def _retention_kernel(lg_ref, qf_ref, kf_ref, vf_ref, qb_ref, kb_ref, vb_ref, of_ref, ob_ref,
                      state_ref, decay_ref, xi_ref, zeta_ref, gl_ref):
    c = RET_CHUNK
    npairs = RET_HEADS * HEAD // LANES
    step = pl.program_id(0)

    @pl.when(step == 0)
    def _init():
        state_ref[...] = jnp.zeros_like(state_ref)
        ci = lax.broadcasted_iota(jnp.int32, (c, c), 0).astype(F32)
        mi = lax.broadcasted_iota(jnp.int32, (c, c), 1).astype(F32)
        pos = lax.broadcasted_iota(jnp.int32, (c, RET_HEADS * HEAD), 0).astype(F32)
        lane_head = _lane((1, RET_HEADS * HEAD)) // HEAD
        for dr in range(2):
            lgv = jnp.zeros((1, RET_HEADS * HEAD), F32)
            for hd in range(RET_HEADS):
                lg = lg_ref[dr, hd]
                rel = (ci - mi) if dr == 0 else (mi - ci)
                decay_ref[dr, hd] = jnp.where(rel >= 0, jnp.exp(jnp.maximum(rel, 0.0) * lg), 0.0)
                lgv = jnp.where(lane_head == hd, lg, lgv)
            p = pos if dr == 0 else (c - 1.0 - pos)
            xi_ref[dr] = jnp.exp((p + 1.0) * lgv)
            zeta_ref[dr] = jnp.exp((c - 1.0 - p) * lgv)
            gl_ref[dr] = jnp.exp(float(c) * lgv)

    low = _lane((c, LANES)) < HEAD
    r_i = lax.broadcasted_iota(jnp.int32, (LANES, LANES), 0) // HEAD
    c_i = lax.broadcasted_iota(jnp.int32, (LANES, LANES), 1) // HEAD
    blockdiag = r_i == c_i
    for dr, (q_ref, k_ref, v_ref, o_ref) in enumerate(((qf_ref, kf_ref, vf_ref, of_ref),
                                                       (qb_ref, kb_ref, vb_ref, ob_ref))):
        for j in range(npairs):
            sl = slice(j * LANES, (j + 1) * LANES)
            q, k, v = q_ref[:, sl], k_ref[:, sl], v_ref[:, sl]
            zero = jnp.zeros_like(q)
            s0 = _dot_nt(jnp.where(low, q, zero), k) * decay_ref[dr, 2 * j]
            s1 = _dot_nt(jnp.where(low, zero, q), k) * decay_ref[dr, 2 * j + 1]
            o = _dot(s0.astype(BF16), jnp.where(low, v, zero)) + _dot(s1.astype(BF16), jnp.where(low, zero, v))
            st = state_ref[dr, j]
            qx = (q.astype(F32) * xi_ref[dr, :, sl]).astype(BF16)
            o = o + _dot(qx, st.astype(BF16))
            o_ref[:, sl] = o.astype(BF16)
            kz = (k.astype(F32) * zeta_ref[dr, :, sl]).T.astype(BF16)
            u = _dot(kz, v)
            state_ref[dr, j] = st * gl_ref[dr, :, sl] + jnp.where(blockdiag, u, 0.0)


def _retention(lg, rq, rk, rv, *, n_x):
    t, w = rq.shape
    c = RET_CHUNK
    nc, ncx = t // c, n_x // c
    fwd = lambda i: ((i + ncx) % nc, 0)
    bwd = lambda i: (nc - 1 - i, 0)
    blk = lambda m: pl.BlockSpec((c, w), m)
    npairs = w // LANES
    return pl.pallas_call(
        _retention_kernel,
        grid=(nc,),
        in_specs=[pl.BlockSpec(memory_space=pltpu.SMEM)] + [blk(fwd)] * 3 + [blk(bwd)] * 3,
        out_specs=[blk(fwd), blk(bwd)],
        out_shape=[jax.ShapeDtypeStruct((t, w), BF16)] * 2,
        scratch_shapes=[pltpu.VMEM((2, npairs, LANES, LANES), F32),
                        pltpu.VMEM((2, RET_HEADS, c, c), F32),
                        pltpu.VMEM((2, c, w), F32), pltpu.VMEM((2, c, w), F32),
                        pltpu.VMEM((2, 1, w), F32)],
        compiler_params=_cparams(("arbitrary",)),
    )(lg, rq, rk, rv, rq, rk, rv)


def _flash_kernel(q0_ref, q1_ref, k0_ref, k1_ref, v0_ref, v1_ref, o_ref,
                  s_ref, p_ref, a_ref, m_ref, acc_ref, *, bk, nkv, rs):
    bq = q0_ref.shape[0]
    q_refs, k_refs, v_refs = (q0_ref, q1_ref), (k0_ref, k1_ref), (v0_ref, v1_ref)
    m_ref[...] = jnp.full(m_ref.shape, -jnp.inf, F32)
    acc_ref[...] = jnp.zeros(acc_ref.shape, F32)

    def keys(t):
        return pl.ds(t * bk if isinstance(t, int) else pl.multiple_of(t * bk, bk), bk)

    def scores(t, slot):
        for h in range(2):
            s_ref[slot, h] = _dot_nt(q_refs[h][...], k_refs[h][keys(t), :])

    def softmax(slot):
        col = lambda c: slice(c * LANES, (c + 1) * LANES)
        for r in range(bq // rs):
            rows = slice(r * rs, (r + 1) * rs)
            for h in range(2):
                mx = s_ref[slot, h, rows, col(0)]
                for c in range(1, bk // LANES):
                    mx = jnp.maximum(mx, s_ref[slot, h, rows, col(c)])
                m_old = m_ref[h, rows, :]
                m_new = jnp.maximum(m_old, jnp.max(mx, axis=-1, keepdims=True))
                a_ref[slot, h, rows, :] = jnp.exp2(m_old - m_new)
                m_ref[h, rows, :] = m_new
                for c in range(bk // LANES):
                    p_ref[slot, h, rows, col(c)] = jnp.exp2(s_ref[slot, h, rows, col(c)] - m_new).astype(BF16)

    def values(t, slot):
        for h in range(2):
            acc_ref[h] = acc_ref[h] * a_ref[slot, h] + _dot(p_ref[slot, h], v_refs[h][keys(t), :])

    scores(0, 0)

    def body(i, carry):
        t = 2 * i
        scores(t + 1, 1)
        softmax(0)
        values(t, 0)
        scores(t + 2, 0)
        softmax(1)
        values(t + 1, 1)
        return carry

    n_loop = (nkv - 1) // 2
    lax.fori_loop(0, n_loop, body, 0)
    last = 2 * n_loop
    if last + 1 < nkv:
        scores(last + 1, 1)
    softmax(0)
    values(last, 0)
    if last + 1 < nkv:
        softmax(1)
        values(last + 1, 1)
    low = _lane((bq, LANES)) < HEAD
    acc0, acc1 = acc_ref[0], acc_ref[1]
    out = jnp.where(low, acc0 / acc0[:, LOW_ONE:LOW_ONE + 1], acc1 / acc1[:, HIGH_ONE:HIGH_ONE + 1])
    o_ref[...] = out.astype(o_ref.dtype)


def _flash(q, kmat, vmat, *, q_maps, k_maps, v_maps, n_q, q_row0, kv_row0, n_kv, n_pairs, bq, bk, rs):
    assert q_row0 % bq == 0 and n_q % bq == 0 and n_kv % bk == 0 and kv_row0 % n_kv == 0 and bq % rs == 0
    qb0, kb0 = q_row0 // bq, kv_row0 // n_kv
    qspec = lambda m: pl.BlockSpec((bq, LANES), lambda j, i: (i + qb0, m(j)))
    kspec = lambda m: pl.BlockSpec((n_kv, LANES), lambda j, i: (kb0, m(j)), pipeline_mode=pl.Buffered(1))
    return pl.pallas_call(
        functools.partial(_flash_kernel, bk=bk, nkv=n_kv // bk, rs=rs),
        grid=(n_pairs, n_q // bq),
        in_specs=[qspec(q_maps[0]), qspec(q_maps[1]), kspec(k_maps[0]), kspec(k_maps[1]),
                  kspec(v_maps[0]), kspec(v_maps[1])],
        out_specs=pl.BlockSpec((bq, LANES), lambda j, i: (i, j)),
        out_shape=jax.ShapeDtypeStruct((n_q, n_pairs * LANES), BF16),
        scratch_shapes=[pltpu.VMEM((2, 2, bq, bk), F32), pltpu.VMEM((2, 2, bq, bk), BF16),
                        pltpu.VMEM((2, 2, bq, LANES), F32), pltpu.VMEM((2, bq, LANES), F32),
                        pltpu.VMEM((2, bq, LANES), F32)],
        compiler_params=_cparams(("parallel", "parallel")),
    )(q, q, kmat, kmat, vmat[0], vmat[1])


def _kv_block(n_kv):
    for bk in (1280, 1024, 512, 256):
        if n_kv % bk == 0:
            return bk
    raise ValueError(f"key count {n_kv} has no supported block")


def _l0_out_kernel(x_ref, mod_ref, of_ref, ob_ref, rg_ref, ao_ref, seg_ref, wo_ref, o_ref, *, tm, n_x, d):
    i = pl.program_id(0)
    seg = seg_ref[...]
    rw = RET_HEADS * HEAD
    acc = _dot(ao_ref[...], wo_ref[rw:, :])
    for g in range(rw // LANES):
        sl = slice(g * LANES, (g + 1) * LANES)
        o = of_ref[:, sl].astype(F32) + ob_ref[:, sl].astype(F32)
        dv = o - _seg_mean(o, seg)
        nrm = dv * lax.rsqrt(_seg_mean(dv * dv, seg) + EPS)
        ra = (nrm * _silu(rg_ref[:, sl].astype(F32))).astype(BF16)
        acc = acc + _dot(ra, wo_ref[g * LANES:(g + 1) * LANES, :])
    o_ref[...] = x_ref[...] + _row_gate(mod_ref, 2, i, tm, n_x, d) * acc


def _l0_out(xa, mod, o_f, o_b, rg, ao, seg, wo, *, tm, n_x):
    t, d = xa.shape
    rw = o_f.shape[1]
    row = lambda i: (i, 0)
    return pl.pallas_call(
        functools.partial(_l0_out_kernel, tm=tm, n_x=n_x, d=d),
        grid=(t // tm,),
        in_specs=[pl.BlockSpec((tm, d), row), _resident(mod.shape)] + [pl.BlockSpec((tm, rw), row)] * 4
                 + [_resident(seg.shape), _resident(wo.shape)],
        out_specs=pl.BlockSpec((tm, d), row),
        out_shape=jax.ShapeDtypeStruct((t, d), F32),
        compiler_params=_cparams(("parallel",)),
    )(xa, mod, o_f, o_b, rg, ao, seg, wo)


def _ffn_kernel(x_ref, mod_ref, g_ref, w1_ref, w3_ref, w2_ref, o_ref, *, tm, n_x, d):
    i = pl.program_id(0)
    x = x_ref[...]
    h = _modulated(x, mod_ref, g_ref, 1, i, tm, n_x, d).astype(BF16)
    a = _dot(h, w1_ref[...])
    u = (_silu(a) * _dot(h, w3_ref[...])).astype(BF16)
    o_ref[...] = x + _row_gate(mod_ref, 5, i, tm, n_x, d) * _dot(u, w2_ref[...])


def _ffn(xa, mod, g, w1, w3, w2, *, tm, n_x):
    t, d = xa.shape
    row = lambda i: (i, 0)
    return pl.pallas_call(
        functools.partial(_ffn_kernel, tm=tm, n_x=n_x, d=d),
        grid=(t // tm,),
        in_specs=[pl.BlockSpec((tm, d), row), _resident(mod.shape), _resident(g.shape),
                  _resident(w1.shape), _resident(w3.shape), _resident(w2.shape)],
        out_specs=pl.BlockSpec((tm, d), row),
        out_shape=jax.ShapeDtypeStruct((t, d), F32),
        compiler_params=_cparams(("parallel",)),
    )(xa, mod, g, w1, w3, w2)


def _l1_proj_kernel(x_ref, mod_ref, g_ref, wq_ref, wkv_ref, wkr_ref, gql_ref, gkvl_ref, wuq_ref, wuk_ref,
                    wuv_ref, seg_ref, gq_ref, gk_ref, gkr_ref, c_ref, s_ref,
                    q_ref, k_ref, vlo_ref, vhi_ref, *, tm, n_x, d):
    i = pl.program_id(0)
    h = _modulated(x_ref[...], mod_ref, g_ref, 0, i, tm, n_x, d).astype(BF16)
    seg = seg_ref[...]
    cos, sin = c_ref[...], s_ref[...]

    def lora_norm(v, g):
        return (v * lax.rsqrt(jnp.mean(v * v, axis=-1, keepdims=True) + EPS) * g).astype(BF16)

    cq = lora_norm(_dot(h, wq_ref[...]), gql_ref[...])
    ckv = lora_norm(_dot(h, wkv_ref[...]), gkvl_ref[...])
    kr = _dot(h, wkr_ref[...])
    kr = kr * lax.rsqrt(_seg_mean(kr * kr, seg) + EPS) * gkr_ref[...]
    kr = _rope128(kr, cos, sin, MLA_ROPE // 2)
    qa = _dot(cq, wuq_ref[...])
    ka = _dot(ckv, wuk_ref[...])
    for hd in range(MLA_HEADS):
        sl = slice(hd * LANES, (hd + 1) * LANES)
        v = qa[:, sl]
        vn = v * lax.rsqrt(_seg_mean(v * v, seg) + EPS) * gq_ref[...]
        q_ref[:, sl] = _rope128(vn, cos, sin, MLA_ROPE // 2).astype(BF16)
        v = ka[:, sl]
        k_ref[:, sl] = (v * lax.rsqrt(_seg_mean(v * v, seg) + EPS) * gk_ref[...] + kr).astype(BF16)
    va = _dot(ckv, wuv_ref[...])
    lane = _lane(va.shape) % LANES
    low = lane < HEAD
    vlo_ref[...] = jnp.where(low, va, jnp.where(lane == LOW_ONE, 1.0, 0.0)).astype(BF16)
    vhi_ref[...] = jnp.where(low, jnp.where(lane == HIGH_ONE, 1.0, 0.0), va).astype(BF16)


def _l1_proj(xa, mod, g, wq, wkv, wkr, gql, gkvl, wuq, wuk, wuv, seg, gq, gk, gkr, cos, sin, *, tm, n_x):
    t, d = xa.shape
    row = lambda i: (i, 0)
    hw = MLA_HEADS * LANES
    vw = MLA_HEADS * HEAD
    consts = (mod, g, wq, wkv, wkr, gql, gkvl, wuq, wuk, wuv, seg, gq, gk, gkr)
    return pl.pallas_call(
        functools.partial(_l1_proj_kernel, tm=tm, n_x=n_x, d=d),
        grid=(t // tm,),
        in_specs=[pl.BlockSpec((tm, d), row)] + [_resident(a.shape) for a in consts]
                 + [pl.BlockSpec((tm, LANES), row)] * 2,
        out_specs=[pl.BlockSpec((tm, hw), row), pl.BlockSpec((tm, hw), row),
                   pl.BlockSpec((tm, vw), row), pl.BlockSpec((tm, vw), row)],
        out_shape=[jax.ShapeDtypeStruct((t, hw), BF16), jax.ShapeDtypeStruct((t, hw), BF16),
                   jax.ShapeDtypeStruct((t, vw), BF16), jax.ShapeDtypeStruct((t, vw), BF16)],
        compiler_params=_cparams(("parallel",)),
    )(xa, *consts, cos, sin)


def _l1_out_kernel(x_ref, mod_ref, g_ref, o_ref, wo_ref, rhi_ref, rlo_ref, x3_ref, h_ref, ei_ref, ew_ref, *, d):
    x3 = x_ref[...] + mod_ref[0:1, 2 * d:3 * d] * _dot(o_ref[...], wo_ref[...])
    x3_ref[...] = x3
    ms = jnp.mean(x3 * x3, axis=-1, keepdims=True)
    h = x3 * lax.rsqrt(ms + EPS) * (g_ref[...] * (1.0 + mod_ref[0:1, 4 * d:5 * d])) + mod_ref[0:1, 3 * d:4 * d]
    hi = h.astype(BF16)
    bits = lax.bitcast_convert_type(hi.astype(F32), jnp.uint32)
    words = (bits[:, :d // 2] >> 16) | (bits[:, d // 2:] & jnp.uint32(0xFFFF0000))
    h_ref[0] = words[:, :d // 4]
    h_ref[1] = words[:, d // 4:]
    lo = (h - hi.astype(F32)).astype(BF16)
    logits = _dot(hi, rhi_ref[...]) + (_dot(hi, rlo_ref[...]) + _dot(lo, rhi_ref[...]))
    lane_i = _lane(logits.shape)
    lane = lane_i.astype(F32)
    logits = jnp.where(lane_i < N_EXPERTS, logits, -jnp.inf)
    v1 = jnp.max(logits, axis=-1, keepdims=True)
    i1 = jnp.min(jnp.where(logits == v1, lane, float(LANES)), axis=-1, keepdims=True)
    rest = jnp.where(lane == i1, -jnp.inf, logits)
    v2 = jnp.max(rest, axis=-1, keepdims=True)
    i2 = jnp.min(jnp.where(rest == v2, lane, float(LANES)), axis=-1, keepdims=True)
    e2 = jnp.exp(v2 - v1)
    den = 1.0 + e2
    ei_ref[...] = jnp.where(lane_i == 0, i1, jnp.where(lane_i == 1, i2, 0.0)).astype(jnp.int32)
    ew_ref[...] = jnp.where(lane_i == 0, 1.0 / den, jnp.where(lane_i == 1, e2 / den, 0.0))


def _l1_out(xa, mod, g, o, wo, rhi, rlo, *, tm):
    n, d = o.shape[0], xa.shape[1]
    row = lambda i: (i, 0)
    return pl.pallas_call(
        functools.partial(_l1_out_kernel, d=d),
        grid=(n // tm,),
        in_specs=[pl.BlockSpec((tm, d), row), _resident(mod.shape), _resident(g.shape),
                  pl.BlockSpec((tm, o.shape[1]), row), _resident(wo.shape), _resident(rhi.shape),
                  _resident(rlo.shape)],
        out_specs=[pl.BlockSpec((tm, d), row), pl.BlockSpec((2, tm, d // 4), lambda i: (0, i, 0)),
                   pl.BlockSpec((tm, LANES), row), pl.BlockSpec((tm, LANES), row)],
        out_shape=[jax.ShapeDtypeStruct((n, d), F32), jax.ShapeDtypeStruct((2, n, d // 4), jnp.uint32),
                   jax.ShapeDtypeStruct((n, LANES), jnp.int32), jax.ShapeDtypeStruct((n, LANES), F32)],
        compiler_params=_cparams(("parallel",)),
    )(xa, mod, g, o, wo, rhi, rlo)


def _moe_kernel(te_ref, nv_ref, x_ref, w1_ref, w3_ref, w2_ref, y_ref, acc_ref, *, nf):
    i, f = pl.program_id(0), pl.program_id(1)

    @pl.when(f == 0)
    def _zero():
        acc_ref[...] = jnp.zeros_like(acc_ref)

    @pl.when(i < nv_ref[0])
    def _compute():
        words = jnp.concatenate([x_ref[0], x_ref[1]], axis=1)
        lo = lax.bitcast_convert_type(words << 16, F32)
        hi = lax.bitcast_convert_type(words & jnp.uint32(0xFFFF0000), F32)
        x = jnp.concatenate([lo, hi], axis=1).astype(BF16)
        a = _dot(x, w1_ref[0])
        u = (_silu(a) * _dot(x, w3_ref[0])).astype(BF16)
        acc_ref[...] += _dot(u, w2_ref[0])

    @pl.when(f == nf - 1)
    def _store():
        y_ref[...] = acc_ref[...].astype(y_ref.dtype)


def _moe(tile_expert, n_valid, xs, w1, w3, w2, *, tm, tf):
    p, d = xs.shape[1], 4 * xs.shape[2]
    fdim = w1.shape[2]
    nf = fdim // tf
    fi = lambda i, f, te, nv: jnp.where(i < nv[0], f, nf - 1)
    grid_spec = pltpu.PrefetchScalarGridSpec(
        num_scalar_prefetch=2,
        grid=(p // tm, nf),
        in_specs=[pl.BlockSpec((2, tm, d // 4), lambda i, f, te, nv: (0, i, 0)),
                  pl.BlockSpec((1, d, tf), lambda i, f, te, nv: (te[i], 0, fi(i, f, te, nv))),
                  pl.BlockSpec((1, d, tf), lambda i, f, te, nv: (te[i], 0, fi(i, f, te, nv))),
                  pl.BlockSpec((1, tf, d), lambda i, f, te, nv: (te[i], fi(i, f, te, nv), 0))],
        out_specs=pl.BlockSpec((tm, d), lambda i, f, te, nv: (i, 0)),
        scratch_shapes=[pltpu.VMEM((tm, d), F32)],
    )
    return pl.pallas_call(
        functools.partial(_moe_kernel, nf=nf),
        grid_spec=grid_spec,
        out_shape=jax.ShapeDtypeStruct((p, d), BF16),
        compiler_params=_cparams(("arbitrary", "arbitrary")),
    )(tile_expert, n_valid, xs, w1, w3, w2)


SC_GATHER_WINDOW = 128


def _gather_rows(x, idx):
    n, d = idx.shape[0], x.shape[1]
    w = SC_GATHER_WINDOW
    assert n % w == 0
    mesh = plsc.VectorSubcoreMesh(core_axis_name="core", subcore_axis_name="subcore")

    @pl.kernel(out_type=jax.ShapeDtypeStruct((n, d), x.dtype), mesh=mesh)
    def gather_kernel(x_hbm, i_hbm, o_hbm):
        def body(i_vmem, o_vmem):
            pltpu.sync_copy(x_hbm.at[i_vmem.at[0]], o_vmem)

        pltpu.emit_pipeline(
            body,
            grid=(n // w,),
            in_specs=[pl.BlockSpec((1, w), lambda i: (0, i))],
            out_specs=[pl.BlockSpec((w, d), lambda i: (i, 0))],
            core_axis_name=("core", "subcore"),
            dimension_semantics=(pltpu.PARALLEL,),
        )(i_hbm, o_hbm)

    return gather_kernel(x, idx.reshape(1, n))


def _combine_kernel(x_ref, mod_ref, ew_ref, ya_ref, yb_ref, o_ref, *, d):
    ew = ew_ref[...]
    y = ew[:, 0:1] * ya_ref[...].astype(F32) + ew[:, 1:2] * yb_ref[...].astype(F32)
    o_ref[...] = x_ref[...] + mod_ref[0:1, 5 * d:6 * d] * y


def _combine(x3, mod, ew, ya, yb, *, tm):
    n, d = x3.shape
    row = lambda i: (i, 0)
    return pl.pallas_call(
        functools.partial(_combine_kernel, d=d),
        grid=(n // tm,),
        in_specs=[pl.BlockSpec((tm, d), row), _resident(mod.shape), pl.BlockSpec((tm, LANES), row),
                  pl.BlockSpec((tm, d), row), pl.BlockSpec((tm, d), row)],
        out_specs=pl.BlockSpec((tm, d), row),
        out_shape=jax.ShapeDtypeStruct((n, d), F32),
        compiler_params=_cparams(("parallel",)),
    )(x3, mod, ew, ya, yb)


def _deinterleave(width):
    return np.concatenate([np.arange(0, width, 2), np.arange(1, width, 2)])


def _rope_tables(n_x, n_ctx, rot_dim, seg_start, seg_repeat):
    rows = n_x // GRID_W
    row = jnp.broadcast_to(jnp.arange(rows)[:, None], (rows, GRID_W)).reshape(n_x).astype(F32)
    col = jnp.broadcast_to(jnp.arange(GRID_W)[None, :], (rows, GRID_W)).reshape(n_x).astype(F32)
    axis_dim = rot_dim // 2
    inv_freq = ROPE_THETA ** (-jnp.arange(0, axis_dim, 2, dtype=F32) / axis_dim)
    ang = jnp.concatenate([row[:, None] * inv_freq, col[:, None] * inv_freq], axis=-1)
    cos, sin = jnp.cos(ang), jnp.sin(ang)
    cseg = jnp.concatenate([cos, cos], axis=-1)
    sseg = jnp.concatenate([-sin, sin], axis=-1)
    c = jnp.ones((n_x, LANES), F32)
    s = jnp.zeros((n_x, LANES), F32)
    for r in range(seg_repeat):
        lo = seg_start + r * rot_dim
        c = c.at[:, lo:lo + rot_dim].set(cseg)
        s = s.at[:, lo:lo + rot_dim].set(sseg)
    c = jnp.concatenate([c, jnp.ones((n_ctx, LANES), F32)], axis=0)
    s = jnp.concatenate([s, jnp.zeros((n_ctx, LANES), F32)], axis=0)
    return c, s


def _segment_matrix(bounds):
    m = np.zeros((LANES, LANES), np.float32)
    for lo, hi in bounds:
        m[lo:hi, lo:hi] = 1.0 / (hi - lo)
    return jnp.asarray(m, BF16)


def _token_tile(t):
    for tm in (640, 512, 256, 128):
        if t % tm == 0:
            return tm
    raise ValueError(f"token count {t} has no supported tile")


def kernel(x, c, ctx, c_ctx, l0_ada_w, l0_ada_b, l0_norm1_g, l0_norm2_g, l0_w_in, l0_ret_log_decay, l0_q_norm_g, l0_k_norm_g, l0_w_out, l0_ffn_w1, l0_ffn_w3, l0_ffn_w2, l1_ada_w, l1_ada_b, l1_norm1_g, l1_norm2_g, l1_w_in, l1_q_lora_g, l1_kv_lora_g, l1_w_uq, l1_w_ukv, l1_q_nope_g, l1_q_rope_g, l1_k_nope_g, l1_k_rope_g, l1_w_out, l1_router, l1_exp_w1, l1_exp_w3, l1_exp_w2):
    b, n_x, d = x.shape
    n_ctx = ctx.shape[1]
    assert b == 1 and n_x % 256 == 0 and n_ctx % 256 == 0 and n_x % GRID_W == 0
    t = n_x + n_ctx
    tm = _token_tile(t)
    tmx = _token_tile(n_x)
    xa = jnp.concatenate([x[0], ctx[0]], axis=0)
    row1 = lambda v: v.reshape(1, -1).astype(F32)

    cvec = jnp.zeros((8, d), F32).at[0].set(c[0]).at[1].set(c_ctx)
    mod0 = _ada(cvec, l0_ada_w, l0_ada_b)
    mod1 = _ada(cvec, l1_ada_w, l1_ada_b)

    rw = RET_HEADS * HEAD
    perm = _deinterleave(HEAD)
    cols = np.arange(l0_w_in.shape[1])
    for hd in range(GQA_HEADS + GQA_KV_HEADS):
        lo = 4 * rw + hd * HEAD
        cols[lo:lo + HEAD] = lo + perm
    colscale = np.ones((l0_w_in.shape[1],), np.float32)
    colscale[rw:2 * rw] = HEAD ** -0.5
    w_in0 = (l0_w_in[:, cols] * colscale).astype(BF16)
    seg64 = _segment_matrix([(0, HEAD), (HEAD, 2 * HEAD)])
    gq0 = row1(jnp.tile(l0_q_norm_g[perm], 2) * (HEAD ** -0.5 * LOG2E))
    gk0 = row1(jnp.tile(l0_k_norm_g[perm], 2))
    cos0, sin0 = _rope_tables(n_x, n_ctx, HEAD, 0, 2)

    rq, rk, rv, rg, gq, gkx, gvx = _l0_proj(xa, mod0, row1(l0_norm1_g), w_in0, seg64, gq0, gk0, cos0, sin0,
                                             tm=tm, n_x=n_x)
    o_f, o_b = _retention(l0_ret_log_decay.astype(F32), rq, rk, rv, n_x=n_x)

    gqa_maps = dict(q_maps=(lambda j: j, lambda j: j),
                    k_maps=(lambda j: 2 * (j // 2), lambda j: 2 * (j // 2) + 1),
                    v_maps=(lambda j: 2 * (j // 2), lambda j: 2 * (j // 2) + 1), n_pairs=GQA_HEADS // 2)
    bq = 512 if n_x % 512 == 0 else 256
    ao_x = _flash(gq, gkx, (gvx, gvx), n_q=n_x, q_row0=0, kv_row0=0, n_kv=t, bq=bq, bk=_kv_block(t), rs=32,
                  **gqa_maps)
    ao_c = _flash(gq, gkx, (gvx, gvx), n_q=n_ctx, q_row0=n_x, kv_row0=n_x, n_kv=n_ctx, bq=n_ctx,
                  bk=_kv_block(n_ctx), rs=32, **gqa_maps)
    ao = jnp.concatenate([ao_x, ao_c], axis=0)

    xa = _l0_out(xa, mod0, o_f, o_b, rg, ao, seg64, l0_w_out.astype(BF16), tm=tm, n_x=n_x)
    xa = _ffn(xa, mod0, row1(l0_norm2_g), l0_ffn_w1.astype(BF16), l0_ffn_w3.astype(BF16),
              l0_ffn_w2.astype(BF16), tm=tm, n_x=n_x)

    rperm = _deinterleave(MLA_ROPE)
    qk_w = MLA_NOPE + MLA_ROPE
    wuq = jnp.zeros((MLA_Q_RANK, MLA_HEADS * LANES), F32)
    wuk = jnp.zeros((MLA_KV_RANK, MLA_HEADS * LANES), F32)
    wuv = []
    for hd in range(MLA_HEADS):
        src = l1_w_uq[:, hd * qk_w:(hd + 1) * qk_w]
        wuq = wuq.at[:, hd * LANES:hd * LANES + MLA_NOPE].set(src[:, :MLA_NOPE])
        wuq = wuq.at[:, hd * LANES + MLA_NOPE:hd * LANES + qk_w].set(src[:, MLA_NOPE:][:, rperm])
        kvsrc = l1_w_ukv[:, hd * 2 * HEAD:(hd + 1) * 2 * HEAD]
        wuk = wuk.at[:, hd * LANES:hd * LANES + MLA_NOPE].set(kvsrc[:, :MLA_NOPE])
        wuv.append(kvsrc[:, MLA_NOPE:])
    wuv = jnp.concatenate(wuv, axis=1)
    wkr = jnp.zeros((d, LANES), F32).at[:, MLA_NOPE:qk_w].set(l1_w_in[:, MLA_Q_RANK + MLA_KV_RANK:][:, rperm])
    pad = jnp.zeros((LANES - qk_w,), F32)
    zn = jnp.zeros((MLA_NOPE,), F32)
    gq1 = row1(jnp.concatenate([l1_q_nope_g, l1_q_rope_g[rperm], pad]) * (qk_w ** -0.5 * LOG2E))
    gk1 = row1(jnp.concatenate([l1_k_nope_g, jnp.zeros((LANES - MLA_NOPE,), F32)]))
    gkr1 = row1(jnp.concatenate([zn, l1_k_rope_g[rperm], pad]))
    seg_mla = _segment_matrix([(0, MLA_NOPE), (MLA_NOPE, qk_w)])
    cos1, sin1 = _rope_tables(n_x, n_ctx, MLA_ROPE, MLA_NOPE, 1)

    mq, mk, mvlo, mvhi = _l1_proj(
        xa, mod1, row1(l1_norm1_g), l1_w_in[:, :MLA_Q_RANK].astype(BF16),
        l1_w_in[:, MLA_Q_RANK:MLA_Q_RANK + MLA_KV_RANK].astype(BF16), wkr.astype(BF16),
        row1(l1_q_lora_g), row1(l1_kv_lora_g), wuq.astype(BF16), wuk.astype(BF16), wuv.astype(BF16),
        seg_mla, gq1, gk1, gkr1, cos1, sin1, tm=tm, n_x=n_x)
    mo = _flash(mq, mk, (mvlo, mvhi), q_maps=(lambda j: 2 * j, lambda j: 2 * j + 1),
                k_maps=(lambda j: 2 * j, lambda j: 2 * j + 1), v_maps=(lambda j: j, lambda j: j),
                n_pairs=MLA_HEADS // 2, n_q=n_x, q_row0=0, kv_row0=0, n_kv=t, bq=bq, bk=_kv_block(t), rs=32)

    router = jnp.zeros((d, LANES), F32).at[:, :N_EXPERTS].set(l1_router)
    r_hi = router.astype(BF16)
    r_lo = (router - r_hi.astype(F32)).astype(BF16)
    x3, hmoe, ei, ew = _l1_out(xa, mod1, row1(l1_norm2_g), mo, l1_w_out.astype(BF16), r_hi, r_lo, tm=tmx)

    tme = 512 if n_x >= 4096 else 128
    e_flat = ei[:, :TOP_K].reshape(-1)
    onehot = (e_flat[:, None] == jnp.arange(N_EXPERTS)[None, :]).astype(jnp.int32)
    csum = jnp.cumsum(onehot, axis=0)
    rank = jnp.sum((csum - onehot) * onehot, axis=1)
    counts = csum[-1]
    padded = ((counts + tme - 1) // tme) * tme
    ends = jnp.cumsum(padded)
    pos = (ends - padded)[e_flat] + rank
    n_tiles = (TOP_K * n_x) // tme + N_EXPERTS
    p_rows = n_tiles * tme
    src = jnp.zeros((p_rows,), jnp.int32).at[pos].set(jnp.arange(TOP_K * n_x, dtype=jnp.int32) // TOP_K)
    n_valid = (ends[-1] // tme).astype(jnp.int32)
    tile_ids = jnp.minimum(jnp.arange(n_tiles, dtype=jnp.int32), n_valid - 1)
    tile_expert = jnp.sum((ends[None, :] <= (tile_ids * tme)[:, None]).astype(jnp.int32), axis=1)
    tile_expert = jnp.minimum(tile_expert, N_EXPERTS - 1)
    xs = _gather_rows(hmoe.reshape(2 * n_x, d // 4), jnp.concatenate([src, src + n_x])).reshape(2, p_rows, d // 4)
    fdim = l1_exp_w1.shape[2]
    tf = fdim // 2 if (fdim // 2) % LANES == 0 else fdim
    ys = _moe(tile_expert, n_valid.reshape(1), xs, l1_exp_w1.astype(BF16), l1_exp_w3.astype(BF16),
              l1_exp_w2.astype(BF16), tm=tme, tf=tf)
    pos2 = pos.reshape(n_x, TOP_K)
    out = _combine(x3, mod1, ew, jnp.take(ys, pos2[:, 0], axis=0, mode="clip"),
                   jnp.take(ys, pos2[:, 1], axis=0, mode="clip"), tm=tmx)
    return out[None]
```

```python
import functools
import math

import numpy as np
import jax
import jax.numpy as jnp
from jax import lax
from jax.experimental import pallas as pl
from jax.experimental.pallas import tpu as pltpu
from jax.experimental.pallas import tpu_sc as plsc

F32 = jnp.float32
BF16 = jnp.bfloat16

EPS = 1e-6
ROPE_THETA = 10000.0
GRID_W = 64
LANES = 128
HEAD = 64
RET_CHUNK = 128
RET_HEADS = 8
GQA_HEADS = 8
GQA_KV_HEADS = 2
MLA_HEADS = 8
MLA_Q_RANK = 384
MLA_KV_RANK = 256
MLA_NOPE = 64
MLA_ROPE = 32
N_EXPERTS = 8
TOP_K = 2
LOW_ONE = HEAD
HIGH_ONE = 0
LOG2E = math.log2(math.e)
VMEM_LIMIT = 56 * 1024 * 1024


def _cparams(sem, vmem=VMEM_LIMIT):
    return pltpu.CompilerParams(dimension_semantics=sem, vmem_limit_bytes=vmem)


def _resident(shape):
    nd = len(shape)
    return pl.BlockSpec(shape, lambda *_: (0,) * nd, pipeline_mode=pl.Buffered(1))


def _dot(a, b):
    return jnp.dot(a, b, preferred_element_type=F32)


def _dot_nt(a, b):
    return lax.dot_general(a, b, (((1,), (1,)), ((), ())), preferred_element_type=F32)


def _seg_mean(v, seg):
    hi = v.astype(BF16)
    lo = (v - hi.astype(F32)).astype(BF16)
    return _dot(hi, seg) + _dot(lo, seg)


def _silu(x):
    return x * jax.nn.sigmoid(x)


def _modulated(x, mod_ref, g_ref, which, tile, tm, n_x, d):
    ms = jnp.mean(x * x, axis=-1, keepdims=True)
    xn = x * lax.rsqrt(ms + EPS)
    g = g_ref[...]
    sh, sc = 3 * which, 3 * which + 1
    a_x = g * (1.0 + mod_ref[0:1, sc * d:(sc + 1) * d])
    a_c = g * (1.0 + mod_ref[1:2, sc * d:(sc + 1) * d])
    b_x = mod_ref[0:1, sh * d:(sh + 1) * d]
    b_c = mod_ref[1:2, sh * d:(sh + 1) * d]
    row = tile * tm + lax.broadcasted_iota(jnp.int32, (tm, 1), 0)
    is_ctx = row >= n_x
    return xn * jnp.where(is_ctx, a_c, a_x) + jnp.where(is_ctx, b_c, b_x)


def _row_gate(mod_ref, idx, tile, tm, n_x, d):
    row = tile * tm + lax.broadcasted_iota(jnp.int32, (tm, 1), 0)
    return jnp.where(row >= n_x, mod_ref[1:2, idx * d:(idx + 1) * d], mod_ref[0:1, idx * d:(idx + 1) * d])


def _lane(shape):
    return lax.broadcasted_iota(jnp.int32, shape, len(shape) - 1)


def _ada_kernel(c_ref, w_ref, b_ref, o_ref):
    c = c_ref[...]
    o_ref[...] = jnp.dot(_silu(c), w_ref[...], preferred_element_type=F32,
                         precision=lax.Precision.HIGHEST) + b_ref[...]


def _ada(cvec8, w, b):
    d, n = w.shape
    tn = n // 4
    return pl.pallas_call(
        _ada_kernel,
        grid=(n // tn,),
        in_specs=[pl.BlockSpec((8, d), lambda j: (0, 0)),
                  pl.BlockSpec((d, tn), lambda j: (0, j)),
                  pl.BlockSpec((1, tn), lambda j: (0, j))],
        out_specs=pl.BlockSpec((8, tn), lambda j: (0, j)),
        out_shape=jax.ShapeDtypeStruct((8, n), F32),
        compiler_params=_cparams(("arbitrary",)),
    )(cvec8, w, b.reshape(1, n))


def _rope128(v, c, s, half):
    lane = _lane(v.shape)
    swapped = jnp.where(lane % (2 * half) < half, pltpu.roll(v, LANES - half, 1), pltpu.roll(v, half, 1))
    return v * c + swapped * s


def _l0_proj_kernel(x_ref, mod_ref, g_ref, w_ref, seg_ref, gq_ref, gk_ref, c_ref, s_ref,
                    rq_ref, rk_ref, rv_ref, rg_ref, q_ref, k_ref, v_ref, *, tm, n_x, d):
    i = pl.program_id(0)
    h = _modulated(x_ref[...], mod_ref, g_ref, 0, i, tm, n_x, d).astype(BF16)
    rw = RET_HEADS * HEAD
    for idx, ref in enumerate((rq_ref, rk_ref, rv_ref, rg_ref)):
        ref[...] = _dot(h, w_ref[:, idx * rw:(idx + 1) * rw]).astype(BF16)
    seg = seg_ref[...]
    cos, sin = c_ref[...], s_ref[...]
    base = 4 * rw
    qw = GQA_HEADS * HEAD
    qa = _dot(h, w_ref[:, base:base + qw])
    for g in range(qw // LANES):
        v = qa[:, g * LANES:(g + 1) * LANES]
        vn = v * lax.rsqrt(_seg_mean(v * v, seg) + EPS) * gq_ref[...]
        q_ref[:, g * LANES:(g + 1) * LANES] = _rope128(vn, cos, sin, HEAD // 2).astype(BF16)
    kv = _dot(h, w_ref[:, base + qw:base + qw + 2 * LANES])
    kk = kv[:, :LANES]
    kk = kk * lax.rsqrt(_seg_mean(kk * kk, seg) + EPS) * gk_ref[...]
    kk = _rope128(kk, cos, sin, HEAD // 2)
    vv = kv[:, LANES:]
    lane = _lane(kk.shape)
    low = lane < HEAD
    for src, ref, one in ((kk, k_ref, 0.0), (vv, v_ref, 1.0)):
        sw = pltpu.roll(src, HEAD, 1)
        lo_fill = jnp.where(lane == LOW_ONE, one, 0.0)
        hi_fill = jnp.where(lane == HIGH_ONE, one, 0.0)
        ref[:, 0 * LANES:1 * LANES] = jnp.where(low, src, lo_fill).astype(BF16)
        ref[:, 1 * LANES:2 * LANES] = jnp.where(low, hi_fill, sw).astype(BF16)
        ref[:, 2 * LANES:3 * LANES] = jnp.where(low, sw, lo_fill).astype(BF16)
        ref[:, 3 * LANES:4 * LANES] = jnp.where(low, hi_fill, src).astype(BF16)


def _l0_proj(xa, mod, g, w, seg, gq, gk, cos, sin, *, tm, n_x):
    t, d = xa.shape
    rw = RET_HEADS * HEAD
    row = lambda i: (i, 0)
    outs = [jax.ShapeDtypeStruct((t, rw), BF16)] * 7
    return pl.pallas_call(
        functools.partial(_l0_proj_kernel, tm=tm, n_x=n_x, d=d),
        grid=(t // tm,),
        in_specs=[pl.BlockSpec((tm, d), row), _resident(mod.shape), _resident(g.shape), _resident(w.shape),
                  _resident(seg.shape), _resident(gq.shape), _resident(gk.shape),
                  pl.BlockSpec((tm, LANES), row), pl.BlockSpec((tm, LANES), row)],
        out_specs=[pl.BlockSpec((tm, rw), row)] * 7,
        out_shape=outs,
        compiler_params=_cparams(("parallel",)),
    )(xa, mod, g, w, seg, gq, gk, cos, sin)


def _retention_kernel(lg_ref, qf_ref, kf_ref, vf_ref, qb_ref, kb_ref, vb_ref, of_ref, ob_ref,
                      state_ref, decay_ref, xi_ref, zeta_ref, gl_ref):
    c = RET_CHUNK
    npairs = RET_HEADS * HEAD // LANES
    step = pl.program_id(0)

    @pl.when(step == 0)
    def _init():
        state_ref[...] = jnp.zeros_like(state_ref)
        ci = lax.broadcasted_iota(jnp.int32, (c, c), 0).astype(F32)
        mi = lax.broadcasted_iota(jnp.int32, (c, c), 1).astype(F32)
        pos = lax.broadcasted_iota(jnp.int32, (c, RET_HEADS * HEAD), 0).astype(F32)
        lane_head = _lane((1, RET_HEADS * HEAD)) // HEAD
        for dr in range(2):
            lgv = jnp.zeros((1, RET_HEADS * HEAD), F32)
            for hd in range(RET_HEADS):
                lg = lg_ref[dr, hd]
                rel = (ci - mi) if dr == 0 else (mi - ci)
                decay_ref[dr, hd] = jnp.where(rel >= 0, jnp.exp(jnp.maximum(rel, 0.0) * lg), 0.0)
                lgv = jnp.where(lane_head == hd, lg, lgv)
            p = pos if dr == 0 else (c - 1.0 - pos)
            xi_ref[dr] = jnp.exp((p + 1.0) * lgv)
            zeta_ref[dr] = jnp.exp((c - 1.0 - p) * lgv)
            gl_ref[dr] = jnp.exp(float(c) * lgv)

    low = _lane((c, LANES)) < HEAD
    r_i = lax.broadcasted_iota(jnp.int32, (LANES, LANES), 0) // HEAD
    c_i = lax.broadcasted_iota(jnp.int32, (LANES, LANES), 1) // HEAD
    blockdiag = r_i == c_i
    for dr, (q_ref, k_ref, v_ref, o_ref) in enumerate(((qf_ref, kf_ref, vf_ref, of_ref),
                                                       (qb_ref, kb_ref, vb_ref, ob_ref))):
        for j in range(npairs):
            sl = slice(j * LANES, (j + 1) * LANES)
            q, k, v = q_ref[:, sl], k_ref[:, sl], v_ref[:, sl]
            zero = jnp.zeros_like(q)
            s0 = _dot_nt(jnp.where(low, q, zero), k) * decay_ref[dr, 2 * j]
            s1 = _dot_nt(jnp.where(low, zero, q), k) * decay_ref[dr, 2 * j + 1]
            o = _dot(s0.astype(BF16), jnp.where(low, v, zero)) + _dot(s1.astype(BF16), jnp.where(low, zero, v))
            st = state_ref[dr, j]
            qx = (q.astype(F32) * xi_ref[dr, :, sl]).astype(BF16)
            o = o + _dot(qx, st.astype(BF16))
            o_ref[:, sl] = o.astype(BF16)
            kz = (k.astype(F32) * zeta_ref[dr, :, sl]).T.astype(BF16)
            u = _dot(kz, v)
            state_ref[dr, j] = st * gl_ref[dr, :, sl] + jnp.where(blockdiag, u, 0.0)


def _retention(lg, rq, rk, rv, *, n_x):
    t, w = rq.shape
    c = RET_CHUNK
    nc, ncx = t // c, n_x // c
    fwd = lambda i: ((i + ncx) % nc, 0)
    bwd = lambda i: (nc - 1 - i, 0)
    blk = lambda m: pl.BlockSpec((c, w), m)
    npairs = w // LANES
    return pl.pallas_call(
        _retention_kernel,
        grid=(nc,),
        in_specs=[pl.BlockSpec(memory_space=pltpu.SMEM)] + [blk(fwd)] * 3 + [blk(bwd)] * 3,
        out_specs=[blk(fwd), blk(bwd)],
        out_shape=[jax.ShapeDtypeStruct((t, w), BF16)] * 2,
        scratch_shapes=[pltpu.VMEM((2, npairs, LANES, LANES), F32),
                        pltpu.VMEM((2, RET_HEADS, c, c), F32),
                        pltpu.VMEM((2, c, w), F32), pltpu.VMEM((2, c, w), F32),
                        pltpu.VMEM((2, 1, w), F32)],
        compiler_params=_cparams(("arbitrary",)),
    )(lg, rq, rk, rv, rq, rk, rv)


def _flash_kernel(q0_ref, q1_ref, k0_ref, k1_ref, v0_ref, v1_ref, o_ref,
                  s_ref, p_ref, a_ref, m_ref, acc_ref, *, bk, nkv, rs):
    bq = q0_ref.shape[0]
    q_refs, k_refs, v_refs = (q0_ref, q1_ref), (k0_ref, k1_ref), (v0_ref, v1_ref)
    m_ref[...] = jnp.full(m_ref.shape, -jnp.inf, F32)
    acc_ref[...] = jnp.zeros(acc_ref.shape, F32)

    def keys(t):
        return pl.ds(t * bk if isinstance(t, int) else pl.multiple_of(t * bk, bk), bk)

    def scores(t, slot):
        for h in range(2):
            s_ref[slot, h] = _dot_nt(q_refs[h][...], k_refs[h][keys(t), :])

    def softmax(slot):
        col = lambda c: slice(c * LANES, (c + 1) * LANES)
        for r in range(bq // rs):
            rows = slice(r * rs, (r + 1) * rs)
            for h in range(2):
                mx = s_ref[slot, h, rows, col(0)]
                for c in range(1, bk // LANES):
                    mx = jnp.maximum(mx, s_ref[slot, h, rows, col(c)])
                m_old = m_ref[h, rows, :]
                m_new = jnp.maximum(m_old, jnp.max(mx, axis=-1, keepdims=True))
                a_ref[slot, h, rows, :] = jnp.exp2(m_old - m_new)
                m_ref[h, rows, :] = m_new
                for c in range(bk // LANES):
                    p_ref[slot, h, rows, col(c)] = jnp.exp2(s_ref[slot, h, rows, col(c)] - m_new).astype(BF16)

    def values(t, slot):
        for h in range(2):
            acc_ref[h] = acc_ref[h] * a_ref[slot, h] + _dot(p_ref[slot, h], v_refs[h][keys(t), :])

    scores(0, 0)

    def body(i, carry):
        t = 2 * i
        scores(t + 1, 1)
        softmax(0)
        values(t, 0)
        scores(t + 2, 0)
        softmax(1)
        values(t + 1, 1)
        return carry

    n_loop = (nkv - 1) // 2
    lax.fori_loop(0, n_loop, body, 0)
    last = 2 * n_loop
    if last + 1 < nkv:
        scores(last + 1, 1)
    softmax(0)
    values(last, 0)
    if last + 1 < nkv:
        softmax(1)
        values(last + 1, 1)
    low = _lane((bq, LANES)) < HEAD
    acc0, acc1 = acc_ref[0], acc_ref[1]
    out = jnp.where(low, acc0 / acc0[:, LOW_ONE:LOW_ONE + 1], acc1 / acc1[:, HIGH_ONE:HIGH_ONE + 1])
    o_ref[...] = out.astype(o_ref.dtype)


def _flash(q, kmat, vmat, *, q_maps, k_maps, v_maps, n_q, q_row0, kv_row0, n_kv, n_pairs, bq, bk, rs):
    assert q_row0 % bq == 0 and n_q % bq == 0 and n_kv % bk == 0 and kv_row0 % n_kv == 0 and bq % rs == 0
    qb0, kb0 = q_row0 // bq, kv_row0 // n_kv
    qspec = lambda m: pl.BlockSpec((bq, LANES), lambda j, i: (i + qb0, m(j)))
    kspec = lambda m: pl.BlockSpec((n_kv, LANES), lambda j, i: (kb0, m(j)), pipeline_mode=pl.Buffered(1))
    return pl.pallas_call(
        functools.partial(_flash_kernel, bk=bk, nkv=n_kv // bk, rs=rs),
        grid=(n_pairs, n_q // bq),
        in_specs=[qspec(q_maps[0]), qspec(q_maps[1]), kspec(k_maps[0]), kspec(k_maps[1]),
                  kspec(v_maps[0]), kspec(v_maps[1])],
        out_specs=pl.BlockSpec((bq, LANES), lambda j, i: (i, j)),
        out_shape=jax.ShapeDtypeStruct((n_q, n_pairs * LANES), BF16),
        scratch_shapes=[pltpu.VMEM((2, 2, bq, bk), F32), pltpu.VMEM((2, 2, bq, bk), BF16),
                        pltpu.VMEM((2, 2, bq, LANES), F32), pltpu.VMEM((2, bq, LANES), F32),
                        pltpu.VMEM((2, bq, LANES), F32)],
        compiler_params=_cparams(("parallel", "parallel")),
    )(q, q, kmat, kmat, vmat[0], vmat[1])


def _kv_block(n_kv):
    for bk in (1280, 1024, 512, 256):
        if n_kv % bk == 0:
            return bk
    raise ValueError(f"key count {n_kv} has no supported block")


def _l0_out_kernel(x_ref, mod_ref, of_ref, ob_ref, rg_ref, ao_ref, seg_ref, wo_ref, o_ref, *, tm, n_x, d):
    i = pl.program_id(0)
    seg = seg_ref[...]
    rw = RET_HEADS * HEAD
    acc = _dot(ao_ref[...], wo_ref[rw:, :])
    for g in range(rw // LANES):
        sl = slice(g * LANES, (g + 1) * LANES)
        o = of_ref[:, sl].astype(F32) + ob_ref[:, sl].astype(F32)
        dv = o - _seg_mean(o, seg)
        nrm = dv * lax.rsqrt(_seg_mean(dv * dv, seg) + EPS)
        ra = (nrm * _silu(rg_ref[:, sl].astype(F32))).astype(BF16)
        acc = acc + _dot(ra, wo_ref[g * LANES:(g + 1) * LANES, :])
    o_ref[...] = x_ref[...] + _row_gate(mod_ref, 2, i, tm, n_x, d) * acc


def _l0_out(xa, mod, o_f, o_b, rg, ao, seg, wo, *, tm, n_x):
    t, d = xa.shape
    rw = o_f.shape[1]
    row = lambda i: (i, 0)
    return pl.pallas_call(
        functools.partial(_l0_out_kernel, tm=tm, n_x=n_x, d=d),
        grid=(t // tm,),
        in_specs=[pl.BlockSpec((tm, d), row), _resident(mod.shape)] + [pl.BlockSpec((tm, rw), row)] * 4
                 + [_resident(seg.shape), _resident(wo.shape)],
        out_specs=pl.BlockSpec((tm, d), row),
        out_shape=jax.ShapeDtypeStruct((t, d), F32),
        compiler_params=_cparams(("parallel",)),
    )(xa, mod, o_f, o_b, rg, ao, seg, wo)


def _ffn_kernel(x_ref, mod_ref, g_ref, w1_ref, w3_ref, w2_ref, o_ref, *, tm, n_x, d):
    i = pl.program_id(0)
    x = x_ref[...]
    h = _modulated(x, mod_ref, g_ref, 1, i, tm, n_x, d).astype(BF16)
    a = _dot(h, w1_ref[...])
    u = (_silu(a) * _dot(h, w3_ref[...])).astype(BF16)
    o_ref[...] = x + _row_gate(mod_ref, 5, i, tm, n_x, d) * _dot(u, w2_ref[...])


def _ffn(xa, mod, g, w1, w3, w2, *, tm, n_x):
    t, d = xa.shape
    row = lambda i: (i, 0)
    return pl.pallas_call(
        functools.partial(_ffn_kernel, tm=tm, n_x=n_x, d=d),
        grid=(t // tm,),
        in_specs=[pl.BlockSpec((tm, d), row), _resident(mod.shape), _resident(g.shape),
                  _resident(w1.shape), _resident(w3.shape), _resident(w2.shape)],
        out_specs=pl.BlockSpec((tm, d), row),
        out_shape=jax.ShapeDtypeStruct((t, d), F32),
        compiler_params=_cparams(("parallel",)),
    )(xa, mod, g, w1, w3, w2)


def _l1_proj_kernel(x_ref, mod_ref, g_ref, wq_ref, wkv_ref, wkr_ref, gql_ref, gkvl_ref, wuq_ref, wuk_ref,
                    wuv_ref, seg_ref, gq_ref, gk_ref, gkr_ref, c_ref, s_ref,
                    q_ref, k_ref, vlo_ref, vhi_ref, *, tm, n_x, d):
    i = pl.program_id(0)
    h = _modulated(x_ref[...], mod_ref, g_ref, 0, i, tm, n_x, d).astype(BF16)
    seg = seg_ref[...]
    cos, sin = c_ref[...], s_ref[...]

    def lora_norm(v, g):
        return (v * lax.rsqrt(jnp.mean(v * v, axis=-1, keepdims=True) + EPS) * g).astype(BF16)

    cq = lora_norm(_dot(h, wq_ref[...]), gql_ref[...])
    ckv = lora_norm(_dot(h, wkv_ref[...]), gkvl_ref[...])
    kr = _dot(h, wkr_ref[...])
    kr = kr * lax.rsqrt(_seg_mean(kr * kr, seg) + EPS) * gkr_ref[...]
    kr = _rope128(kr, cos, sin, MLA_ROPE // 2)
    qa = _dot(cq, wuq_ref[...])
    ka = _dot(ckv, wuk_ref[...])
    for hd in range(MLA_HEADS):
        sl = slice(hd * LANES, (hd + 1) * LANES)
        v = qa[:, sl]
        vn = v * lax.rsqrt(_seg_mean(v * v, seg) + EPS) * gq_ref[...]
        q_ref[:, sl] = _rope128(vn, cos, sin, MLA_ROPE // 2).astype(BF16)
        v = ka[:, sl]
        k_ref[:, sl] = (v * lax.rsqrt(_seg_mean(v * v, seg) + EPS) * gk_ref[...] + kr).astype(BF16)
    va = _dot(ckv, wuv_ref[...])
    lane = _lane(va.shape) % LANES
    low = lane < HEAD
    vlo_ref[...] = jnp.where(low, va, jnp.where(lane == LOW_ONE, 1.0, 0.0)).astype(BF16)
    vhi_ref[...] = jnp.where(low, jnp.where(lane == HIGH_ONE, 1.0, 0.0), va).astype(BF16)


def _l1_proj(xa, mod, g, wq, wkv, wkr, gql, gkvl, wuq, wuk, wuv, seg, gq, gk, gkr, cos, sin, *, tm, n_x):
    t, d = xa.shape
    row = lambda i: (i, 0)
    hw = MLA_HEADS * LANES
    vw = MLA_HEADS * HEAD
    consts = (mod, g, wq, wkv, wkr, gql, gkvl, wuq, wuk, wuv, seg, gq, gk, gkr)
    return pl.pallas_call(
        functools.partial(_l1_proj_kernel, tm=tm, n_x=n_x, d=d),
        grid=(t // tm,),
        in_specs=[pl.BlockSpec((tm, d), row)] + [_resident(a.shape) for a in consts]
                 + [pl.BlockSpec((tm, LANES), row)] * 2,
        out_specs=[pl.BlockSpec((tm, hw), row), pl.BlockSpec((tm, hw), row),
                   pl.BlockSpec((tm, vw), row), pl.BlockSpec((tm, vw), row)],
        out_shape=[jax.ShapeDtypeStruct((t, hw), BF16), jax.ShapeDtypeStruct((t, hw), BF16),
                   jax.ShapeDtypeStruct((t, vw), BF16), jax.ShapeDtypeStruct((t, vw), BF16)],
        compiler_params=_cparams(("parallel",)),
    )(xa, *consts, cos, sin)


def _l1_out_kernel(x_ref, mod_ref, g_ref, o_ref, wo_ref, rhi_ref, rlo_ref, x3_ref, h_ref, ei_ref, ew_ref, *, d):
    x3 = x_ref[...] + mod_ref[0:1, 2 * d:3 * d] * _dot(o_ref[...], wo_ref[...])
    x3_ref[...] = x3
    ms = jnp.mean(x3 * x3, axis=-1, keepdims=True)
    h = x3 * lax.rsqrt(ms + EPS) * (g_ref[...] * (1.0 + mod_ref[0:1, 4 * d:5 * d])) + mod_ref[0:1, 3 * d:4 * d]
    hi = h.astype(BF16)
    bits = lax.bitcast_convert_type(hi.astype(F32), jnp.uint32)
    words = (bits[:, :d // 2] >> 16) | (bits[:, d // 2:] & jnp.uint32(0xFFFF0000))
    h_ref[0] = words[:, :d // 4]
    h_ref[1] = words[:, d // 4:]
    lo = (h - hi.astype(F32)).astype(BF16)
    logits = _dot(hi, rhi_ref[...]) + (_dot(hi, rlo_ref[...]) + _dot(lo, rhi_ref[...]))
    lane_i = _lane(logits.shape)
    lane = lane_i.astype(F32)
    logits = jnp.where(lane_i < N_EXPERTS, logits, -jnp.inf)
    v1 = jnp.max(logits, axis=-1, keepdims=True)
    i1 = jnp.min(jnp.where(logits == v1, lane, float(LANES)), axis=-1, keepdims=True)
    rest = jnp.where(lane == i1, -jnp.inf, logits)
    v2 = jnp.max(rest, axis=-1, keepdims=True)
    i2 = jnp.min(jnp.where(rest == v2, lane, float(LANES)), axis=-1, keepdims=True)
    e2 = jnp.exp(v2 - v1)
    den = 1.0 + e2
    ei_ref[...] = jnp.where(lane_i == 0, i1, jnp.where(lane_i == 1, i2, 0.0)).astype(jnp.int32)
    ew_ref[...] = jnp.where(lane_i == 0, 1.0 / den, jnp.where(lane_i == 1, e2 / den, 0.0))


def _l1_out(xa, mod, g, o, wo, rhi, rlo, *, tm):
    n, d = o.shape[0], xa.shape[1]
    row = lambda i: (i, 0)
    return pl.pallas_call(
        functools.partial(_l1_out_kernel, d=d),
        grid=(n // tm,),
        in_specs=[pl.BlockSpec((tm, d), row), _resident(mod.shape), _resident(g.shape),
                  pl.BlockSpec((tm, o.shape[1]), row), _resident(wo.shape), _resident(rhi.shape),
                  _resident(rlo.shape)],
        out_specs=[pl.BlockSpec((tm, d), row), pl.BlockSpec((2, tm, d // 4), lambda i: (0, i, 0)),
                   pl.BlockSpec((tm, LANES), row), pl.BlockSpec((tm, LANES), row)],
        out_shape=[jax.ShapeDtypeStruct((n, d), F32), jax.ShapeDtypeStruct((2, n, d // 4), jnp.uint32),
                   jax.ShapeDtypeStruct((n, LANES), jnp.int32), jax.ShapeDtypeStruct((n, LANES), F32)],
        compiler_params=_cparams(("parallel",)),
    )(xa, mod, g, o, wo, rhi, rlo)


def _moe_kernel(te_ref, nv_ref, x_ref, w1_ref, w3_ref, w2_ref, y_ref, acc_ref, *, nf):
    i, f = pl.program_id(0), pl.program_id(1)

    @pl.when(f == 0)
    def _zero():
        acc_ref[...] = jnp.zeros_like(acc_ref)

    @pl.when(i < nv_ref[0])
    def _compute():
        words = jnp.concatenate([x_ref[0], x_ref[1]], axis=1)
        lo = lax.bitcast_convert_type(words << 16, F32)
        hi = lax.bitcast_convert_type(words & jnp.uint32(0xFFFF0000), F32)
        x = jnp.concatenate([lo, hi], axis=1).astype(BF16)
        a = _dot(x, w1_ref[0])
        u = (_silu(a) * _dot(x, w3_ref[0])).astype(BF16)
        acc_ref[...] += _dot(u, w2_ref[0])

    @pl.when(f == nf - 1)
    def _store():
        y_ref[...] = acc_ref[...].astype(y_ref.dtype)


def _moe(tile_expert, n_valid, xs, w1, w3, w2, *, tm, tf):
    p, d = xs.shape[1], 4 * xs.shape[2]
    fdim = w1.shape[2]
    nf = fdim // tf
    fi = lambda i, f, te, nv: jnp.where(i < nv[0], f, nf - 1)
    grid_spec = pltpu.PrefetchScalarGridSpec(
        num_scalar_prefetch=2,
        grid=(p // tm, nf),
        in_specs=[pl.BlockSpec((2, tm, d // 4), lambda i, f, te, nv: (0, i, 0)),
                  pl.BlockSpec((1, d, tf), lambda i, f, te, nv: (te[i], 0, fi(i, f, te, nv))),
                  pl.BlockSpec((1, d, tf), lambda i, f, te, nv: (te[i], 0, fi(i, f, te, nv))),
                  pl.BlockSpec((1, tf, d), lambda i, f, te, nv: (te[i], fi(i, f, te, nv), 0))],
        out_specs=pl.BlockSpec((tm, d), lambda i, f, te, nv: (i, 0)),
        scratch_shapes=[pltpu.VMEM((tm, d), F32)],
    )
    return pl.pallas_call(
        functools.partial(_moe_kernel, nf=nf),
        grid_spec=grid_spec,
        out_shape=jax.ShapeDtypeStruct((p, d), BF16),
        compiler_params=_cparams(("arbitrary", "arbitrary")),
    )(tile_expert, n_valid, xs, w1, w3, w2)


SC_GATHER_WINDOW = 128


def _gather_rows(x, idx):
    n, d = idx.shape[0], x.shape[1]
    w = SC_GATHER_WINDOW
    assert n % w == 0
    mesh = plsc.VectorSubcoreMesh(core_axis_name="core", subcore_axis_name="subcore")

    @pl.kernel(out_type=jax.ShapeDtypeStruct((n, d), x.dtype), mesh=mesh)
    def gather_kernel(x_hbm, i_hbm, o_hbm):
        def body(i_vmem, o_vmem):
            pltpu.sync_copy(x_hbm.at[i_vmem.at[0]], o_vmem)

        pltpu.emit_pipeline(
            body,
            grid=(n // w,),
            in_specs=[pl.BlockSpec((1, w), lambda i: (0, i))],
            out_specs=[pl.BlockSpec((w, d), lambda i: (i, 0))],
            core_axis_name=("core", "subcore"),
            dimension_semantics=(pltpu.PARALLEL,),
        )(i_hbm, o_hbm)

    return gather_kernel(x, idx.reshape(1, n))


def _combine_kernel(x_ref, mod_ref, ew_ref, ya_ref, yb_ref, o_ref, *, d):
    ew = ew_ref[...]
    y = ew[:, 0:1] * ya_ref[...].astype(F32) + ew[:, 1:2] * yb_ref[...].astype(F32)
    o_ref[...] = x_ref[...] + mod_ref[0:1, 5 * d:6 * d] * y


def _combine(x3, mod, ew, ya, yb, *, tm):
    n, d = x3.shape
    row = lambda i: (i, 0)
    return pl.pallas_call(
        functools.partial(_combine_kernel, d=d),
        grid=(n // tm,),
        in_specs=[pl.BlockSpec((tm, d), row), _resident(mod.shape), pl.BlockSpec((tm, LANES), row),
                  pl.BlockSpec((tm, d), row), pl.BlockSpec((tm, d), row)],
        out_specs=pl.BlockSpec((tm, d), row),
        out_shape=jax.ShapeDtypeStruct((n, d), F32),
        compiler_params=_cparams(("parallel",)),
    )(x3, mod, ew, ya, yb)


def _deinterleave(width):
    return np.concatenate([np.arange(0, width, 2), np.arange(1, width, 2)])


def _rope_tables(n_x, n_ctx, rot_dim, seg_start, seg_repeat):
    rows = n_x // GRID_W
    row = jnp.broadcast_to(jnp.arange(rows)[:, None], (rows, GRID_W)).reshape(n_x).astype(F32)
    col = jnp.broadcast_to(jnp.arange(GRID_W)[None, :], (rows, GRID_W)).reshape(n_x).astype(F32)
    axis_dim = rot_dim // 2
    inv_freq = ROPE_THETA ** (-jnp.arange(0, axis_dim, 2, dtype=F32) / axis_dim)
    ang = jnp.concatenate([row[:, None] * inv_freq, col[:, None] * inv_freq], axis=-1)
    cos, sin = jnp.cos(ang), jnp.sin(ang)
    tail = LANES - seg_start - seg_repeat * rot_dim
    c = jnp.concatenate([jnp.ones((n_x, seg_start), F32)] + [cos, cos] * seg_repeat + [jnp.ones((n_x, tail), F32)],
                        axis=-1)
    s = jnp.concatenate([jnp.zeros((n_x, seg_start), F32)] + [-sin, sin] * seg_repeat
                        + [jnp.zeros((n_x, tail), F32)], axis=-1)
    c = jnp.concatenate([c, jnp.ones((n_ctx, LANES), F32)], axis=0)
    s = jnp.concatenate([s, jnp.zeros((n_ctx, LANES), F32)], axis=0)
    return c, s


def _segment_matrix(bounds):
    m = np.zeros((LANES, LANES), np.float32)
    for lo, hi in bounds:
        m[lo:hi, lo:hi] = 1.0 / (hi - lo)
    return jnp.asarray(m, BF16)


def _token_tile(t):
    for tm in (640, 512, 256, 128):
        if t % tm == 0:
            return tm
    raise ValueError(f"token count {t} has no supported tile")


def kernel(x, c, ctx, c_ctx, l0_ada_w, l0_ada_b, l0_norm1_g, l0_norm2_g, l0_w_in, l0_ret_log_decay, l0_q_norm_g, l0_k_norm_g, l0_w_out, l0_ffn_w1, l0_ffn_w3, l0_ffn_w2, l1_ada_w, l1_ada_b, l1_norm1_g, l1_norm2_g, l1_w_in, l1_q_lora_g, l1_kv_lora_g, l1_w_uq, l1_w_ukv, l1_q_nope_g, l1_q_rope_g, l1_k_nope_g, l1_k_rope_g, l1_w_out, l1_router, l1_exp_w1, l1_exp_w3, l1_exp_w2):
    b, n_x, d = x.shape
    n_ctx = ctx.shape[1]
    assert b == 1 and n_x % 256 == 0 and n_ctx % 256 == 0 and n_x % GRID_W == 0
    t = n_x + n_ctx
    tm = _token_tile(t)
    tmx = _token_tile(n_x)
    xa = jnp.concatenate([x[0], ctx[0]], axis=0)
    row1 = lambda v: v.reshape(1, -1).astype(F32)

    cvec = jnp.zeros((8, d), F32).at[0].set(c[0]).at[1].set(c_ctx)
    mod0 = _ada(cvec, l0_ada_w, l0_ada_b)
    mod1 = _ada(cvec, l1_ada_w, l1_ada_b)

    rw = RET_HEADS * HEAD
    perm = _deinterleave(HEAD)
    n_qk = GQA_HEADS + GQA_KV_HEADS
    qk_cols = l0_w_in[:, 4 * rw:4 * rw + n_qk * HEAD].reshape(d, n_qk, HEAD // 2, 2)
    qk_cols = jnp.swapaxes(qk_cols, 2, 3).reshape(d, n_qk * HEAD)
    w_in0 = jnp.concatenate([l0_w_in[:, :rw], l0_w_in[:, rw:2 * rw] * (HEAD ** -0.5), l0_w_in[:, 2 * rw:4 * rw],
                             qk_cols, l0_w_in[:, 4 * rw + n_qk * HEAD:]], axis=1).astype(BF16)
    seg64 = _segment_matrix([(0, HEAD), (HEAD, 2 * HEAD)])
    gq0 = row1(jnp.tile(l0_q_norm_g[perm], 2) * (HEAD ** -0.5 * LOG2E))
    gk0 = row1(jnp.tile(l0_k_norm_g[perm], 2))
    cos0, sin0 = _rope_tables(n_x, n_ctx, HEAD, 0, 2)

    rq, rk, rv, rg, gq, gkx, gvx = _l0_proj(xa, mod0, row1(l0_norm1_g), w_in0, seg64, gq0, gk0, cos0, sin0,
                                             tm=tm, n_x=n_x)
    o_f, o_b = _retention(l0_ret_log_decay.astype(F32), rq, rk, rv, n_x=n_x)

    gqa_maps = dict(q_maps=(lambda j: j, lambda j: j),
                    k_maps=(lambda j: 2 * (j // 2), lambda j: 2 * (j // 2) + 1),
                    v_maps=(lambda j: 2 * (j // 2), lambda j: 2 * (j // 2) + 1), n_pairs=GQA_HEADS // 2)
    bq = 512 if n_x % 512 == 0 else 256
    ao_x = _flash(gq, gkx, (gvx, gvx), n_q=n_x, q_row0=0, kv_row0=0, n_kv=t, bq=bq, bk=_kv_block(t), rs=32,
                  **gqa_maps)
    ao_c = _flash(gq, gkx, (gvx, gvx), n_q=n_ctx, q_row0=n_x, kv_row0=n_x, n_kv=n_ctx, bq=n_ctx,
                  bk=_kv_block(n_ctx), rs=32, **gqa_maps)
    ao = jnp.concatenate([ao_x, ao_c], axis=0)

    xa = _l0_out(xa, mod0, o_f, o_b, rg, ao, seg64, l0_w_out.astype(BF16), tm=tm, n_x=n_x)
    xa = _ffn(xa, mod0, row1(l0_norm2_g), l0_ffn_w1.astype(BF16), l0_ffn_w3.astype(BF16),
              l0_ffn_w2.astype(BF16), tm=tm, n_x=n_x)

    rperm = _deinterleave(MLA_ROPE)
    qk_w = MLA_NOPE + MLA_ROPE
    wuq = jnp.zeros((MLA_Q_RANK, MLA_HEADS * LANES), F32)
    wuk = jnp.zeros((MLA_KV_RANK, MLA_HEADS * LANES), F32)
    wuv = []
    for hd in range(MLA_HEADS):
        src = l1_w_uq[:, hd * qk_w:(hd + 1) * qk_w]
        wuq = wuq.at[:, hd * LANES:hd * LANES + MLA_NOPE].set(src[:, :MLA_NOPE])
        wuq = wuq.at[:, hd * LANES + MLA_NOPE:hd * LANES + qk_w].set(src[:, MLA_NOPE:][:, rperm])
        kvsrc = l1_w_ukv[:, hd * 2 * HEAD:(hd + 1) * 2 * HEAD]
        wuk = wuk.at[:, hd * LANES:hd * LANES + MLA_NOPE].set(kvsrc[:, :MLA_NOPE])
        wuv.append(kvsrc[:, MLA_NOPE:])
    wuv = jnp.concatenate(wuv, axis=1)
    wkr = jnp.zeros((d, LANES), F32).at[:, MLA_NOPE:qk_w].set(l1_w_in[:, MLA_Q_RANK + MLA_KV_RANK:][:, rperm])
    pad = jnp.zeros((LANES - qk_w,), F32)
    zn = jnp.zeros((MLA_NOPE,), F32)
    gq1 = row1(jnp.concatenate([l1_q_nope_g, l1_q_rope_g[rperm], pad]) * (qk_w ** -0.5 * LOG2E))
    gk1 = row1(jnp.concatenate([l1_k_nope_g, jnp.zeros((LANES - MLA_NOPE,), F32)]))
    gkr1 = row1(jnp.concatenate([zn, l1_k_rope_g[rperm], pad]))
    seg_mla = _segment_matrix([(0, MLA_NOPE), (MLA_NOPE, qk_w)])
    cos1, sin1 = _rope_tables(n_x, n_ctx, MLA_ROPE, MLA_NOPE, 1)

    mq, mk, mvlo, mvhi = _l1_proj(
        xa, mod1, row1(l1_norm1_g), l1_w_in[:, :MLA_Q_RANK].astype(BF16),
        l1_w_in[:, MLA_Q_RANK:MLA_Q_RANK + MLA_KV_RANK].astype(BF16), wkr.astype(BF16),
        row1(l1_q_lora_g), row1(l1_kv_lora_g), wuq.astype(BF16), wuk.astype(BF16), wuv.astype(BF16),
        seg_mla, gq1, gk1, gkr1, cos1, sin1, tm=tm, n_x=n_x)
    mo = _flash(mq, mk, (mvlo, mvhi), q_maps=(lambda j: 2 * j, lambda j: 2 * j + 1),
                k_maps=(lambda j: 2 * j, lambda j: 2 * j + 1), v_maps=(lambda j: j, lambda j: j),
                n_pairs=MLA_HEADS // 2, n_q=n_x, q_row0=0, kv_row0=0, n_kv=t, bq=bq, bk=_kv_block(t), rs=32)

    router = jnp.zeros((d, LANES), F32).at[:, :N_EXPERTS].set(l1_router)
    r_hi = router.astype(BF16)
    r_lo = (router - r_hi.astype(F32)).astype(BF16)
    x3, hmoe, ei, ew = _l1_out(xa, mod1, row1(l1_norm2_g), mo, l1_w_out.astype(BF16), r_hi, r_lo, tm=tmx)

    tme = 512 if n_x >= 4096 else 128
    e_flat = ei[:, :TOP_K].reshape(-1)
    onehot = (e_flat[:, None] == jnp.arange(N_EXPERTS)[None, :]).astype(jnp.int32)
    csum = jnp.cumsum(onehot, axis=0)
    rank = jnp.sum((csum - onehot) * onehot, axis=1)
    counts = csum[-1]
    padded = ((counts + tme - 1) // tme) * tme
    ends = jnp.cumsum(padded)
    pos = (ends - padded)[e_flat] + rank
    n_tiles = (TOP_K * n_x) // tme + N_EXPERTS
    p_rows = n_tiles * tme
    src = jnp.zeros((p_rows,), jnp.int32).at[pos].set(jnp.arange(TOP_K * n_x, dtype=jnp.int32) // TOP_K)
    n_valid = (ends[-1] // tme).astype(jnp.int32)
    tile_ids = jnp.minimum(jnp.arange(n_tiles, dtype=jnp.int32), n_valid - 1)
    tile_expert = jnp.sum((ends[None, :] <= (tile_ids * tme)[:, None]).astype(jnp.int32), axis=1)
    tile_expert = jnp.minimum(tile_expert, N_EXPERTS - 1)
    idx2 = jnp.concatenate([src, src + n_x])
    hm2 = hmoe.reshape(2 * n_x, d // 4)
    n_ch = 6 if (2 * p_rows) % 6 == 0 else 1
    xs = jnp.concatenate([jnp.take(hm2, i_, axis=0, mode="clip") for i_ in jnp.split(idx2, n_ch)], axis=0)
    xs = xs.reshape(2, p_rows, d // 4)
    fdim = l1_exp_w1.shape[2]
    tf = fdim // 2 if (fdim // 2) % LANES == 0 else fdim
    ys = _moe(tile_expert, n_valid.reshape(1), xs, l1_exp_w1.astype(BF16), l1_exp_w3.astype(BF16),
              l1_exp_w2.astype(BF16), tm=tme, tf=tf)
    pos2 = pos.reshape(n_x, TOP_K)
    out = _combine(x3, mod1, ew, jnp.take(ys, pos2[:, 0], axis=0, mode="clip"),
                   jnp.take(ys, pos2[:, 1], axis=0, mode="clip"), tm=tmx)
    return out[None]
```

```python
import functools
import math

import numpy as np
import jax
import jax.numpy as jnp
from jax import lax
from jax.experimental import pallas as pl
from jax.experimental.pallas import tpu as pltpu
from jax.experimental.pallas import tpu_sc as plsc

F32 = jnp.float32
BF16 = jnp.bfloat16

EPS = 1e-6
ROPE_THETA = 10000.0
GRID_W = 64
LANES = 128
HEAD = 64
RET_CHUNK = 128
RET_HEADS = 8
GQA_HEADS = 8
GQA_KV_HEADS = 2
MLA_HEADS = 8
MLA_Q_RANK = 384
MLA_KV_RANK = 256
MLA_NOPE = 64
MLA_ROPE = 32
N_EXPERTS = 8
TOP_K = 2
LOW_ONE = HEAD
HIGH_ONE = 0
LOG2E = math.log2(math.e)
VMEM_LIMIT = 56 * 1024 * 1024


def _cparams(sem, vmem=VMEM_LIMIT):
    return pltpu.CompilerParams(dimension_semantics=sem, vmem_limit_bytes=vmem)


def _resident(shape):
    nd = len(shape)
    return pl.BlockSpec(shape, lambda *_: (0,) * nd, pipeline_mode=pl.Buffered(1))


def _dot(a, b):
    return jnp.dot(a, b, preferred_element_type=F32)


def _dot_nt(a, b):
    return lax.dot_general(a, b, (((1,), (1,)), ((), ())), preferred_element_type=F32)


def _seg_mean(v, seg):
    hi = v.astype(BF16)
    lo = (v - hi.astype(F32)).astype(BF16)
    return _dot(hi, seg) + _dot(lo, seg)


def _silu(x):
    return x * jax.nn.sigmoid(x)


def _modulated(x, mod_ref, g_ref, which, tile, tm, n_x, d):
    ms = jnp.mean(x * x, axis=-1, keepdims=True)
    xn = x * lax.rsqrt(ms + EPS)
    g = g_ref[...]
    sh, sc = 3 * which, 3 * which + 1
    a_x = g * (1.0 + mod_ref[0:1, sc * d:(sc + 1) * d])
    a_c = g * (1.0 + mod_ref[1:2, sc * d:(sc + 1) * d])
    b_x = mod_ref[0:1, sh * d:(sh + 1) * d]
    b_c = mod_ref[1:2, sh * d:(sh + 1) * d]
    row = tile * tm + lax.broadcasted_iota(jnp.int32, (tm, 1), 0)
    is_ctx = row >= n_x
    return xn * jnp.where(is_ctx, a_c, a_x) + jnp.where(is_ctx, b_c, b_x)


def _row_gate(mod_ref, idx, tile, tm, n_x, d):
    row = tile * tm + lax.broadcasted_iota(jnp.int32, (tm, 1), 0)
    return jnp.where(row >= n_x, mod_ref[1:2, idx * d:(idx + 1) * d], mod_ref[0:1, idx * d:(idx + 1) * d])


def _lane(shape):
    return lax.broadcasted_iota(jnp.int32, shape, len(shape) - 1)


def _ada_kernel(c_ref, w_ref, b_ref, o_ref):
    c = c_ref[...]
    o_ref[...] = jnp.dot(_silu(c), w_ref[...], preferred_element_type=F32,
                         precision=lax.Precision.HIGHEST) + b_ref[...]


def _ada(cvec8, w, b):
    d, n = w.shape
    tn = n // 4
    return pl.pallas_call(
        _ada_kernel,
        grid=(n // tn,),
        in_specs=[pl.BlockSpec((8, d), lambda j: (0, 0)),
                  pl.BlockSpec((d, tn), lambda j: (0, j)),
                  pl.BlockSpec((1, tn), lambda j: (0, j))],
        out_specs=pl.BlockSpec((8, tn), lambda j: (0, j)),
        out_shape=jax.ShapeDtypeStruct((8, n), F32),
        compiler_params=_cparams(("arbitrary",)),
    )(cvec8, w, b.reshape(1, n))


def _rope128(v, c, s, half):
    lane = _lane(v.shape)
    swapped = jnp.where(lane % (2 * half) < half, pltpu.roll(v, LANES - half, 1), pltpu.roll(v, half, 1))
    return v * c + swapped * s


def _l0_proj_kernel(x_ref, mod_ref, g_ref, w_ref, seg_ref, gq_ref, gk_ref, c_ref, s_ref,
                    rq_ref, rk_ref, rv_ref, rg_ref, q_ref, k_ref, v_ref, *, tm, n_x, d):
    i = pl.program_id(0)
    h = _modulated(x_ref[...], mod_ref, g_ref, 0, i, tm, n_x, d).astype(BF16)
    rw = RET_HEADS * HEAD
    for idx, ref in enumerate((rq_ref, rk_ref, rv_ref, rg_ref)):
        ref[...] = _dot(h, w_ref[:, idx * rw:(idx + 1) * rw]).astype(BF16)
    seg = seg_ref[...]
    cos, sin = c_ref[...], s_ref[...]
    base = 4 * rw
    qw = GQA_HEADS * HEAD
    qa = _dot(h, w_ref[:, base:base + qw])
    for g in range(qw // LANES):
        v = qa[:, g * LANES:(g + 1) * LANES]
        vn = v * lax.rsqrt(_seg_mean(v * v, seg) + EPS) * gq_ref[...]
        q_ref[:, g * LANES:(g + 1) * LANES] = _rope128(vn, cos, sin, HEAD // 2).astype(BF16)
    kv = _dot(h, w_ref[:, base + qw:base + qw + 2 * LANES])
    kk = kv[:, :LANES]
    kk = kk * lax.rsqrt(_seg_mean(kk * kk, seg) + EPS) * gk_ref[...]
    kk = _rope128(kk, cos, sin, HEAD // 2)
    vv = kv[:, LANES:]
    lane = _lane(kk.shape)
    low = lane < HEAD
    for src, ref, one in ((kk, k_ref, 0.0), (vv, v_ref, 1.0)):
        sw = pltpu.roll(src, HEAD, 1)
        lo_fill = jnp.where(lane == LOW_ONE, one, 0.0)
        hi_fill = jnp.where(lane == HIGH_ONE, one, 0.0)
        ref[:, 0 * LANES:1 * LANES] = jnp.where(low, src, lo_fill).astype(BF16)
        ref[:, 1 * LANES:2 * LANES] = jnp.where(low, hi_fill, sw).astype(BF16)
        ref[:, 2 * LANES:3 * LANES] = jnp.where(low, sw, lo_fill).astype(BF16)
        ref[:, 3 * LANES:4 * LANES] = jnp.where(low, hi_fill, src).astype(BF16)


def _l0_proj(xa, mod, g, w, seg, gq, gk, cos, sin, *, tm, n_x):
    t, d = xa.shape
    rw = RET_HEADS * HEAD
    row = lambda i: (i, 0)
    outs = [jax.ShapeDtypeStruct((t, rw), BF16)] * 7
    return pl.pallas_call(
        functools.partial(_l0_proj_kernel, tm=tm, n_x=n_x, d=d),
        grid=(t // tm,),
        in_specs=[pl.BlockSpec((tm, d), row), _resident(mod.shape), _resident(g.shape), _resident(w.shape),
                  _resident(seg.shape), _resident(gq.shape), _resident(gk.shape),
                  pl.BlockSpec((tm, LANES), row), pl.BlockSpec((tm, LANES), row)],
        out_specs=[pl.BlockSpec((tm, rw), row)] * 7,
        out_shape=outs,
        compiler_params=_cparams(("parallel",)),
    )(xa, mod, g, w, seg, gq, gk, cos, sin)


def _retention_kernel(lg_ref, qf_ref, kf_ref, vf_ref, qb_ref, kb_ref, vb_ref, of_ref, ob_ref,
                      state_ref, decay_ref, xi_ref, zeta_ref, gl_ref):
    c = RET_CHUNK
    npairs = RET_HEADS * HEAD // LANES
    step = pl.program_id(0)

    @pl.when(step == 0)
    def _init():
        state_ref[...] = jnp.zeros_like(state_ref)
        ci = lax.broadcasted_iota(jnp.int32, (c, c), 0).astype(F32)
        mi = lax.broadcasted_iota(jnp.int32, (c, c), 1).astype(F32)
        pos = lax.broadcasted_iota(jnp.int32, (c, RET_HEADS * HEAD), 0).astype(F32)
        lane_head = _lane((1, RET_HEADS * HEAD)) // HEAD
        for dr in range(2):
            lgv = jnp.zeros((1, RET_HEADS * HEAD), F32)
            for hd in range(RET_HEADS):
                lg = lg_ref[dr, hd]
                rel = (ci - mi) if dr == 0 else (mi - ci)
                decay_ref[dr, hd] = jnp.where(rel >= 0, jnp.exp(jnp.maximum(rel, 0.0) * lg), 0.0)
                lgv = jnp.where(lane_head == hd, lg, lgv)
            p = pos if dr == 0 else (c - 1.0 - pos)
            xi_ref[dr] = jnp.exp((p + 1.0) * lgv)
            zeta_ref[dr] = jnp.exp((c - 1.0 - p) * lgv)
            gl_ref[dr] = jnp.exp(float(c) * lgv)

    low = _lane((c, LANES)) < HEAD
    r_i = lax.broadcasted_iota(jnp.int32, (LANES, LANES), 0) // HEAD
    c_i = lax.broadcasted_iota(jnp.int32, (LANES, LANES), 1) // HEAD
    blockdiag = r_i == c_i
    for dr, (q_ref, k_ref, v_ref, o_ref) in enumerate(((qf_ref, kf_ref, vf_ref, of_ref),
                                                       (qb_ref, kb_ref, vb_ref, ob_ref))):
        for j in range(npairs):
            sl = slice(j * LANES, (j + 1) * LANES)
            q, k, v = q_ref[:, sl], k_ref[:, sl], v_ref[:, sl]
            zero = jnp.zeros_like(q)
            s0 = _dot_nt(jnp.where(low, q, zero), k) * decay_ref[dr, 2 * j]
            s1 = _dot_nt(jnp.where(low, zero, q), k) * decay_ref[dr, 2 * j + 1]
            o = _dot(s0.astype(BF16), jnp.where(low, v, zero)) + _dot(s1.astype(BF16), jnp.where(low, zero, v))
            st = state_ref[dr, j]
            qx = (q.astype(F32) * xi_ref[dr, :, sl]).astype(BF16)
            o = o + _dot(qx, st.astype(BF16))
            o_ref[:, sl] = o.astype(BF16)
            kz = (k.astype(F32) * zeta_ref[dr, :, sl]).T.astype(BF16)
            u = _dot(kz, v)
            state_ref[dr, j] = st * gl_ref[dr, :, sl] + jnp.where(blockdiag, u, 0.0)


def _retention(lg, rq, rk, rv, *, n_x):
    t, w = rq.shape
    c = RET_CHUNK
    nc, ncx = t // c, n_x // c
    fwd = lambda i: ((i + ncx) % nc, 0)
    bwd = lambda i: (nc - 1 - i, 0)
    blk = lambda m: pl.BlockSpec((c, w), m)
    npairs = w // LANES
    return pl.pallas_call(
        _retention_kernel,
        grid=(nc,),
        in_specs=[pl.BlockSpec(memory_space=pltpu.SMEM)] + [blk(fwd)] * 3 + [blk(bwd)] * 3,
        out_specs=[blk(fwd), blk(bwd)],
        out_shape=[jax.ShapeDtypeStruct((t, w), BF16)] * 2,
        scratch_shapes=[pltpu.VMEM((2, npairs, LANES, LANES), F32),
                        pltpu.VMEM((2, RET_HEADS, c, c), F32),
                        pltpu.VMEM((2, c, w), F32), pltpu.VMEM((2, c, w), F32),
                        pltpu.VMEM((2, 1, w), F32)],
        compiler_params=_cparams(("arbitrary",)),
    )(lg, rq, rk, rv, rq, rk, rv)


def _flash_kernel(q0_ref, q1_ref, k0_ref, k1_ref, v0_ref, v1_ref, o_ref,
                  s_ref, p_ref, a_ref, m_ref, acc_ref, *, bk, nkv, rs):
    bq = q0_ref.shape[0]
    q_refs, k_refs, v_refs = (q0_ref, q1_ref), (k0_ref, k1_ref), (v0_ref, v1_ref)
    m_ref[...] = jnp.full(m_ref.shape, -jnp.inf, F32)
    acc_ref[...] = jnp.zeros(acc_ref.shape, F32)

    def keys(t):
        return pl.ds(t * bk if isinstance(t, int) else pl.multiple_of(t * bk, bk), bk)

    def scores(t, slot):
        for h in range(2):
            s_ref[slot, h] = _dot_nt(q_refs[h][...], k_refs[h][keys(t), :])

    def softmax(slot):
        col = lambda c: slice(c * LANES, (c + 1) * LANES)
        for r in range(bq // rs):
            rows = slice(r * rs, (r + 1) * rs)
            for h in range(2):
                mx = s_ref[slot, h, rows, col(0)]
                for c in range(1, bk // LANES):
                    mx = jnp.maximum(mx, s_ref[slot, h, rows, col(c)])
                m_old = m_ref[h, rows, :]
                m_new = jnp.maximum(m_old, jnp.max(mx, axis=-1, keepdims=True))
                a_ref[slot, h, rows, :] = jnp.exp2(m_old - m_new)
                m_ref[h, rows, :] = m_new
                for c in range(bk // LANES):
                    p_ref[slot, h, rows, col(c)] = jnp.exp2(s_ref[slot, h, rows, col(c)] - m_new).astype(BF16)

    def values(t, slot):
        for h in range(2):
            acc_ref[h] = acc_ref[h] * a_ref[slot, h] + _dot(p_ref[slot, h], v_refs[h][keys(t), :])

    scores(0, 0)

    def body(i, carry):
        t = 2 * i
        scores(t + 1, 1)
        softmax(0)
        values(t, 0)
        scores(t + 2, 0)
        softmax(1)
        values(t + 1, 1)
        return carry

    n_loop = (nkv - 1) // 2
    lax.fori_loop(0, n_loop, body, 0)
    last = 2 * n_loop
    if last + 1 < nkv:
        scores(last + 1, 1)
    softmax(0)
    values(last, 0)
    if last + 1 < nkv:
        softmax(1)
        values(last + 1, 1)
    low = _lane((bq, LANES)) < HEAD
    acc0, acc1 = acc_ref[0], acc_ref[1]
    out = jnp.where(low, acc0 / acc0[:, LOW_ONE:LOW_ONE + 1], acc1 / acc1[:, HIGH_ONE:HIGH_ONE + 1])
    o_ref[...] = out.astype(o_ref.dtype)


def _flash(q, kmat, vmat, *, q_maps, k_maps, v_maps, n_q, q_row0, kv_row0, n_kv, n_pairs, bq, bk, rs):
    assert q_row0 % bq == 0 and n_q % bq == 0 and n_kv % bk == 0 and kv_row0 % n_kv == 0 and bq % rs == 0
    qb0, kb0 = q_row0 // bq, kv_row0 // n_kv
    qspec = lambda m: pl.BlockSpec((bq, LANES), lambda j, i: (i + qb0, m(j)))
    kspec = lambda m: pl.BlockSpec((n_kv, LANES), lambda j, i: (kb0, m(j)), pipeline_mode=pl.Buffered(1))
    return pl.pallas_call(
        functools.partial(_flash_kernel, bk=bk, nkv=n_kv // bk, rs=rs),
        grid=(n_pairs, n_q // bq),
        in_specs=[qspec(q_maps[0]), qspec(q_maps[1]), kspec(k_maps[0]), kspec(k_maps[1]),
                  kspec(v_maps[0]), kspec(v_maps[1])],
        out_specs=pl.BlockSpec((bq, LANES), lambda j, i: (i, j)),
        out_shape=jax.ShapeDtypeStruct((n_q, n_pairs * LANES), BF16),
        scratch_shapes=[pltpu.VMEM((2, 2, bq, bk), F32), pltpu.VMEM((2, 2, bq, bk), BF16),
                        pltpu.VMEM((2, 2, bq, LANES), F32), pltpu.VMEM((2, bq, LANES), F32),
                        pltpu.VMEM((2, bq, LANES), F32)],
        compiler_params=_cparams(("parallel", "parallel")),
    )(q, q, kmat, kmat, vmat[0], vmat[1])


def _kv_block(n_kv):
    for bk in (1280, 1024, 512, 256):
        if n_kv % bk == 0:
            return bk
    raise ValueError(f"key count {n_kv} has no supported block")


def _l0_out_kernel(x_ref, mod_ref, of_ref, ob_ref, rg_ref, ao_ref, seg_ref, wo_ref, o_ref, *, tm, n_x, d):
    i = pl.program_id(0)
    seg = seg_ref[...]
    rw = RET_HEADS * HEAD
    acc = _dot(ao_ref[...], wo_ref[rw:, :])
    for g in range(rw // LANES):
        sl = slice(g * LANES, (g + 1) * LANES)
        o = of_ref[:, sl].astype(F32) + ob_ref[:, sl].astype(F32)
        dv = o - _seg_mean(o, seg)
        nrm = dv * lax.rsqrt(_seg_mean(dv * dv, seg) + EPS)
        ra = (nrm * _silu(rg_ref[:, sl].astype(F32))).astype(BF16)
        acc = acc + _dot(ra, wo_ref[g * LANES:(g + 1) * LANES, :])
    o_ref[...] = x_ref[...] + _row_gate(mod_ref, 2, i, tm, n_x, d) * acc


def _l0_out(xa, mod, o_f, o_b, rg, ao, seg, wo, *, tm, n_x):
    t, d = xa.shape
    rw = o_f.shape[1]
    row = lambda i: (i, 0)
    return pl.pallas_call(
        functools.partial(_l0_out_kernel, tm=tm, n_x=n_x, d=d),
        grid=(t // tm,),
        in_specs=[pl.BlockSpec((tm, d), row), _resident(mod.shape)] + [pl.BlockSpec((tm, rw), row)] * 4
                 + [_resident(seg.shape), _resident(wo.shape)],
        out_specs=pl.BlockSpec((tm, d), row),
        out_shape=jax.ShapeDtypeStruct((t, d), F32),
        compiler_params=_cparams(("parallel",)),
    )(xa, mod, o_f, o_b, rg, ao, seg, wo)


def _ffn_kernel(x_ref, mod_ref, g_ref, w1_ref, w3_ref, w2_ref, o_ref, *, tm, n_x, d):
    i = pl.program_id(0)
    x = x_ref[...]
    h = _modulated(x, mod_ref, g_ref, 1, i, tm, n_x, d).astype(BF16)
    a = _dot(h, w1_ref[...])
    u = (_silu(a) * _dot(h, w3_ref[...])).astype(BF16)
    o_ref[...] = x + _row_gate(mod_ref, 5, i, tm, n_x, d) * _dot(u, w2_ref[...])


def _ffn(xa, mod, g, w1, w3, w2, *, tm, n_x):
    t, d = xa.shape
    row = lambda i: (i, 0)
    return pl.pallas_call(
        functools.partial(_ffn_kernel, tm=tm, n_x=n_x, d=d),
        grid=(t // tm,),
        in_specs=[pl.BlockSpec((tm, d), row), _resident(mod.shape), _resident(g.shape),
                  _resident(w1.shape), _resident(w3.shape), _resident(w2.shape)],
        out_specs=pl.BlockSpec((tm, d), row),
        out_shape=jax.ShapeDtypeStruct((t, d), F32),
        compiler_params=_cparams(("parallel",)),
    )(xa, mod, g, w1, w3, w2)


def _l1_proj_kernel(x_ref, mod_ref, g_ref, wq_ref, wkv_ref, wkr_ref, gql_ref, gkvl_ref, wuq_ref, wuk_ref,
                    wuv_ref, seg_ref, gq_ref, gk_ref, gkr_ref, c_ref, s_ref,
                    q_ref, k_ref, vlo_ref, vhi_ref, *, tm, n_x, d):
    i = pl.program_id(0)
    h = _modulated(x_ref[...], mod_ref, g_ref, 0, i, tm, n_x, d).astype(BF16)
    seg = seg_ref[...]
    cos, sin = c_ref[...], s_ref[...]

    def lora_norm(v, g):
        return (v * lax.rsqrt(jnp.mean(v * v, axis=-1, keepdims=True) + EPS) * g).astype(BF16)

    cq = lora_norm(_dot(h, wq_ref[...]), gql_ref[...])
    ckv = lora_norm(_dot(h, wkv_ref[...]), gkvl_ref[...])
    kr = _dot(h, wkr_ref[...])
    kr = kr * lax.rsqrt(_seg_mean(kr * kr, seg) + EPS) * gkr_ref[...]
    kr = _rope128(kr, cos, sin, MLA_ROPE // 2)
    qa = _dot(cq, wuq_ref[...])
    ka = _dot(ckv, wuk_ref[...])
    for hd in range(MLA_HEADS):
        sl = slice(hd * LANES, (hd + 1) * LANES)
        v = qa[:, sl]
        vn = v * lax.rsqrt(_seg_mean(v * v, seg) + EPS) * gq_ref[...]
        q_ref[:, sl] = _rope128(vn, cos, sin, MLA_ROPE // 2).astype(BF16)
        v = ka[:, sl]
        k_ref[:, sl] = (v * lax.rsqrt(_seg_mean(v * v, seg) + EPS) * gk_ref[...] + kr).astype(BF16)
    va = _dot(ckv, wuv_ref[...])
    lane = _lane(va.shape) % LANES
    low = lane < HEAD
    vlo_ref[...] = jnp.where(low, va, jnp.where(lane == LOW_ONE, 1.0, 0.0)).astype(BF16)
    vhi_ref[...] = jnp.where(low, jnp.where(lane == HIGH_ONE, 1.0, 0.0), va).astype(BF16)


def _l1_proj(xa, mod, g, wq, wkv, wkr, gql, gkvl, wuq, wuk, wuv, seg, gq, gk, gkr, cos, sin, *, tm, n_x):
    t, d = xa.shape
    row = lambda i: (i, 0)
    hw = MLA_HEADS * LANES
    vw = MLA_HEADS * HEAD
    consts = (mod, g, wq, wkv, wkr, gql, gkvl, wuq, wuk, wuv, seg, gq, gk, gkr)
    return pl.pallas_call(
        functools.partial(_l1_proj_kernel, tm=tm, n_x=n_x, d=d),
        grid=(t // tm,),
        in_specs=[pl.BlockSpec((tm, d), row)] + [_resident(a.shape) for a in consts]
                 + [pl.BlockSpec((tm, LANES), row)] * 2,
        out_specs=[pl.BlockSpec((tm, hw), row), pl.BlockSpec((tm, hw), row),
                   pl.BlockSpec((tm, vw), row), pl.BlockSpec((tm, vw), row)],
        out_shape=[jax.ShapeDtypeStruct((t, hw), BF16), jax.ShapeDtypeStruct((t, hw), BF16),
                   jax.ShapeDtypeStruct((t, vw), BF16), jax.ShapeDtypeStruct((t, vw), BF16)],
        compiler_params=_cparams(("parallel",)),
    )(xa, *consts, cos, sin)


def _l1_out_kernel(x_ref, mod_ref, g_ref, o_ref, wo_ref, rhi_ref, rlo_ref, x3_ref, h_ref, ei_ref, ew_ref, *, d):
    x3 = x_ref[...] + mod_ref[0:1, 2 * d:3 * d] * _dot(o_ref[...], wo_ref[...])
    x3_ref[...] = x3
    ms = jnp.mean(x3 * x3, axis=-1, keepdims=True)
    h = x3 * lax.rsqrt(ms + EPS) * (g_ref[...] * (1.0 + mod_ref[0:1, 4 * d:5 * d])) + mod_ref[0:1, 3 * d:4 * d]
    hi = h.astype(BF16)
    bits = lax.bitcast_convert_type(hi.astype(F32), jnp.uint32)
    words = (bits[:, :d // 2] >> 16) | (bits[:, d // 2:] & jnp.uint32(0xFFFF0000))
    h_ref[0] = words[:, :d // 4]
    h_ref[1] = words[:, d // 4:]
    lo = (h - hi.astype(F32)).astype(BF16)
    logits = _dot(hi, rhi_ref[...]) + (_dot(hi, rlo_ref[...]) + _dot(lo, rhi_ref[...]))
    lane_i = _lane(logits.shape)
    lane = lane_i.astype(F32)
    logits = jnp.where(lane_i < N_EXPERTS, logits, -jnp.inf)
    v1 = jnp.max(logits, axis=-1, keepdims=True)
    i1 = jnp.min(jnp.where(logits == v1, lane, float(LANES)), axis=-1, keepdims=True)
    rest = jnp.where(lane == i1, -jnp.inf, logits)
    v2 = jnp.max(rest, axis=-1, keepdims=True)
    i2 = jnp.min(jnp.where(rest == v2, lane, float(LANES)), axis=-1, keepdims=True)
    e2 = jnp.exp(v2 - v1)
    den = 1.0 + e2
    ei_ref[...] = jnp.where(lane_i == 0, i1, jnp.where(lane_i == 1, i2, 0.0)).astype(jnp.int32)
    ew_ref[...] = jnp.where(lane_i == 0, 1.0 / den, jnp.where(lane_i == 1, e2 / den, 0.0))


def _l1_out(xa, mod, g, o, wo, rhi, rlo, *, tm):
    n, d = o.shape[0], xa.shape[1]
    row = lambda i: (i, 0)
    return pl.pallas_call(
        functools.partial(_l1_out_kernel, d=d),
        grid=(n // tm,),
        in_specs=[pl.BlockSpec((tm, d), row), _resident(mod.shape), _resident(g.shape),
                  pl.BlockSpec((tm, o.shape[1]), row), _resident(wo.shape), _resident(rhi.shape),
                  _resident(rlo.shape)],
        out_specs=[pl.BlockSpec((tm, d), row), pl.BlockSpec((2, tm, d // 4), lambda i: (0, i, 0)),
                   pl.BlockSpec((tm, LANES), row), pl.BlockSpec((tm, LANES), row)],
        out_shape=[jax.ShapeDtypeStruct((n, d), F32), jax.ShapeDtypeStruct((2, n, d // 4), jnp.uint32),
                   jax.ShapeDtypeStruct((n, LANES), jnp.int32), jax.ShapeDtypeStruct((n, LANES), F32)],
        compiler_params=_cparams(("parallel",)),
    )(xa, mod, g, o, wo, rhi, rlo)


def _moe_kernel(te_ref, nv_ref, x_ref, w1_ref, w3_ref, w2_ref, y_ref, acc_ref, *, nf):
    i, f = pl.program_id(0), pl.program_id(1)

    @pl.when(f == 0)
    def _zero():
        acc_ref[...] = jnp.zeros_like(acc_ref)

    @pl.when(i < nv_ref[0])
    def _compute():
        words = jnp.concatenate([x_ref[0], x_ref[1]], axis=1)
        lo = lax.bitcast_convert_type(words << 16, F32)
        hi = lax.bitcast_convert_type(words & jnp.uint32(0xFFFF0000), F32)
        x = jnp.concatenate([lo, hi], axis=1).astype(BF16)
        a = _dot(x, w1_ref[0])
        u = (_silu(a) * _dot(x, w3_ref[0])).astype(BF16)
        acc_ref[...] += _dot(u, w2_ref[0])

    @pl.when(f == nf - 1)
    def _store():
        y_ref[...] = acc_ref[...].astype(y_ref.dtype)


def _moe(tile_expert, n_valid, xs, w1, w3, w2, *, tm, tf):
    p, d = xs.shape[1], 4 * xs.shape[2]
    fdim = w1.shape[2]
    nf = fdim // tf
    fi = lambda i, f, te, nv: jnp.where(i < nv[0], f, nf - 1)
    grid_spec = pltpu.PrefetchScalarGridSpec(
        num_scalar_prefetch=2,
        grid=(p // tm, nf),
        in_specs=[pl.BlockSpec((2, tm, d // 4), lambda i, f, te, nv: (0, i, 0)),
                  pl.BlockSpec((1, d, tf), lambda i, f, te, nv: (te[i], 0, fi(i, f, te, nv))),
                  pl.BlockSpec((1, d, tf), lambda i, f, te, nv: (te[i], 0, fi(i, f, te, nv))),
                  pl.BlockSpec((1, tf, d), lambda i, f, te, nv: (te[i], fi(i, f, te, nv), 0))],
        out_specs=pl.BlockSpec((tm, d), lambda i, f, te, nv: (i, 0)),
        scratch_shapes=[pltpu.VMEM((tm, d), F32)],
    )
    return pl.pallas_call(
        functools.partial(_moe_kernel, nf=nf),
        grid_spec=grid_spec,
        out_shape=jax.ShapeDtypeStruct((p, d), BF16),
        compiler_params=_cparams(("arbitrary", "arbitrary")),
    )(tile_expert, n_valid, xs, w1, w3, w2)


SC_GATHER_WINDOW = 128
SC_LANES = 16


def _gather_rows(x, idx):
    n, d = idx.shape[0], x.shape[1]
    w = SC_GATHER_WINDOW
    assert n % w == 0
    mesh = plsc.VectorSubcoreMesh(core_axis_name="core", subcore_axis_name="subcore")

    @pl.kernel(out_type=jax.ShapeDtypeStruct((n, d), x.dtype), mesh=mesh,
               scratch_types=[pltpu.SemaphoreType.DMA])
    def gather_kernel(x_hbm, i_hbm, o_hbm, sem):
        def body(i_vmem, o_vmem):
            copies = []
            for k in range(w // SC_LANES):
                grp = pl.ds(k * SC_LANES, SC_LANES)
                copies.append(pltpu.async_copy(x_hbm.at[i_vmem[0, grp]], o_vmem.at[grp], sem))
            for cp in copies:
                cp.wait()

        pltpu.emit_pipeline(
            body,
            grid=(n // w,),
            in_specs=[pl.BlockSpec((1, w), lambda i: (0, i))],
            out_specs=[pl.BlockSpec((w, d), lambda i: (i, 0))],
            core_axis_name=("core", "subcore"),
            dimension_semantics=(pltpu.PARALLEL,),
        )(i_hbm, o_hbm)

    return gather_kernel(x, idx.reshape(1, n))


def _combine_kernel(x_ref, mod_ref, ew_ref, ya_ref, yb_ref, o_ref, *, d):
    ew = ew_ref[...]
    y = ew[:, 0:1] * ya_ref[...].astype(F32) + ew[:, 1:2] * yb_ref[...].astype(F32)
    o_ref[...] = x_ref[...] + mod_ref[0:1, 5 * d:6 * d] * y


def _combine(x3, mod, ew, ya, yb, *, tm):
    n, d = x3.shape
    row = lambda i: (i, 0)
    return pl.pallas_call(
        functools.partial(_combine_kernel, d=d),
        grid=(n // tm,),
        in_specs=[pl.BlockSpec((tm, d), row), _resident(mod.shape), pl.BlockSpec((tm, LANES), row),
                  pl.BlockSpec((tm, d), row), pl.BlockSpec((tm, d), row)],
        out_specs=pl.BlockSpec((tm, d), row),
        out_shape=jax.ShapeDtypeStruct((n, d), F32),
        compiler_params=_cparams(("parallel",)),
    )(x3, mod, ew, ya, yb)


def _deinterleave(width):
    return np.concatenate([np.arange(0, width, 2), np.arange(1, width, 2)])


def _rope_tables(n_x, n_ctx, rot_dim, seg_start, seg_repeat):
    rows = n_x // GRID_W
    row = jnp.broadcast_to(jnp.arange(rows)[:, None], (rows, GRID_W)).reshape(n_x).astype(F32)
    col = jnp.broadcast_to(jnp.arange(GRID_W)[None, :], (rows, GRID_W)).reshape(n_x).astype(F32)
    axis_dim = rot_dim // 2
    inv_freq = ROPE_THETA ** (-jnp.arange(0, axis_dim, 2, dtype=F32) / axis_dim)
    ang = jnp.concatenate([row[:, None] * inv_freq, col[:, None] * inv_freq], axis=-1)
    cos, sin = jnp.cos(ang), jnp.sin(ang)
    tail = LANES - seg_start - seg_repeat * rot_dim
    c = jnp.concatenate([jnp.ones((n_x, seg_start), F32)] + [cos, cos] * seg_repeat + [jnp.ones((n_x, tail), F32)],
                        axis=-1)
    s = jnp.concatenate([jnp.zeros((n_x, seg_start), F32)] + [-sin, sin] * seg_repeat
                        + [jnp.zeros((n_x, tail), F32)], axis=-1)
    c = jnp.concatenate([c, jnp.ones((n_ctx, LANES), F32)], axis=0)
    s = jnp.concatenate([s, jnp.zeros((n_ctx, LANES), F32)], axis=0)
    return c, s


def _segment_matrix(bounds):
    m = np.zeros((LANES, LANES), np.float32)
    for lo, hi in bounds:
        m[lo:hi, lo:hi] = 1.0 / (hi - lo)
    return jnp.asarray(m, BF16)


def _token_tile(t):
    for tm in (640, 512, 256, 128):
        if t % tm == 0:
            return tm
    raise ValueError(f"token count {t} has no supported tile")


def kernel(x, c, ctx, c_ctx, l0_ada_w, l0_ada_b, l0_norm1_g, l0_norm2_g, l0_w_in, l0_ret_log_decay, l0_q_norm_g, l0_k_norm_g, l0_w_out, l0_ffn_w1, l0_ffn_w3, l0_ffn_w2, l1_ada_w, l1_ada_b, l1_norm1_g, l1_norm2_g, l1_w_in, l1_q_lora_g, l1_kv_lora_g, l1_w_uq, l1_w_ukv, l1_q_nope_g, l1_q_rope_g, l1_k_nope_g, l1_k_rope_g, l1_w_out, l1_router, l1_exp_w1, l1_exp_w3, l1_exp_w2):
    b, n_x, d = x.shape
    n_ctx = ctx.shape[1]
    assert b == 1 and n_x % 256 == 0 and n_ctx % 256 == 0 and n_x % GRID_W == 0
    t = n_x + n_ctx
    tm = _token_tile(t)
    tmx = _token_tile(n_x)
    xa = jnp.concatenate([x[0], ctx[0]], axis=0)
    row1 = lambda v: v.reshape(1, -1).astype(F32)

    cvec = jnp.zeros((8, d), F32).at[0].set(c[0]).at[1].set(c_ctx)
    mod0 = _ada(cvec, l0_ada_w, l0_ada_b)
    mod1 = _ada(cvec, l1_ada_w, l1_ada_b)

    rw = RET_HEADS * HEAD
    perm = _deinterleave(HEAD)
    n_qk = GQA_HEADS + GQA_KV_HEADS
    qk_cols = l0_w_in[:, 4 * rw:4 * rw + n_qk * HEAD].reshape(d, n_qk, HEAD // 2, 2)
    qk_cols = jnp.swapaxes(qk_cols, 2, 3).reshape(d, n_qk * HEAD)
    w_in0 = jnp.concatenate([l0_w_in[:, :rw], l0_w_in[:, rw:2 * rw] * (HEAD ** -0.5), l0_w_in[:, 2 * rw:4 * rw],
                             qk_cols, l0_w_in[:, 4 * rw + n_qk * HEAD:]], axis=1).astype(BF16)
    seg64 = _segment_matrix([(0, HEAD), (HEAD, 2 * HEAD)])
    gq0 = row1(jnp.tile(l0_q_norm_g[perm], 2) * (HEAD ** -0.5 * LOG2E))
    gk0 = row1(jnp.tile(l0_k_norm_g[perm], 2))
    cos0, sin0 = _rope_tables(n_x, n_ctx, HEAD, 0, 2)

    rq, rk, rv, rg, gq, gkx, gvx = _l0_proj(xa, mod0, row1(l0_norm1_g), w_in0, seg64, gq0, gk0, cos0, sin0,
                                             tm=tm, n_x=n_x)
    o_f, o_b = _retention(l0_ret_log_decay.astype(F32), rq, rk, rv, n_x=n_x)

    gqa_maps = dict(q_maps=(lambda j: j, lambda j: j),
                    k_maps=(lambda j: 2 * (j // 2), lambda j: 2 * (j // 2) + 1),
                    v_maps=(lambda j: 2 * (j // 2), lambda j: 2 * (j // 2) + 1), n_pairs=GQA_HEADS // 2)
    bq = 512 if n_x % 512 == 0 else 256
    ao_x = _flash(gq, gkx, (gvx, gvx), n_q=n_x, q_row0=0, kv_row0=0, n_kv=t, bq=bq, bk=_kv_block(t), rs=32,
                  **gqa_maps)
    ao_c = _flash(gq, gkx, (gvx, gvx), n_q=n_ctx, q_row0=n_x, kv_row0=n_x, n_kv=n_ctx, bq=n_ctx,
                  bk=_kv_block(n_ctx), rs=32, **gqa_maps)
    ao = jnp.concatenate([ao_x, ao_c], axis=0)

    xa = _l0_out(xa, mod0, o_f, o_b, rg, ao, seg64, l0_w_out.astype(BF16), tm=tm, n_x=n_x)
    xa = _ffn(xa, mod0, row1(l0_norm2_g), l0_ffn_w1.astype(BF16), l0_ffn_w3.astype(BF16),
              l0_ffn_w2.astype(BF16), tm=tm, n_x=n_x)

    rperm = _deinterleave(MLA_ROPE)
    qk_w = MLA_NOPE + MLA_ROPE
    wuq = jnp.zeros((MLA_Q_RANK, MLA_HEADS * LANES), F32)
    wuk = jnp.zeros((MLA_KV_RANK, MLA_HEADS * LANES), F32)
    wuv = []
    for hd in range(MLA_HEADS):
        src = l1_w_uq[:, hd * qk_w:(hd + 1) * qk_w]
        wuq = wuq.at[:, hd * LANES:hd * LANES + MLA_NOPE].set(src[:, :MLA_NOPE])
        wuq = wuq.at[:, hd * LANES + MLA_NOPE:hd * LANES + qk_w].set(src[:, MLA_NOPE:][:, rperm])
        kvsrc = l1_w_ukv[:, hd * 2 * HEAD:(hd + 1) * 2 * HEAD]
        wuk = wuk.at[:, hd * LANES:hd * LANES + MLA_NOPE].set(kvsrc[:, :MLA_NOPE])
        wuv.append(kvsrc[:, MLA_NOPE:])
    wuv = jnp.concatenate(wuv, axis=1)
    wkr = jnp.zeros((d, LANES), F32).at[:, MLA_NOPE:qk_w].set(l1_w_in[:, MLA_Q_RANK + MLA_KV_RANK:][:, rperm])
    pad = jnp.zeros((LANES - qk_w,), F32)
    zn = jnp.zeros((MLA_NOPE,), F32)
    gq1 = row1(jnp.concatenate([l1_q_nope_g, l1_q_rope_g[rperm], pad]) * (qk_w ** -0.5 * LOG2E))
    gk1 = row1(jnp.concatenate([l1_k_nope_g, jnp.zeros((LANES - MLA_NOPE,), F32)]))
    gkr1 = row1(jnp.concatenate([zn, l1_k_rope_g[rperm], pad]))
    seg_mla = _segment_matrix([(0, MLA_NOPE), (MLA_NOPE, qk_w)])
    cos1, sin1 = _rope_tables(n_x, n_ctx, MLA_ROPE, MLA_NOPE, 1)

    mq, mk, mvlo, mvhi = _l1_proj(
        xa, mod1, row1(l1_norm1_g), l1_w_in[:, :MLA_Q_RANK].astype(BF16),
        l1_w_in[:, MLA_Q_RANK:MLA_Q_RANK + MLA_KV_RANK].astype(BF16), wkr.astype(BF16),
        row1(l1_q_lora_g), row1(l1_kv_lora_g), wuq.astype(BF16), wuk.astype(BF16), wuv.astype(BF16),
        seg_mla, gq1, gk1, gkr1, cos1, sin1, tm=tm, n_x=n_x)
    mo = _flash(mq, mk, (mvlo, mvhi), q_maps=(lambda j: 2 * j, lambda j: 2 * j + 1),
                k_maps=(lambda j: 2 * j, lambda j: 2 * j + 1), v_maps=(lambda j: j, lambda j: j),
                n_pairs=MLA_HEADS // 2, n_q=n_x, q_row0=0, kv_row0=0, n_kv=t, bq=bq, bk=_kv_block(t), rs=32)

    router = jnp.zeros((d, LANES), F32).at[:, :N_EXPERTS].set(l1_router)
    r_hi = router.astype(BF16)
    r_lo = (router - r_hi.astype(F32)).astype(BF16)
    x3, hmoe, ei, ew = _l1_out(xa, mod1, row1(l1_norm2_g), mo, l1_w_out.astype(BF16), r_hi, r_lo, tm=tmx)

    tme = 512 if n_x >= 4096 else 128
    e_flat = ei[:, :TOP_K].reshape(-1)
    onehot = (e_flat[:, None] == jnp.arange(N_EXPERTS)[None, :]).astype(jnp.int32)
    csum = jnp.cumsum(onehot, axis=0)
    rank = jnp.sum((csum - onehot) * onehot, axis=1)
    counts = csum[-1]
    padded = ((counts + tme - 1) // tme) * tme
    ends = jnp.cumsum(padded)
    pos = (ends - padded)[e_flat] + rank
    n_tiles = (TOP_K * n_x) // tme + N_EXPERTS
    p_rows = n_tiles * tme
    src = jnp.zeros((p_rows,), jnp.int32).at[pos].set(jnp.arange(TOP_K * n_x, dtype=jnp.int32) // TOP_K)
    n_valid = (ends[-1] // tme).astype(jnp.int32)
    tile_ids = jnp.minimum(jnp.arange(n_tiles, dtype=jnp.int32), n_valid - 1)
    tile_expert = jnp.sum((ends[None, :] <= (tile_ids * tme)[:, None]).astype(jnp.int32), axis=1)
    tile_expert = jnp.minimum(tile_expert, N_EXPERTS - 1)
    xs = _gather_rows(hmoe.reshape(2 * n_x, d // 4), jnp.concatenate([src, src + n_x])).reshape(2, p_rows, d // 4)
    fdim = l1_exp_w1.shape[2]
    tf = fdim // 2 if (fdim // 2) % LANES == 0 else fdim
    ys = _moe(tile_expert, n_valid.reshape(1), xs, l1_exp_w1.astype(BF16), l1_exp_w3.astype(BF16),
              l1_exp_w2.astype(BF16), tm=tme, tf=tf)
    pos2 = pos.reshape(n_x, TOP_K)
    out = _combine(x3, mod1, ew, jnp.take(ys, pos2[:, 0], axis=0, mode="clip"),
                   jnp.take(ys, pos2[:, 1], axis=0, mode="clip"), tm=tmx)
    return out[None]
```

```python
import functools
import math

import numpy as np
import jax
import jax.numpy as jnp
from jax import lax
from jax.experimental import pallas as pl
from jax.experimental.pallas import tpu as pltpu
from jax.experimental.pallas import tpu_sc as plsc

F32 = jnp.float32
BF16 = jnp.bfloat16

EPS = 1e-6
ROPE_THETA = 10000.0
GRID_W = 64
LANES = 128
HEAD = 64
RET_CHUNK = 128
RET_HEADS = 8
GQA_HEADS = 8
GQA_KV_HEADS = 2
MLA_HEADS = 8
MLA_Q_RANK = 384
MLA_KV_RANK = 256
MLA_NOPE = 64
MLA_ROPE = 32
N_EXPERTS = 8
TOP_K = 2
LOW_ONE = HEAD
HIGH_ONE = 0
LOG2E = math.log2(math.e)
VMEM_LIMIT = 56 * 1024 * 1024


def _cparams(sem, vmem=VMEM_LIMIT):
    return pltpu.CompilerParams(dimension_semantics=sem, vmem_limit_bytes=vmem)


def _resident(shape):
    nd = len(shape)
    return pl.BlockSpec(shape, lambda *_: (0,) * nd, pipeline_mode=pl.Buffered(1))


def _dot(a, b):
    return jnp.dot(a, b, preferred_element_type=F32)


def _dot_nt(a, b):
    return lax.dot_general(a, b, (((1,), (1,)), ((), ())), preferred_element_type=F32)


def _seg_mean(v, seg):
    hi = v.astype(BF16)
    lo = (v - hi.astype(F32)).astype(BF16)
    return _dot(hi, seg) + _dot(lo, seg)


def _silu(x):
    return x * jax.nn.sigmoid(x)


def _modulated(x, mod_ref, g_ref, which, tile, tm, n_x, d):
    ms = jnp.mean(x * x, axis=-1, keepdims=True)
    xn = x * lax.rsqrt(ms + EPS)
    g = g_ref[...]
    sh, sc = 3 * which, 3 * which + 1
    a_x = g * (1.0 + mod_ref[0:1, sc * d:(sc + 1) * d])
    a_c = g * (1.0 + mod_ref[1:2, sc * d:(sc + 1) * d])
    b_x = mod_ref[0:1, sh * d:(sh + 1) * d]
    b_c = mod_ref[1:2, sh * d:(sh + 1) * d]
    row = tile * tm + lax.broadcasted_iota(jnp.int32, (tm, 1), 0)
    is_ctx = row >= n_x
    return xn * jnp.where(is_ctx, a_c, a_x) + jnp.where(is_ctx, b_c, b_x)


def _row_gate(mod_ref, idx, tile, tm, n_x, d):
    row = tile * tm + lax.broadcasted_iota(jnp.int32, (tm, 1), 0)
    return jnp.where(row >= n_x, mod_ref[1:2, idx * d:(idx + 1) * d], mod_ref[0:1, idx * d:(idx + 1) * d])


def _lane(shape):
    return lax.broadcasted_iota(jnp.int32, shape, len(shape) - 1)


def _ada_kernel(c_ref, w_ref, b_ref, o_ref):
    c = c_ref[...]
    o_ref[...] = jnp.dot(_silu(c), w_ref[...], preferred_element_type=F32,
                         precision=lax.Precision.HIGHEST) + b_ref[...]


def _ada(cvec8, w, b):
    d, n = w.shape
    tn = n // 4
    return pl.pallas_call(
        _ada_kernel,
        grid=(n // tn,),
        in_specs=[pl.BlockSpec((8, d), lambda j: (0, 0)),
                  pl.BlockSpec((d, tn), lambda j: (0, j)),
                  pl.BlockSpec((1, tn), lambda j: (0, j))],
        out_specs=pl.BlockSpec((8, tn), lambda j: (0, j)),
        out_shape=jax.ShapeDtypeStruct((8, n), F32),
        compiler_params=_cparams(("arbitrary",)),
    )(cvec8, w, b.reshape(1, n))


def _rope128(v, c, s, half):
    lane = _lane(v.shape)
    swapped = jnp.where(lane % (2 * half) < half, pltpu.roll(v, LANES - half, 1), pltpu.roll(v, half, 1))
    return v * c + swapped * s


def _l0_proj_kernel(x_ref, mod_ref, g_ref, w_ref, seg_ref, gq_ref, gk_ref, c_ref, s_ref,
                    rq_ref, rk_ref, rv_ref, rg_ref, q_ref, k_ref, v_ref, *, tm, n_x, d):
    i = pl.program_id(0)
    h = _modulated(x_ref[...], mod_ref, g_ref, 0, i, tm, n_x, d).astype(BF16)
    rw = RET_HEADS * HEAD
    for idx, ref in enumerate((rq_ref, rk_ref, rv_ref, rg_ref)):
        ref[...] = _dot(h, w_ref[:, idx * rw:(idx + 1) * rw]).astype(BF16)
    seg = seg_ref[...]
    cos, sin = c_ref[...], s_ref[...]
    base = 4 * rw
    qw = GQA_HEADS * HEAD
    qa = _dot(h, w_ref[:, base:base + qw])
    for g in range(qw // LANES):
        v = qa[:, g * LANES:(g + 1) * LANES]
        vn = v * lax.rsqrt(_seg_mean(v * v, seg) + EPS) * gq_ref[...]
        q_ref[:, g * LANES:(g + 1) * LANES] = _rope128(vn, cos, sin, HEAD // 2).astype(BF16)
    kv = _dot(h, w_ref[:, base + qw:base + qw + 2 * LANES])
    kk = kv[:, :LANES]
    kk = kk * lax.rsqrt(_seg_mean(kk * kk, seg) + EPS) * gk_ref[...]
    kk = _rope128(kk, cos, sin, HEAD // 2)
    vv = kv[:, LANES:]
    lane = _lane(kk.shape)
    low = lane < HEAD
    for src, ref, one in ((kk, k_ref, 0.0), (vv, v_ref, 1.0)):
        sw = pltpu.roll(src, HEAD, 1)
        lo_fill = jnp.where(lane == LOW_ONE, one, 0.0)
        hi_fill = jnp.where(lane == HIGH_ONE, one, 0.0)
        ref[:, 0 * LANES:1 * LANES] = jnp.where(low, src, lo_fill).astype(BF16)
        ref[:, 1 * LANES:2 * LANES] = jnp.where(low, hi_fill, sw).astype(BF16)
        ref[:, 2 * LANES:3 * LANES] = jnp.where(low, sw, lo_fill).astype(BF16)
        ref[:, 3 * LANES:4 * LANES] = jnp.where(low, hi_fill, src).astype(BF16)


def _l0_proj(xa, mod, g, w, seg, gq, gk, cos, sin, *, tm, n_x):
    t, d = xa.shape
    rw = RET_HEADS * HEAD
    row = lambda i: (i, 0)
    outs = [jax.ShapeDtypeStruct((t, rw), BF16)] * 7
    return pl.pallas_call(
        functools.partial(_l0_proj_kernel, tm=tm, n_x=n_x, d=d),
        grid=(t // tm,),
        in_specs=[pl.BlockSpec((tm, d), row), _resident(mod.shape), _resident(g.shape), _resident(w.shape),
                  _resident(seg.shape), _resident(gq.shape), _resident(gk.shape),
                  pl.BlockSpec((tm, LANES), row), pl.BlockSpec((tm, LANES), row)],
        out_specs=[pl.BlockSpec((tm, rw), row)] * 7,
        out_shape=outs,
        compiler_params=_cparams(("parallel",)),
    )(xa, mod, g, w, seg, gq, gk, cos, sin)


def _retention_kernel(lg_ref, qf_ref, kf_ref, vf_ref, qb_ref, kb_ref, vb_ref, of_ref, ob_ref,
                      state_ref, decay_ref, xi_ref, zeta_ref, gl_ref):
    c = RET_CHUNK
    npairs = RET_HEADS * HEAD // LANES
    step = pl.program_id(0)

    @pl.when(step == 0)
    def _init():
        state_ref[...] = jnp.zeros_like(state_ref)
        ci = lax.broadcasted_iota(jnp.int32, (c, c), 0).astype(F32)
        mi = lax.broadcasted_iota(jnp.int32, (c, c), 1).astype(F32)
        pos = lax.broadcasted_iota(jnp.int32, (c, RET_HEADS * HEAD), 0).astype(F32)
        lane_head = _lane((1, RET_HEADS * HEAD)) // HEAD
        for dr in range(2):
            lgv = jnp.zeros((1, RET_HEADS * HEAD), F32)
            for hd in range(RET_HEADS):
                lg = lg_ref[dr, hd]
                rel = (ci - mi) if dr == 0 else (mi - ci)
                half = slice((hd % 2) * c, (hd % 2 + 1) * c)
                decay_ref[dr, hd // 2, :, half] = jnp.where(rel >= 0, jnp.exp(jnp.maximum(rel, 0.0) * lg), 0.0)
                lgv = jnp.where(lane_head == hd, lg, lgv)
            p = pos if dr == 0 else (c - 1.0 - pos)
            xi_ref[dr] = jnp.exp((p + 1.0) * lgv)
            zeta_ref[dr] = jnp.exp((c - 1.0 - p) * lgv)
            gl_ref[dr] = jnp.exp(float(c) * lgv)

    low = _lane((c, LANES)) < HEAD
    r_i = lax.broadcasted_iota(jnp.int32, (LANES, LANES), 0) // HEAD
    c_i = lax.broadcasted_iota(jnp.int32, (LANES, LANES), 1) // HEAD
    blockdiag = r_i == c_i
    for dr, (q_ref, k_ref, v_ref, o_ref) in enumerate(((qf_ref, kf_ref, vf_ref, of_ref),
                                                       (qb_ref, kb_ref, vb_ref, ob_ref))):
        for j in range(npairs):
            sl = slice(j * LANES, (j + 1) * LANES)
            q, k, v = q_ref[:, sl], k_ref[:, sl], v_ref[:, sl]
            zero = jnp.zeros_like(k)
            k2 = jnp.concatenate([jnp.where(low, k, zero), jnp.where(low, zero, k)], axis=0)
            v2 = jnp.concatenate([jnp.where(low, v, zero), jnp.where(low, zero, v)], axis=0)
            s = _dot_nt(q, k2) * decay_ref[dr, j]
            o = _dot(s.astype(BF16), v2)
            st = state_ref[dr, j]
            qx = (q.astype(F32) * xi_ref[dr, :, sl]).astype(BF16)
            o = o + _dot(qx, st.astype(BF16))
            o_ref[:, sl] = o.astype(BF16)
            kz = (k.astype(F32) * zeta_ref[dr, :, sl]).T.astype(BF16)
            u = _dot(kz, v)
            state_ref[dr, j] = st * gl_ref[dr, :, sl] + jnp.where(blockdiag, u, 0.0)


def _retention(lg, rq, rk, rv, *, n_x):
    t, w = rq.shape
    c = RET_CHUNK
    nc, ncx = t // c, n_x // c
    fwd = lambda i: ((i + ncx) % nc, 0)
    bwd = lambda i: (nc - 1 - i, 0)
    blk = lambda m: pl.BlockSpec((c, w), m)
    npairs = w // LANES
    return pl.pallas_call(
        _retention_kernel,
        grid=(nc,),
        in_specs=[pl.BlockSpec(memory_space=pltpu.SMEM)] + [blk(fwd)] * 3 + [blk(bwd)] * 3,
        out_specs=[blk(fwd), blk(bwd)],
        out_shape=[jax.ShapeDtypeStruct((t, w), BF16)] * 2,
        scratch_shapes=[pltpu.VMEM((2, npairs, LANES, LANES), F32),
                        pltpu.VMEM((2, npairs, c, 2 * c), F32),
                        pltpu.VMEM((2, c, w), F32), pltpu.VMEM((2, c, w), F32),
                        pltpu.VMEM((2, 1, w), F32)],
        compiler_params=_cparams(("arbitrary",)),
    )(lg, rq, rk, rv, rq, rk, rv)


def _flash_kernel(q0_ref, q1_ref, k0_ref, k1_ref, v0_ref, v1_ref, o_ref,
                  s_ref, p_ref, a_ref, m_ref, acc_ref, *, bq, bk, nkv, rs):
    nq = q0_ref.shape[0] // bq
    q_refs, k_refs, v_refs = (q0_ref, q1_ref), (k0_ref, k1_ref), (v0_ref, v1_ref)
    m_ref[...] = jnp.full(m_ref.shape, -jnp.inf, F32)
    acc_ref[...] = jnp.zeros(acc_ref.shape, F32)

    def unit(u):
        if isinstance(u, int):
            qi, t = divmod(u, nkv)
            return qi, pl.ds(qi * bq, bq), pl.ds(t * bk, bk)
        qi, t = lax.div(u, nkv), lax.rem(u, nkv)
        return qi, pl.ds(pl.multiple_of(qi * bq, bq), bq), pl.ds(pl.multiple_of(t * bk, bk), bk)

    def scores(u, slot):
        _, qrows, krows = unit(u)
        for h in range(2):
            s_ref[slot, h] = _dot_nt(q_refs[h][qrows, :], k_refs[h][krows, :])

    def softmax(u, slot):
        qi = unit(u)[0]
        col = lambda c: slice(c * LANES, (c + 1) * LANES)
        for r in range(bq // rs):
            rows = slice(r * rs, (r + 1) * rs)
            for h in range(2):
                mx = s_ref[slot, h, rows, col(0)]
                for c in range(1, bk // LANES):
                    mx = jnp.maximum(mx, s_ref[slot, h, rows, col(c)])
                m_old = m_ref[qi, h, rows, :]
                m_new = jnp.maximum(m_old, jnp.max(mx, axis=-1, keepdims=True))
                a_ref[slot, h, rows, :] = jnp.exp2(m_old - m_new)
                m_ref[qi, h, rows, :] = m_new
                for c in range(bk // LANES):
                    p_ref[slot, h, rows, col(c)] = jnp.exp2(s_ref[slot, h, rows, col(c)] - m_new).astype(BF16)

    def values(u, slot):
        qi, _, krows = unit(u)
        for h in range(2):
            acc_ref[qi, h] = acc_ref[qi, h] * a_ref[slot, h] + _dot(p_ref[slot, h], v_refs[h][krows, :])

    total = nq * nkv
    scores(0, 0)

    def body(i, carry):
        u = 2 * i
        scores(u + 1, 1)
        softmax(u, 0)
        values(u, 0)
        scores(u + 2, 0)
        softmax(u + 1, 1)
        values(u + 1, 1)
        return carry

    n_loop = (total - 1) // 2
    lax.fori_loop(0, n_loop, body, 0)
    last = 2 * n_loop
    if last + 1 < total:
        scores(last + 1, 1)
    softmax(last, 0)
    values(last, 0)
    if last + 1 < total:
        softmax(last + 1, 1)
        values(last + 1, 1)
    low = _lane((bq, LANES)) < HEAD
    for qi in range(nq):
        acc0, acc1 = acc_ref[qi, 0], acc_ref[qi, 1]
        out = jnp.where(low, acc0 / acc0[:, LOW_ONE:LOW_ONE + 1], acc1 / acc1[:, HIGH_ONE:HIGH_ONE + 1])
        o_ref[qi * bq:(qi + 1) * bq, :] = out.astype(o_ref.dtype)


def _flash(q, kmat, vmat, *, q_maps, k_maps, v_maps, n_q, q_row0, kv_row0, n_kv, n_pairs, bq, nq, bk, rs):
    rows = bq * nq
    assert q_row0 % rows == 0 and n_q % rows == 0 and n_kv % bk == 0 and kv_row0 % n_kv == 0 and bq % rs == 0
    qb0, kb0 = q_row0 // rows, kv_row0 // n_kv
    qspec = lambda m: pl.BlockSpec((rows, LANES), lambda j, i: (i + qb0, m(j)))
    kspec = lambda m: pl.BlockSpec((n_kv, LANES), lambda j, i: (kb0, m(j)), pipeline_mode=pl.Buffered(1))
    return pl.pallas_call(
        functools.partial(_flash_kernel, bq=bq, bk=bk, nkv=n_kv // bk, rs=rs),
        grid=(n_pairs, n_q // rows),
        in_specs=[qspec(q_maps[0]), qspec(q_maps[1]), kspec(k_maps[0]), kspec(k_maps[1]),
                  kspec(v_maps[0]), kspec(v_maps[1])],
        out_specs=pl.BlockSpec((rows, LANES), lambda j, i: (i, j)),
        out_shape=jax.ShapeDtypeStruct((n_q, n_pairs * LANES), BF16),
        scratch_shapes=[pltpu.VMEM((2, 2, bq, bk), F32), pltpu.VMEM((2, 2, bq, bk), BF16),
                        pltpu.VMEM((2, 2, bq, LANES), F32), pltpu.VMEM((nq, 2, bq, LANES), F32),
                        pltpu.VMEM((nq, 2, bq, LANES), F32)],
        compiler_params=_cparams(("parallel", "parallel")),
    )(q, q, kmat, kmat, vmat[0], vmat[1])


def _kv_block(n_kv):
    for bk in (1280, 1024, 512, 256):
        if n_kv % bk == 0:
            return bk
    raise ValueError(f"key count {n_kv} has no supported block")


def _l0_out_kernel(x_ref, mod_ref, of_ref, ob_ref, rg_ref, ao_ref, seg_ref, wo_ref, o_ref, *, tm, n_x, d):
    i = pl.program_id(0)
    seg = seg_ref[...]
    rw = RET_HEADS * HEAD
    acc = _dot(ao_ref[...], wo_ref[rw:, :])
    for g in range(rw // LANES):
        sl = slice(g * LANES, (g + 1) * LANES)
        o = of_ref[:, sl].astype(F32) + ob_ref[:, sl].astype(F32)
        dv = o - _seg_mean(o, seg)
        nrm = dv * lax.rsqrt(_seg_mean(dv * dv, seg) + EPS)
        ra = (nrm * _silu(rg_ref[:, sl].astype(F32))).astype(BF16)
        acc = acc + _dot(ra, wo_ref[g * LANES:(g + 1) * LANES, :])
    o_ref[...] = x_ref[...] + _row_gate(mod_ref, 2, i, tm, n_x, d) * acc


def _l0_out(xa, mod, o_f, o_b, rg, ao, seg, wo, *, tm, n_x):
    t, d = xa.shape
    rw = o_f.shape[1]
    row = lambda i: (i, 0)
    return pl.pallas_call(
        functools.partial(_l0_out_kernel, tm=tm, n_x=n_x, d=d),
        grid=(t // tm,),
        in_specs=[pl.BlockSpec((tm, d), row), _resident(mod.shape)] + [pl.BlockSpec((tm, rw), row)] * 4
                 + [_resident(seg.shape), _resident(wo.shape)],
        out_specs=pl.BlockSpec((tm, d), row),
        out_shape=jax.ShapeDtypeStruct((t, d), F32),
        compiler_params=_cparams(("parallel",)),
    )(xa, mod, o_f, o_b, rg, ao, seg, wo)


def _ffn_kernel(x_ref, mod_ref, g_ref, w1_ref, w3_ref, w2_ref, o_ref, *, tm, n_x, d):
    i = pl.program_id(0)
    x = x_ref[...]
    h = _modulated(x, mod_ref, g_ref, 1, i, tm, n_x, d).astype(BF16)
    a = _dot(h, w1_ref[...])
    u = (_silu(a) * _dot(h, w3_ref[...])).astype(BF16)
    o_ref[...] = x + _row_gate(mod_ref, 5, i, tm, n_x, d) * _dot(u, w2_ref[...])


def _ffn(xa, mod, g, w1, w3, w2, *, tm, n_x):
    t, d = xa.shape
    row = lambda i: (i, 0)
    return pl.pallas_call(
        functools.partial(_ffn_kernel, tm=tm, n_x=n_x, d=d),
        grid=(t // tm,),
        in_specs=[pl.BlockSpec((tm, d), row), _resident(mod.shape), _resident(g.shape),
                  _resident(w1.shape), _resident(w3.shape), _resident(w2.shape)],
        out_specs=pl.BlockSpec((tm, d), row),
        out_shape=jax.ShapeDtypeStruct((t, d), F32),
        compiler_params=_cparams(("parallel",)),
    )(xa, mod, g, w1, w3, w2)


def _l1_proj_kernel(x_ref, mod_ref, g_ref, wq_ref, wkv_ref, wkr_ref, gql_ref, gkvl_ref, wuq_ref, wuk_ref,
                    wuv_ref, seg_ref, gq_ref, gk_ref, gkr_ref, c_ref, s_ref,
                    q_ref, k_ref, vlo_ref, vhi_ref, *, tm, n_x, d):
    i = pl.program_id(0)
    h = _modulated(x_ref[...], mod_ref, g_ref, 0, i, tm, n_x, d).astype(BF16)
    seg = seg_ref[...]
    cos, sin = c_ref[...], s_ref[...]

    def lora_norm(v, g):
        return (v * lax.rsqrt(jnp.mean(v * v, axis=-1, keepdims=True) + EPS) * g).astype(BF16)

    cq = lora_norm(_dot(h, wq_ref[...]), gql_ref[...])
    ckv = lora_norm(_dot(h, wkv_ref[...]), gkvl_ref[...])
    kr = _dot(h, wkr_ref[...])
    kr = kr * lax.rsqrt(_seg_mean(kr * kr, seg) + EPS) * gkr_ref[...]
    kr = _rope128(kr, cos, sin, MLA_ROPE // 2)
    qa = _dot(cq, wuq_ref[...])
    ka = _dot(ckv, wuk_ref[...])
    for hd in range(MLA_HEADS):
        sl = slice(hd * LANES, (hd + 1) * LANES)
        v = qa[:, sl]
        vn = v * lax.rsqrt(_seg_mean(v * v, seg) + EPS) * gq_ref[...]
        q_ref[:, sl] = _rope128(vn, cos, sin, MLA_ROPE // 2).astype(BF16)
        v = ka[:, sl]
        k_ref[:, sl] = (v * lax.rsqrt(_seg_mean(v * v, seg) + EPS) * gk_ref[...] + kr).astype(BF16)
    va = _dot(ckv, wuv_ref[...])
    lane = _lane(va.shape) % LANES
    low = lane < HEAD
    vlo_ref[...] = jnp.where(low, va, jnp.where(lane == LOW_ONE, 1.0, 0.0)).astype(BF16)
    vhi_ref[...] = jnp.where(low, jnp.where(lane == HIGH_ONE, 1.0, 0.0), va).astype(BF16)


def _l1_proj(xa, mod, g, wq, wkv, wkr, gql, gkvl, wuq, wuk, wuv, seg, gq, gk, gkr, cos, sin, *, tm, n_x):
    t, d = xa.shape
    row = lambda i: (i, 0)
    hw = MLA_HEADS * LANES
    vw = MLA_HEADS * HEAD
    consts = (mod, g, wq, wkv, wkr, gql, gkvl, wuq, wuk, wuv, seg, gq, gk, gkr)
    return pl.pallas_call(
        functools.partial(_l1_proj_kernel, tm=tm, n_x=n_x, d=d),
        grid=(t // tm,),
        in_specs=[pl.BlockSpec((tm, d), row)] + [_resident(a.shape) for a in consts]
                 + [pl.BlockSpec((tm, LANES), row)] * 2,
        out_specs=[pl.BlockSpec((tm, hw), row), pl.BlockSpec((tm, hw), row),
                   pl.BlockSpec((tm, vw), row), pl.BlockSpec((tm, vw), row)],
        out_shape=[jax.ShapeDtypeStruct((t, hw), BF16), jax.ShapeDtypeStruct((t, hw), BF16),
                   jax.ShapeDtypeStruct((t, vw), BF16), jax.ShapeDtypeStruct((t, vw), BF16)],
        compiler_params=_cparams(("parallel",)),
    )(xa, *consts, cos, sin)


def _l1_out_kernel(x_ref, mod_ref, g_ref, o_ref, wo_ref, rhi_ref, rlo_ref, x3_ref, h_ref, ei_ref, ew_ref, *, d):
    x3 = x_ref[...] + mod_ref[0:1, 2 * d:3 * d] * _dot(o_ref[...], wo_ref[...])
    x3_ref[...] = x3
    ms = jnp.mean(x3 * x3, axis=-1, keepdims=True)
    h = x3 * lax.rsqrt(ms + EPS) * (g_ref[...] * (1.0 + mod_ref[0:1, 4 * d:5 * d])) + mod_ref[0:1, 3 * d:4 * d]
    hi = h.astype(BF16)
    bits = lax.bitcast_convert_type(hi.astype(F32), jnp.uint32)
    words = (bits[:, :d // 2] >> 16) | (bits[:, d // 2:] & jnp.uint32(0xFFFF0000))
    h_ref[0] = words[:, :d // 4]
    h_ref[1] = words[:, d // 4:]
    lo = (h - hi.astype(F32)).astype(BF16)
    logits = _dot(hi, rhi_ref[...]) + (_dot(hi, rlo_ref[...]) + _dot(lo, rhi_ref[...]))
    lane_i = _lane(logits.shape)
    lane = lane_i.astype(F32)
    logits = jnp.where(lane_i < N_EXPERTS, logits, -jnp.inf)
    v1 = jnp.max(logits, axis=-1, keepdims=True)
    i1 = jnp.min(jnp.where(logits == v1, lane, float(LANES)), axis=-1, keepdims=True)
    rest = jnp.where(lane == i1, -jnp.inf, logits)
    v2 = jnp.max(rest, axis=-1, keepdims=True)
    i2 = jnp.min(jnp.where(rest == v2, lane, float(LANES)), axis=-1, keepdims=True)
    e2 = jnp.exp(v2 - v1)
    den = 1.0 + e2
    ei_ref[...] = jnp.where(lane_i == 0, i1, jnp.where(lane_i == 1, i2, 0.0)).astype(jnp.int32)
    ew_ref[...] = jnp.where(lane_i == 0, 1.0 / den, jnp.where(lane_i == 1, e2 / den, 0.0))


def _l1_out(xa, mod, g, o, wo, rhi, rlo, *, tm):
    n, d = o.shape[0], xa.shape[1]
    row = lambda i: (i, 0)
    return pl.pallas_call(
        functools.partial(_l1_out_kernel, d=d),
        grid=(n // tm,),
        in_specs=[pl.BlockSpec((tm, d), row), _resident(mod.shape), _resident(g.shape),
                  pl.BlockSpec((tm, o.shape[1]), row), _resident(wo.shape), _resident(rhi.shape),
                  _resident(rlo.shape)],
        out_specs=[pl.BlockSpec((tm, d), row), pl.BlockSpec((2, tm, d // 4), lambda i: (0, i, 0)),
                   pl.BlockSpec((tm, LANES), row), pl.BlockSpec((tm, LANES), row)],
        out_shape=[jax.ShapeDtypeStruct((n, d), F32), jax.ShapeDtypeStruct((2, n, d // 4), jnp.uint32),
                   jax.ShapeDtypeStruct((n, LANES), jnp.int32), jax.ShapeDtypeStruct((n, LANES), F32)],
        compiler_params=_cparams(("parallel",)),
    )(xa, mod, g, o, wo, rhi, rlo)


def _moe_kernel(te_ref, nv_ref, x_ref, w1_ref, w3_ref, w2_ref, y_ref, acc_ref, *, nf):
    i, f = pl.program_id(0), pl.program_id(1)

    @pl.when(f == 0)
    def _zero():
        acc_ref[...] = jnp.zeros_like(acc_ref)

    @pl.when(i < nv_ref[0])
    def _compute():
        words = jnp.concatenate([x_ref[0], x_ref[1]], axis=1)
        lo = lax.bitcast_convert_type(words << 16, F32)
        hi = lax.bitcast_convert_type(words & jnp.uint32(0xFFFF0000), F32)
        x = jnp.concatenate([lo, hi], axis=1).astype(BF16)
        a = _dot(x, w1_ref[0])
        u = (_silu(a) * _dot(x, w3_ref[0])).astype(BF16)
        acc_ref[...] += _dot(u, w2_ref[0])

    @pl.when(f == nf - 1)
    def _store():
        y_ref[...] = acc_ref[...].astype(y_ref.dtype)


def _moe(tile_expert, n_valid, xs, w1, w3, w2, *, tm, tf):
    p, d = xs.shape[1], 4 * xs.shape[2]
    fdim = w1.shape[2]
    nf = fdim // tf
    fi = lambda i, f, te, nv: jnp.where(i < nv[0], f, nf - 1)
    grid_spec = pltpu.PrefetchScalarGridSpec(
        num_scalar_prefetch=2,
        grid=(p // tm, nf),
        in_specs=[pl.BlockSpec((2, tm, d // 4), lambda i, f, te, nv: (0, i, 0)),
                  pl.BlockSpec((1, d, tf), lambda i, f, te, nv: (te[i], 0, fi(i, f, te, nv))),
                  pl.BlockSpec((1, d, tf), lambda i, f, te, nv: (te[i], 0, fi(i, f, te, nv))),
                  pl.BlockSpec((1, tf, d), lambda i, f, te, nv: (te[i], fi(i, f, te, nv), 0))],
        out_specs=pl.BlockSpec((tm, d), lambda i, f, te, nv: (i, 0)),
        scratch_shapes=[pltpu.VMEM((tm, d), F32)],
    )
    return pl.pallas_call(
        functools.partial(_moe_kernel, nf=nf),
        grid_spec=grid_spec,
        out_shape=jax.ShapeDtypeStruct((p, d), BF16),
        compiler_params=_cparams(("arbitrary", "arbitrary")),
    )(tile_expert, n_valid, xs, w1, w3, w2)


SC_GATHER_WINDOW = 128
SC_LANES = 16


def _gather_rows(x, idx):
    n, d = idx.shape[0], x.shape[1]
    w = SC_GATHER_WINDOW
    assert n % w == 0
    mesh = plsc.VectorSubcoreMesh(core_axis_name="core", subcore_axis_name="subcore")

    @pl.kernel(out_type=jax.ShapeDtypeStruct((n, d), x.dtype), mesh=mesh,
               scratch_types=[pltpu.SemaphoreType.DMA])
    def gather_kernel(x_hbm, i_hbm, o_hbm, sem):
        def body(i_vmem, o_vmem):
            copies = []
            for k in range(w // SC_LANES):
                grp = pl.ds(k * SC_LANES, SC_LANES)
                copies.append(pltpu.async_copy(x_hbm.at[i_vmem[0, grp]], o_vmem.at[grp], sem))
            for cp in copies:
                cp.wait()

        pltpu.emit_pipeline(
            body,
            grid=(n // w,),
            in_specs=[pl.BlockSpec((1, w), lambda i: (0, i))],
            out_specs=[pl.BlockSpec((w, d), lambda i: (i, 0))],
            core_axis_name=("core", "subcore"),
            dimension_semantics=(pltpu.PARALLEL,),
        )(i_hbm, o_hbm)

    return gather_kernel(x, idx.reshape(1, n))


def _combine_kernel(x_ref, mod_ref, ew_ref, ya_ref, yb_ref, o_ref, *, d):
    ew = ew_ref[...]
    y = ew[:, 0:1] * ya_ref[...].astype(F32) + ew[:, 1:2] * yb_ref[...].astype(F32)
    o_ref[...] = x_ref[...] + mod_ref[0:1, 5 * d:6 * d] * y


def _combine(x3, mod, ew, ya, yb, *, tm):
    n, d = x3.shape
    row = lambda i: (i, 0)
    return pl.pallas_call(
        functools.partial(_combine_kernel, d=d),
        grid=(n // tm,),
        in_specs=[pl.BlockSpec((tm, d), row), _resident(mod.shape), pl.BlockSpec((tm, LANES), row),
                  pl.BlockSpec((tm, d), row), pl.BlockSpec((tm, d), row)],
        out_specs=pl.BlockSpec((tm, d), row),
        out_shape=jax.ShapeDtypeStruct((n, d), F32),
        compiler_params=_cparams(("parallel",)),
    )(x3, mod, ew, ya, yb)


def _deinterleave(width):
    return np.concatenate([np.arange(0, width, 2), np.arange(1, width, 2)])


def _rope_tables(n_x, n_ctx, rot_dim, seg_start, seg_repeat):
    rows = n_x // GRID_W
    row = jnp.broadcast_to(jnp.arange(rows)[:, None], (rows, GRID_W)).reshape(n_x).astype(F32)
    col = jnp.broadcast_to(jnp.arange(GRID_W)[None, :], (rows, GRID_W)).reshape(n_x).astype(F32)
    axis_dim = rot_dim // 2
    inv_freq = ROPE_THETA ** (-jnp.arange(0, axis_dim, 2, dtype=F32) / axis_dim)
    ang = jnp.concatenate([row[:, None] * inv_freq, col[:, None] * inv_freq], axis=-1)
    cos, sin = jnp.cos(ang), jnp.sin(ang)
    tail = LANES - seg_start - seg_repeat * rot_dim
    c = jnp.concatenate([jnp.ones((n_x, seg_start), F32)] + [cos, cos] * seg_repeat + [jnp.ones((n_x, tail), F32)],
                        axis=-1)
    s = jnp.concatenate([jnp.zeros((n_x, seg_start), F32)] + [-sin, sin] * seg_repeat
                        + [jnp.zeros((n_x, tail), F32)], axis=-1)
    c = jnp.concatenate([c, jnp.ones((n_ctx, LANES), F32)], axis=0)
    s = jnp.concatenate([s, jnp.zeros((n_ctx, LANES), F32)], axis=0)
    return c, s


def _segment_matrix(bounds):
    m = np.zeros((LANES, LANES), np.float32)
    for lo, hi in bounds:
        m[lo:hi, lo:hi] = 1.0 / (hi - lo)
    return jnp.asarray(m, BF16)


def _token_tile(t):
    for tm in (640, 512, 256, 128):
        if t % tm == 0:
            return tm
    raise ValueError(f"token count {t} has no supported tile")


def kernel(x, c, ctx, c_ctx, l0_ada_w, l0_ada_b, l0_norm1_g, l0_norm2_g, l0_w_in, l0_ret_log_decay, l0_q_norm_g, l0_k_norm_g, l0_w_out, l0_ffn_w1, l0_ffn_w3, l0_ffn_w2, l1_ada_w, l1_ada_b, l1_norm1_g, l1_norm2_g, l1_w_in, l1_q_lora_g, l1_kv_lora_g, l1_w_uq, l1_w_ukv, l1_q_nope_g, l1_q_rope_g, l1_k_nope_g, l1_k_rope_g, l1_w_out, l1_router, l1_exp_w1, l1_exp_w3, l1_exp_w2):
    b, n_x, d = x.shape
    n_ctx = ctx.shape[1]
    assert b == 1 and n_x % 256 == 0 and n_ctx % 256 == 0 and n_x % GRID_W == 0
    t = n_x + n_ctx
    tm = _token_tile(t)
    tmx = _token_tile(n_x)
    xa = jnp.concatenate([x[0], ctx[0]], axis=0)
    row1 = lambda v: v.reshape(1, -1).astype(F32)

    cvec = jnp.zeros((8, d), F32).at[0].set(c[0]).at[1].set(c_ctx)
    mod0 = _ada(cvec, l0_ada_w, l0_ada_b)
    mod1 = _ada(cvec, l1_ada_w, l1_ada_b)

    rw = RET_HEADS * HEAD
    perm = _deinterleave(HEAD)
    n_qk = GQA_HEADS + GQA_KV_HEADS
    qk_cols = l0_w_in[:, 4 * rw:4 * rw + n_qk * HEAD].reshape(d, n_qk, HEAD // 2, 2)
    qk_cols = jnp.swapaxes(qk_cols, 2, 3).reshape(d, n_qk * HEAD)
    w_in0 = jnp.concatenate([l0_w_in[:, :rw], l0_w_in[:, rw:2 * rw] * (HEAD ** -0.5), l0_w_in[:, 2 * rw:4 * rw],
                             qk_cols, l0_w_in[:, 4 * rw + n_qk * HEAD:]], axis=1).astype(BF16)
    seg64 = _segment_matrix([(0, HEAD), (HEAD, 2 * HEAD)])
    gq0 = row1(jnp.tile(l0_q_norm_g[perm], 2) * (HEAD ** -0.5 * LOG2E))
    gk0 = row1(jnp.tile(l0_k_norm_g[perm], 2))
    cos0, sin0 = _rope_tables(n_x, n_ctx, HEAD, 0, 2)

    rq, rk, rv, rg, gq, gkx, gvx = _l0_proj(xa, mod0, row1(l0_norm1_g), w_in0, seg64, gq0, gk0, cos0, sin0,
                                             tm=tm, n_x=n_x)
    o_f, o_b = _retention(l0_ret_log_decay.astype(F32), rq, rk, rv, n_x=n_x)

    gqa_maps = dict(q_maps=(lambda j: j, lambda j: j),
                    k_maps=(lambda j: 2 * (j // 2), lambda j: 2 * (j // 2) + 1),
                    v_maps=(lambda j: 2 * (j // 2), lambda j: 2 * (j // 2) + 1), n_pairs=GQA_HEADS // 2)
    bq = 512 if n_x % 512 == 0 else 256
    nq = max(f for f in (4, 2, 1) if n_x % (bq * f) == 0)
    ao_x = _flash(gq, gkx, (gvx, gvx), n_q=n_x, q_row0=0, kv_row0=0, n_kv=t, bq=bq, nq=nq, bk=_kv_block(t), rs=32,
                  **gqa_maps)
    ao_c = _flash(gq, gkx, (gvx, gvx), n_q=n_ctx, q_row0=n_x, kv_row0=n_x, n_kv=n_ctx, bq=n_ctx, nq=1,
                  bk=_kv_block(n_ctx), rs=32, **gqa_maps)
    ao = jnp.concatenate([ao_x, ao_c], axis=0)

    xa = _l0_out(xa, mod0, o_f, o_b, rg, ao, seg64, l0_w_out.astype(BF16), tm=tm, n_x=n_x)
    xa = _ffn(xa, mod0, row1(l0_norm2_g), l0_ffn_w1.astype(BF16), l0_ffn_w3.astype(BF16),
              l0_ffn_w2.astype(BF16), tm=tm, n_x=n_x)

    rperm = _deinterleave(MLA_ROPE)
    qk_w = MLA_NOPE + MLA_ROPE
    wuq = jnp.zeros((MLA_Q_RANK, MLA_HEADS * LANES), F32)
    wuk = jnp.zeros((MLA_KV_RANK, MLA_HEADS * LANES), F32)
    wuv = []
    for hd in range(MLA_HEADS):
        src = l1_w_uq[:, hd * qk_w:(hd + 1) * qk_w]
        wuq = wuq.at[:, hd * LANES:hd * LANES + MLA_NOPE].set(src[:, :MLA_NOPE])
        wuq = wuq.at[:, hd * LANES + MLA_NOPE:hd * LANES + qk_w].set(src[:, MLA_NOPE:][:, rperm])
        kvsrc = l1_w_ukv[:, hd * 2 * HEAD:(hd + 1) * 2 * HEAD]
        wuk = wuk.at[:, hd * LANES:hd * LANES + MLA_NOPE].set(kvsrc[:, :MLA_NOPE])
        wuv.append(kvsrc[:, MLA_NOPE:])
    wuv = jnp.concatenate(wuv, axis=1)
    wkr = jnp.zeros((d, LANES), F32).at[:, MLA_NOPE:qk_w].set(l1_w_in[:, MLA_Q_RANK + MLA_KV_RANK:][:, rperm])
    pad = jnp.zeros((LANES - qk_w,), F32)
    zn = jnp.zeros((MLA_NOPE,), F32)
    gq1 = row1(jnp.concatenate([l1_q_nope_g, l1_q_rope_g[rperm], pad]) * (qk_w ** -0.5 * LOG2E))
    gk1 = row1(jnp.concatenate([l1_k_nope_g, jnp.zeros((LANES - MLA_NOPE,), F32)]))
    gkr1 = row1(jnp.concatenate([zn, l1_k_rope_g[rperm], pad]))
    seg_mla = _segment_matrix([(0, MLA_NOPE), (MLA_NOPE, qk_w)])
    cos1, sin1 = _rope_tables(n_x, n_ctx, MLA_ROPE, MLA_NOPE, 1)

    mq, mk, mvlo, mvhi = _l1_proj(
        xa, mod1, row1(l1_norm1_g), l1_w_in[:, :MLA_Q_RANK].astype(BF16),
        l1_w_in[:, MLA_Q_RANK:MLA_Q_RANK + MLA_KV_RANK].astype(BF16), wkr.astype(BF16),
        row1(l1_q_lora_g), row1(l1_kv_lora_g), wuq.astype(BF16), wuk.astype(BF16), wuv.astype(BF16),
        seg_mla, gq1, gk1, gkr1, cos1, sin1, tm=tm, n_x=n_x)
    mo = _flash(mq, mk, (mvlo, mvhi), q_maps=(lambda j: 2 * j, lambda j: 2 * j + 1),
                k_maps=(lambda j: 2 * j, lambda j: 2 * j + 1), v_maps=(lambda j: j, lambda j: j),
                n_pairs=MLA_HEADS // 2, n_q=n_x, q_row0=0, kv_row0=0, n_kv=t, bq=bq, nq=nq, bk=_kv_block(t), rs=32)

    router = jnp.zeros((d, LANES), F32).at[:, :N_EXPERTS].set(l1_router)
    r_hi = router.astype(BF16)
    r_lo = (router - r_hi.astype(F32)).astype(BF16)
    x3, hmoe, ei, ew = _l1_out(xa, mod1, row1(l1_norm2_g), mo, l1_w_out.astype(BF16), r_hi, r_lo, tm=tmx)

    tme = 512 if n_x >= 4096 else 128
    e_flat = ei[:, :TOP_K].reshape(-1)
    onehot = (e_flat[:, None] == jnp.arange(N_EXPERTS)[None, :]).astype(jnp.int32)
    csum = jnp.cumsum(onehot, axis=0)
    rank = jnp.sum((csum - onehot) * onehot, axis=1)
    counts = csum[-1]
    padded = ((counts + tme - 1) // tme) * tme
    ends = jnp.cumsum(padded)
    pos = (ends - padded)[e_flat] + rank
    n_tiles = (TOP_K * n_x) // tme + N_EXPERTS
    p_rows = n_tiles * tme
    src = jnp.zeros((p_rows,), jnp.int32).at[pos].set(jnp.arange(TOP_K * n_x, dtype=jnp.int32) // TOP_K)
    n_valid = (ends[-1] // tme).astype(jnp.int32)
    tile_ids = jnp.minimum(jnp.arange(n_tiles, dtype=jnp.int32), n_valid - 1)
    tile_expert = jnp.sum((ends[None, :] <= (tile_ids * tme)[:, None]).astype(jnp.int32), axis=1)
    tile_expert = jnp.minimum(tile_expert, N_EXPERTS - 1)
    xs = _gather_rows(hmoe.reshape(2 * n_x, d // 4), jnp.concatenate([src, src + n_x])).reshape(2, p_rows, d // 4)
    fdim = l1_exp_w1.shape[2]
    tf = fdim // 2 if (fdim // 2) % LANES == 0 else fdim
    ys = _moe(tile_expert, n_valid.reshape(1), xs, l1_exp_w1.astype(BF16), l1_exp_w3.astype(BF16),
              l1_exp_w2.astype(BF16), tm=tme, tf=tf)
    pos2 = pos.reshape(n_x, TOP_K)
    out = _combine(x3, mod1, ew, jnp.take(ys, pos2[:, 0], axis=0, mode="clip"),
                   jnp.take(ys, pos2[:, 1], axis=0, mode="clip"), tm=tmx)
    return out[None]
```

```python
import functools
import math

import numpy as np
import jax
import jax.numpy as jnp
from jax import lax
from jax.experimental import pallas as pl
from jax.experimental.pallas import tpu as pltpu
from jax.experimental.pallas import tpu_sc as plsc

F32 = jnp.float32
BF16 = jnp.bfloat16

EPS = 1e-6
ROPE_THETA = 10000.0
GRID_W = 64
LANES = 128
HEAD = 64
RET_CHUNK = 128
RET_HEADS = 8
GQA_HEADS = 8
GQA_KV_HEADS = 2
MLA_HEADS = 8
MLA_Q_RANK = 384
MLA_KV_RANK = 256
MLA_NOPE = 64
MLA_ROPE = 32
N_EXPERTS = 8
TOP_K = 2
LOW_ONE = HEAD
HIGH_ONE = 0
LOG2E = math.log2(math.e)
VMEM_LIMIT = 56 * 1024 * 1024


def _cparams(sem, vmem=VMEM_LIMIT):
    return pltpu.CompilerParams(dimension_semantics=sem, vmem_limit_bytes=vmem)


def _resident(shape):
    nd = len(shape)
    return pl.BlockSpec(shape, lambda *_: (0,) * nd, pipeline_mode=pl.Buffered(1))


def _dot(a, b):
    return jnp.dot(a, b, preferred_element_type=F32)


def _dot_nt(a, b):
    return lax.dot_general(a, b, (((1,), (1,)), ((), ())), preferred_element_type=F32)


def _seg_mean(v, seg):
    hi = v.astype(BF16)
    lo = (v - hi.astype(F32)).astype(BF16)
    return _dot(hi, seg) + _dot(lo, seg)


def _silu(x):
    return x * jax.nn.sigmoid(x)


def _modulated(x, mod_ref, g_ref, which, tile, tm, n_x, d):
    ms = jnp.mean(x * x, axis=-1, keepdims=True)
    xn = x * lax.rsqrt(ms + EPS)
    g = g_ref[...]
    sh, sc = 3 * which, 3 * which + 1
    a_x = g * (1.0 + mod_ref[0:1, sc * d:(sc + 1) * d])
    a_c = g * (1.0 + mod_ref[1:2, sc * d:(sc + 1) * d])
    b_x = mod_ref[0:1, sh * d:(sh + 1) * d]
    b_c = mod_ref[1:2, sh * d:(sh + 1) * d]
    row = tile * tm + lax.broadcasted_iota(jnp.int32, (tm, 1), 0)
    is_ctx = row >= n_x
    return xn * jnp.where(is_ctx, a_c, a_x) + jnp.where(is_ctx, b_c, b_x)


def _row_gate(mod_ref, idx, tile, tm, n_x, d):
    row = tile * tm + lax.broadcasted_iota(jnp.int32, (tm, 1), 0)
    return jnp.where(row >= n_x, mod_ref[1:2, idx * d:(idx + 1) * d], mod_ref[0:1, idx * d:(idx + 1) * d])


def _lane(shape):
    return lax.broadcasted_iota(jnp.int32, shape, len(shape) - 1)


def _ada_kernel(c_ref, w_ref, b_ref, o_ref):
    c = c_ref[...]
    o_ref[...] = jnp.dot(_silu(c), w_ref[...], preferred_element_type=F32,
                         precision=lax.Precision.HIGHEST) + b_ref[...]


def _ada(cvec8, w, b):
    d, n = w.shape
    tn = n // 4
    return pl.pallas_call(
        _ada_kernel,
        grid=(n // tn,),
        in_specs=[pl.BlockSpec((8, d), lambda j: (0, 0)),
                  pl.BlockSpec((d, tn), lambda j: (0, j)),
                  pl.BlockSpec((1, tn), lambda j: (0, j))],
        out_specs=pl.BlockSpec((8, tn), lambda j: (0, j)),
        out_shape=jax.ShapeDtypeStruct((8, n), F32),
        compiler_params=_cparams(("arbitrary",)),
    )(cvec8, w, b.reshape(1, n))


def _rope128(v, c, s, half):
    lane = _lane(v.shape)
    swapped = jnp.where(lane % (2 * half) < half, pltpu.roll(v, LANES - half, 1), pltpu.roll(v, half, 1))
    return v * c + swapped * s


def _l0_proj_kernel(x_ref, mod_ref, g_ref, w_ref, seg_ref, gq_ref, gk_ref, c_ref, s_ref,
                    rq_ref, rk_ref, rv_ref, rg_ref, q_ref, k_ref, v_ref, *, tm, n_x, d):
    i = pl.program_id(0)
    h = _modulated(x_ref[...], mod_ref, g_ref, 0, i, tm, n_x, d).astype(BF16)
    rw = RET_HEADS * HEAD
    for idx, ref in enumerate((rq_ref, rk_ref, rv_ref, rg_ref)):
        ref[...] = _dot(h, w_ref[:, idx * rw:(idx + 1) * rw]).astype(BF16)
    seg = seg_ref[...]
    cos, sin = c_ref[...], s_ref[...]
    base = 4 * rw
    qw = GQA_HEADS * HEAD
    qa = _dot(h, w_ref[:, base:base + qw])
    for g in range(qw // LANES):
        v = qa[:, g * LANES:(g + 1) * LANES]
        vn = v * lax.rsqrt(_seg_mean(v * v, seg) + EPS) * gq_ref[...]
        q_ref[:, g * LANES:(g + 1) * LANES] = _rope128(vn, cos, sin, HEAD // 2).astype(BF16)
    kv = _dot(h, w_ref[:, base + qw:base + qw + 2 * LANES])
    kk = kv[:, :LANES]
    kk = kk * lax.rsqrt(_seg_mean(kk * kk, seg) + EPS) * gk_ref[...]
    kk = _rope128(kk, cos, sin, HEAD // 2)
    vv = kv[:, LANES:]
    lane = _lane(kk.shape)
    low = lane < HEAD
    for src, ref, one in ((kk, k_ref, 0.0), (vv, v_ref, 1.0)):
        sw = pltpu.roll(src, HEAD, 1)
        lo_fill = jnp.where(lane == LOW_ONE, one, 0.0)
        hi_fill = jnp.where(lane == HIGH_ONE, one, 0.0)
        ref[:, 0 * LANES:1 * LANES] = jnp.where(low, src, lo_fill).astype(BF16)
        ref[:, 1 * LANES:2 * LANES] = jnp.where(low, hi_fill, sw).astype(BF16)
        ref[:, 2 * LANES:3 * LANES] = jnp.where(low, sw, lo_fill).astype(BF16)
        ref[:, 3 * LANES:4 * LANES] = jnp.where(low, hi_fill, src).astype(BF16)


def _l0_proj(xa, mod, g, w, seg, gq, gk, cos, sin, *, tm, n_x):
    t, d = xa.shape
    rw = RET_HEADS * HEAD
    row = lambda i: (i, 0)
    outs = [jax.ShapeDtypeStruct((t, rw), BF16)] * 7
    return pl.pallas_call(
        functools.partial(_l0_proj_kernel, tm=tm, n_x=n_x, d=d),
        grid=(t // tm,),
        in_specs=[pl.BlockSpec((tm, d), row), _resident(mod.shape), _resident(g.shape), _resident(w.shape),
                  _resident(seg.shape), _resident(gq.shape), _resident(gk.shape),
                  pl.BlockSpec((tm, LANES), row), pl.BlockSpec((tm, LANES), row)],
        out_specs=[pl.BlockSpec((tm, rw), row)] * 7,
        out_shape=outs,
        compiler_params=_cparams(("parallel",)),
    )(xa, mod, g, w, seg, gq, gk, cos, sin)


def _retention_kernel(lg_ref, qf_ref, kf_ref, vf_ref, qb_ref, kb_ref, vb_ref, of_ref, ob_ref,
                      state_ref, decay_ref, xi_ref, zeta_ref, gl_ref):
    c = RET_CHUNK
    npairs = RET_HEADS * HEAD // LANES
    step = pl.program_id(0)

    @pl.when(step == 0)
    def _init():
        state_ref[...] = jnp.zeros_like(state_ref)
        ci = lax.broadcasted_iota(jnp.int32, (c, c), 0).astype(F32)
        mi = lax.broadcasted_iota(jnp.int32, (c, c), 1).astype(F32)
        pos = lax.broadcasted_iota(jnp.int32, (c, RET_HEADS * HEAD), 0).astype(F32)
        lane_head = _lane((1, RET_HEADS * HEAD)) // HEAD
        for dr in range(2):
            lgv = jnp.zeros((1, RET_HEADS * HEAD), F32)
            for hd in range(RET_HEADS):
                lg = lg_ref[dr, hd]
                rel = (ci - mi) if dr == 0 else (mi - ci)
                half = slice((hd % 2) * c, (hd % 2 + 1) * c)
                decay_ref[dr, hd // 2, :, half] = jnp.where(rel >= 0, jnp.exp(jnp.maximum(rel, 0.0) * lg), 0.0)
                lgv = jnp.where(lane_head == hd, lg, lgv)
            p = pos if dr == 0 else (c - 1.0 - pos)
            xi_ref[dr] = jnp.exp((p + 1.0) * lgv)
            zeta_ref[dr] = jnp.exp((c - 1.0 - p) * lgv)
            gl_ref[dr] = jnp.exp(float(c) * lgv)

    low = _lane((c, LANES)) < HEAD
    r_i = lax.broadcasted_iota(jnp.int32, (LANES, LANES), 0) // HEAD
    c_i = lax.broadcasted_iota(jnp.int32, (LANES, LANES), 1) // HEAD
    blockdiag = r_i == c_i
    for dr, (q_ref, k_ref, v_ref, o_ref) in enumerate(((qf_ref, kf_ref, vf_ref, of_ref),
                                                       (qb_ref, kb_ref, vb_ref, ob_ref))):
        for j in range(npairs):
            sl = slice(j * LANES, (j + 1) * LANES)
            q, k, v = q_ref[:, sl], k_ref[:, sl], v_ref[:, sl]
            zero = jnp.zeros_like(k)
            k2 = jnp.concatenate([jnp.where(low, k, zero), jnp.where(low, zero, k)], axis=0)
            v2 = jnp.concatenate([jnp.where(low, v, zero), jnp.where(low, zero, v)], axis=0)
            s = _dot_nt(q, k2) * decay_ref[dr, j]
            o = _dot(s.astype(BF16), v2)
            st = state_ref[dr, j]
            qx = (q.astype(F32) * xi_ref[dr, :, sl]).astype(BF16)
            o = o + _dot(qx, st.astype(BF16))
            o_ref[:, sl] = o.astype(BF16)
            kz = (k.astype(F32) * zeta_ref[dr, :, sl]).T.astype(BF16)
            u = _dot(kz, v)
            state_ref[dr, j] = st * gl_ref[dr, :, sl] + jnp.where(blockdiag, u, 0.0)


def _retention(lg, rq, rk, rv, *, n_x):
    t, w = rq.shape
    c = RET_CHUNK
    nc, ncx = t // c, n_x // c
    fwd = lambda i: ((i + ncx) % nc, 0)
    bwd = lambda i: (nc - 1 - i, 0)
    blk = lambda m: pl.BlockSpec((c, w), m)
    npairs = w // LANES
    return pl.pallas_call(
        _retention_kernel,
        grid=(nc,),
        in_specs=[pl.BlockSpec(memory_space=pltpu.SMEM)] + [blk(fwd)] * 3 + [blk(bwd)] * 3,
        out_specs=[blk(fwd), blk(bwd)],
        out_shape=[jax.ShapeDtypeStruct((t, w), BF16)] * 2,
        scratch_shapes=[pltpu.VMEM((2, npairs, LANES, LANES), F32),
                        pltpu.VMEM((2, npairs, c, 2 * c), F32),
                        pltpu.VMEM((2, c, w), F32), pltpu.VMEM((2, c, w), F32),
                        pltpu.VMEM((2, 1, w), F32)],
        compiler_params=_cparams(("arbitrary",)),
    )(lg, rq, rk, rv, rq, rk, rv)


def _flash_kernel(q0_ref, q1_ref, k0_ref, k1_ref, v0_ref, v1_ref, o_ref,
                  s_ref, p_ref, a_ref, m_ref, acc_ref, *, bk, nkv, rs):
    bq = q0_ref.shape[0]
    q_refs, k_refs, v_refs = (q0_ref, q1_ref), (k0_ref, k1_ref), (v0_ref, v1_ref)
    m_ref[...] = jnp.full(m_ref.shape, -jnp.inf, F32)
    acc_ref[...] = jnp.zeros(acc_ref.shape, F32)

    def keys(t):
        return pl.ds(t * bk if isinstance(t, int) else pl.multiple_of(t * bk, bk), bk)

    def scores(t, slot):
        for h in range(2):
            s_ref[slot, h] = _dot_nt(q_refs[h][...], k_refs[h][keys(t), :])

    def softmax(slot):
        col = lambda c: slice(c * LANES, (c + 1) * LANES)
        for r in range(bq // rs):
            rows = slice(r * rs, (r + 1) * rs)
            for h in range(2):
                mx = s_ref[slot, h, rows, col(0)]
                for c in range(1, bk // LANES):
                    mx = jnp.maximum(mx, s_ref[slot, h, rows, col(c)])
                m_old = m_ref[h, rows, :]
                m_new = jnp.maximum(m_old, jnp.max(mx, axis=-1, keepdims=True))
                a_ref[slot, h, rows, :] = jnp.exp2(m_old - m_new)
                m_ref[h, rows, :] = m_new
                for c in range(bk // LANES):
                    p_ref[slot, h, rows, col(c)] = jnp.exp2(s_ref[slot, h, rows, col(c)] - m_new).astype(BF16)

    def values(t, slot):
        for h in range(2):
            acc_ref[h] = acc_ref[h] * a_ref[slot, h] + _dot(p_ref[slot, h], v_refs[h][keys(t), :])

    scores(0, 0)

    def body(i, carry):
        t = 2 * i
        scores(t + 1, 1)
        softmax(0)
        values(t, 0)
        scores(t + 2, 0)
        softmax(1)
        values(t + 1, 1)
        return carry

    n_loop = (nkv - 1) // 2
    lax.fori_loop(0, n_loop, body, 0)
    last = 2 * n_loop
    if last + 1 < nkv:
        scores(last + 1, 1)
    softmax(0)
    values(last, 0)
    if last + 1 < nkv:
        softmax(1)
        values(last + 1, 1)
    low = _lane((bq, LANES)) < HEAD
    acc0, acc1 = acc_ref[0], acc_ref[1]
    out = jnp.where(low, acc0 / acc0[:, LOW_ONE:LOW_ONE + 1], acc1 / acc1[:, HIGH_ONE:HIGH_ONE + 1])
    o_ref[...] = out.astype(o_ref.dtype)


def _flash(q, kmat, vmat, *, q_maps, k_maps, v_maps, n_q, q_row0, kv_row0, n_kv, n_pairs, bq, bk, rs):
    assert q_row0 % bq == 0 and n_q % bq == 0 and n_kv % bk == 0 and kv_row0 % n_kv == 0 and bq % rs == 0
    qb0, kb0 = q_row0 // bq, kv_row0 // n_kv
    qspec = lambda m: pl.BlockSpec((bq, LANES), lambda j, i: (i + qb0, m(j)))
    kspec = lambda m: pl.BlockSpec((n_kv, LANES), lambda j, i: (kb0, m(j)), pipeline_mode=pl.Buffered(1))
    return pl.pallas_call(
        functools.partial(_flash_kernel, bk=bk, nkv=n_kv // bk, rs=rs),
        grid=(n_pairs, n_q // bq),
        in_specs=[qspec(q_maps[0]), qspec(q_maps[1]), kspec(k_maps[0]), kspec(k_maps[1]),
                  kspec(v_maps[0]), kspec(v_maps[1])],
        out_specs=pl.BlockSpec((bq, LANES), lambda j, i: (i, j)),
        out_shape=jax.ShapeDtypeStruct((n_q, n_pairs * LANES), BF16),
        scratch_shapes=[pltpu.VMEM((2, 2, bq, bk), F32), pltpu.VMEM((2, 2, bq, bk), BF16),
                        pltpu.VMEM((2, 2, bq, LANES), F32), pltpu.VMEM((2, bq, LANES), F32),
                        pltpu.VMEM((2, bq, LANES), F32)],
        compiler_params=_cparams(("parallel", "parallel")),
    )(q, q, kmat, kmat, vmat[0], vmat[1])


def _kv_block(n_kv):
    for bk in (1280, 1024, 512, 256):
        if n_kv % bk == 0:
            return bk
    raise ValueError(f"key count {n_kv} has no supported block")


def _l0_out_kernel(x_ref, mod_ref, of_ref, ob_ref, rg_ref, ao_ref, seg_ref, wo_ref, o_ref, *, tm, n_x, d):
    i = pl.program_id(0)
    seg = seg_ref[...]
    rw = RET_HEADS * HEAD
    acc = _dot(ao_ref[...], wo_ref[rw:, :])
    for g in range(rw // LANES):
        sl = slice(g * LANES, (g + 1) * LANES)
        o = of_ref[:, sl].astype(F32) + ob_ref[:, sl].astype(F32)
        dv = o - _seg_mean(o, seg)
        nrm = dv * lax.rsqrt(_seg_mean(dv * dv, seg) + EPS)
        ra = (nrm * _silu(rg_ref[:, sl].astype(F32))).astype(BF16)
        acc = acc + _dot(ra, wo_ref[g * LANES:(g + 1) * LANES, :])
    o_ref[...] = x_ref[...] + _row_gate(mod_ref, 2, i, tm, n_x, d) * acc


def _l0_out(xa, mod, o_f, o_b, rg, ao, seg, wo, *, tm, n_x):
    t, d = xa.shape
    rw = o_f.shape[1]
    row = lambda i: (i, 0)
    return pl.pallas_call(
        functools.partial(_l0_out_kernel, tm=tm, n_x=n_x, d=d),
        grid=(t // tm,),
        in_specs=[pl.BlockSpec((tm, d), row), _resident(mod.shape)] + [pl.BlockSpec((tm, rw), row)] * 4
                 + [_resident(seg.shape), _resident(wo.shape)],
        out_specs=pl.BlockSpec((tm, d), row),
        out_shape=jax.ShapeDtypeStruct((t, d), F32),
        compiler_params=_cparams(("parallel",)),
    )(xa, mod, o_f, o_b, rg, ao, seg, wo)


def _ffn_kernel(x_ref, mod_ref, g_ref, w1_ref, w3_ref, w2_ref, o_ref, *, tm, n_x, d):
    i = pl.program_id(0)
    x = x_ref[...]
    h = _modulated(x, mod_ref, g_ref, 1, i, tm, n_x, d).astype(BF16)
    a = _dot(h, w1_ref[...])
    u = (_silu(a) * _dot(h, w3_ref[...])).astype(BF16)
    o_ref[...] = x + _row_gate(mod_ref, 5, i, tm, n_x, d) * _dot(u, w2_ref[...])


def _ffn(xa, mod, g, w1, w3, w2, *, tm, n_x):
    t, d = xa.shape
    row = lambda i: (i, 0)
    return pl.pallas_call(
        functools.partial(_ffn_kernel, tm=tm, n_x=n_x, d=d),
        grid=(t // tm,),
        in_specs=[pl.BlockSpec((tm, d), row), _resident(mod.shape), _resident(g.shape),
                  _resident(w1.shape), _resident(w3.shape), _resident(w2.shape)],
        out_specs=pl.BlockSpec((tm, d), row),
        out_shape=jax.ShapeDtypeStruct((t, d), F32),
        compiler_params=_cparams(("parallel",)),
    )(xa, mod, g, w1, w3, w2)


def _l1_proj_kernel(x_ref, mod_ref, g_ref, wq_ref, wkv_ref, wkr_ref, gql_ref, gkvl_ref, wuq_ref, wuk_ref,
                    wuv_ref, seg_ref, gq_ref, gk_ref, gkr_ref, c_ref, s_ref,
                    q_ref, k_ref, vlo_ref, vhi_ref, *, tm, n_x, d):
    i = pl.program_id(0)
    h = _modulated(x_ref[...], mod_ref, g_ref, 0, i, tm, n_x, d).astype(BF16)
    seg = seg_ref[...]
    cos, sin = c_ref[...], s_ref[...]

    def lora_norm(v, g):
        return (v * lax.rsqrt(jnp.mean(v * v, axis=-1, keepdims=True) + EPS) * g).astype(BF16)

    cq = lora_norm(_dot(h, wq_ref[...]), gql_ref[...])
    ckv = lora_norm(_dot(h, wkv_ref[...]), gkvl_ref[...])
    kr = _dot(h, wkr_ref[...])
    kr = kr * lax.rsqrt(_seg_mean(kr * kr, seg) + EPS) * gkr_ref[...]
    kr = _rope128(kr, cos, sin, MLA_ROPE // 2)
    qa = _dot(cq, wuq_ref[...])
    ka = _dot(ckv, wuk_ref[...])
    for hd in range(MLA_HEADS):
        sl = slice(hd * LANES, (hd + 1) * LANES)
        v = qa[:, sl]
        vn = v * lax.rsqrt(_seg_mean(v * v, seg) + EPS) * gq_ref[...]
        q_ref[:, sl] = _rope128(vn, cos, sin, MLA_ROPE // 2).astype(BF16)
        v = ka[:, sl]
        k_ref[:, sl] = (v * lax.rsqrt(_seg_mean(v * v, seg) + EPS) * gk_ref[...] + kr).astype(BF16)
    va = _dot(ckv, wuv_ref[...])
    lane = _lane(va.shape) % LANES
    low = lane < HEAD
    vlo_ref[...] = jnp.where(low, va, jnp.where(lane == LOW_ONE, 1.0, 0.0)).astype(BF16)
    vhi_ref[...] = jnp.where(low, jnp.where(lane == HIGH_ONE, 1.0, 0.0), va).astype(BF16)


def _l1_proj(xa, mod, g, wq, wkv, wkr, gql, gkvl, wuq, wuk, wuv, seg, gq, gk, gkr, cos, sin, *, tm, n_x):
    t, d = xa.shape
    row = lambda i: (i, 0)
    hw = MLA_HEADS * LANES
    vw = MLA_HEADS * HEAD
    consts = (mod, g, wq, wkv, wkr, gql, gkvl, wuq, wuk, wuv, seg, gq, gk, gkr)
    return pl.pallas_call(
        functools.partial(_l1_proj_kernel, tm=tm, n_x=n_x, d=d),
        grid=(t // tm,),
        in_specs=[pl.BlockSpec((tm, d), row)] + [_resident(a.shape) for a in consts]
                 + [pl.BlockSpec((tm, LANES), row)] * 2,
        out_specs=[pl.BlockSpec((tm, hw), row), pl.BlockSpec((tm, hw), row),
                   pl.BlockSpec((tm, vw), row), pl.BlockSpec((tm, vw), row)],
        out_shape=[jax.ShapeDtypeStruct((t, hw), BF16), jax.ShapeDtypeStruct((t, hw), BF16),
                   jax.ShapeDtypeStruct((t, vw), BF16), jax.ShapeDtypeStruct((t, vw), BF16)],
        compiler_params=_cparams(("parallel",)),
    )(xa, *consts, cos, sin)


def _l1_out_kernel(x_ref, mod_ref, g_ref, o_ref, wo_ref, rhi_ref, rlo_ref, x3_ref, h_ref, ei_ref, ew_ref, *, d):
    x3 = x_ref[...] + mod_ref[0:1, 2 * d:3 * d] * _dot(o_ref[...], wo_ref[...])
    x3_ref[...] = x3
    ms = jnp.mean(x3 * x3, axis=-1, keepdims=True)
    h = x3 * lax.rsqrt(ms + EPS) * (g_ref[...] * (1.0 + mod_ref[0:1, 4 * d:5 * d])) + mod_ref[0:1, 3 * d:4 * d]
    hi = h.astype(BF16)
    bits = lax.bitcast_convert_type(hi.astype(F32), jnp.uint32)
    words = (bits[:, :d // 2] >> 16) | (bits[:, d // 2:] & jnp.uint32(0xFFFF0000))
    h_ref[0] = words[:, :d // 4]
    h_ref[1] = words[:, d // 4:]
    lo = (h - hi.astype(F32)).astype(BF16)
    logits = _dot(hi, rhi_ref[...]) + (_dot(hi, rlo_ref[...]) + _dot(lo, rhi_ref[...]))
    lane_i = _lane(logits.shape)
    lane = lane_i.astype(F32)
    logits = jnp.where(lane_i < N_EXPERTS, logits, -jnp.inf)
    v1 = jnp.max(logits, axis=-1, keepdims=True)
    i1 = jnp.min(jnp.where(logits == v1, lane, float(LANES)), axis=-1, keepdims=True)
    rest = jnp.where(lane == i1, -jnp.inf, logits)
    v2 = jnp.max(rest, axis=-1, keepdims=True)
    i2 = jnp.min(jnp.where(rest == v2, lane, float(LANES)), axis=-1, keepdims=True)
    e2 = jnp.exp(v2 - v1)
    den = 1.0 + e2
    ei_ref[...] = jnp.where(lane_i == 0, i1, jnp.where(lane_i == 1, i2, 0.0)).astype(jnp.int32)
    ew_ref[...] = jnp.where(lane_i == 0, 1.0 / den, jnp.where(lane_i == 1, e2 / den, 0.0))


def _l1_out(xa, mod, g, o, wo, rhi, rlo, *, tm):
    n, d = o.shape[0], xa.shape[1]
    row = lambda i: (i, 0)
    return pl.pallas_call(
        functools.partial(_l1_out_kernel, d=d),
        grid=(n // tm,),
        in_specs=[pl.BlockSpec((tm, d), row), _resident(mod.shape), _resident(g.shape),
                  pl.BlockSpec((tm, o.shape[1]), row), _resident(wo.shape), _resident(rhi.shape),
                  _resident(rlo.shape)],
        out_specs=[pl.BlockSpec((tm, d), row), pl.BlockSpec((2, tm, d // 4), lambda i: (0, i, 0)),
                   pl.BlockSpec((tm, LANES), row), pl.BlockSpec((tm, LANES), row)],
        out_shape=[jax.ShapeDtypeStruct((n, d), F32), jax.ShapeDtypeStruct((2, n, d // 4), jnp.uint32),
                   jax.ShapeDtypeStruct((n, LANES), jnp.int32), jax.ShapeDtypeStruct((n, LANES), F32)],
        compiler_params=_cparams(("parallel",)),
    )(xa, mod, g, o, wo, rhi, rlo)


def _moe_kernel(te_ref, nv_ref, x_ref, w1_ref, w3_ref, w2_ref, y_ref, acc_ref, *, nf):
    i, f = pl.program_id(0), pl.program_id(1)

    @pl.when(f == 0)
    def _zero():
        acc_ref[...] = jnp.zeros_like(acc_ref)

    @pl.when(i < nv_ref[0])
    def _compute():
        words = jnp.concatenate([x_ref[0], x_ref[1]], axis=1)
        lo = lax.bitcast_convert_type(words << 16, F32)
        hi = lax.bitcast_convert_type(words & jnp.uint32(0xFFFF0000), F32)
        x = jnp.concatenate([lo, hi], axis=1).astype(BF16)
        a = _dot(x, w1_ref[0])
        u = (_silu(a) * _dot(x, w3_ref[0])).astype(BF16)
        acc_ref[...] += _dot(u, w2_ref[0])

    @pl.when(f == nf - 1)
    def _store():
        y_ref[...] = acc_ref[...].astype(y_ref.dtype)


def _moe(tile_expert, n_valid, xs, w1, w3, w2, *, tm, tf):
    p, d = xs.shape[1], 4 * xs.shape[2]
    fdim = w1.shape[2]
    nf = fdim // tf
    fi = lambda i, f, te, nv: jnp.where(i < nv[0], f, nf - 1)
    grid_spec = pltpu.PrefetchScalarGridSpec(
        num_scalar_prefetch=2,
        grid=(p // tm, nf),
        in_specs=[pl.BlockSpec((2, tm, d // 4), lambda i, f, te, nv: (0, i, 0)),
                  pl.BlockSpec((1, d, tf), lambda i, f, te, nv: (te[i], 0, fi(i, f, te, nv))),
                  pl.BlockSpec((1, d, tf), lambda i, f, te, nv: (te[i], 0, fi(i, f, te, nv))),
                  pl.BlockSpec((1, tf, d), lambda i, f, te, nv: (te[i], fi(i, f, te, nv), 0))],
        out_specs=pl.BlockSpec((tm, d), lambda i, f, te, nv: (i, 0)),
        scratch_shapes=[pltpu.VMEM((tm, d), F32)],
    )
    return pl.pallas_call(
        functools.partial(_moe_kernel, nf=nf),
        grid_spec=grid_spec,
        out_shape=jax.ShapeDtypeStruct((p, d), BF16),
        compiler_params=_cparams(("arbitrary", "arbitrary")),
    )(tile_expert, n_valid, xs, w1, w3, w2)


SC_GATHER_WINDOW = 128
SC_LANES = 16


def _gather_rows(x, idx):
    n, d = idx.shape[0], x.shape[1]
    w = SC_GATHER_WINDOW
    assert n % w == 0
    mesh = plsc.VectorSubcoreMesh(core_axis_name="core", subcore_axis_name="subcore")

    @pl.kernel(out_type=jax.ShapeDtypeStruct((n, d), x.dtype), mesh=mesh,
               scratch_types=[pltpu.SemaphoreType.DMA])
    def gather_kernel(x_hbm, i_hbm, o_hbm, sem):
        def body(i_vmem, o_vmem):
            copies = []
            for k in range(w // SC_LANES):
                grp = pl.ds(k * SC_LANES, SC_LANES)
                copies.append(pltpu.async_copy(x_hbm.at[i_vmem[0, grp]], o_vmem.at[grp], sem))
            for cp in copies:
                cp.wait()

        pltpu.emit_pipeline(
            body,
            grid=(n // w,),
            in_specs=[pl.BlockSpec((1, w), lambda i: (0, i))],
            out_specs=[pl.BlockSpec((w, d), lambda i: (i, 0))],
            core_axis_name=("core", "subcore"),
            dimension_semantics=(pltpu.PARALLEL,),
        )(i_hbm, o_hbm)

    return gather_kernel(x, idx.reshape(1, n))


def _combine_kernel(x_ref, mod_ref, ew_ref, ya_ref, yb_ref, o_ref, *, d):
    ew = ew_ref[...]
    y = ew[:, 0:1] * ya_ref[...].astype(F32) + ew[:, 1:2] * yb_ref[...].astype(F32)
    o_ref[...] = x_ref[...] + mod_ref[0:1, 5 * d:6 * d] * y


def _combine(x3, mod, ew, ya, yb, *, tm):
    n, d = x3.shape
    row = lambda i: (i, 0)
    return pl.pallas_call(
        functools.partial(_combine_kernel, d=d),
        grid=(n // tm,),
        in_specs=[pl.BlockSpec((tm, d), row), _resident(mod.shape), pl.BlockSpec((tm, LANES), row),
                  pl.BlockSpec((tm, d), row), pl.BlockSpec((tm, d), row)],
        out_specs=pl.BlockSpec((tm, d), row),
        out_shape=jax.ShapeDtypeStruct((n, d), F32),
        compiler_params=_cparams(("parallel",)),
    )(x3, mod, ew, ya, yb)


def _deinterleave(width):
    return np.concatenate([np.arange(0, width, 2), np.arange(1, width, 2)])


def _rope_tables(n_x, n_ctx, rot_dim, seg_start, seg_repeat):
    rows = n_x // GRID_W
    row = jnp.broadcast_to(jnp.arange(rows)[:, None], (rows, GRID_W)).reshape(n_x).astype(F32)
    col = jnp.broadcast_to(jnp.arange(GRID_W)[None, :], (rows, GRID_W)).reshape(n_x).astype(F32)
    axis_dim = rot_dim // 2
    inv_freq = ROPE_THETA ** (-jnp.arange(0, axis_dim, 2, dtype=F32) / axis_dim)
    ang = jnp.concatenate([row[:, None] * inv_freq, col[:, None] * inv_freq], axis=-1)
    cos, sin = jnp.cos(ang), jnp.sin(ang)
    tail = LANES - seg_start - seg_repeat * rot_dim
    c = jnp.concatenate([jnp.ones((n_x, seg_start), F32)] + [cos, cos] * seg_repeat + [jnp.ones((n_x, tail), F32)],
                        axis=-1)
    s = jnp.concatenate([jnp.zeros((n_x, seg_start), F32)] + [-sin, sin] * seg_repeat
                        + [jnp.zeros((n_x, tail), F32)], axis=-1)
    c = jnp.concatenate([c, jnp.ones((n_ctx, LANES), F32)], axis=0)
    s = jnp.concatenate([s, jnp.zeros((n_ctx, LANES), F32)], axis=0)
    return c, s


def _segment_matrix(bounds):
    m = np.zeros((LANES, LANES), np.float32)
    for lo, hi in bounds:
        m[lo:hi, lo:hi] = 1.0 / (hi - lo)
    return jnp.asarray(m, BF16)


def _token_tile(t):
    for tm in (640, 512, 256, 128):
        if t % tm == 0:
            return tm
    raise ValueError(f"token count {t} has no supported tile")


def kernel(x, c, ctx, c_ctx, l0_ada_w, l0_ada_b, l0_norm1_g, l0_norm2_g, l0_w_in, l0_ret_log_decay, l0_q_norm_g, l0_k_norm_g, l0_w_out, l0_ffn_w1, l0_ffn_w3, l0_ffn_w2, l1_ada_w, l1_ada_b, l1_norm1_g, l1_norm2_g, l1_w_in, l1_q_lora_g, l1_kv_lora_g, l1_w_uq, l1_w_ukv, l1_q_nope_g, l1_q_rope_g, l1_k_nope_g, l1_k_rope_g, l1_w_out, l1_router, l1_exp_w1, l1_exp_w3, l1_exp_w2):
    b, n_x, d = x.shape
    n_ctx = ctx.shape[1]
    assert b == 1 and n_x % 256 == 0 and n_ctx % 256 == 0 and n_x % GRID_W == 0
    t = n_x + n_ctx
    tm = _token_tile(t)
    tmx = _token_tile(n_x)
    xa = jnp.concatenate([x[0], ctx[0]], axis=0)
    row1 = lambda v: v.reshape(1, -1).astype(F32)

    cvec = jnp.zeros((8, d), F32).at[0].set(c[0]).at[1].set(c_ctx)
    mod0 = _ada(cvec, l0_ada_w, l0_ada_b)
    mod1 = _ada(cvec, l1_ada_w, l1_ada_b)

    rw = RET_HEADS * HEAD
    perm = _deinterleave(HEAD)
    n_qk = GQA_HEADS + GQA_KV_HEADS
    qk_cols = l0_w_in[:, 4 * rw:4 * rw + n_qk * HEAD].reshape(d, n_qk, HEAD // 2, 2)
    qk_cols = jnp.swapaxes(qk_cols, 2, 3).reshape(d, n_qk * HEAD)
    w_in0 = jnp.concatenate([l0_w_in[:, :rw], l0_w_in[:, rw:2 * rw] * (HEAD ** -0.5), l0_w_in[:, 2 * rw:4 * rw],
                             qk_cols, l0_w_in[:, 4 * rw + n_qk * HEAD:]], axis=1).astype(BF16)
    seg64 = _segment_matrix([(0, HEAD), (HEAD, 2 * HEAD)])
    gq0 = row1(jnp.tile(l0_q_norm_g[perm], 2) * (HEAD ** -0.5 * LOG2E))
    gk0 = row1(jnp.tile(l0_k_norm_g[perm], 2))
    cos0, sin0 = _rope_tables(n_x, n_ctx, HEAD, 0, 2)

    rq, rk, rv, rg, gq, gkx, gvx = _l0_proj(xa, mod0, row1(l0_norm1_g), w_in0, seg64, gq0, gk0, cos0, sin0,
                                             tm=tm, n_x=n_x)
    o_f, o_b = _retention(l0_ret_log_decay.astype(F32), rq, rk, rv, n_x=n_x)

    gqa_maps = dict(q_maps=(lambda j: j, lambda j: j),
                    k_maps=(lambda j: 2 * (j // 2), lambda j: 2 * (j // 2) + 1),
                    v_maps=(lambda j: 2 * (j // 2), lambda j: 2 * (j // 2) + 1), n_pairs=GQA_HEADS // 2)
    bq = 512 if n_x % 512 == 0 else 256
    ao_x = _flash(gq, gkx, (gvx, gvx), n_q=n_x, q_row0=0, kv_row0=0, n_kv=t, bq=bq, bk=_kv_block(t), rs=32,
                  **gqa_maps)
    ao_c = _flash(gq, gkx, (gvx, gvx), n_q=n_ctx, q_row0=n_x, kv_row0=n_x, n_kv=n_ctx, bq=n_ctx,
                  bk=_kv_block(n_ctx), rs=32, **gqa_maps)
    ao = jnp.concatenate([ao_x, ao_c], axis=0)

    xa = _l0_out(xa, mod0, o_f, o_b, rg, ao, seg64, l0_w_out.astype(BF16), tm=tm, n_x=n_x)
    xa = _ffn(xa, mod0, row1(l0_norm2_g), l0_ffn_w1.astype(BF16), l0_ffn_w3.astype(BF16),
              l0_ffn_w2.astype(BF16), tm=tm, n_x=n_x)

    rperm = _deinterleave(MLA_ROPE)
    qk_w = MLA_NOPE + MLA_ROPE
    wuq = jnp.zeros((MLA_Q_RANK, MLA_HEADS * LANES), F32)
    wuk = jnp.zeros((MLA_KV_RANK, MLA_HEADS * LANES), F32)
    wuv = []
    for hd in range(MLA_HEADS):
        src = l1_w_uq[:, hd * qk_w:(hd + 1) * qk_w]
        wuq = wuq.at[:, hd * LANES:hd * LANES + MLA_NOPE].set(src[:, :MLA_NOPE])
        wuq = wuq.at[:, hd * LANES + MLA_NOPE:hd * LANES + qk_w].set(src[:, MLA_NOPE:][:, rperm])
        kvsrc = l1_w_ukv[:, hd * 2 * HEAD:(hd + 1) * 2 * HEAD]
        wuk = wuk.at[:, hd * LANES:hd * LANES + MLA_NOPE].set(kvsrc[:, :MLA_NOPE])
        wuv.append(kvsrc[:, MLA_NOPE:])
    wuv = jnp.concatenate(wuv, axis=1)
    wkr = jnp.zeros((d, LANES), F32).at[:, MLA_NOPE:qk_w].set(l1_w_in[:, MLA_Q_RANK + MLA_KV_RANK:][:, rperm])
    pad = jnp.zeros((LANES - qk_w,), F32)
    zn = jnp.zeros((MLA_NOPE,), F32)
    gq1 = row1(jnp.concatenate([l1_q_nope_g, l1_q_rope_g[rperm], pad]) * (qk_w ** -0.5 * LOG2E))
    gk1 = row1(jnp.concatenate([l1_k_nope_g, jnp.zeros((LANES - MLA_NOPE,), F32)]))
    gkr1 = row1(jnp.concatenate([zn, l1_k_rope_g[rperm], pad]))
    seg_mla = _segment_matrix([(0, MLA_NOPE), (MLA_NOPE, qk_w)])
    cos1, sin1 = _rope_tables(n_x, n_ctx, MLA_ROPE, MLA_NOPE, 1)

    mq, mk, mvlo, mvhi = _l1_proj(
        xa, mod1, row1(l1_norm1_g), l1_w_in[:, :MLA_Q_RANK].astype(BF16),
        l1_w_in[:, MLA_Q_RANK:MLA_Q_RANK + MLA_KV_RANK].astype(BF16), wkr.astype(BF16),
        row1(l1_q_lora_g), row1(l1_kv_lora_g), wuq.astype(BF16), wuk.astype(BF16), wuv.astype(BF16),
        seg_mla, gq1, gk1, gkr1, cos1, sin1, tm=tm, n_x=n_x)
    mo = _flash(mq, mk, (mvlo, mvhi), q_maps=(lambda j: 2 * j, lambda j: 2 * j + 1),
                k_maps=(lambda j: 2 * j, lambda j: 2 * j + 1), v_maps=(lambda j: j, lambda j: j),
                n_pairs=MLA_HEADS // 2, n_q=n_x, q_row0=0, kv_row0=0, n_kv=t, bq=bq, bk=_kv_block(t), rs=32)

    router = jnp.zeros((d, LANES), F32).at[:, :N_EXPERTS].set(l1_router)
    r_hi = router.astype(BF16)
    r_lo = (router - r_hi.astype(F32)).astype(BF16)
    x3, hmoe, ei, ew = _l1_out(xa, mod1, row1(l1_norm2_g), mo, l1_w_out.astype(BF16), r_hi, r_lo, tm=tmx)

    tme = 512 if n_x >= 4096 else 128
    e_flat = ei[:, :TOP_K].reshape(-1)
    onehot = (e_flat[:, None] == jnp.arange(N_EXPERTS)[None, :]).astype(jnp.int32)
    csum = jnp.cumsum(onehot, axis=0)
    rank = jnp.sum((csum - onehot) * onehot, axis=1)
    counts = csum[-1]
    padded = ((counts + tme - 1) // tme) * tme
    ends = jnp.cumsum(padded)
    pos = (ends - padded)[e_flat] + rank
    n_tiles = (TOP_K * n_x) // tme + N_EXPERTS
    p_rows = n_tiles * tme
    src = jnp.zeros((p_rows,), jnp.int32).at[pos].set(jnp.arange(TOP_K * n_x, dtype=jnp.int32) // TOP_K)
    n_valid = (ends[-1] // tme).astype(jnp.int32)
    tile_ids = jnp.minimum(jnp.arange(n_tiles, dtype=jnp.int32), n_valid - 1)
    tile_expert = jnp.sum((ends[None, :] <= (tile_ids * tme)[:, None]).astype(jnp.int32), axis=1)
    tile_expert = jnp.minimum(tile_expert, N_EXPERTS - 1)
    xs = _gather_rows(hmoe.reshape(2 * n_x, d // 4), jnp.concatenate([src, src + n_x])).reshape(2, p_rows, d // 4)
    fdim = l1_exp_w1.shape[2]
    tf = fdim // 2 if (fdim // 2) % LANES == 0 else fdim
    ys = _moe(tile_expert, n_valid.reshape(1), xs, l1_exp_w1.astype(BF16), l1_exp_w3.astype(BF16),
              l1_exp_w2.astype(BF16), tm=tme, tf=tf)
    pos2 = pos.reshape(n_x, TOP_K)
    out = _combine(x3, mod1, ew, jnp.take(ys, pos2[:, 0], axis=0, mode="clip"),
                   jnp.take(ys, pos2[:, 1], axis=0, mode="clip"), tm=tmx)
    return out[None]
```

```python
import functools
import math

import numpy as np
import jax
import jax.numpy as jnp
from jax import lax
from jax.experimental import pallas as pl
from jax.experimental.pallas import tpu as pltpu
from jax.experimental.pallas import tpu_sc as plsc

F32 = jnp.float32
BF16 = jnp.bfloat16

EPS = 1e-6
ROPE_THETA = 10000.0
GRID_W = 64
LANES = 128
HEAD = 64
RET_CHUNK = 128
RET_HEADS = 8
GQA_HEADS = 8
GQA_KV_HEADS = 2
MLA_HEADS = 8
MLA_Q_RANK = 384
MLA_KV_RANK = 256
MLA_NOPE = 64
MLA_ROPE = 32
N_EXPERTS = 8
TOP_K = 2
LOW_ONE = HEAD
HIGH_ONE = 0
LOG2E = math.log2(math.e)
VMEM_LIMIT = 56 * 1024 * 1024


def _cparams(sem, vmem=VMEM_LIMIT):
    return pltpu.CompilerParams(dimension_semantics=sem, vmem_limit_bytes=vmem)


def _resident(shape):
    nd = len(shape)
    return pl.BlockSpec(shape, lambda *_: (0,) * nd, pipeline_mode=pl.Buffered(1))


def _dot(a, b):
    return jnp.dot(a, b, preferred_element_type=F32)


def _dot_nt(a, b):
    return lax.dot_general(a, b, (((1,), (1,)), ((), ())), preferred_element_type=F32)


def _seg_mean(v, seg):
    hi = v.astype(BF16)
    lo = (v - hi.astype(F32)).astype(BF16)
    return _dot(hi, seg) + _dot(lo, seg)


def _silu(x):
    return x * jax.nn.sigmoid(x)


def _modulated(x, mod_ref, g_ref, which, tile, tm, n_x, d):
    ms = jnp.mean(x * x, axis=-1, keepdims=True)
    xn = x * lax.rsqrt(ms + EPS)
    g = g_ref[...]
    sh, sc = 3 * which, 3 * which + 1
    a_x = g * (1.0 + mod_ref[0:1, sc * d:(sc + 1) * d])
    a_c = g * (1.0 + mod_ref[1:2, sc * d:(sc + 1) * d])
    b_x = mod_ref[0:1, sh * d:(sh + 1) * d]
    b_c = mod_ref[1:2, sh * d:(sh + 1) * d]
    row = tile * tm + lax.broadcasted_iota(jnp.int32, (tm, 1), 0)
    is_ctx = row >= n_x
    return xn * jnp.where(is_ctx, a_c, a_x) + jnp.where(is_ctx, b_c, b_x)


def _row_gate(mod_ref, idx, tile, tm, n_x, d):
    row = tile * tm + lax.broadcasted_iota(jnp.int32, (tm, 1), 0)
    return jnp.where(row >= n_x, mod_ref[1:2, idx * d:(idx + 1) * d], mod_ref[0:1, idx * d:(idx + 1) * d])


def _lane(shape):
    return lax.broadcasted_iota(jnp.int32, shape, len(shape) - 1)


def _ada_kernel(c_ref, w_ref, b_ref, o_ref):
    c = c_ref[...]
    o_ref[...] = jnp.dot(_silu(c), w_ref[...], preferred_element_type=F32,
                         precision=lax.Precision.HIGHEST) + b_ref[...]


def _ada(cvec8, w, b):
    d, n = w.shape
    tn = n // 4
    return pl.pallas_call(
        _ada_kernel,
        grid=(n // tn,),
        in_specs=[pl.BlockSpec((8, d), lambda j: (0, 0)),
                  pl.BlockSpec((d, tn), lambda j: (0, j)),
                  pl.BlockSpec((1, tn), lambda j: (0, j))],
        out_specs=pl.BlockSpec((8, tn), lambda j: (0, j)),
        out_shape=jax.ShapeDtypeStruct((8, n), F32),
        compiler_params=_cparams(("arbitrary",)),
    )(cvec8, w, b.reshape(1, n))


def _rope128(v, c, s, half):
    lane = _lane(v.shape)
    swapped = jnp.where(lane % (2 * half) < half, pltpu.roll(v, LANES - half, 1), pltpu.roll(v, half, 1))
    return v * c + swapped * s


def _l0_proj_kernel(x_ref, mod_ref, g_ref, w_ref, seg_ref, gq_ref, gk_ref, c_ref, s_ref,
                    rq_ref, rk_ref, rv_ref, rg_ref, q_ref, k_ref, v_ref, *, tm, n_x, d):
    i = pl.program_id(0)
    h = _modulated(x_ref[...], mod_ref, g_ref, 0, i, tm, n_x, d).astype(BF16)
    rw = RET_HEADS * HEAD
    for idx, ref in enumerate((rq_ref, rk_ref, rv_ref, rg_ref)):
        ref[...] = _dot(h, w_ref[:, idx * rw:(idx + 1) * rw]).astype(BF16)
    seg = seg_ref[...]
    cos, sin = c_ref[...], s_ref[...]
    base = 4 * rw
    qw = GQA_HEADS * HEAD
    qa = _dot(h, w_ref[:, base:base + qw])
    for g in range(qw // LANES):
        v = qa[:, g * LANES:(g + 1) * LANES]
        vn = v * lax.rsqrt(_seg_mean(v * v, seg) + EPS) * gq_ref[...]
        q_ref[:, g * LANES:(g + 1) * LANES] = _rope128(vn, cos, sin, HEAD // 2).astype(BF16)
    kv = _dot(h, w_ref[:, base + qw:base + qw + 2 * LANES])
    kk = kv[:, :LANES]
    kk = kk * lax.rsqrt(_seg_mean(kk * kk, seg) + EPS) * gk_ref[...]
    kk = _rope128(kk, cos, sin, HEAD // 2)
    vv = kv[:, LANES:]
    lane = _lane(kk.shape)
    low = lane < HEAD
    for src, ref, one in ((kk, k_ref, 0.0), (vv, v_ref, 1.0)):
        sw = pltpu.roll(src, HEAD, 1)
        lo_fill = jnp.where(lane == LOW_ONE, one, 0.0)
        hi_fill = jnp.where(lane == HIGH_ONE, one, 0.0)
        ref[:, 0 * LANES:1 * LANES] = jnp.where(low, src, lo_fill).astype(BF16)
        ref[:, 1 * LANES:2 * LANES] = jnp.where(low, hi_fill, sw).astype(BF16)
        ref[:, 2 * LANES:3 * LANES] = jnp.where(low, sw, lo_fill).astype(BF16)
        ref[:, 3 * LANES:4 * LANES] = jnp.where(low, hi_fill, src).astype(BF16)


def _l0_proj(xa, mod, g, w, seg, gq, gk, cos, sin, *, tm, n_x):
    t, d = xa.shape
    rw = RET_HEADS * HEAD
    row = lambda i: (i, 0)
    outs = [jax.ShapeDtypeStruct((t, rw), BF16)] * 7
    return pl.pallas_call(
        functools.partial(_l0_proj_kernel, tm=tm, n_x=n_x, d=d),
        grid=(t // tm,),
        in_specs=[pl.BlockSpec((tm, d), row), _resident(mod.shape), _resident(g.shape), _resident(w.shape),
                  _resident(seg.shape), _resident(gq.shape), _resident(gk.shape),
                  pl.BlockSpec((tm, LANES), row), pl.BlockSpec((tm, LANES), row)],
        out_specs=[pl.BlockSpec((tm, rw), row)] * 7,
        out_shape=outs,
        compiler_params=_cparams(("parallel",)),
    )(xa, mod, g, w, seg, gq, gk, cos, sin)


def _retention_kernel(lg_ref, qf_ref, kf_ref, vf_ref, qb_ref, kb_ref, vb_ref, of_ref, ob_ref,
                      state_ref, decay_ref, xi_ref, zeta_ref, gl_ref):
    c = RET_CHUNK
    npairs = RET_HEADS * HEAD // LANES
    step = pl.program_id(0)

    @pl.when(step == 0)
    def _init():
        state_ref[...] = jnp.zeros_like(state_ref)
        ci = lax.broadcasted_iota(jnp.int32, (c, c), 0).astype(F32)
        mi = lax.broadcasted_iota(jnp.int32, (c, c), 1).astype(F32)
        pos = lax.broadcasted_iota(jnp.int32, (c, RET_HEADS * HEAD), 0).astype(F32)
        lane_head = _lane((1, RET_HEADS * HEAD)) // HEAD
        for dr in range(2):
            lgv = jnp.zeros((1, RET_HEADS * HEAD), F32)
            for hd in range(RET_HEADS):
                lg = lg_ref[dr, hd]
                rel = (ci - mi) if dr == 0 else (mi - ci)
                half = slice((hd % 2) * c, (hd % 2 + 1) * c)
                decay_ref[dr, hd // 2, :, half] = jnp.where(rel >= 0, jnp.exp(jnp.maximum(rel, 0.0) * lg), 0.0)
                lgv = jnp.where(lane_head == hd, lg, lgv)
            p = pos if dr == 0 else (c - 1.0 - pos)
            xi_ref[dr] = jnp.exp((p + 1.0) * lgv)
            zeta_ref[dr] = jnp.exp((c - 1.0 - p) * lgv)
            gl_ref[dr] = jnp.exp(float(c) * lgv)

    low = _lane((c, LANES)) < HEAD
    r_i = lax.broadcasted_iota(jnp.int32, (LANES, LANES), 0) // HEAD
    c_i = lax.broadcasted_iota(jnp.int32, (LANES, LANES), 1) // HEAD
    blockdiag = r_i == c_i
    for dr, (q_ref, k_ref, v_ref, o_ref) in enumerate(((qf_ref, kf_ref, vf_ref, of_ref),
                                                       (qb_ref, kb_ref, vb_ref, ob_ref))):
        for j in range(npairs):
            sl = slice(j * LANES, (j + 1) * LANES)
            q, k, v = q_ref[:, sl], k_ref[:, sl], v_ref[:, sl]
            zero = jnp.zeros_like(k)
            k2 = jnp.concatenate([jnp.where(low, k, zero), jnp.where(low, zero, k)], axis=0)
            v2 = jnp.concatenate([jnp.where(low, v, zero), jnp.where(low, zero, v)], axis=0)
            s = _dot_nt(q, k2) * decay_ref[dr, j]
            o = _dot(s.astype(BF16), v2)
            st = state_ref[dr, j]
            qx = (q.astype(F32) * xi_ref[dr, :, sl]).astype(BF16)
            o = o + _dot(qx, st.astype(BF16))
            o_ref[:, sl] = o.astype(BF16)
            kz = (k.astype(F32) * zeta_ref[dr, :, sl]).T.astype(BF16)
            u = _dot(kz, v)
            state_ref[dr, j] = st * gl_ref[dr, :, sl] + jnp.where(blockdiag, u, 0.0)


def _retention(lg, rq, rk, rv, *, n_x):
    t, w = rq.shape
    c = RET_CHUNK
    nc, ncx = t // c, n_x // c
    fwd = lambda i: ((i + ncx) % nc, 0)
    bwd = lambda i: (nc - 1 - i, 0)
    blk = lambda m: pl.BlockSpec((c, w), m)
    npairs = w // LANES
    return pl.pallas_call(
        _retention_kernel,
        grid=(nc,),
        in_specs=[pl.BlockSpec(memory_space=pltpu.SMEM)] + [blk(fwd)] * 3 + [blk(bwd)] * 3,
        out_specs=[blk(fwd), blk(bwd)],
        out_shape=[jax.ShapeDtypeStruct((t, w), BF16)] * 2,
        scratch_shapes=[pltpu.VMEM((2, npairs, LANES, LANES), F32),
                        pltpu.VMEM((2, npairs, c, 2 * c), F32),
                        pltpu.VMEM((2, c, w), F32), pltpu.VMEM((2, c, w), F32),
                        pltpu.VMEM((2, 1, w), F32)],
        compiler_params=_cparams(("arbitrary",)),
    )(lg, rq, rk, rv, rq, rk, rv)


def _flash_kernel(q0_ref, q1_ref, k0_ref, k1_ref, v0_ref, v1_ref, o_ref,
                  s_ref, p_ref, a_ref, m_ref, acc_ref, *, bk, nkv, rs):
    bq = q0_ref.shape[0]
    q_refs, k_refs, v_refs = (q0_ref, q1_ref), (k0_ref, k1_ref), (v0_ref, v1_ref)
    m_ref[...] = jnp.full(m_ref.shape, -jnp.inf, F32)
    acc_ref[...] = jnp.zeros(acc_ref.shape, F32)

    def keys(t):
        return pl.ds(t * bk if isinstance(t, int) else pl.multiple_of(t * bk, bk), bk)

    def scores(t, slot):
        for h in range(2):
            s_ref[slot, h] = _dot_nt(q_refs[h][...], k_refs[h][keys(t), :])

    def softmax(slot):
        col = lambda c: slice(c * LANES, (c + 1) * LANES)
        for r in range(bq // rs):
            rows = slice(r * rs, (r + 1) * rs)
            for h in range(2):
                mx = s_ref[slot, h, rows, col(0)]
                for c in range(1, bk // LANES):
                    mx = jnp.maximum(mx, s_ref[slot, h, rows, col(c)])
                m_old = m_ref[h, rows, :]
                m_new = jnp.maximum(m_old, jnp.max(mx, axis=-1, keepdims=True))
                a_ref[slot, h, rows, :] = jnp.exp2(m_old - m_new)
                m_ref[h, rows, :] = m_new
                for c in range(bk // LANES):
                    p_ref[slot, h, rows, col(c)] = jnp.exp2(s_ref[slot, h, rows, col(c)] - m_new).astype(BF16)

    def values(t, slot):
        for h in range(2):
            acc_ref[h] = acc_ref[h] * a_ref[slot, h] + _dot(p_ref[slot, h], v_refs[h][keys(t), :])

    scores(0, 0)

    def body(i, carry):
        t = 2 * i
        scores(t + 1, 1)
        softmax(0)
        values(t, 0)
        scores(t + 2, 0)
        softmax(1)
        values(t + 1, 1)
        return carry

    n_loop = (nkv - 1) // 2
    lax.fori_loop(0, n_loop, body, 0)
    last = 2 * n_loop
    if last + 1 < nkv:
        scores(last + 1, 1)
    softmax(0)
    values(last, 0)
    if last + 1 < nkv:
        softmax(1)
        values(last + 1, 1)
    low = _lane((bq, LANES)) < HEAD
    acc0, acc1 = acc_ref[0], acc_ref[1]
    out = jnp.where(low, acc0 / acc0[:, LOW_ONE:LOW_ONE + 1], acc1 / acc1[:, HIGH_ONE:HIGH_ONE + 1])
    o_ref[...] = out.astype(o_ref.dtype)


def _flash(q, kmat, vmat, *, q_maps, k_maps, v_maps, n_q, q_row0, kv_row0, n_kv, n_pairs, bq, bk, rs):
    assert q_row0 % bq == 0 and n_q % bq == 0 and n_kv % bk == 0 and kv_row0 % n_kv == 0 and bq % rs == 0
    qb0, kb0 = q_row0 // bq, kv_row0 // n_kv
    qspec = lambda m: pl.BlockSpec((bq, LANES), lambda j, i: (i + qb0, m(j)))
    kspec = lambda m: pl.BlockSpec((n_kv, LANES), lambda j, i: (kb0, m(j)), pipeline_mode=pl.Buffered(1))
    return pl.pallas_call(
        functools.partial(_flash_kernel, bk=bk, nkv=n_kv // bk, rs=rs),
        grid=(n_pairs, n_q // bq),
        in_specs=[qspec(q_maps[0]), qspec(q_maps[1]), kspec(k_maps[0]), kspec(k_maps[1]),
                  kspec(v_maps[0]), kspec(v_maps[1])],
        out_specs=pl.BlockSpec((bq, LANES), lambda j, i: (i, j)),
        out_shape=jax.ShapeDtypeStruct((n_q, n_pairs * LANES), BF16),
        scratch_shapes=[pltpu.VMEM((2, 2, bq, bk), F32), pltpu.VMEM((2, 2, bq, bk), BF16),
                        pltpu.VMEM((2, 2, bq, LANES), F32), pltpu.VMEM((2, bq, LANES), F32),
                        pltpu.VMEM((2, bq, LANES), F32)],
        compiler_params=_cparams(("parallel", "parallel")),
    )(q, q, kmat, kmat, vmat[0], vmat[1])


def _kv_block(n_kv):
    for bk in (1280, 1024, 512, 256):
        if n_kv % bk == 0:
            return bk
    raise ValueError(f"key count {n_kv} has no supported block")


def _l0_out_kernel(x_ref, mod_ref, of_ref, ob_ref, rg_ref, ao_ref, seg_ref, wo_ref, o_ref, *, tm, n_x, d):
    i = pl.program_id(0)
    seg = seg_ref[...]
    rw = RET_HEADS * HEAD
    acc = _dot(ao_ref[...], wo_ref[rw:, :])
    for g in range(rw // LANES):
        sl = slice(g * LANES, (g + 1) * LANES)
        o = of_ref[:, sl].astype(F32) + ob_ref[:, sl].astype(F32)
        dv = o - _seg_mean(o, seg)
        nrm = dv * lax.rsqrt(_seg_mean(dv * dv, seg) + EPS)
        ra = (nrm * _silu(rg_ref[:, sl].astype(F32))).astype(BF16)
        acc = acc + _dot(ra, wo_ref[g * LANES:(g + 1) * LANES, :])
    o_ref[...] = x_ref[...] + _row_gate(mod_ref, 2, i, tm, n_x, d) * acc


def _l0_out(xa, mod, o_f, o_b, rg, ao, seg, wo, *, tm, n_x):
    t, d = xa.shape
    rw = o_f.shape[1]
    row = lambda i: (i, 0)
    return pl.pallas_call(
        functools.partial(_l0_out_kernel, tm=tm, n_x=n_x, d=d),
        grid=(t // tm,),
        in_specs=[pl.BlockSpec((tm, d), row), _resident(mod.shape)] + [pl.BlockSpec((tm, rw), row)] * 4
                 + [_resident(seg.shape), _resident(wo.shape)],
        out_specs=pl.BlockSpec((tm, d), row),
        out_shape=jax.ShapeDtypeStruct((t, d), F32),
        compiler_params=_cparams(("parallel",)),
    )(xa, mod, o_f, o_b, rg, ao, seg, wo)


def _ffn_kernel(x_ref, mod_ref, g_ref, w1_ref, w3_ref, w2_ref, o_ref, *, tm, n_x, d):
    i = pl.program_id(0)
    x = x_ref[...]
    h = _modulated(x, mod_ref, g_ref, 1, i, tm, n_x, d).astype(BF16)
    a = _dot(h, w1_ref[...])
    u = (_silu(a) * _dot(h, w3_ref[...])).astype(BF16)
    o_ref[...] = x + _row_gate(mod_ref, 5, i, tm, n_x, d) * _dot(u, w2_ref[...])


def _ffn(xa, mod, g, w1, w3, w2, *, tm, n_x):
    t, d = xa.shape
    row = lambda i: (i, 0)
    return pl.pallas_call(
        functools.partial(_ffn_kernel, tm=tm, n_x=n_x, d=d),
        grid=(t // tm,),
        in_specs=[pl.BlockSpec((tm, d), row), _resident(mod.shape), _resident(g.shape),
                  _resident(w1.shape), _resident(w3.shape), _resident(w2.shape)],
        out_specs=pl.BlockSpec((tm, d), row),
        out_shape=jax.ShapeDtypeStruct((t, d), F32),
        compiler_params=_cparams(("parallel",)),
    )(xa, mod, g, w1, w3, w2)


def _l1_proj_kernel(x_ref, mod_ref, g_ref, wq_ref, wkv_ref, wkr_ref, gql_ref, gkvl_ref, wuq_ref, wuk_ref,
                    wuv_ref, seg_ref, gq_ref, gk_ref, gkr_ref, c_ref, s_ref,
                    q_ref, k_ref, vlo_ref, vhi_ref, *, tm, n_x, d):
    i = pl.program_id(0)
    h = _modulated(x_ref[...], mod_ref, g_ref, 0, i, tm, n_x, d).astype(BF16)
    seg = seg_ref[...]
    cos, sin = c_ref[...], s_ref[...]

    def lora_norm(v, g):
        return (v * lax.rsqrt(jnp.mean(v * v, axis=-1, keepdims=True) + EPS) * g).astype(BF16)

    cq = lora_norm(_dot(h, wq_ref[...]), gql_ref[...])
    ckv = lora_norm(_dot(h, wkv_ref[...]), gkvl_ref[...])
    kr = _dot(h, wkr_ref[...])
    kr = kr * lax.rsqrt(_seg_mean(kr * kr, seg) + EPS) * gkr_ref[...]
    kr = _rope128(kr, cos, sin, MLA_ROPE // 2)
    qa = _dot(cq, wuq_ref[...])
    ka = _dot(ckv, wuk_ref[...])
    for hd in range(MLA_HEADS):
        sl = slice(hd * LANES, (hd + 1) * LANES)
        v = qa[:, sl]
        vn = v * lax.rsqrt(_seg_mean(v * v, seg) + EPS) * gq_ref[...]
        q_ref[:, sl] = _rope128(vn, cos, sin, MLA_ROPE // 2).astype(BF16)
        v = ka[:, sl]
        k_ref[:, sl] = (v * lax.rsqrt(_seg_mean(v * v, seg) + EPS) * gk_ref[...] + kr).astype(BF16)
    va = _dot(ckv, wuv_ref[...])
    lane = _lane(va.shape) % LANES
    low = lane < HEAD
    vlo_ref[...] = jnp.where(low, va, jnp.where(lane == LOW_ONE, 1.0, 0.0)).astype(BF16)
    vhi_ref[...] = jnp.where(low, jnp.where(lane == HIGH_ONE, 1.0, 0.0), va).astype(BF16)


def _l1_proj(xa, mod, g, wq, wkv, wkr, gql, gkvl, wuq, wuk, wuv, seg, gq, gk, gkr, cos, sin, *, tm, n_x):
    t, d = xa.shape
    row = lambda i: (i, 0)
    hw = MLA_HEADS * LANES
    vw = MLA_HEADS * HEAD
    consts = (mod, g, wq, wkv, wkr, gql, gkvl, wuq, wuk, wuv, seg, gq, gk, gkr)
    return pl.pallas_call(
        functools.partial(_l1_proj_kernel, tm=tm, n_x=n_x, d=d),
        grid=(t // tm,),
        in_specs=[pl.BlockSpec((tm, d), row)] + [_resident(a.shape) for a in consts]
                 + [pl.BlockSpec((tm, LANES), row)] * 2,
        out_specs=[pl.BlockSpec((tm, hw), row), pl.BlockSpec((tm, hw), row),
                   pl.BlockSpec((tm, vw), row), pl.BlockSpec((tm, vw), row)],
        out_shape=[jax.ShapeDtypeStruct((t, hw), BF16), jax.ShapeDtypeStruct((t, hw), BF16),
                   jax.ShapeDtypeStruct((t, vw), BF16), jax.ShapeDtypeStruct((t, vw), BF16)],
        compiler_params=_cparams(("parallel",)),
    )(xa, *consts, cos, sin)


def _l1_out_kernel(x_ref, mod_ref, g_ref, o_ref, wo_ref, rhi_ref, rlo_ref, x3_ref, h_ref, ei_ref, ew_ref, *, d):
    x3 = x_ref[...] + mod_ref[0:1, 2 * d:3 * d] * _dot(o_ref[...], wo_ref[...])
    x3_ref[...] = x3
    ms = jnp.mean(x3 * x3, axis=-1, keepdims=True)
    h = x3 * lax.rsqrt(ms + EPS) * (g_ref[...] * (1.0 + mod_ref[0:1, 4 * d:5 * d])) + mod_ref[0:1, 3 * d:4 * d]
    hi = h.astype(BF16)
    bits = lax.bitcast_convert_type(hi.astype(F32), jnp.uint32)
    words = (bits[:, :d // 2] >> 16) | (bits[:, d // 2:] & jnp.uint32(0xFFFF0000))
    h_ref[0] = words[:, :d // 4]
    h_ref[1] = words[:, d // 4:]
    lo = (h - hi.astype(F32)).astype(BF16)
    logits = _dot(hi, rhi_ref[...]) + (_dot(hi, rlo_ref[...]) + _dot(lo, rhi_ref[...]))
    lane_i = _lane(logits.shape)
    lane = lane_i.astype(F32)
    logits = jnp.where(lane_i < N_EXPERTS, logits, -jnp.inf)
    v1 = jnp.max(logits, axis=-1, keepdims=True)
    i1 = jnp.min(jnp.where(logits == v1, lane, float(LANES)), axis=-1, keepdims=True)
    rest = jnp.where(lane == i1, -jnp.inf, logits)
    v2 = jnp.max(rest, axis=-1, keepdims=True)
    i2 = jnp.min(jnp.where(rest == v2, lane, float(LANES)), axis=-1, keepdims=True)
    e2 = jnp.exp(v2 - v1)
    den = 1.0 + e2
    ei_ref[...] = jnp.where(lane_i == 0, i1, jnp.where(lane_i == 1, i2, 0.0)).astype(jnp.int32)
    ew_ref[...] = jnp.where(lane_i == 0, 1.0 / den, jnp.where(lane_i == 1, e2 / den, 0.0))


def _l1_out(xa, mod, g, o, wo, rhi, rlo, *, tm):
    n, d = o.shape[0], xa.shape[1]
    row = lambda i: (i, 0)
    return pl.pallas_call(
        functools.partial(_l1_out_kernel, d=d),
        grid=(n // tm,),
        in_specs=[pl.BlockSpec((tm, d), row), _resident(mod.shape), _resident(g.shape),
                  pl.BlockSpec((tm, o.shape[1]), row), _resident(wo.shape), _resident(rhi.shape),
                  _resident(rlo.shape)],
        out_specs=[pl.BlockSpec((tm, d), row), pl.BlockSpec((2, tm, d // 4), lambda i: (0, i, 0)),
                   pl.BlockSpec((tm, LANES), row), pl.BlockSpec((tm, LANES), row)],
        out_shape=[jax.ShapeDtypeStruct((n, d), F32), jax.ShapeDtypeStruct((2, n, d // 4), jnp.uint32),
                   jax.ShapeDtypeStruct((n, LANES), jnp.int32), jax.ShapeDtypeStruct((n, LANES), F32)],
        compiler_params=_cparams(("parallel",)),
    )(xa, mod, g, o, wo, rhi, rlo)


def _moe_kernel(te_ref, nv_ref, x_ref, w1_ref, w3_ref, w2_ref, *rest, nf):
    y_ref, acc_ref = rest[-2:]
    i, f = pl.program_id(0), pl.program_id(1)

    @pl.when(f == 0)
    def _zero():
        acc_ref[...] = jnp.zeros_like(acc_ref)

    @pl.when(i < nv_ref[0])
    def _compute():
        words = jnp.concatenate([x_ref[0], x_ref[1]], axis=1)
        lo = lax.bitcast_convert_type(words << 16, F32)
        hi = lax.bitcast_convert_type(words & jnp.uint32(0xFFFF0000), F32)
        x = jnp.concatenate([lo, hi], axis=1).astype(BF16)
        a = _dot(x, w1_ref[0])
        u = (_silu(a) * _dot(x, w3_ref[0])).astype(BF16)
        acc_ref[...] += _dot(u, w2_ref[0])

    @pl.when(f == nf - 1)
    def _store():
        y_ref[...] = acc_ref[...].astype(y_ref.dtype)


def _moe(tile_expert, n_valid, xs, w1, w3, w2, *, tm, tf, tile0, total_tiles, earlier=None):
    p, d = xs.shape[1], 4 * xs.shape[2]
    fdim = w1.shape[2]
    nf = fdim // tf
    fi = lambda i, f, te, nv: jnp.where(i < nv[0], f, nf - 1)
    in_specs = [pl.BlockSpec((2, tm, d // 4), lambda i, f, te, nv: (0, i, 0)),
                pl.BlockSpec((1, d, tf), lambda i, f, te, nv: (te[i], 0, fi(i, f, te, nv))),
                pl.BlockSpec((1, d, tf), lambda i, f, te, nv: (te[i], 0, fi(i, f, te, nv))),
                pl.BlockSpec((1, tf, d), lambda i, f, te, nv: (te[i], fi(i, f, te, nv), 0))]
    args = [tile_expert, n_valid, xs, w1, w3, w2]
    aliases = {}
    if earlier is not None:
        in_specs.append(pl.BlockSpec(memory_space=pl.ANY))
        aliases = {len(args): 0}
        args.append(earlier)
    grid_spec = pltpu.PrefetchScalarGridSpec(
        num_scalar_prefetch=2,
        grid=(p // tm, nf),
        in_specs=in_specs,
        out_specs=pl.BlockSpec((tm, d), lambda i, f, te, nv: (i + tile0, 0)),
        scratch_shapes=[pltpu.VMEM((tm, d), F32)],
    )
    return pl.pallas_call(
        functools.partial(_moe_kernel, nf=nf),
        grid_spec=grid_spec,
        out_shape=jax.ShapeDtypeStruct((total_tiles * tm, d), BF16),
        input_output_aliases=aliases,
        compiler_params=_cparams(("arbitrary", "arbitrary")),
    )(*args)


SC_GATHER_WINDOW = 128
SC_LANES = 16


def _gather_rows(x, idx):
    n, d = idx.shape[0], x.shape[1]
    w = SC_GATHER_WINDOW
    assert n % w == 0
    mesh = plsc.VectorSubcoreMesh(core_axis_name="core", subcore_axis_name="subcore")

    @pl.kernel(out_type=jax.ShapeDtypeStruct((n, d), x.dtype), mesh=mesh,
               scratch_types=[pltpu.SemaphoreType.DMA])
    def gather_kernel(x_hbm, i_hbm, o_hbm, sem):
        def body(i_vmem, o_vmem):
            copies = []
            for k in range(w // SC_LANES):
                grp = pl.ds(k * SC_LANES, SC_LANES)
                copies.append(pltpu.async_copy(x_hbm.at[i_vmem[0, grp]], o_vmem.at[grp], sem))
            for cp in copies:
                cp.wait()

        pltpu.emit_pipeline(
            body,
            grid=(n // w,),
            in_specs=[pl.BlockSpec((1, w), lambda i: (0, i))],
            out_specs=[pl.BlockSpec((w, d), lambda i: (i, 0))],
            core_axis_name=("core", "subcore"),
            dimension_semantics=(pltpu.PARALLEL,),
        )(i_hbm, o_hbm)

    return gather_kernel(x, idx.reshape(1, n))


def _combine_kernel(x_ref, mod_ref, ew_ref, ya_ref, yb_ref, o_ref, *, d):
    ew = ew_ref[...]
    y = ew[:, 0:1] * ya_ref[...].astype(F32) + ew[:, 1:2] * yb_ref[...].astype(F32)
    o_ref[...] = x_ref[...] + mod_ref[0:1, 5 * d:6 * d] * y


def _combine(x3, mod, ew, ya, yb, *, tm):
    n, d = x3.shape
    row = lambda i: (i, 0)
    return pl.pallas_call(
        functools.partial(_combine_kernel, d=d),
        grid=(n // tm,),
        in_specs=[pl.BlockSpec((tm, d), row), _resident(mod.shape), pl.BlockSpec((tm, LANES), row),
                  pl.BlockSpec((tm, d), row), pl.BlockSpec((tm, d), row)],
        out_specs=pl.BlockSpec((tm, d), row),
        out_shape=jax.ShapeDtypeStruct((n, d), F32),
        compiler_params=_cparams(("parallel",)),
    )(x3, mod, ew, ya, yb)


def _deinterleave(width):
    return np.concatenate([np.arange(0, width, 2), np.arange(1, width, 2)])


def _rope_tables(n_x, n_ctx, rot_dim, seg_start, seg_repeat):
    rows = n_x // GRID_W
    row = jnp.broadcast_to(jnp.arange(rows)[:, None], (rows, GRID_W)).reshape(n_x).astype(F32)
    col = jnp.broadcast_to(jnp.arange(GRID_W)[None, :], (rows, GRID_W)).reshape(n_x).astype(F32)
    axis_dim = rot_dim // 2
    inv_freq = ROPE_THETA ** (-jnp.arange(0, axis_dim, 2, dtype=F32) / axis_dim)
    ang = jnp.concatenate([row[:, None] * inv_freq, col[:, None] * inv_freq], axis=-1)
    cos, sin = jnp.cos(ang), jnp.sin(ang)
    tail = LANES - seg_start - seg_repeat * rot_dim
    c = jnp.concatenate([jnp.ones((n_x, seg_start), F32)] + [cos, cos] * seg_repeat + [jnp.ones((n_x, tail), F32)],
                        axis=-1)
    s = jnp.concatenate([jnp.zeros((n_x, seg_start), F32)] + [-sin, sin] * seg_repeat
                        + [jnp.zeros((n_x, tail), F32)], axis=-1)
    c = jnp.concatenate([c, jnp.ones((n_ctx, LANES), F32)], axis=0)
    s = jnp.concatenate([s, jnp.zeros((n_ctx, LANES), F32)], axis=0)
    return c, s


def _segment_matrix(bounds):
    m = np.zeros((LANES, LANES), np.float32)
    for lo, hi in bounds:
        m[lo:hi, lo:hi] = 1.0 / (hi - lo)
    return jnp.asarray(m, BF16)


def _token_tile(t):
    for tm in (640, 512, 256, 128):
        if t % tm == 0:
            return tm
    raise ValueError(f"token count {t} has no supported tile")


def kernel(x, c, ctx, c_ctx, l0_ada_w, l0_ada_b, l0_norm1_g, l0_norm2_g, l0_w_in, l0_ret_log_decay, l0_q_norm_g, l0_k_norm_g, l0_w_out, l0_ffn_w1, l0_ffn_w3, l0_ffn_w2, l1_ada_w, l1_ada_b, l1_norm1_g, l1_norm2_g, l1_w_in, l1_q_lora_g, l1_kv_lora_g, l1_w_uq, l1_w_ukv, l1_q_nope_g, l1_q_rope_g, l1_k_nope_g, l1_k_rope_g, l1_w_out, l1_router, l1_exp_w1, l1_exp_w3, l1_exp_w2):
    b, n_x, d = x.shape
    n_ctx = ctx.shape[1]
    assert b == 1 and n_x % 256 == 0 and n_ctx % 256 == 0 and n_x % GRID_W == 0
    t = n_x + n_ctx
    tm = _token_tile(t)
    tmx = _token_tile(n_x)
    xa = jnp.concatenate([x[0], ctx[0]], axis=0)
    row1 = lambda v: v.reshape(1, -1).astype(F32)

    cvec = jnp.zeros((8, d), F32).at[0].set(c[0]).at[1].set(c_ctx)
    mod0 = _ada(cvec, l0_ada_w, l0_ada_b)
    mod1 = _ada(cvec, l1_ada_w, l1_ada_b)

    rw = RET_HEADS * HEAD
    perm = _deinterleave(HEAD)
    n_qk = GQA_HEADS + GQA_KV_HEADS
    qk_cols = l0_w_in[:, 4 * rw:4 * rw + n_qk * HEAD].reshape(d, n_qk, HEAD // 2, 2)
    qk_cols = jnp.swapaxes(qk_cols, 2, 3).reshape(d, n_qk * HEAD)
    w_in0 = jnp.concatenate([l0_w_in[:, :rw], l0_w_in[:, rw:2 * rw] * (HEAD ** -0.5), l0_w_in[:, 2 * rw:4 * rw],
                             qk_cols, l0_w_in[:, 4 * rw + n_qk * HEAD:]], axis=1).astype(BF16)
    seg64 = _segment_matrix([(0, HEAD), (HEAD, 2 * HEAD)])
    gq0 = row1(jnp.tile(l0_q_norm_g[perm], 2) * (HEAD ** -0.5 * LOG2E))
    gk0 = row1(jnp.tile(l0_k_norm_g[perm], 2))
    cos0, sin0 = _rope_tables(n_x, n_ctx, HEAD, 0, 2)

    rq, rk, rv, rg, gq, gkx, gvx = _l0_proj(xa, mod0, row1(l0_norm1_g), w_in0, seg64, gq0, gk0, cos0, sin0,
                                             tm=tm, n_x=n_x)
    o_f, o_b = _retention(l0_ret_log_decay.astype(F32), rq, rk, rv, n_x=n_x)

    gqa_maps = dict(q_maps=(lambda j: j, lambda j: j),
                    k_maps=(lambda j: 2 * (j // 2), lambda j: 2 * (j // 2) + 1),
                    v_maps=(lambda j: 2 * (j // 2), lambda j: 2 * (j // 2) + 1), n_pairs=GQA_HEADS // 2)
    bq = 512 if n_x % 512 == 0 else 256
    ao_x = _flash(gq, gkx, (gvx, gvx), n_q=n_x, q_row0=0, kv_row0=0, n_kv=t, bq=bq, bk=_kv_block(t), rs=32,
                  **gqa_maps)
    ao_c = _flash(gq, gkx, (gvx, gvx), n_q=n_ctx, q_row0=n_x, kv_row0=n_x, n_kv=n_ctx, bq=n_ctx,
                  bk=_kv_block(n_ctx), rs=32, **gqa_maps)
    ao = jnp.concatenate([ao_x, ao_c], axis=0)

    xa = _l0_out(xa, mod0, o_f, o_b, rg, ao, seg64, l0_w_out.astype(BF16), tm=tm, n_x=n_x)
    xa = _ffn(xa, mod0, row1(l0_norm2_g), l0_ffn_w1.astype(BF16), l0_ffn_w3.astype(BF16),
              l0_ffn_w2.astype(BF16), tm=tm, n_x=n_x)

    rperm = _deinterleave(MLA_ROPE)
    qk_w = MLA_NOPE + MLA_ROPE
    wuq = jnp.zeros((MLA_Q_RANK, MLA_HEADS * LANES), F32)
    wuk = jnp.zeros((MLA_KV_RANK, MLA_HEADS * LANES), F32)
    wuv = []
    for hd in range(MLA_HEADS):
        src = l1_w_uq[:, hd * qk_w:(hd + 1) * qk_w]
        wuq = wuq.at[:, hd * LANES:hd * LANES + MLA_NOPE].set(src[:, :MLA_NOPE])
        wuq = wuq.at[:, hd * LANES + MLA_NOPE:hd * LANES + qk_w].set(src[:, MLA_NOPE:][:, rperm])
        kvsrc = l1_w_ukv[:, hd * 2 * HEAD:(hd + 1) * 2 * HEAD]
        wuk = wuk.at[:, hd * LANES:hd * LANES + MLA_NOPE].set(kvsrc[:, :MLA_NOPE])
        wuv.append(kvsrc[:, MLA_NOPE:])
    wuv = jnp.concatenate(wuv, axis=1)
    wkr = jnp.zeros((d, LANES), F32).at[:, MLA_NOPE:qk_w].set(l1_w_in[:, MLA_Q_RANK + MLA_KV_RANK:][:, rperm])
    pad = jnp.zeros((LANES - qk_w,), F32)
    zn = jnp.zeros((MLA_NOPE,), F32)
    gq1 = row1(jnp.concatenate([l1_q_nope_g, l1_q_rope_g[rperm], pad]) * (qk_w ** -0.5 * LOG2E))
    gk1 = row1(jnp.concatenate([l1_k_nope_g, jnp.zeros((LANES - MLA_NOPE,), F32)]))
    gkr1 = row1(jnp.concatenate([zn, l1_k_rope_g[rperm], pad]))
    seg_mla = _segment_matrix([(0, MLA_NOPE), (MLA_NOPE, qk_w)])
    cos1, sin1 = _rope_tables(n_x, n_ctx, MLA_ROPE, MLA_NOPE, 1)

    mq, mk, mvlo, mvhi = _l1_proj(
        xa, mod1, row1(l1_norm1_g), l1_w_in[:, :MLA_Q_RANK].astype(BF16),
        l1_w_in[:, MLA_Q_RANK:MLA_Q_RANK + MLA_KV_RANK].astype(BF16), wkr.astype(BF16),
        row1(l1_q_lora_g), row1(l1_kv_lora_g), wuq.astype(BF16), wuk.astype(BF16), wuv.astype(BF16),
        seg_mla, gq1, gk1, gkr1, cos1, sin1, tm=tm, n_x=n_x)
    mo = _flash(mq, mk, (mvlo, mvhi), q_maps=(lambda j: 2 * j, lambda j: 2 * j + 1),
                k_maps=(lambda j: 2 * j, lambda j: 2 * j + 1), v_maps=(lambda j: j, lambda j: j),
                n_pairs=MLA_HEADS // 2, n_q=n_x, q_row0=0, kv_row0=0, n_kv=t, bq=bq, bk=_kv_block(t), rs=32)

    router = jnp.zeros((d, LANES), F32).at[:, :N_EXPERTS].set(l1_router)
    r_hi = router.astype(BF16)
    r_lo = (router - r_hi.astype(F32)).astype(BF16)
    x3, hmoe, ei, ew = _l1_out(xa, mod1, row1(l1_norm2_g), mo, l1_w_out.astype(BF16), r_hi, r_lo, tm=tmx)

    tme = 512 if n_x >= 4096 else 128
    e_flat = ei[:, :TOP_K].reshape(-1)
    onehot = (e_flat[:, None] == jnp.arange(N_EXPERTS)[None, :]).astype(jnp.int32)
    csum = jnp.cumsum(onehot, axis=0)
    rank = jnp.sum((csum - onehot) * onehot, axis=1)
    counts = csum[-1]
    padded = ((counts + tme - 1) // tme) * tme
    ends = jnp.cumsum(padded)
    pos = (ends - padded)[e_flat] + rank
    n_tiles = (TOP_K * n_x) // tme + N_EXPERTS
    p_rows = n_tiles * tme
    src = jnp.zeros((p_rows,), jnp.int32).at[pos].set(jnp.arange(TOP_K * n_x, dtype=jnp.int32) // TOP_K)
    n_valid = (ends[-1] // tme).astype(jnp.int32)
    tile_ids = jnp.minimum(jnp.arange(n_tiles, dtype=jnp.int32), n_valid - 1)
    tile_expert = jnp.sum((ends[None, :] <= (tile_ids * tme)[:, None]).astype(jnp.int32), axis=1)
    tile_expert = jnp.minimum(tile_expert, N_EXPERTS - 1)
    fdim = l1_exp_w1.shape[2]
    tf = fdim // 2 if (fdim // 2) % LANES == 0 else fdim
    ew1, ew3, ew2 = l1_exp_w1.astype(BF16), l1_exp_w3.astype(BF16), l1_exp_w2.astype(BF16)
    hm_rows = hmoe.reshape(2 * n_x, d // 4)
    n_stage = 2 if n_tiles % 2 == 0 else 1
    st_tiles = n_tiles // n_stage
    ys = None
    for st in range(n_stage):
        t0 = st * st_tiles
        src_st = src[t0 * tme:(t0 + st_tiles) * tme]
        xs = _gather_rows(hm_rows, jnp.concatenate([src_st, src_st + n_x])).reshape(2, st_tiles * tme, d // 4)
        nv_st = jnp.clip(n_valid - t0, 0, st_tiles).reshape(1)
        ys = _moe(tile_expert[t0:t0 + st_tiles], nv_st, xs, ew1, ew3, ew2, tm=tme, tf=tf, tile0=t0,
                  total_tiles=n_tiles, earlier=ys)
    pos2 = pos.reshape(n_x, TOP_K)
    out = _combine(x3, mod1, ew, jnp.take(ys, pos2[:, 0], axis=0, mode="clip"),
                   jnp.take(ys, pos2[:, 1], axis=0, mode="clip"), tm=tmx)
    return out[None]
```

```python
import functools
import math

import numpy as np
import jax
import jax.numpy as jnp
from jax import lax
from jax.experimental import pallas as pl
from jax.experimental.pallas import tpu as pltpu
from jax.experimental.pallas import tpu_sc as plsc

F32 = jnp.float32
BF16 = jnp.bfloat16

EPS = 1e-6
ROPE_THETA = 10000.0
GRID_W = 64
LANES = 128
HEAD = 64
RET_CHUNK = 128
RET_HEADS = 8
GQA_HEADS = 8
GQA_KV_HEADS = 2
MLA_HEADS = 8
MLA_Q_RANK = 384
MLA_KV_RANK = 256
MLA_NOPE = 64
MLA_ROPE = 32
N_EXPERTS = 8
TOP_K = 2
LOW_ONE = HEAD
HIGH_ONE = 0
LOG2E = math.log2(math.e)
VMEM_LIMIT = 56 * 1024 * 1024


def _cparams(sem, vmem=VMEM_LIMIT):
    return pltpu.CompilerParams(dimension_semantics=sem, vmem_limit_bytes=vmem)


def _resident(shape):
    nd = len(shape)
    return pl.BlockSpec(shape, lambda *_: (0,) * nd, pipeline_mode=pl.Buffered(1))


def _dot(a, b):
    return jnp.dot(a, b, preferred_element_type=F32)


def _dot_nt(a, b):
    return lax.dot_general(a, b, (((1,), (1,)), ((), ())), preferred_element_type=F32)


def _seg_mean(v, seg):
    hi = v.astype(BF16)
    lo = (v - hi.astype(F32)).astype(BF16)
    return _dot(hi, seg) + _dot(lo, seg)


def _silu(x):
    return x * jax.nn.sigmoid(x)


def _modulated(x, mod_ref, g_ref, which, tile, tm, n_x, d):
    ms = jnp.mean(x * x, axis=-1, keepdims=True)
    xn = x * lax.rsqrt(ms + EPS)
    g = g_ref[...]
    sh, sc = 3 * which, 3 * which + 1
    a_x = g * (1.0 + mod_ref[0:1, sc * d:(sc + 1) * d])
    a_c = g * (1.0 + mod_ref[1:2, sc * d:(sc + 1) * d])
    b_x = mod_ref[0:1, sh * d:(sh + 1) * d]
    b_c = mod_ref[1:2, sh * d:(sh + 1) * d]
    row = tile * tm + lax.broadcasted_iota(jnp.int32, (tm, 1), 0)
    is_ctx = row >= n_x
    return xn * jnp.where(is_ctx, a_c, a_x) + jnp.where(is_ctx, b_c, b_x)


def _row_gate(mod_ref, idx, tile, tm, n_x, d):
    row = tile * tm + lax.broadcasted_iota(jnp.int32, (tm, 1), 0)
    return jnp.where(row >= n_x, mod_ref[1:2, idx * d:(idx + 1) * d], mod_ref[0:1, idx * d:(idx + 1) * d])


def _lane(shape):
    return lax.broadcasted_iota(jnp.int32, shape, len(shape) - 1)


def _ada_kernel(c_ref, w_ref, b_ref, o_ref):
    c = c_ref[...]
    o_ref[...] = jnp.dot(_silu(c), w_ref[...], preferred_element_type=F32,
                         precision=lax.Precision.HIGHEST) + b_ref[...]


def _ada(cvec8, w, b):
    d, n = w.shape
    tn = n // 4
    return pl.pallas_call(
        _ada_kernel,
        grid=(n // tn,),
        in_specs=[pl.BlockSpec((8, d), lambda j: (0, 0)),
                  pl.BlockSpec((d, tn), lambda j: (0, j)),
                  pl.BlockSpec((1, tn), lambda j: (0, j))],
        out_specs=pl.BlockSpec((8, tn), lambda j: (0, j)),
        out_shape=jax.ShapeDtypeStruct((8, n), F32),
        compiler_params=_cparams(("arbitrary",)),
    )(cvec8, w, b.reshape(1, n))


def _rope128(v, c, s, half):
    lane = _lane(v.shape)
    swapped = jnp.where(lane % (2 * half) < half, pltpu.roll(v, LANES - half, 1), pltpu.roll(v, half, 1))
    return v * c + swapped * s


def _l0_proj_kernel(x_ref, mod_ref, g_ref, w_ref, seg_ref, gq_ref, gk_ref, c_ref, s_ref,
                    rq_ref, rk_ref, rv_ref, rg_ref, q_ref, k_ref, v_ref, *, tm, n_x, d):
    i = pl.program_id(0)
    h = _modulated(x_ref[...], mod_ref, g_ref, 0, i, tm, n_x, d).astype(BF16)
    rw = RET_HEADS * HEAD
    for idx, ref in enumerate((rq_ref, rk_ref, rv_ref, rg_ref)):
        ref[...] = _dot(h, w_ref[:, idx * rw:(idx + 1) * rw]).astype(BF16)
    seg = seg_ref[...]
    cos, sin = c_ref[...], s_ref[...]
    base = 4 * rw
    qw = GQA_HEADS * HEAD
    qa = _dot(h, w_ref[:, base:base + qw])
    for g in range(qw // LANES):
        v = qa[:, g * LANES:(g + 1) * LANES]
        vn = v * lax.rsqrt(_seg_mean(v * v, seg) + EPS) * gq_ref[...]
        q_ref[:, g * LANES:(g + 1) * LANES] = _rope128(vn, cos, sin, HEAD // 2).astype(BF16)
    kv = _dot(h, w_ref[:, base + qw:base + qw + 2 * LANES])
    kk = kv[:, :LANES]
    kk = kk * lax.rsqrt(_seg_mean(kk * kk, seg) + EPS) * gk_ref[...]
    kk = _rope128(kk, cos, sin, HEAD // 2)
    vv = kv[:, LANES:]
    lane = _lane(kk.shape)
    low = lane < HEAD
    for src, ref, one in ((kk, k_ref, 0.0), (vv, v_ref, 1.0)):
        sw = pltpu.roll(src, HEAD, 1)
        lo_fill = jnp.where(lane == LOW_ONE, one, 0.0)
        hi_fill = jnp.where(lane == HIGH_ONE, one, 0.0)
        ref[:, 0 * LANES:1 * LANES] = jnp.where(low, src, lo_fill).astype(BF16)
        ref[:, 1 * LANES:2 * LANES] = jnp.where(low, hi_fill, sw).astype(BF16)
        ref[:, 2 * LANES:3 * LANES] = jnp.where(low, sw, lo_fill).astype(BF16)
        ref[:, 3 * LANES:4 * LANES] = jnp.where(low, hi_fill, src).astype(BF16)


def _l0_proj(xa, mod, g, w, seg, gq, gk, cos, sin, *, tm, n_x):
    t, d = xa.shape
    rw = RET_HEADS * HEAD
    row = lambda i: (i, 0)
    outs = [jax.ShapeDtypeStruct((t, rw), BF16)] * 7
    return pl.pallas_call(
        functools.partial(_l0_proj_kernel, tm=tm, n_x=n_x, d=d),
        grid=(t // tm,),
        in_specs=[pl.BlockSpec((tm, d), row), _resident(mod.shape), _resident(g.shape), _resident(w.shape),
                  _resident(seg.shape), _resident(gq.shape), _resident(gk.shape),
                  pl.BlockSpec((tm, LANES), row), pl.BlockSpec((tm, LANES), row)],
        out_specs=[pl.BlockSpec((tm, rw), row)] * 7,
        out_shape=outs,
        compiler_params=_cparams(("parallel",)),
    )(xa, mod, g, w, seg, gq, gk, cos, sin)


def _retention_kernel(lg_ref, qf_ref, kf_ref, vf_ref, qb_ref, kb_ref, vb_ref, of_ref, ob_ref,
                      state_ref, decay_ref, xi_ref, zeta_ref, gl_ref):
    c = RET_CHUNK
    npairs = RET_HEADS * HEAD // LANES
    step = pl.program_id(0)

    @pl.when(step == 0)
    def _init():
        state_ref[...] = jnp.zeros_like(state_ref)
        ci = lax.broadcasted_iota(jnp.int32, (c, c), 0).astype(F32)
        mi = lax.broadcasted_iota(jnp.int32, (c, c), 1).astype(F32)
        pos = lax.broadcasted_iota(jnp.int32, (c, RET_HEADS * HEAD), 0).astype(F32)
        lane_head = _lane((1, RET_HEADS * HEAD)) // HEAD
        for dr in range(2):
            lgv = jnp.zeros((1, RET_HEADS * HEAD), F32)
            for hd in range(RET_HEADS):
                lg = lg_ref[dr, hd]
                rel = (ci - mi) if dr == 0 else (mi - ci)
                half = slice((hd % 2) * c, (hd % 2 + 1) * c)
                decay_ref[dr, hd // 2, :, half] = jnp.where(rel >= 0, jnp.exp(jnp.maximum(rel, 0.0) * lg), 0.0)
                lgv = jnp.where(lane_head == hd, lg, lgv)
            p = pos if dr == 0 else (c - 1.0 - pos)
            xi_ref[dr] = jnp.exp((p + 1.0) * lgv)
            zeta_ref[dr] = jnp.exp((c - 1.0 - p) * lgv)
            gl_ref[dr] = jnp.exp(float(c) * lgv)

    low = _lane((c, LANES)) < HEAD
    r_i = lax.broadcasted_iota(jnp.int32, (LANES, LANES), 0) // HEAD
    c_i = lax.broadcasted_iota(jnp.int32, (LANES, LANES), 1) // HEAD
    blockdiag = r_i == c_i
    for dr, (q_ref, k_ref, v_ref, o_ref) in enumerate(((qf_ref, kf_ref, vf_ref, of_ref),
                                                       (qb_ref, kb_ref, vb_ref, ob_ref))):
        for j in range(npairs):
            sl = slice(j * LANES, (j + 1) * LANES)
            q, k, v = q_ref[:, sl], k_ref[:, sl], v_ref[:, sl]
            zero = jnp.zeros_like(k)
            k2 = jnp.concatenate([jnp.where(low, k, zero), jnp.where(low, zero, k)], axis=0)
            v2 = jnp.concatenate([jnp.where(low, v, zero), jnp.where(low, zero, v)], axis=0)
            s = _dot_nt(q, k2) * decay_ref[dr, j]
            o = _dot(s.astype(BF16), v2)
            st = state_ref[dr, j]
            qx = (q.astype(F32) * xi_ref[dr, :, sl]).astype(BF16)
            o = o + _dot(qx, st.astype(BF16))
            o_ref[:, sl] = o.astype(BF16)
            kz = (k.astype(F32) * zeta_ref[dr, :, sl]).T.astype(BF16)
            u = _dot(kz, v)
            state_ref[dr, j] = st * gl_ref[dr, :, sl] + jnp.where(blockdiag, u, 0.0)


def _retention(lg, rq, rk, rv, *, n_x):
    t, w = rq.shape
    c = RET_CHUNK
    nc, ncx = t // c, n_x // c
    fwd = lambda i: ((i + ncx) % nc, 0)
    bwd = lambda i: (nc - 1 - i, 0)
    blk = lambda m: pl.BlockSpec((c, w), m)
    npairs = w // LANES
    return pl.pallas_call(
        _retention_kernel,
        grid=(nc,),
        in_specs=[pl.BlockSpec(memory_space=pltpu.SMEM)] + [blk(fwd)] * 3 + [blk(bwd)] * 3,
        out_specs=[blk(fwd), blk(bwd)],
        out_shape=[jax.ShapeDtypeStruct((t, w), BF16)] * 2,
        scratch_shapes=[pltpu.VMEM((2, npairs, LANES, LANES), F32),
                        pltpu.VMEM((2, npairs, c, 2 * c), F32),
                        pltpu.VMEM((2, c, w), F32), pltpu.VMEM((2, c, w), F32),
                        pltpu.VMEM((2, 1, w), F32)],
        compiler_params=_cparams(("arbitrary",)),
    )(lg, rq, rk, rv, rq, rk, rv)


def _flash_kernel(q0_ref, q1_ref, k0_ref, k1_ref, v0_ref, v1_ref, *rest, bk, nkv, rs, n_side):
    side_in, o_ref, side_out = rest[:n_side], rest[n_side], rest[n_side + 1:2 * n_side + 1]
    s_ref, p_ref, a_ref, m_ref, acc_ref = rest[2 * n_side + 1:]
    for src_ref, dst_ref in zip(side_in, side_out):
        dst_ref[...] = src_ref[...].astype(dst_ref.dtype)
    bq = q0_ref.shape[0]
    q_refs, k_refs, v_refs = (q0_ref, q1_ref), (k0_ref, k1_ref), (v0_ref, v1_ref)
    m_ref[...] = jnp.full(m_ref.shape, -jnp.inf, F32)
    acc_ref[...] = jnp.zeros(acc_ref.shape, F32)

    def keys(t):
        return pl.ds(t * bk if isinstance(t, int) else pl.multiple_of(t * bk, bk), bk)

    def scores(t, slot):
        for h in range(2):
            s_ref[slot, h] = _dot_nt(q_refs[h][...], k_refs[h][keys(t), :])

    def softmax(slot):
        col = lambda c: slice(c * LANES, (c + 1) * LANES)
        for r in range(bq // rs):
            rows = slice(r * rs, (r + 1) * rs)
            for h in range(2):
                mx = s_ref[slot, h, rows, col(0)]
                for c in range(1, bk // LANES):
                    mx = jnp.maximum(mx, s_ref[slot, h, rows, col(c)])
                m_old = m_ref[h, rows, :]
                m_new = jnp.maximum(m_old, jnp.max(mx, axis=-1, keepdims=True))
                a_ref[slot, h, rows, :] = jnp.exp2(m_old - m_new)
                m_ref[h, rows, :] = m_new
                for c in range(bk // LANES):
                    p_ref[slot, h, rows, col(c)] = jnp.exp2(s_ref[slot, h, rows, col(c)] - m_new).astype(BF16)

    def values(t, slot):
        for h in range(2):
            acc_ref[h] = acc_ref[h] * a_ref[slot, h] + _dot(p_ref[slot, h], v_refs[h][keys(t), :])

    scores(0, 0)

    def body(i, carry):
        t = 2 * i
        scores(t + 1, 1)
        softmax(0)
        values(t, 0)
        scores(t + 2, 0)
        softmax(1)
        values(t + 1, 1)
        return carry

    n_loop = (nkv - 1) // 2
    lax.fori_loop(0, n_loop, body, 0)
    last = 2 * n_loop
    if last + 1 < nkv:
        scores(last + 1, 1)
    softmax(0)
    values(last, 0)
    if last + 1 < nkv:
        softmax(1)
        values(last + 1, 1)
    low = _lane((bq, LANES)) < HEAD
    acc0, acc1 = acc_ref[0], acc_ref[1]
    out = jnp.where(low, acc0 / acc0[:, LOW_ONE:LOW_ONE + 1], acc1 / acc1[:, HIGH_ONE:HIGH_ONE + 1])
    o_ref[...] = out.astype(o_ref.dtype)


def _flash(q, kmat, vmat, *, q_maps, k_maps, v_maps, n_q, q_row0, kv_row0, n_kv, n_pairs, bq, bk, rs, side=()):
    assert q_row0 % bq == 0 and n_q % bq == 0 and n_kv % bk == 0 and kv_row0 % n_kv == 0 and bq % rs == 0
    qb0, kb0 = q_row0 // bq, kv_row0 // n_kv
    n_i = n_q // bq
    steps = n_pairs * n_i
    assert all(a.shape[0] % (16 * steps) == 0 for a in side)
    qspec = lambda m: pl.BlockSpec((bq, LANES), lambda j, i: (i + qb0, m(j)))
    kspec = lambda m: pl.BlockSpec((n_kv, LANES), lambda j, i: (kb0, m(j)), pipeline_mode=pl.Buffered(1))
    side_specs = [pl.BlockSpec((a.shape[0] // steps, a.shape[1]), lambda j, i: (j * n_i + i, 0)) for a in side]
    outs = pl.pallas_call(
        functools.partial(_flash_kernel, bk=bk, nkv=n_kv // bk, rs=rs, n_side=len(side)),
        grid=(n_pairs, n_i),
        in_specs=[qspec(q_maps[0]), qspec(q_maps[1]), kspec(k_maps[0]), kspec(k_maps[1]),
                  kspec(v_maps[0]), kspec(v_maps[1])] + side_specs,
        out_specs=[pl.BlockSpec((bq, LANES), lambda j, i: (i, j))] + side_specs,
        out_shape=[jax.ShapeDtypeStruct((n_q, n_pairs * LANES), BF16)]
                  + [jax.ShapeDtypeStruct(a.shape, BF16) for a in side],
        scratch_shapes=[pltpu.VMEM((2, 2, bq, bk), F32), pltpu.VMEM((2, 2, bq, bk), BF16),
                        pltpu.VMEM((2, 2, bq, LANES), F32), pltpu.VMEM((2, bq, LANES), F32),
                        pltpu.VMEM((2, bq, LANES), F32)],
        compiler_params=_cparams(("parallel", "parallel")),
    )(q, q, kmat, kmat, vmat[0], vmat[1], *side)
    return outs if side else outs[0]


def _kv_block(n_kv):
    for bk in (1280, 1024, 512, 256):
        if n_kv % bk == 0:
            return bk
    raise ValueError(f"key count {n_kv} has no supported block")


def _l0_out_kernel(x_ref, mod_ref, of_ref, ob_ref, rg_ref, ao_ref, seg_ref, wo_ref, o_ref, *, tm, n_x, d):
    i = pl.program_id(0)
    seg = seg_ref[...]
    rw = RET_HEADS * HEAD
    acc = _dot(ao_ref[...], wo_ref[rw:, :])
    for g in range(rw // LANES):
        sl = slice(g * LANES, (g + 1) * LANES)
        o = of_ref[:, sl].astype(F32) + ob_ref[:, sl].astype(F32)
        dv = o - _seg_mean(o, seg)
        nrm = dv * lax.rsqrt(_seg_mean(dv * dv, seg) + EPS)
        ra = (nrm * _silu(rg_ref[:, sl].astype(F32))).astype(BF16)
        acc = acc + _dot(ra, wo_ref[g * LANES:(g + 1) * LANES, :])
    o_ref[...] = x_ref[...] + _row_gate(mod_ref, 2, i, tm, n_x, d) * acc


def _l0_out(xa, mod, o_f, o_b, rg, ao, seg, wo, *, tm, n_x):
    t, d = xa.shape
    rw = o_f.shape[1]
    row = lambda i: (i, 0)
    return pl.pallas_call(
        functools.partial(_l0_out_kernel, tm=tm, n_x=n_x, d=d),
        grid=(t // tm,),
        in_specs=[pl.BlockSpec((tm, d), row), _resident(mod.shape)] + [pl.BlockSpec((tm, rw), row)] * 4
                 + [_resident(seg.shape), _resident(wo.shape)],
        out_specs=pl.BlockSpec((tm, d), row),
        out_shape=jax.ShapeDtypeStruct((t, d), F32),
        compiler_params=_cparams(("parallel",)),
    )(xa, mod, o_f, o_b, rg, ao, seg, wo)


def _ffn_kernel(x_ref, mod_ref, g_ref, w1_ref, w3_ref, w2_ref, o_ref, *, tm, n_x, d):
    i = pl.program_id(0)
    x = x_ref[...]
    h = _modulated(x, mod_ref, g_ref, 1, i, tm, n_x, d).astype(BF16)
    a = _dot(h, w1_ref[...])
    u = (_silu(a) * _dot(h, w3_ref[...])).astype(BF16)
    o_ref[...] = x + _row_gate(mod_ref, 5, i, tm, n_x, d) * _dot(u, w2_ref[...])


def _ffn(xa, mod, g, w1, w3, w2, *, tm, n_x):
    t, d = xa.shape
    row = lambda i: (i, 0)
    return pl.pallas_call(
        functools.partial(_ffn_kernel, tm=tm, n_x=n_x, d=d),
        grid=(t // tm,),
        in_specs=[pl.BlockSpec((tm, d), row), _resident(mod.shape), _resident(g.shape),
                  _resident(w1.shape), _resident(w3.shape), _resident(w2.shape)],
        out_specs=pl.BlockSpec((tm, d), row),
        out_shape=jax.ShapeDtypeStruct((t, d), F32),
        compiler_params=_cparams(("parallel",)),
    )(xa, mod, g, w1, w3, w2)


def _l1_proj_kernel(x_ref, mod_ref, g_ref, wq_ref, wkv_ref, wkr_ref, gql_ref, gkvl_ref, wuq_ref, wuk_ref,
                    wuv_ref, seg_ref, gq_ref, gk_ref, gkr_ref, c_ref, s_ref,
                    q_ref, k_ref, vlo_ref, vhi_ref, *, tm, n_x, d):
    i = pl.program_id(0)
    h = _modulated(x_ref[...], mod_ref, g_ref, 0, i, tm, n_x, d).astype(BF16)
    seg = seg_ref[...]
    cos, sin = c_ref[...], s_ref[...]

    def lora_norm(v, g):
        return (v * lax.rsqrt(jnp.mean(v * v, axis=-1, keepdims=True) + EPS) * g).astype(BF16)

    cq = lora_norm(_dot(h, wq_ref[...]), gql_ref[...])
    ckv = lora_norm(_dot(h, wkv_ref[...]), gkvl_ref[...])
    kr = _dot(h, wkr_ref[...])
    kr = kr * lax.rsqrt(_seg_mean(kr * kr, seg) + EPS) * gkr_ref[...]
    kr = _rope128(kr, cos, sin, MLA_ROPE // 2)
    qa = _dot(cq, wuq_ref[...])
    ka = _dot(ckv, wuk_ref[...])
    for hd in range(MLA_HEADS):
        sl = slice(hd * LANES, (hd + 1) * LANES)
        v = qa[:, sl]
        vn = v * lax.rsqrt(_seg_mean(v * v, seg) + EPS) * gq_ref[...]
        q_ref[:, sl] = _rope128(vn, cos, sin, MLA_ROPE // 2).astype(BF16)
        v = ka[:, sl]
        k_ref[:, sl] = (v * lax.rsqrt(_seg_mean(v * v, seg) + EPS) * gk_ref[...] + kr).astype(BF16)
    va = _dot(ckv, wuv_ref[...])
    lane = _lane(va.shape) % LANES
    low = lane < HEAD
    vlo_ref[...] = jnp.where(low, va, jnp.where(lane == LOW_ONE, 1.0, 0.0)).astype(BF16)
    vhi_ref[...] = jnp.where(low, jnp.where(lane == HIGH_ONE, 1.0, 0.0), va).astype(BF16)


def _l1_proj(xa, mod, g, wq, wkv, wkr, gql, gkvl, wuq, wuk, wuv, seg, gq, gk, gkr, cos, sin, *, tm, n_x):
    t, d = xa.shape
    row = lambda i: (i, 0)
    hw = MLA_HEADS * LANES
    vw = MLA_HEADS * HEAD
    consts = (mod, g, wq, wkv, wkr, gql, gkvl, wuq, wuk, wuv, seg, gq, gk, gkr)
    return pl.pallas_call(
        functools.partial(_l1_proj_kernel, tm=tm, n_x=n_x, d=d),
        grid=(t // tm,),
        in_specs=[pl.BlockSpec((tm, d), row)] + [_resident(a.shape) for a in consts]
                 + [pl.BlockSpec((tm, LANES), row)] * 2,
        out_specs=[pl.BlockSpec((tm, hw), row), pl.BlockSpec((tm, hw), row),
                   pl.BlockSpec((tm, vw), row), pl.BlockSpec((tm, vw), row)],
        out_shape=[jax.ShapeDtypeStruct((t, hw), BF16), jax.ShapeDtypeStruct((t, hw), BF16),
                   jax.ShapeDtypeStruct((t, vw), BF16), jax.ShapeDtypeStruct((t, vw), BF16)],
        compiler_params=_cparams(("parallel",)),
    )(xa, *consts, cos, sin)


def _l1_out_kernel(x_ref, mod_ref, g_ref, o_ref, wo_ref, rhi_ref, rlo_ref, x3_ref, h_ref, ei_ref, ew_ref, *, d):
    x3 = x_ref[...] + mod_ref[0:1, 2 * d:3 * d] * _dot(o_ref[...], wo_ref[...])
    x3_ref[...] = x3
    ms = jnp.mean(x3 * x3, axis=-1, keepdims=True)
    h = x3 * lax.rsqrt(ms + EPS) * (g_ref[...] * (1.0 + mod_ref[0:1, 4 * d:5 * d])) + mod_ref[0:1, 3 * d:4 * d]
    hi = h.astype(BF16)
    bits = lax.bitcast_convert_type(hi.astype(F32), jnp.uint32)
    words = (bits[:, :d // 2] >> 16) | (bits[:, d // 2:] & jnp.uint32(0xFFFF0000))
    h_ref[0] = words[:, :d // 4]
    h_ref[1] = words[:, d // 4:]
    lo = (h - hi.astype(F32)).astype(BF16)
    logits = _dot(hi, rhi_ref[...]) + (_dot(hi, rlo_ref[...]) + _dot(lo, rhi_ref[...]))
    lane_i = _lane(logits.shape)
    lane = lane_i.astype(F32)
    logits = jnp.where(lane_i < N_EXPERTS, logits, -jnp.inf)
    v1 = jnp.max(logits, axis=-1, keepdims=True)
    i1 = jnp.min(jnp.where(logits == v1, lane, float(LANES)), axis=-1, keepdims=True)
    rest = jnp.where(lane == i1, -jnp.inf, logits)
    v2 = jnp.max(rest, axis=-1, keepdims=True)
    i2 = jnp.min(jnp.where(rest == v2, lane, float(LANES)), axis=-1, keepdims=True)
    e2 = jnp.exp(v2 - v1)
    den = 1.0 + e2
    ei_ref[...] = jnp.where(lane_i == 0, i1, jnp.where(lane_i == 1, i2, 0.0)).astype(jnp.int32)
    ew_ref[...] = jnp.where(lane_i == 0, 1.0 / den, jnp.where(lane_i == 1, e2 / den, 0.0))


def _l1_out(xa, mod, g, o, wo, rhi, rlo, *, tm):
    n, d = o.shape[0], xa.shape[1]
    row = lambda i: (i, 0)
    return pl.pallas_call(
        functools.partial(_l1_out_kernel, d=d),
        grid=(n // tm,),
        in_specs=[pl.BlockSpec((tm, d), row), _resident(mod.shape), _resident(g.shape),
                  pl.BlockSpec((tm, o.shape[1]), row), _resident(wo.shape), _resident(rhi.shape),
                  _resident(rlo.shape)],
        out_specs=[pl.BlockSpec((tm, d), row), pl.BlockSpec((2, tm, d // 4), lambda i: (0, i, 0)),
                   pl.BlockSpec((tm, LANES), row), pl.BlockSpec((tm, LANES), row)],
        out_shape=[jax.ShapeDtypeStruct((n, d), F32), jax.ShapeDtypeStruct((2, n, d // 4), jnp.uint32),
                   jax.ShapeDtypeStruct((n, LANES), jnp.int32), jax.ShapeDtypeStruct((n, LANES), F32)],
        compiler_params=_cparams(("parallel",)),
    )(xa, mod, g, o, wo, rhi, rlo)


def _moe_kernel(te_ref, nv_ref, x_ref, w1_ref, w3_ref, w2_ref, y_ref, acc_ref, *, nf):
    i, f = pl.program_id(0), pl.program_id(1)

    @pl.when(f == 0)
    def _zero():
        acc_ref[...] = jnp.zeros_like(acc_ref)

    @pl.when(i < nv_ref[0])
    def _compute():
        words = jnp.concatenate([x_ref[0], x_ref[1]], axis=1)
        lo = lax.bitcast_convert_type(words << 16, F32)
        hi = lax.bitcast_convert_type(words & jnp.uint32(0xFFFF0000), F32)
        x = jnp.concatenate([lo, hi], axis=1).astype(BF16)
        a = _dot(x, w1_ref[0])
        u = (_silu(a) * _dot(x, w3_ref[0])).astype(BF16)
        acc_ref[...] += _dot(u, w2_ref[0])

    @pl.when(f == nf - 1)
    def _store():
        y_ref[...] = acc_ref[...].astype(y_ref.dtype)


def _moe(tile_expert, n_valid, xs, w1, w3, w2, *, tm, tf):
    p, d = xs.shape[1], 4 * xs.shape[2]
    fdim = w1.shape[2]
    nf = fdim // tf
    fi = lambda i, f, te, nv: jnp.where(i < nv[0], f, nf - 1)
    grid_spec = pltpu.PrefetchScalarGridSpec(
        num_scalar_prefetch=2,
        grid=(p // tm, nf),
        in_specs=[pl.BlockSpec((2, tm, d // 4), lambda i, f, te, nv: (0, i, 0)),
                  pl.BlockSpec((1, d, tf), lambda i, f, te, nv: (te[i], 0, fi(i, f, te, nv))),
                  pl.BlockSpec((1, d, tf), lambda i, f, te, nv: (te[i], 0, fi(i, f, te, nv))),
                  pl.BlockSpec((1, tf, d), lambda i, f, te, nv: (te[i], fi(i, f, te, nv), 0))],
        out_specs=pl.BlockSpec((tm, d), lambda i, f, te, nv: (i, 0)),
        scratch_shapes=[pltpu.VMEM((tm, d), F32)],
    )
    return pl.pallas_call(
        functools.partial(_moe_kernel, nf=nf),
        grid_spec=grid_spec,
        out_shape=jax.ShapeDtypeStruct((p, d), BF16),
        compiler_params=_cparams(("arbitrary", "arbitrary")),
    )(tile_expert, n_valid, xs, w1, w3, w2)


SC_GATHER_WINDOW = 128
SC_LANES = 16


def _gather_rows(x, idx):
    n, d = idx.shape[0], x.shape[1]
    w = SC_GATHER_WINDOW
    assert n % w == 0
    mesh = plsc.VectorSubcoreMesh(core_axis_name="core", subcore_axis_name="subcore")

    @pl.kernel(out_type=jax.ShapeDtypeStruct((n, d), x.dtype), mesh=mesh,
               scratch_types=[pltpu.SemaphoreType.DMA])
    def gather_kernel(x_hbm, i_hbm, o_hbm, sem):
        def body(i_vmem, o_vmem):
            copies = []
            for k in range(w // SC_LANES):
                grp = pl.ds(k * SC_LANES, SC_LANES)
                copies.append(pltpu.async_copy(x_hbm.at[i_vmem[0, grp]], o_vmem.at[grp], sem))
            for cp in copies:
                cp.wait()

        pltpu.emit_pipeline(
            body,
            grid=(n // w,),
            in_specs=[pl.BlockSpec((1, w), lambda i: (0, i))],
            out_specs=[pl.BlockSpec((w, d), lambda i: (i, 0))],
            core_axis_name=("core", "subcore"),
            dimension_semantics=(pltpu.PARALLEL,),
        )(i_hbm, o_hbm)

    return gather_kernel(x, idx.reshape(1, n))


def _combine_kernel(x_ref, mod_ref, ew_ref, ya_ref, yb_ref, o_ref, *, d):
    ew = ew_ref[...]
    y = ew[:, 0:1] * ya_ref[...].astype(F32) + ew[:, 1:2] * yb_ref[...].astype(F32)
    o_ref[...] = x_ref[...] + mod_ref[0:1, 5 * d:6 * d] * y


def _combine(x3, mod, ew, ya, yb, *, tm):
    n, d = x3.shape
    row = lambda i: (i, 0)
    return pl.pallas_call(
        functools.partial(_combine_kernel, d=d),
        grid=(n // tm,),
        in_specs=[pl.BlockSpec((tm, d), row), _resident(mod.shape), pl.BlockSpec((tm, LANES), row),
                  pl.BlockSpec((tm, d), row), pl.BlockSpec((tm, d), row)],
        out_specs=pl.BlockSpec((tm, d), row),
        out_shape=jax.ShapeDtypeStruct((n, d), F32),
        compiler_params=_cparams(("parallel",)),
    )(x3, mod, ew, ya, yb)


def _deinterleave(width):
    return np.concatenate([np.arange(0, width, 2), np.arange(1, width, 2)])


def _rope_tables(n_x, n_ctx, rot_dim, seg_start, seg_repeat):
    rows = n_x // GRID_W
    row = jnp.broadcast_to(jnp.arange(rows)[:, None], (rows, GRID_W)).reshape(n_x).astype(F32)
    col = jnp.broadcast_to(jnp.arange(GRID_W)[None, :], (rows, GRID_W)).reshape(n_x).astype(F32)
    axis_dim = rot_dim // 2
    inv_freq = ROPE_THETA ** (-jnp.arange(0, axis_dim, 2, dtype=F32) / axis_dim)
    ang = jnp.concatenate([row[:, None] * inv_freq, col[:, None] * inv_freq], axis=-1)
    cos, sin = jnp.cos(ang), jnp.sin(ang)
    tail = LANES - seg_start - seg_repeat * rot_dim
    c = jnp.concatenate([jnp.ones((n_x, seg_start), F32)] + [cos, cos] * seg_repeat + [jnp.ones((n_x, tail), F32)],
                        axis=-1)
    s = jnp.concatenate([jnp.zeros((n_x, seg_start), F32)] + [-sin, sin] * seg_repeat
                        + [jnp.zeros((n_x, tail), F32)], axis=-1)
    c = jnp.concatenate([c, jnp.ones((n_ctx, LANES), F32)], axis=0)
    s = jnp.concatenate([s, jnp.zeros((n_ctx, LANES), F32)], axis=0)
    return c, s


def _segment_matrix(bounds):
    m = np.zeros((LANES, LANES), np.float32)
    for lo, hi in bounds:
        m[lo:hi, lo:hi] = 1.0 / (hi - lo)
    return jnp.asarray(m, BF16)


def _token_tile(t):
    for tm in (640, 512, 256, 128):
        if t % tm == 0:
            return tm
    raise ValueError(f"token count {t} has no supported tile")


def kernel(x, c, ctx, c_ctx, l0_ada_w, l0_ada_b, l0_norm1_g, l0_norm2_g, l0_w_in, l0_ret_log_decay, l0_q_norm_g, l0_k_norm_g, l0_w_out, l0_ffn_w1, l0_ffn_w3, l0_ffn_w2, l1_ada_w, l1_ada_b, l1_norm1_g, l1_norm2_g, l1_w_in, l1_q_lora_g, l1_kv_lora_g, l1_w_uq, l1_w_ukv, l1_q_nope_g, l1_q_rope_g, l1_k_nope_g, l1_k_rope_g, l1_w_out, l1_router, l1_exp_w1, l1_exp_w3, l1_exp_w2):
    b, n_x, d = x.shape
    n_ctx = ctx.shape[1]
    assert b == 1 and n_x % 256 == 0 and n_ctx % 256 == 0 and n_x % GRID_W == 0
    t = n_x + n_ctx
    tm = _token_tile(t)
    tmx = _token_tile(n_x)
    xa = jnp.concatenate([x[0], ctx[0]], axis=0)
    row1 = lambda v: v.reshape(1, -1).astype(F32)

    cvec = jnp.zeros((8, d), F32).at[0].set(c[0]).at[1].set(c_ctx)
    mod0 = _ada(cvec, l0_ada_w, l0_ada_b)
    mod1 = _ada(cvec, l1_ada_w, l1_ada_b)

    rw = RET_HEADS * HEAD
    perm = _deinterleave(HEAD)
    n_qk = GQA_HEADS + GQA_KV_HEADS
    qk_cols = l0_w_in[:, 4 * rw:4 * rw + n_qk * HEAD].reshape(d, n_qk, HEAD // 2, 2)
    qk_cols = jnp.swapaxes(qk_cols, 2, 3).reshape(d, n_qk * HEAD)
    w_in0 = jnp.concatenate([l0_w_in[:, :rw], l0_w_in[:, rw:2 * rw] * (HEAD ** -0.5), l0_w_in[:, 2 * rw:4 * rw],
                             qk_cols, l0_w_in[:, 4 * rw + n_qk * HEAD:]], axis=1).astype(BF16)
    seg64 = _segment_matrix([(0, HEAD), (HEAD, 2 * HEAD)])
    gq0 = row1(jnp.tile(l0_q_norm_g[perm], 2) * (HEAD ** -0.5 * LOG2E))
    gk0 = row1(jnp.tile(l0_k_norm_g[perm], 2))
    cos0, sin0 = _rope_tables(n_x, n_ctx, HEAD, 0, 2)

    rq, rk, rv, rg, gq, gkx, gvx = _l0_proj(xa, mod0, row1(l0_norm1_g), w_in0, seg64, gq0, gk0, cos0, sin0,
                                             tm=tm, n_x=n_x)
    o_f, o_b = _retention(l0_ret_log_decay.astype(F32), rq, rk, rv, n_x=n_x)

    gqa_maps = dict(q_maps=(lambda j: j, lambda j: j),
                    k_maps=(lambda j: 2 * (j // 2), lambda j: 2 * (j // 2) + 1),
                    v_maps=(lambda j: 2 * (j // 2), lambda j: 2 * (j // 2) + 1), n_pairs=GQA_HEADS // 2)
    bq = 512 if n_x % 512 == 0 else 256
    ao_x = _flash(gq, gkx, (gvx, gvx), n_q=n_x, q_row0=0, kv_row0=0, n_kv=t, bq=bq, bk=_kv_block(t), rs=32,
                  **gqa_maps)
    ao_c = _flash(gq, gkx, (gvx, gvx), n_q=n_ctx, q_row0=n_x, kv_row0=n_x, n_kv=n_ctx, bq=n_ctx,
                  bk=_kv_block(n_ctx), rs=32, **gqa_maps)
    ao = jnp.concatenate([ao_x, ao_c], axis=0)

    xa = _l0_out(xa, mod0, o_f, o_b, rg, ao, seg64, l0_w_out.astype(BF16), tm=tm, n_x=n_x)
    xa = _ffn(xa, mod0, row1(l0_norm2_g), l0_ffn_w1.astype(BF16), l0_ffn_w3.astype(BF16),
              l0_ffn_w2.astype(BF16), tm=tm, n_x=n_x)

    rperm = _deinterleave(MLA_ROPE)
    qk_w = MLA_NOPE + MLA_ROPE
    wuq = jnp.zeros((MLA_Q_RANK, MLA_HEADS * LANES), F32)
    wuk = jnp.zeros((MLA_KV_RANK, MLA_HEADS * LANES), F32)
    wuv = []
    for hd in range(MLA_HEADS):
        src = l1_w_uq[:, hd * qk_w:(hd + 1) * qk_w]
        wuq = wuq.at[:, hd * LANES:hd * LANES + MLA_NOPE].set(src[:, :MLA_NOPE])
        wuq = wuq.at[:, hd * LANES + MLA_NOPE:hd * LANES + qk_w].set(src[:, MLA_NOPE:][:, rperm])
        kvsrc = l1_w_ukv[:, hd * 2 * HEAD:(hd + 1) * 2 * HEAD]
        wuk = wuk.at[:, hd * LANES:hd * LANES + MLA_NOPE].set(kvsrc[:, :MLA_NOPE])
        wuv.append(kvsrc[:, MLA_NOPE:])
    wuv = jnp.concatenate(wuv, axis=1)
    wkr = jnp.zeros((d, LANES), F32).at[:, MLA_NOPE:qk_w].set(l1_w_in[:, MLA_Q_RANK + MLA_KV_RANK:][:, rperm])
    pad = jnp.zeros((LANES - qk_w,), F32)
    zn = jnp.zeros((MLA_NOPE,), F32)
    gq1 = row1(jnp.concatenate([l1_q_nope_g, l1_q_rope_g[rperm], pad]) * (qk_w ** -0.5 * LOG2E))
    gk1 = row1(jnp.concatenate([l1_k_nope_g, jnp.zeros((LANES - MLA_NOPE,), F32)]))
    gkr1 = row1(jnp.concatenate([zn, l1_k_rope_g[rperm], pad]))
    seg_mla = _segment_matrix([(0, MLA_NOPE), (MLA_NOPE, qk_w)])
    cos1, sin1 = _rope_tables(n_x, n_ctx, MLA_ROPE, MLA_NOPE, 1)

    mq, mk, mvlo, mvhi = _l1_proj(
        xa, mod1, row1(l1_norm1_g), l1_w_in[:, :MLA_Q_RANK].astype(BF16),
        l1_w_in[:, MLA_Q_RANK:MLA_Q_RANK + MLA_KV_RANK].astype(BF16), wkr.astype(BF16),
        row1(l1_q_lora_g), row1(l1_kv_lora_g), wuq.astype(BF16), wuk.astype(BF16), wuv.astype(BF16),
        seg_mla, gq1, gk1, gkr1, cos1, sin1, tm=tm, n_x=n_x)
    n_e, _, fdim = l1_exp_w1.shape
    mo, ew1, ew3, ew2 = _flash(
        mq, mk, (mvlo, mvhi), q_maps=(lambda j: 2 * j, lambda j: 2 * j + 1),
        k_maps=(lambda j: 2 * j, lambda j: 2 * j + 1), v_maps=(lambda j: j, lambda j: j),
        n_pairs=MLA_HEADS // 2, n_q=n_x, q_row0=0, kv_row0=0, n_kv=t, bq=bq, bk=_kv_block(t), rs=32,
        side=(l1_exp_w1.reshape(n_e * d, fdim), l1_exp_w3.reshape(n_e * d, fdim),
              l1_exp_w2.reshape(n_e * fdim, d)))
    ew1, ew3, ew2 = ew1.reshape(n_e, d, fdim), ew3.reshape(n_e, d, fdim), ew2.reshape(n_e, fdim, d)

    router = jnp.zeros((d, LANES), F32).at[:, :N_EXPERTS].set(l1_router)
    r_hi = router.astype(BF16)
    r_lo = (router - r_hi.astype(F32)).astype(BF16)
    x3, hmoe, ei, ew = _l1_out(xa, mod1, row1(l1_norm2_g), mo, l1_w_out.astype(BF16), r_hi, r_lo, tm=tmx)

    tme = 512 if n_x >= 4096 else 128
    e_flat = ei[:, :TOP_K].reshape(-1)
    onehot = (e_flat[:, None] == jnp.arange(N_EXPERTS)[None, :]).astype(jnp.int32)
    csum = jnp.cumsum(onehot, axis=0)
    rank = jnp.sum((csum - onehot) * onehot, axis=1)
    counts = csum[-1]
    padded = ((counts + tme - 1) // tme) * tme
    ends = jnp.cumsum(padded)
    pos = (ends - padded)[e_flat] + rank
    n_tiles = (TOP_K * n_x) // tme + N_EXPERTS
    p_rows = n_tiles * tme
    src = jnp.zeros((p_rows,), jnp.int32).at[pos].set(jnp.arange(TOP_K * n_x, dtype=jnp.int32) // TOP_K)
    n_valid = (ends[-1] // tme).astype(jnp.int32)
    tile_ids = jnp.minimum(jnp.arange(n_tiles, dtype=jnp.int32), n_valid - 1)
    tile_expert = jnp.sum((ends[None, :] <= (tile_ids * tme)[:, None]).astype(jnp.int32), axis=1)
    tile_expert = jnp.minimum(tile_expert, N_EXPERTS - 1)
    xs = _gather_rows(hmoe.reshape(2 * n_x, d // 4), jnp.concatenate([src, src + n_x])).reshape(2, p_rows, d // 4)
    tf = fdim // 2 if (fdim // 2) % LANES == 0 else fdim
    ys = _moe(tile_expert, n_valid.reshape(1), xs, ew1, ew3, ew2, tm=tme, tf=tf)
    pos2 = pos.reshape(n_x, TOP_K)
    out = _combine(x3, mod1, ew, jnp.take(ys, pos2[:, 0], axis=0, mode="clip"),
                   jnp.take(ys, pos2[:, 1], axis=0, mode="clip"), tm=tmx)
    return out[None]
```

```python
import functools
import math

import numpy as np
import jax
import jax.numpy as jnp
from jax import lax
from jax.experimental import pallas as pl
from jax.experimental.pallas import tpu as pltpu
from jax.experimental.pallas import tpu_sc as plsc

F32 = jnp.float32
BF16 = jnp.bfloat16

EPS = 1e-6
ROPE_THETA = 10000.0
GRID_W = 64
LANES = 128
HEAD = 64
RET_CHUNK = 128
RET_HEADS = 8
GQA_HEADS = 8
GQA_KV_HEADS = 2
MLA_HEADS = 8
MLA_Q_RANK = 384
MLA_KV_RANK = 256
MLA_NOPE = 64
MLA_ROPE = 32
N_EXPERTS = 8
TOP_K = 2
LOW_ONE = HEAD
HIGH_ONE = 0
LOG2E = math.log2(math.e)
VMEM_LIMIT = 56 * 1024 * 1024


def _cparams(sem, vmem=VMEM_LIMIT):
    return pltpu.CompilerParams(dimension_semantics=sem, vmem_limit_bytes=vmem)


def _resident(shape):
    nd = len(shape)
    return pl.BlockSpec(shape, lambda *_: (0,) * nd, pipeline_mode=pl.Buffered(1))


def _dot(a, b):
    return jnp.dot(a, b, preferred_element_type=F32)


def _dot_nt(a, b):
    return lax.dot_general(a, b, (((1,), (1,)), ((), ())), preferred_element_type=F32)


def _seg_mean(v, seg):
    hi = v.astype(BF16)
    lo = (v - hi.astype(F32)).astype(BF16)
    return _dot(hi, seg) + _dot(lo, seg)


def _silu(x):
    return x * jax.nn.sigmoid(x)


def _modulated(x, mod_ref, g_ref, which, tile, tm, n_x, d):
    ms = jnp.mean(x * x, axis=-1, keepdims=True)
    xn = x * lax.rsqrt(ms + EPS)
    g = g_ref[...]
    sh, sc = 3 * which, 3 * which + 1
    a_x = g * (1.0 + mod_ref[0:1, sc * d:(sc + 1) * d])
    a_c = g * (1.0 + mod_ref[1:2, sc * d:(sc + 1) * d])
    b_x = mod_ref[0:1, sh * d:(sh + 1) * d]
    b_c = mod_ref[1:2, sh * d:(sh + 1) * d]
    row = tile * tm + lax.broadcasted_iota(jnp.int32, (tm, 1), 0)
    is_ctx = row >= n_x
    return xn * jnp.where(is_ctx, a_c, a_x) + jnp.where(is_ctx, b_c, b_x)


def _row_gate(mod_ref, idx, tile, tm, n_x, d):
    row = tile * tm + lax.broadcasted_iota(jnp.int32, (tm, 1), 0)
    return jnp.where(row >= n_x, mod_ref[1:2, idx * d:(idx + 1) * d], mod_ref[0:1, idx * d:(idx + 1) * d])


def _lane(shape):
    return lax.broadcasted_iota(jnp.int32, shape, len(shape) - 1)


def _ada_kernel(c_ref, w_ref, b_ref, o_ref):
    c = c_ref[...]
    o_ref[...] = jnp.dot(_silu(c), w_ref[...], preferred_element_type=F32,
                         precision=lax.Precision.HIGHEST) + b_ref[...]


def _ada(cvec8, w, b):
    d, n = w.shape
    tn = n // 4
    return pl.pallas_call(
        _ada_kernel,
        grid=(n // tn,),
        in_specs=[pl.BlockSpec((8, d), lambda j: (0, 0)),
                  pl.BlockSpec((d, tn), lambda j: (0, j)),
                  pl.BlockSpec((1, tn), lambda j: (0, j))],
        out_specs=pl.BlockSpec((8, tn), lambda j: (0, j)),
        out_shape=jax.ShapeDtypeStruct((8, n), F32),
        compiler_params=_cparams(("arbitrary",)),
    )(cvec8, w, b.reshape(1, n))


def _rope128(v, c, s, half):
    lane = _lane(v.shape)
    swapped = jnp.where(lane % (2 * half) < half, pltpu.roll(v, LANES - half, 1), pltpu.roll(v, half, 1))
    return v * c + swapped * s


def _l0_proj_kernel(x_ref, mod_ref, g_ref, w_ref, seg_ref, gq_ref, gk_ref, c_ref, s_ref,
                    rq_ref, rk_ref, rv_ref, rg_ref, q_ref, k_ref, v_ref, *, tm, n_x, d):
    i = pl.program_id(0)
    h = _modulated(x_ref[...], mod_ref, g_ref, 0, i, tm, n_x, d).astype(BF16)
    rw = RET_HEADS * HEAD
    for idx, ref in enumerate((rq_ref, rk_ref, rv_ref, rg_ref)):
        ref[...] = _dot(h, w_ref[:, idx * rw:(idx + 1) * rw]).astype(BF16)
    seg = seg_ref[...]
    cos, sin = c_ref[...], s_ref[...]
    base = 4 * rw
    qw = GQA_HEADS * HEAD
    qa = _dot(h, w_ref[:, base:base + qw])
    for g in range(qw // LANES):
        v = qa[:, g * LANES:(g + 1) * LANES]
        vn = v * lax.rsqrt(_seg_mean(v * v, seg) + EPS) * gq_ref[...]
        q_ref[:, g * LANES:(g + 1) * LANES] = _rope128(vn, cos, sin, HEAD // 2).astype(BF16)
    kv = _dot(h, w_ref[:, base + qw:base + qw + 2 * LANES])
    kk = kv[:, :LANES]
    kk = kk * lax.rsqrt(_seg_mean(kk * kk, seg) + EPS) * gk_ref[...]
    kk = _rope128(kk, cos, sin, HEAD // 2)
    vv = kv[:, LANES:]
    lane = _lane(kk.shape)
    low = lane < HEAD
    for src, ref, one in ((kk, k_ref, 0.0), (vv, v_ref, 1.0)):
        sw = pltpu.roll(src, HEAD, 1)
        lo_fill = jnp.where(lane == LOW_ONE, one, 0.0)
        hi_fill = jnp.where(lane == HIGH_ONE, one, 0.0)
        ref[:, 0 * LANES:1 * LANES] = jnp.where(low, src, lo_fill).astype(BF16)
        ref[:, 1 * LANES:2 * LANES] = jnp.where(low, hi_fill, sw).astype(BF16)
        ref[:, 2 * LANES:3 * LANES] = jnp.where(low, sw, lo_fill).astype(BF16)
        ref[:, 3 * LANES:4 * LANES] = jnp.where(low, hi_fill, src).astype(BF16)


def _l0_proj(xa, mod, g, w, seg, gq, gk, cos, sin, *, tm, n_x):
    t, d = xa.shape
    rw = RET_HEADS * HEAD
    row = lambda i: (i, 0)
    outs = [jax.ShapeDtypeStruct((t, rw), BF16)] * 7
    return pl.pallas_call(
        functools.partial(_l0_proj_kernel, tm=tm, n_x=n_x, d=d),
        grid=(t // tm,),
        in_specs=[pl.BlockSpec((tm, d), row), _resident(mod.shape), _resident(g.shape), _resident(w.shape),
                  _resident(seg.shape), _resident(gq.shape), _resident(gk.shape),
                  pl.BlockSpec((tm, LANES), row), pl.BlockSpec((tm, LANES), row)],
        out_specs=[pl.BlockSpec((tm, rw), row)] * 7,
        out_shape=outs,
        compiler_params=_cparams(("parallel",)),
    )(xa, mod, g, w, seg, gq, gk, cos, sin)


def _retention_kernel(lg_ref, qf_ref, kf_ref, vf_ref, qb_ref, kb_ref, vb_ref, of_ref, ob_ref,
                      state_ref, decay_ref, xi_ref, zeta_ref, gl_ref):
    c = RET_CHUNK
    npairs = RET_HEADS * HEAD // LANES
    step = pl.program_id(0)

    @pl.when(step == 0)
    def _init():
        state_ref[...] = jnp.zeros_like(state_ref)
        ci = lax.broadcasted_iota(jnp.int32, (c, c), 0).astype(F32)
        mi = lax.broadcasted_iota(jnp.int32, (c, c), 1).astype(F32)
        pos = lax.broadcasted_iota(jnp.int32, (c, RET_HEADS * HEAD), 0).astype(F32)
        lane_head = _lane((1, RET_HEADS * HEAD)) // HEAD
        for dr in range(2):
            lgv = jnp.zeros((1, RET_HEADS * HEAD), F32)
            for hd in range(RET_HEADS):
                lg = lg_ref[dr, hd]
                rel = (ci - mi) if dr == 0 else (mi - ci)
                half = slice((hd % 2) * c, (hd % 2 + 1) * c)
                decay_ref[dr, hd // 2, :, half] = jnp.where(rel >= 0, jnp.exp(jnp.maximum(rel, 0.0) * lg), 0.0)
                lgv = jnp.where(lane_head == hd, lg, lgv)
            p = pos if dr == 0 else (c - 1.0 - pos)
            xi_ref[dr] = jnp.exp((p + 1.0) * lgv)
            zeta_ref[dr] = jnp.exp((c - 1.0 - p) * lgv)
            gl_ref[dr] = jnp.exp(float(c) * lgv)

    low = _lane((c, LANES)) < HEAD
    r_i = lax.broadcasted_iota(jnp.int32, (LANES, LANES), 0) // HEAD
    c_i = lax.broadcasted_iota(jnp.int32, (LANES, LANES), 1) // HEAD
    blockdiag = r_i == c_i
    for dr, (q_ref, k_ref, v_ref, o_ref) in enumerate(((qf_ref, kf_ref, vf_ref, of_ref),
                                                       (qb_ref, kb_ref, vb_ref, ob_ref))):
        for j in range(npairs):
            sl = slice(j * LANES, (j + 1) * LANES)
            q, k, v = q_ref[:, sl], k_ref[:, sl], v_ref[:, sl]
            zero = jnp.zeros_like(k)
            k2 = jnp.concatenate([jnp.where(low, k, zero), jnp.where(low, zero, k)], axis=0)
            v2 = jnp.concatenate([jnp.where(low, v, zero), jnp.where(low, zero, v)], axis=0)
            s = _dot_nt(q, k2) * decay_ref[dr, j]
            o = _dot(s.astype(BF16), v2)
            st = state_ref[dr, j]
            qx = (q.astype(F32) * xi_ref[dr, :, sl]).astype(BF16)
            o = o + _dot(qx, st.astype(BF16))
            o_ref[:, sl] = o.astype(BF16)
            kz = (k.astype(F32) * zeta_ref[dr, :, sl]).T.astype(BF16)
            u = _dot(kz, v)
            state_ref[dr, j] = st * gl_ref[dr, :, sl] + jnp.where(blockdiag, u, 0.0)


def _retention(lg, rq, rk, rv, *, n_x):
    t, w = rq.shape
    c = RET_CHUNK
    nc, ncx = t // c, n_x // c
    fwd = lambda i: ((i + ncx) % nc, 0)
    bwd = lambda i: (nc - 1 - i, 0)
    blk = lambda m: pl.BlockSpec((c, w), m)
    npairs = w // LANES
    return pl.pallas_call(
        _retention_kernel,
        grid=(nc,),
        in_specs=[pl.BlockSpec(memory_space=pltpu.SMEM)] + [blk(fwd)] * 3 + [blk(bwd)] * 3,
        out_specs=[blk(fwd), blk(bwd)],
        out_shape=[jax.ShapeDtypeStruct((t, w), BF16)] * 2,
        scratch_shapes=[pltpu.VMEM((2, npairs, LANES, LANES), F32),
                        pltpu.VMEM((2, npairs, c, 2 * c), F32),
                        pltpu.VMEM((2, c, w), F32), pltpu.VMEM((2, c, w), F32),
                        pltpu.VMEM((2, 1, w), F32)],
        compiler_params=_cparams(("arbitrary",)),
    )(lg, rq, rk, rv, rq, rk, rv)


def _flash_kernel(q0_ref, q1_ref, k0_ref, k1_ref, v0_ref, v1_ref, *rest, bk, nkv, rs, n_side):
    side_in, o_ref, side_out = rest[:n_side], rest[n_side], rest[n_side + 1:2 * n_side + 1]
    s_ref, p_ref, a_ref, m_ref, acc_ref = rest[2 * n_side + 1:]
    for src_ref, dst_ref in zip(side_in, side_out):
        dst_ref[...] = src_ref[...].astype(dst_ref.dtype)
    bq = q0_ref.shape[0]
    q_refs, k_refs, v_refs = (q0_ref, q1_ref), (k0_ref, k1_ref), (v0_ref, v1_ref)
    m_ref[...] = jnp.full(m_ref.shape, -jnp.inf, F32)
    acc_ref[...] = jnp.zeros(acc_ref.shape, F32)

    def keys(t):
        return pl.ds(t * bk if isinstance(t, int) else pl.multiple_of(t * bk, bk), bk)

    def scores(t, slot):
        for h in range(2):
            s_ref[slot, h] = _dot_nt(q_refs[h][...], k_refs[h][keys(t), :])

    def softmax(slot):
        col = lambda c: slice(c * LANES, (c + 1) * LANES)
        for r in range(bq // rs):
            rows = slice(r * rs, (r + 1) * rs)
            for h in range(2):
                mx = s_ref[slot, h, rows, col(0)]
                for c in range(1, bk // LANES):
                    mx = jnp.maximum(mx, s_ref[slot, h, rows, col(c)])
                m_old = m_ref[h, rows, :]
                m_new = jnp.maximum(m_old, jnp.max(mx, axis=-1, keepdims=True))
                a_ref[slot, h, rows, :] = jnp.exp2(m_old - m_new)
                m_ref[h, rows, :] = m_new
                for c in range(bk // LANES):
                    p_ref[slot, h, rows, col(c)] = jnp.exp2(s_ref[slot, h, rows, col(c)] - m_new).astype(BF16)

    def values(t, slot):
        for h in range(2):
            acc_ref[h] = acc_ref[h] * a_ref[slot, h] + _dot(p_ref[slot, h], v_refs[h][keys(t), :])

    scores(0, 0)

    def body(i, carry):
        t = 2 * i
        scores(t + 1, 1)
        softmax(0)
        values(t, 0)
        scores(t + 2, 0)
        softmax(1)
        values(t + 1, 1)
        return carry

    n_loop = (nkv - 1) // 2
    lax.fori_loop(0, n_loop, body, 0)
    last = 2 * n_loop
    if last + 1 < nkv:
        scores(last + 1, 1)
    softmax(0)
    values(last, 0)
    if last + 1 < nkv:
        softmax(1)
        values(last + 1, 1)
    low = _lane((bq, LANES)) < HEAD
    acc0, acc1 = acc_ref[0], acc_ref[1]
    out = jnp.where(low, acc0 / acc0[:, LOW_ONE:LOW_ONE + 1], acc1 / acc1[:, HIGH_ONE:HIGH_ONE + 1])
    o_ref[...] = out.astype(o_ref.dtype)


def _flash(q, kmat, vmat, *, q_maps, k_maps, v_maps, n_q, q_row0, kv_row0, n_kv, n_pairs, bq, bk, rs, side=()):
    assert q_row0 % bq == 0 and n_q % bq == 0 and n_kv % bk == 0 and kv_row0 % n_kv == 0 and bq % rs == 0
    qb0, kb0 = q_row0 // bq, kv_row0 // n_kv
    n_i = n_q // bq
    steps = n_pairs * n_i
    assert all(a.shape[0] % (16 * steps) == 0 for a in side)
    qspec = lambda m: pl.BlockSpec((bq, LANES), lambda j, i: (i + qb0, m(j)))
    kspec = lambda m: pl.BlockSpec((n_kv, LANES), lambda j, i: (kb0, m(j)), pipeline_mode=pl.Buffered(1))
    side_specs = [pl.BlockSpec((a.shape[0] // steps, a.shape[1]), lambda j, i: (j * n_i + i, 0)) for a in side]
    outs = pl.pallas_call(
        functools.partial(_flash_kernel, bk=bk, nkv=n_kv // bk, rs=rs, n_side=len(side)),
        grid=(n_pairs, n_i),
        in_specs=[qspec(q_maps[0]), qspec(q_maps[1]), kspec(k_maps[0]), kspec(k_maps[1]),
                  kspec(v_maps[0]), kspec(v_maps[1])] + side_specs,
        out_specs=[pl.BlockSpec((bq, LANES), lambda j, i: (i, j))] + side_specs,
        out_shape=[jax.ShapeDtypeStruct((n_q, n_pairs * LANES), BF16)]
                  + [jax.ShapeDtypeStruct(a.shape, BF16) for a in side],
        scratch_shapes=[pltpu.VMEM((2, 2, bq, bk), F32), pltpu.VMEM((2, 2, bq, bk), BF16),
                        pltpu.VMEM((2, 2, bq, LANES), F32), pltpu.VMEM((2, bq, LANES), F32),
                        pltpu.VMEM((2, bq, LANES), F32)],
        compiler_params=_cparams(("parallel", "parallel")),
    )(q, q, kmat, kmat, vmat[0], vmat[1], *side)
    return outs if side else outs[0]


def _kv_block(n_kv):
    for bk in (1280, 1024, 512, 256):
        if n_kv % bk == 0:
            return bk
    raise ValueError(f"key count {n_kv} has no supported block")


def _l0_out_kernel(x_ref, mod_ref, of_ref, ob_ref, rg_ref, ao_ref, seg_ref, wo_ref, o_ref, *, tm, n_x, d):
    i = pl.program_id(0)
    seg = seg_ref[...]
    rw = RET_HEADS * HEAD
    acc = _dot(ao_ref[...], wo_ref[rw:, :])
    for g in range(rw // LANES):
        sl = slice(g * LANES, (g + 1) * LANES)
        o = of_ref[:, sl].astype(F32) + ob_ref[:, sl].astype(F32)
        dv = o - _seg_mean(o, seg)
        nrm = dv * lax.rsqrt(_seg_mean(dv * dv, seg) + EPS)
        ra = (nrm * _silu(rg_ref[:, sl].astype(F32))).astype(BF16)
        acc = acc + _dot(ra, wo_ref[g * LANES:(g + 1) * LANES, :])
    o_ref[...] = x_ref[...] + _row_gate(mod_ref, 2, i, tm, n_x, d) * acc


def _l0_out(xa, mod, o_f, o_b, rg, ao, seg, wo, *, tm, n_x):
    t, d = xa.shape
    rw = o_f.shape[1]
    row = lambda i: (i, 0)
    return pl.pallas_call(
        functools.partial(_l0_out_kernel, tm=tm, n_x=n_x, d=d),
        grid=(t // tm,),
        in_specs=[pl.BlockSpec((tm, d), row), _resident(mod.shape)] + [pl.BlockSpec((tm, rw), row)] * 4
                 + [_resident(seg.shape), _resident(wo.shape)],
        out_specs=pl.BlockSpec((tm, d), row),
        out_shape=jax.ShapeDtypeStruct((t, d), F32),
        compiler_params=_cparams(("parallel",)),
    )(xa, mod, o_f, o_b, rg, ao, seg, wo)


def _ffn_kernel(x_ref, mod_ref, g_ref, w1_ref, w3_ref, w2_ref, o_ref, *, tm, n_x, d):
    i = pl.program_id(0)
    x = x_ref[...]
    h = _modulated(x, mod_ref, g_ref, 1, i, tm, n_x, d).astype(BF16)
    a = _dot(h, w1_ref[...])
    u = (_silu(a) * _dot(h, w3_ref[...])).astype(BF16)
    o_ref[...] = x + _row_gate(mod_ref, 5, i, tm, n_x, d) * _dot(u, w2_ref[...])


def _ffn(xa, mod, g, w1, w3, w2, *, tm, n_x):
    t, d = xa.shape
    row = lambda i: (i, 0)
    return pl.pallas_call(
        functools.partial(_ffn_kernel, tm=tm, n_x=n_x, d=d),
        grid=(t // tm,),
        in_specs=[pl.BlockSpec((tm, d), row), _resident(mod.shape), _resident(g.shape),
                  _resident(w1.shape), _resident(w3.shape), _resident(w2.shape)],
        out_specs=pl.BlockSpec((tm, d), row),
        out_shape=jax.ShapeDtypeStruct((t, d), F32),
        compiler_params=_cparams(("parallel",)),
    )(xa, mod, g, w1, w3, w2)


def _l1_proj_kernel(x_ref, mod_ref, g_ref, wq_ref, wkv_ref, wkr_ref, gql_ref, gkvl_ref, wuq_ref, wuk_ref,
                    wuv_ref, seg_ref, gq_ref, gk_ref, gkr_ref, c_ref, s_ref,
                    q_ref, k_ref, vlo_ref, vhi_ref, *, tm, n_x, d):
    i = pl.program_id(0)
    h = _modulated(x_ref[...], mod_ref, g_ref, 0, i, tm, n_x, d).astype(BF16)
    seg = seg_ref[...]
    cos, sin = c_ref[...], s_ref[...]

    def lora_norm(v, g):
        return (v * lax.rsqrt(jnp.mean(v * v, axis=-1, keepdims=True) + EPS) * g).astype(BF16)

    cq = lora_norm(_dot(h, wq_ref[...]), gql_ref[...])
    ckv = lora_norm(_dot(h, wkv_ref[...]), gkvl_ref[...])
    kr = _dot(h, wkr_ref[...])
    kr = kr * lax.rsqrt(_seg_mean(kr * kr, seg) + EPS) * gkr_ref[...]
    kr = _rope128(kr, cos, sin, MLA_ROPE // 2)
    qa = _dot(cq, wuq_ref[...])
    ka = _dot(ckv, wuk_ref[...])
    for hd in range(MLA_HEADS):
        sl = slice(hd * LANES, (hd + 1) * LANES)
        v = qa[:, sl]
        vn = v * lax.rsqrt(_seg_mean(v * v, seg) + EPS) * gq_ref[...]
        q_ref[:, sl] = _rope128(vn, cos, sin, MLA_ROPE // 2).astype(BF16)
        v = ka[:, sl]
        k_ref[:, sl] = (v * lax.rsqrt(_seg_mean(v * v, seg) + EPS) * gk_ref[...] + kr).astype(BF16)
    va = _dot(ckv, wuv_ref[...])
    lane = _lane(va.shape) % LANES
    low = lane < HEAD
    vlo_ref[...] = jnp.where(low, va, jnp.where(lane == LOW_ONE, 1.0, 0.0)).astype(BF16)
    vhi_ref[...] = jnp.where(low, jnp.where(lane == HIGH_ONE, 1.0, 0.0), va).astype(BF16)


def _l1_proj(xa, mod, g, wq, wkv, wkr, gql, gkvl, wuq, wuk, wuv, seg, gq, gk, gkr, cos, sin, *, tm, n_x):
    t, d = xa.shape
    row = lambda i: (i, 0)
    hw = MLA_HEADS * LANES
    vw = MLA_HEADS * HEAD
    consts = (mod, g, wq, wkv, wkr, gql, gkvl, wuq, wuk, wuv, seg, gq, gk, gkr)
    return pl.pallas_call(
        functools.partial(_l1_proj_kernel, tm=tm, n_x=n_x, d=d),
        grid=(t // tm,),
        in_specs=[pl.BlockSpec((tm, d), row)] + [_resident(a.shape) for a in consts]
                 + [pl.BlockSpec((tm, LANES), row)] * 2,
        out_specs=[pl.BlockSpec((tm, hw), row), pl.BlockSpec((tm, hw), row),
                   pl.BlockSpec((tm, vw), row), pl.BlockSpec((tm, vw), row)],
        out_shape=[jax.ShapeDtypeStruct((t, hw), BF16), jax.ShapeDtypeStruct((t, hw), BF16),
                   jax.ShapeDtypeStruct((t, vw), BF16), jax.ShapeDtypeStruct((t, vw), BF16)],
        compiler_params=_cparams(("parallel",)),
    )(xa, *consts, cos, sin)


def _l1_out_kernel(x_ref, mod_ref, g_ref, o_ref, wo_ref, rhi_ref, rlo_ref, x3_ref, h_ref, ei_ref, ew_ref, *, d):
    x3 = x_ref[...] + mod_ref[0:1, 2 * d:3 * d] * _dot(o_ref[...], wo_ref[...])
    x3_ref[...] = x3
    ms = jnp.mean(x3 * x3, axis=-1, keepdims=True)
    h = x3 * lax.rsqrt(ms + EPS) * (g_ref[...] * (1.0 + mod_ref[0:1, 4 * d:5 * d])) + mod_ref[0:1, 3 * d:4 * d]
    hi = h.astype(BF16)
    bits = lax.bitcast_convert_type(hi.astype(F32), jnp.uint32)
    words = (bits[:, :d // 2] >> 16) | (bits[:, d // 2:] & jnp.uint32(0xFFFF0000))
    h_ref[...] = words
    lo = (h - hi.astype(F32)).astype(BF16)
    logits = _dot(hi, rhi_ref[...]) + (_dot(hi, rlo_ref[...]) + _dot(lo, rhi_ref[...]))
    lane_i = _lane(logits.shape)
    lane = lane_i.astype(F32)
    logits = jnp.where(lane_i < N_EXPERTS, logits, -jnp.inf)
    v1 = jnp.max(logits, axis=-1, keepdims=True)
    i1 = jnp.min(jnp.where(logits == v1, lane, float(LANES)), axis=-1, keepdims=True)
    rest = jnp.where(lane == i1, -jnp.inf, logits)
    v2 = jnp.max(rest, axis=-1, keepdims=True)
    i2 = jnp.min(jnp.where(rest == v2, lane, float(LANES)), axis=-1, keepdims=True)
    e2 = jnp.exp(v2 - v1)
    den = 1.0 + e2
    ei_ref[...] = jnp.where(lane_i == 0, i1, jnp.where(lane_i == 1, i2, 0.0)).astype(jnp.int32)
    ew_ref[...] = jnp.where(lane_i == 0, 1.0 / den, jnp.where(lane_i == 1, e2 / den, 0.0))


def _l1_out(xa, mod, g, o, wo, rhi, rlo, *, tm):
    n, d = o.shape[0], xa.shape[1]
    row = lambda i: (i, 0)
    return pl.pallas_call(
        functools.partial(_l1_out_kernel, d=d),
        grid=(n // tm,),
        in_specs=[pl.BlockSpec((tm, d), row), _resident(mod.shape), _resident(g.shape),
                  pl.BlockSpec((tm, o.shape[1]), row), _resident(wo.shape), _resident(rhi.shape),
                  _resident(rlo.shape)],
        out_specs=[pl.BlockSpec((tm, d), row), pl.BlockSpec((tm, d // 2), row),
                   pl.BlockSpec((tm, LANES), row), pl.BlockSpec((tm, LANES), row)],
        out_shape=[jax.ShapeDtypeStruct((n, d), F32), jax.ShapeDtypeStruct((n, d // 2), jnp.uint32),
                   jax.ShapeDtypeStruct((n, LANES), jnp.int32), jax.ShapeDtypeStruct((n, LANES), F32)],
        compiler_params=_cparams(("parallel",)),
    )(xa, mod, g, o, wo, rhi, rlo)


def _moe_kernel(te_ref, nv_ref, x_ref, w1_ref, w3_ref, w2_ref, y_ref, acc_ref, *, nf):
    i, f = pl.program_id(0), pl.program_id(1)

    @pl.when(f == 0)
    def _zero():
        acc_ref[...] = jnp.zeros_like(acc_ref)

    @pl.when(i < nv_ref[0])
    def _compute():
        words = x_ref[...]
        lo = lax.bitcast_convert_type(words << 16, F32)
        hi = lax.bitcast_convert_type(words & jnp.uint32(0xFFFF0000), F32)
        x = jnp.concatenate([lo, hi], axis=1).astype(BF16)
        a = _dot(x, w1_ref[0])
        u = (_silu(a) * _dot(x, w3_ref[0])).astype(BF16)
        acc_ref[...] += _dot(u, w2_ref[0])

    @pl.when(f == nf - 1)
    def _store():
        y_ref[...] = acc_ref[...].astype(y_ref.dtype)


def _moe(tile_expert, n_valid, xs, w1, w3, w2, *, tm, tf):
    p, d = xs.shape[0], 2 * xs.shape[1]
    fdim = w1.shape[2]
    nf = fdim // tf
    fi = lambda i, f, te, nv: jnp.where(i < nv[0], f, nf - 1)
    grid_spec = pltpu.PrefetchScalarGridSpec(
        num_scalar_prefetch=2,
        grid=(p // tm, nf),
        in_specs=[pl.BlockSpec((tm, d // 2), lambda i, f, te, nv: (i, 0)),
                  pl.BlockSpec((1, d, tf), lambda i, f, te, nv: (te[i], 0, fi(i, f, te, nv))),
                  pl.BlockSpec((1, d, tf), lambda i, f, te, nv: (te[i], 0, fi(i, f, te, nv))),
                  pl.BlockSpec((1, tf, d), lambda i, f, te, nv: (te[i], fi(i, f, te, nv), 0))],
        out_specs=pl.BlockSpec((tm, d), lambda i, f, te, nv: (i, 0)),
        scratch_shapes=[pltpu.VMEM((tm, d), F32)],
    )
    return pl.pallas_call(
        functools.partial(_moe_kernel, nf=nf),
        grid_spec=grid_spec,
        out_shape=jax.ShapeDtypeStruct((p, d), BF16),
        compiler_params=_cparams(("arbitrary", "arbitrary")),
    )(tile_expert, n_valid, xs, w1, w3, w2)


SC_GATHER_WINDOW = 128
SC_LANES = 16


def _gather_rows(x, idx):
    n, d = idx.shape[0], x.shape[1]
    w = SC_GATHER_WINDOW
    assert n % w == 0
    mesh = plsc.VectorSubcoreMesh(core_axis_name="core", subcore_axis_name="subcore")

    @pl.kernel(out_type=jax.ShapeDtypeStruct((n, d), x.dtype), mesh=mesh,
               scratch_types=[pltpu.SemaphoreType.DMA])
    def gather_kernel(x_hbm, i_hbm, o_hbm, sem):
        def body(i_vmem, o_vmem):
            copies = []
            for k in range(w // SC_LANES):
                grp = pl.ds(k * SC_LANES, SC_LANES)
                copies.append(pltpu.async_copy(x_hbm.at[i_vmem[0, grp]], o_vmem.at[grp], sem))
            for cp in copies:
                cp.wait()

        pltpu.emit_pipeline(
            body,
            grid=(n // w,),
            in_specs=[pl.BlockSpec((1, w), lambda i: (0, i))],
            out_specs=[pl.BlockSpec((w, d), lambda i: (i, 0), pipeline_mode=pl.Buffered(1))],
            core_axis_name=("core", "subcore"),
            dimension_semantics=(pltpu.PARALLEL,),
        )(i_hbm, o_hbm)

    return gather_kernel(x, idx.reshape(1, n))


def _combine_kernel(x_ref, mod_ref, ew_ref, ya_ref, yb_ref, o_ref, *, d):
    ew = ew_ref[...]
    y = ew[:, 0:1] * ya_ref[...].astype(F32) + ew[:, 1:2] * yb_ref[...].astype(F32)
    o_ref[...] = x_ref[...] + mod_ref[0:1, 5 * d:6 * d] * y


def _combine(x3, mod, ew, ya, yb, *, tm):
    n, d = x3.shape
    row = lambda i: (i, 0)
    return pl.pallas_call(
        functools.partial(_combine_kernel, d=d),
        grid=(n // tm,),
        in_specs=[pl.BlockSpec((tm, d), row), _resident(mod.shape), pl.BlockSpec((tm, LANES), row),
                  pl.BlockSpec((tm, d), row), pl.BlockSpec((tm, d), row)],
        out_specs=pl.BlockSpec((tm, d), row),
        out_shape=jax.ShapeDtypeStruct((n, d), F32),
        compiler_params=_cparams(("parallel",)),
    )(x3, mod, ew, ya, yb)


def _deinterleave(width):
    return np.concatenate([np.arange(0, width, 2), np.arange(1, width, 2)])


def _rope_tables(n_x, n_ctx, rot_dim, seg_start, seg_repeat):
    rows = n_x // GRID_W
    row = jnp.broadcast_to(jnp.arange(rows)[:, None], (rows, GRID_W)).reshape(n_x).astype(F32)
    col = jnp.broadcast_to(jnp.arange(GRID_W)[None, :], (rows, GRID_W)).reshape(n_x).astype(F32)
    axis_dim = rot_dim // 2
    inv_freq = ROPE_THETA ** (-jnp.arange(0, axis_dim, 2, dtype=F32) / axis_dim)
    ang = jnp.concatenate([row[:, None] * inv_freq, col[:, None] * inv_freq], axis=-1)
    cos, sin = jnp.cos(ang), jnp.sin(ang)
    tail = LANES - seg_start - seg_repeat * rot_dim
    c = jnp.concatenate([jnp.ones((n_x, seg_start), F32)] + [cos, cos] * seg_repeat + [jnp.ones((n_x, tail), F32)],
                        axis=-1)
    s = jnp.concatenate([jnp.zeros((n_x, seg_start), F32)] + [-sin, sin] * seg_repeat
                        + [jnp.zeros((n_x, tail), F32)], axis=-1)
    c = jnp.concatenate([c, jnp.ones((n_ctx, LANES), F32)], axis=0)
    s = jnp.concatenate([s, jnp.zeros((n_ctx, LANES), F32)], axis=0)
    return c, s


def _segment_matrix(bounds):
    m = np.zeros((LANES, LANES), np.float32)
    for lo, hi in bounds:
        m[lo:hi, lo:hi] = 1.0 / (hi - lo)
    return jnp.asarray(m, BF16)


def _token_tile(t):
    for tm in (640, 512, 256, 128):
        if t % tm == 0:
            return tm
    raise ValueError(f"token count {t} has no supported tile")


def kernel(x, c, ctx, c_ctx, l0_ada_w, l0_ada_b, l0_norm1_g, l0_norm2_g, l0_w_in, l0_ret_log_decay, l0_q_norm_g, l0_k_norm_g, l0_w_out, l0_ffn_w1, l0_ffn_w3, l0_ffn_w2, l1_ada_w, l1_ada_b, l1_norm1_g, l1_norm2_g, l1_w_in, l1_q_lora_g, l1_kv_lora_g, l1_w_uq, l1_w_ukv, l1_q_nope_g, l1_q_rope_g, l1_k_nope_g, l1_k_rope_g, l1_w_out, l1_router, l1_exp_w1, l1_exp_w3, l1_exp_w2):
    b, n_x, d = x.shape
    n_ctx = ctx.shape[1]
    assert b == 1 and n_x % 256 == 0 and n_ctx % 256 == 0 and n_x % GRID_W == 0
    t = n_x + n_ctx
    tm = _token_tile(t)
    tmx = _token_tile(n_x)
    xa = jnp.concatenate([x[0], ctx[0]], axis=0)
    row1 = lambda v: v.reshape(1, -1).astype(F32)

    cvec = jnp.zeros((8, d), F32).at[0].set(c[0]).at[1].set(c_ctx)
    mod0 = _ada(cvec, l0_ada_w, l0_ada_b)
    mod1 = _ada(cvec, l1_ada_w, l1_ada_b)

    rw = RET_HEADS * HEAD
    perm = _deinterleave(HEAD)
    n_qk = GQA_HEADS + GQA_KV_HEADS
    qk_cols = l0_w_in[:, 4 * rw:4 * rw + n_qk * HEAD].reshape(d, n_qk, HEAD // 2, 2)
    qk_cols = jnp.swapaxes(qk_cols, 2, 3).reshape(d, n_qk * HEAD)
    w_in0 = jnp.concatenate([l0_w_in[:, :rw], l0_w_in[:, rw:2 * rw] * (HEAD ** -0.5), l0_w_in[:, 2 * rw:4 * rw],
                             qk_cols, l0_w_in[:, 4 * rw + n_qk * HEAD:]], axis=1).astype(BF16)
    seg64 = _segment_matrix([(0, HEAD), (HEAD, 2 * HEAD)])
    gq0 = row1(jnp.tile(l0_q_norm_g[perm], 2) * (HEAD ** -0.5 * LOG2E))
    gk0 = row1(jnp.tile(l0_k_norm_g[perm], 2))
    cos0, sin0 = _rope_tables(n_x, n_ctx, HEAD, 0, 2)

    rq, rk, rv, rg, gq, gkx, gvx = _l0_proj(xa, mod0, row1(l0_norm1_g), w_in0, seg64, gq0, gk0, cos0, sin0,
                                             tm=tm, n_x=n_x)
    o_f, o_b = _retention(l0_ret_log_decay.astype(F32), rq, rk, rv, n_x=n_x)

    gqa_maps = dict(q_maps=(lambda j: j, lambda j: j),
                    k_maps=(lambda j: 2 * (j // 2), lambda j: 2 * (j // 2) + 1),
                    v_maps=(lambda j: 2 * (j // 2), lambda j: 2 * (j // 2) + 1), n_pairs=GQA_HEADS // 2)
    bq = 512 if n_x % 512 == 0 else 256
    ao_x = _flash(gq, gkx, (gvx, gvx), n_q=n_x, q_row0=0, kv_row0=0, n_kv=t, bq=bq, bk=_kv_block(t), rs=32,
                  **gqa_maps)
    ao_c = _flash(gq, gkx, (gvx, gvx), n_q=n_ctx, q_row0=n_x, kv_row0=n_x, n_kv=n_ctx, bq=n_ctx,
                  bk=_kv_block(n_ctx), rs=32, **gqa_maps)
    ao = jnp.concatenate([ao_x, ao_c], axis=0)

    xa = _l0_out(xa, mod0, o_f, o_b, rg, ao, seg64, l0_w_out.astype(BF16), tm=tm, n_x=n_x)
    xa = _ffn(xa, mod0, row1(l0_norm2_g), l0_ffn_w1.astype(BF16), l0_ffn_w3.astype(BF16),
              l0_ffn_w2.astype(BF16), tm=tm, n_x=n_x)

    rperm = _deinterleave(MLA_ROPE)
    qk_w = MLA_NOPE + MLA_ROPE
    wuq = jnp.zeros((MLA_Q_RANK, MLA_HEADS * LANES), F32)
    wuk = jnp.zeros((MLA_KV_RANK, MLA_HEADS * LANES), F32)
    wuv = []
    for hd in range(MLA_HEADS):
        src = l1_w_uq[:, hd * qk_w:(hd + 1) * qk_w]
        wuq = wuq.at[:, hd * LANES:hd * LANES + MLA_NOPE].set(src[:, :MLA_NOPE])
        wuq = wuq.at[:, hd * LANES + MLA_NOPE:hd * LANES + qk_w].set(src[:, MLA_NOPE:][:, rperm])
        kvsrc = l1_w_ukv[:, hd * 2 * HEAD:(hd + 1) * 2 * HEAD]
        wuk = wuk.at[:, hd * LANES:hd * LANES + MLA_NOPE].set(kvsrc[:, :MLA_NOPE])
        wuv.append(kvsrc[:, MLA_NOPE:])
    wuv = jnp.concatenate(wuv, axis=1)
    wkr = jnp.zeros((d, LANES), F32).at[:, MLA_NOPE:qk_w].set(l1_w_in[:, MLA_Q_RANK + MLA_KV_RANK:][:, rperm])
    pad = jnp.zeros((LANES - qk_w,), F32)
    zn = jnp.zeros((MLA_NOPE,), F32)
    gq1 = row1(jnp.concatenate([l1_q_nope_g, l1_q_rope_g[rperm], pad]) * (qk_w ** -0.5 * LOG2E))
    gk1 = row1(jnp.concatenate([l1_k_nope_g, jnp.zeros((LANES - MLA_NOPE,), F32)]))
    gkr1 = row1(jnp.concatenate([zn, l1_k_rope_g[rperm], pad]))
    seg_mla = _segment_matrix([(0, MLA_NOPE), (MLA_NOPE, qk_w)])
    cos1, sin1 = _rope_tables(n_x, n_ctx, MLA_ROPE, MLA_NOPE, 1)

    mq, mk, mvlo, mvhi = _l1_proj(
        xa, mod1, row1(l1_norm1_g), l1_w_in[:, :MLA_Q_RANK].astype(BF16),
        l1_w_in[:, MLA_Q_RANK:MLA_Q_RANK + MLA_KV_RANK].astype(BF16), wkr.astype(BF16),
        row1(l1_q_lora_g), row1(l1_kv_lora_g), wuq.astype(BF16), wuk.astype(BF16), wuv.astype(BF16),
        seg_mla, gq1, gk1, gkr1, cos1, sin1, tm=tm, n_x=n_x)
    n_e, _, fdim = l1_exp_w1.shape
    mo, ew1, ew3, ew2 = _flash(
        mq, mk, (mvlo, mvhi), q_maps=(lambda j: 2 * j, lambda j: 2 * j + 1),
        k_maps=(lambda j: 2 * j, lambda j: 2 * j + 1), v_maps=(lambda j: j, lambda j: j),
        n_pairs=MLA_HEADS // 2, n_q=n_x, q_row0=0, kv_row0=0, n_kv=t, bq=bq, bk=_kv_block(t), rs=32,
        side=(l1_exp_w1.reshape(n_e * d, fdim), l1_exp_w3.reshape(n_e * d, fdim),
              l1_exp_w2.reshape(n_e * fdim, d)))
    ew1, ew3, ew2 = ew1.reshape(n_e, d, fdim), ew3.reshape(n_e, d, fdim), ew2.reshape(n_e, fdim, d)

    router = jnp.zeros((d, LANES), F32).at[:, :N_EXPERTS].set(l1_router)
    r_hi = router.astype(BF16)
    r_lo = (router - r_hi.astype(F32)).astype(BF16)
    x3, hmoe, ei, ew = _l1_out(xa, mod1, row1(l1_norm2_g), mo, l1_w_out.astype(BF16), r_hi, r_lo, tm=tmx)

    tme = 512 if n_x >= 4096 else 128
    e_flat = ei[:, :TOP_K].reshape(-1)
    onehot = (e_flat[:, None] == jnp.arange(N_EXPERTS)[None, :]).astype(jnp.int32)
    csum = jnp.cumsum(onehot, axis=0)
    rank = jnp.sum((csum - onehot) * onehot, axis=1)
    counts = csum[-1]
    padded = ((counts + tme - 1) // tme) * tme
    ends = jnp.cumsum(padded)
    pos = (ends - padded)[e_flat] + rank
    n_tiles = (TOP_K * n_x) // tme + N_EXPERTS
    p_rows = n_tiles * tme
    src = jnp.zeros((p_rows,), jnp.int32).at[pos].set(jnp.arange(TOP_K * n_x, dtype=jnp.int32) // TOP_K)
    n_valid = (ends[-1] // tme).astype(jnp.int32)
    tile_ids = jnp.minimum(jnp.arange(n_tiles, dtype=jnp.int32), n_valid - 1)
    tile_expert = jnp.sum((ends[None, :] <= (tile_ids * tme)[:, None]).astype(jnp.int32), axis=1)
    tile_expert = jnp.minimum(tile_expert, N_EXPERTS - 1)
    xs = _gather_rows(hmoe, src)
    tf = fdim // 2 if (fdim // 2) % LANES == 0 else fdim
    ys = _moe(tile_expert, n_valid.reshape(1), xs, ew1, ew3, ew2, tm=tme, tf=tf)
    pos2 = pos.reshape(n_x, TOP_K)
    out = _combine(x3, mod1, ew, jnp.take(ys, pos2[:, 0], axis=0, mode="clip"),
                   jnp.take(ys, pos2[:, 1], axis=0, mode="clip"), tm=tmx)
    return out[None]
```

```python
import functools
import math

import numpy as np
import jax
import jax.numpy as jnp
from jax import lax
from jax.experimental import pallas as pl
from jax.experimental.pallas import tpu as pltpu
from jax.experimental.pallas import tpu_sc as plsc

F32 = jnp.float32
BF16 = jnp.bfloat16

EPS = 1e-6
ROPE_THETA = 10000.0
GRID_W = 64
LANES = 128
HEAD = 64
RET_CHUNK = 128
RET_HEADS = 8
GQA_HEADS = 8
GQA_KV_HEADS = 2
MLA_HEADS = 8
MLA_Q_RANK = 384
MLA_KV_RANK = 256
MLA_NOPE = 64
MLA_ROPE = 32
N_EXPERTS = 8
TOP_K = 2
LOW_ONE = HEAD
HIGH_ONE = 0
LOG2E = math.log2(math.e)
VMEM_LIMIT = 56 * 1024 * 1024


def _cparams(sem, vmem=VMEM_LIMIT):
    return pltpu.CompilerParams(dimension_semantics=sem, vmem_limit_bytes=vmem)


def _resident(shape):
    nd = len(shape)
    return pl.BlockSpec(shape, lambda *_: (0,) * nd, pipeline_mode=pl.Buffered(1))


def _dot(a, b):
    return jnp.dot(a, b, preferred_element_type=F32)


def _dot_nt(a, b):
    return lax.dot_general(a, b, (((1,), (1,)), ((), ())), preferred_element_type=F32)


def _seg_mean(v, seg):
    hi = v.astype(BF16)
    lo = (v - hi.astype(F32)).astype(BF16)
    return _dot(hi, seg) + _dot(lo, seg)


def _silu(x):
    return x * jax.nn.sigmoid(x)


def _modulated(x, mod_ref, g_ref, which, tile, tm, n_x, d):
    ms = jnp.mean(x * x, axis=-1, keepdims=True)
    xn = x * lax.rsqrt(ms + EPS)
    g = g_ref[...]
    sh, sc = 3 * which, 3 * which + 1
    a_x = g * (1.0 + mod_ref[0:1, sc * d:(sc + 1) * d])
    a_c = g * (1.0 + mod_ref[1:2, sc * d:(sc + 1) * d])
    b_x = mod_ref[0:1, sh * d:(sh + 1) * d]
    b_c = mod_ref[1:2, sh * d:(sh + 1) * d]
    row = tile * tm + lax.broadcasted_iota(jnp.int32, (tm, 1), 0)
    is_ctx = row >= n_x
    return xn * jnp.where(is_ctx, a_c, a_x) + jnp.where(is_ctx, b_c, b_x)


def _row_gate(mod_ref, idx, tile, tm, n_x, d):
    row = tile * tm + lax.broadcasted_iota(jnp.int32, (tm, 1), 0)
    return jnp.where(row >= n_x, mod_ref[1:2, idx * d:(idx + 1) * d], mod_ref[0:1, idx * d:(idx + 1) * d])


def _lane(shape):
    return lax.broadcasted_iota(jnp.int32, shape, len(shape) - 1)


def _ada_kernel(c_ref, w_ref, b_ref, o_ref):
    c = c_ref[...]
    o_ref[...] = jnp.dot(_silu(c), w_ref[...], preferred_element_type=F32,
                         precision=lax.Precision.HIGHEST) + b_ref[...]


def _ada(cvec8, w, b):
    d, n = w.shape
    tn = n // 4
    return pl.pallas_call(
        _ada_kernel,
        grid=(n // tn,),
        in_specs=[pl.BlockSpec((8, d), lambda j: (0, 0)),
                  pl.BlockSpec((d, tn), lambda j: (0, j)),
                  pl.BlockSpec((1, tn), lambda j: (0, j))],
        out_specs=pl.BlockSpec((8, tn), lambda j: (0, j)),
        out_shape=jax.ShapeDtypeStruct((8, n), F32),
        compiler_params=_cparams(("arbitrary",)),
    )(cvec8, w, b.reshape(1, n))


def _rope128(v, c, s, half):
    lane = _lane(v.shape)
    swapped = jnp.where(lane % (2 * half) < half, pltpu.roll(v, LANES - half, 1), pltpu.roll(v, half, 1))
    return v * c + swapped * s


def _l0_proj_kernel(x_ref, mod_ref, g_ref, w_ref, seg_ref, gq_ref, gk_ref, c_ref, s_ref,
                    rq_ref, rk_ref, rv_ref, rg_ref, q_ref, k_ref, v_ref, *, tm, n_x, d):
    i = pl.program_id(0)
    h = _modulated(x_ref[...], mod_ref, g_ref, 0, i, tm, n_x, d).astype(BF16)
    rw = RET_HEADS * HEAD
    for idx, ref in enumerate((rq_ref, rk_ref, rv_ref, rg_ref)):
        ref[...] = _dot(h, w_ref[:, idx * rw:(idx + 1) * rw]).astype(BF16)
    seg = seg_ref[...]
    cos, sin = c_ref[...], s_ref[...]
    base = 4 * rw
    qw = GQA_HEADS * HEAD
    qa = _dot(h, w_ref[:, base:base + qw])
    for g in range(qw // LANES):
        v = qa[:, g * LANES:(g + 1) * LANES]
        vn = v * lax.rsqrt(_seg_mean(v * v, seg) + EPS) * gq_ref[...]
        q_ref[:, g * LANES:(g + 1) * LANES] = _rope128(vn, cos, sin, HEAD // 2).astype(BF16)
    kv = _dot(h, w_ref[:, base + qw:base + qw + 2 * LANES])
    kk = kv[:, :LANES]
    kk = kk * lax.rsqrt(_seg_mean(kk * kk, seg) + EPS) * gk_ref[...]
    kk = _rope128(kk, cos, sin, HEAD // 2)
    vv = kv[:, LANES:]
    lane = _lane(kk.shape)
    low = lane < HEAD
    for src, ref, one in ((kk, k_ref, 0.0), (vv, v_ref, 1.0)):
        sw = pltpu.roll(src, HEAD, 1)
        lo_fill = jnp.where(lane == LOW_ONE, one, 0.0)
        hi_fill = jnp.where(lane == HIGH_ONE, one, 0.0)
        ref[:, 0 * LANES:1 * LANES] = jnp.where(low, src, lo_fill).astype(BF16)
        ref[:, 1 * LANES:2 * LANES] = jnp.where(low, hi_fill, sw).astype(BF16)
        ref[:, 2 * LANES:3 * LANES] = jnp.where(low, sw, lo_fill).astype(BF16)
        ref[:, 3 * LANES:4 * LANES] = jnp.where(low, hi_fill, src).astype(BF16)


def _l0_proj(xa, mod, g, w, seg, gq, gk, cos, sin, *, tm, n_x):
    t, d = xa.shape
    rw = RET_HEADS * HEAD
    row = lambda i: (i, 0)
    outs = [jax.ShapeDtypeStruct((t, rw), BF16)] * 7
    return pl.pallas_call(
        functools.partial(_l0_proj_kernel, tm=tm, n_x=n_x, d=d),
        grid=(t // tm,),
        in_specs=[pl.BlockSpec((tm, d), row), _resident(mod.shape), _resident(g.shape), _resident(w.shape),
                  _resident(seg.shape), _resident(gq.shape), _resident(gk.shape),
                  pl.BlockSpec((tm, LANES), row), pl.BlockSpec((tm, LANES), row)],
        out_specs=[pl.BlockSpec((tm, rw), row)] * 7,
        out_shape=outs,
        compiler_params=_cparams(("parallel",)),
    )(xa, mod, g, w, seg, gq, gk, cos, sin)


def _retention_kernel(lg_ref, qf_ref, kf_ref, vf_ref, qb_ref, kb_ref, vb_ref, of_ref, ob_ref,
                      state_ref, decay_ref, xi_ref, zeta_ref, gl_ref):
    c = RET_CHUNK
    npairs = RET_HEADS * HEAD // LANES
    step = pl.program_id(0)

    @pl.when(step == 0)
    def _init():
        state_ref[...] = jnp.zeros_like(state_ref)
        ci = lax.broadcasted_iota(jnp.int32, (c, c), 0).astype(F32)
        mi = lax.broadcasted_iota(jnp.int32, (c, c), 1).astype(F32)
        pos = lax.broadcasted_iota(jnp.int32, (c, RET_HEADS * HEAD), 0).astype(F32)
        lane_head = _lane((1, RET_HEADS * HEAD)) // HEAD
        for dr in range(2):
            lgv = jnp.zeros((1, RET_HEADS * HEAD), F32)
            for hd in range(RET_HEADS):
                lg = lg_ref[dr, hd]
                rel = (ci - mi) if dr == 0 else (mi - ci)
                half = slice((hd % 2) * c, (hd % 2 + 1) * c)
                decay_ref[dr, hd // 2, :, half] = jnp.where(rel >= 0, jnp.exp(jnp.maximum(rel, 0.0) * lg), 0.0)
                lgv = jnp.where(lane_head == hd, lg, lgv)
            p = pos if dr == 0 else (c - 1.0 - pos)
            xi_ref[dr] = jnp.exp((p + 1.0) * lgv)
            zeta_ref[dr] = jnp.exp((c - 1.0 - p) * lgv)
            gl_ref[dr] = jnp.exp(float(c) * lgv)

    low = _lane((c, LANES)) < HEAD
    r_i = lax.broadcasted_iota(jnp.int32, (LANES, LANES), 0) // HEAD
    c_i = lax.broadcasted_iota(jnp.int32, (LANES, LANES), 1) // HEAD
    blockdiag = r_i == c_i
    for dr, (q_ref, k_ref, v_ref, o_ref) in enumerate(((qf_ref, kf_ref, vf_ref, of_ref),
                                                       (qb_ref, kb_ref, vb_ref, ob_ref))):
        for j in range(npairs):
            sl = slice(j * LANES, (j + 1) * LANES)
            q, k, v = q_ref[:, sl], k_ref[:, sl], v_ref[:, sl]
            zero = jnp.zeros_like(k)
            k2 = jnp.concatenate([jnp.where(low, k, zero), jnp.where(low, zero, k)], axis=0)
            v2 = jnp.concatenate([jnp.where(low, v, zero), jnp.where(low, zero, v)], axis=0)
            s = _dot_nt(q, k2) * decay_ref[dr, j]
            o = _dot(s.astype(BF16), v2)
            st = state_ref[dr, j]
            qx = (q.astype(F32) * xi_ref[dr, :, sl]).astype(BF16)
            o = o + _dot(qx, st.astype(BF16))
            o_ref[:, sl] = o.astype(BF16)
            kz = (k.astype(F32) * zeta_ref[dr, :, sl]).T.astype(BF16)
            u = _dot(kz, v)
            state_ref[dr, j] = st * gl_ref[dr, :, sl] + jnp.where(blockdiag, u, 0.0)


def _retention(lg, rq, rk, rv, *, n_x):
    t, w = rq.shape
    c = RET_CHUNK
    nc, ncx = t // c, n_x // c
    fwd = lambda i: ((i + ncx) % nc, 0)
    bwd = lambda i: (nc - 1 - i, 0)
    blk = lambda m: pl.BlockSpec((c, w), m)
    npairs = w // LANES
    return pl.pallas_call(
        _retention_kernel,
        grid=(nc,),
        in_specs=[pl.BlockSpec(memory_space=pltpu.SMEM)] + [blk(fwd)] * 3 + [blk(bwd)] * 3,
        out_specs=[blk(fwd), blk(bwd)],
        out_shape=[jax.ShapeDtypeStruct((t, w), BF16)] * 2,
        scratch_shapes=[pltpu.VMEM((2, npairs, LANES, LANES), F32),
                        pltpu.VMEM((2, npairs, c, 2 * c), F32),
                        pltpu.VMEM((2, c, w), F32), pltpu.VMEM((2, c, w), F32),
                        pltpu.VMEM((2, 1, w), F32)],
        compiler_params=_cparams(("arbitrary",)),
    )(lg, rq, rk, rv, rq, rk, rv)


def _flash_kernel(q0_ref, q1_ref, k0_ref, k1_ref, v0_ref, v1_ref, *rest, bk, nkv, rs, n_side):
    side_in, o_ref, side_out = rest[:n_side], rest[n_side], rest[n_side + 1:2 * n_side + 1]
    s_ref, p_ref, a_ref, m_ref, acc_ref = rest[2 * n_side + 1:]
    for src_ref, dst_ref in zip(side_in, side_out):
        dst_ref[...] = src_ref[...].astype(dst_ref.dtype)
    bq = q0_ref.shape[0]
    q_refs, k_refs, v_refs = (q0_ref, q1_ref), (k0_ref, k1_ref), (v0_ref, v1_ref)
    m_ref[...] = jnp.full(m_ref.shape, -jnp.inf, F32)
    acc_ref[...] = jnp.zeros(acc_ref.shape, F32)

    def keys(t):
        return pl.ds(t * bk if isinstance(t, int) else pl.multiple_of(t * bk, bk), bk)

    def scores(t, slot):
        for h in range(2):
            s_ref[slot, h] = _dot_nt(q_refs[h][...], k_refs[h][keys(t), :])

    def softmax(slot):
        col = lambda c: slice(c * LANES, (c + 1) * LANES)
        for r in range(bq // rs):
            rows = slice(r * rs, (r + 1) * rs)
            for h in range(2):
                mx = s_ref[slot, h, rows, col(0)]
                for c in range(1, bk // LANES):
                    mx = jnp.maximum(mx, s_ref[slot, h, rows, col(c)])
                m_old = m_ref[h, rows, :]
                m_new = jnp.maximum(m_old, jnp.max(mx, axis=-1, keepdims=True))
                a_ref[slot, h, rows, :] = jnp.exp2(m_old - m_new)
                m_ref[h, rows, :] = m_new
                for c in range(bk // LANES):
                    p_ref[slot, h, rows, col(c)] = jnp.exp2(s_ref[slot, h, rows, col(c)] - m_new).astype(BF16)

    def values(t, slot):
        for h in range(2):
            acc_ref[h] = acc_ref[h] * a_ref[slot, h] + _dot(p_ref[slot, h], v_refs[h][keys(t), :])

    scores(0, 0)

    def body(i, carry):
        t = 2 * i
        scores(t + 1, 1)
        softmax(0)
        values(t, 0)
        scores(t + 2, 0)
        softmax(1)
        values(t + 1, 1)
        return carry

    n_loop = (nkv - 1) // 2
    lax.fori_loop(0, n_loop, body, 0)
    last = 2 * n_loop
    if last + 1 < nkv:
        scores(last + 1, 1)
    softmax(0)
    values(last, 0)
    if last + 1 < nkv:
        softmax(1)
        values(last + 1, 1)
    low = _lane((bq, LANES)) < HEAD
    acc0, acc1 = acc_ref[0], acc_ref[1]
    out = jnp.where(low, acc0 / acc0[:, LOW_ONE:LOW_ONE + 1], acc1 / acc1[:, HIGH_ONE:HIGH_ONE + 1])
    o_ref[...] = out.astype(o_ref.dtype)


def _flash(q, kmat, vmat, *, q_maps, k_maps, v_maps, n_q, q_row0, kv_row0, n_kv, n_pairs, bq, bk, rs, side=()):
    assert q_row0 % bq == 0 and n_q % bq == 0 and n_kv % bk == 0 and kv_row0 % n_kv == 0 and bq % rs == 0
    qb0, kb0 = q_row0 // bq, kv_row0 // n_kv
    n_i = n_q // bq
    steps = n_pairs * n_i
    assert all(a.shape[0] % (16 * steps) == 0 for a in side)
    qspec = lambda m: pl.BlockSpec((bq, LANES), lambda j, i: (i + qb0, m(j)))
    kspec = lambda m: pl.BlockSpec((n_kv, LANES), lambda j, i: (kb0, m(j)), pipeline_mode=pl.Buffered(1))
    side_specs = [pl.BlockSpec((a.shape[0] // steps, a.shape[1]), lambda j, i: (j * n_i + i, 0)) for a in side]
    outs = pl.pallas_call(
        functools.partial(_flash_kernel, bk=bk, nkv=n_kv // bk, rs=rs, n_side=len(side)),
        grid=(n_pairs, n_i),
        in_specs=[qspec(q_maps[0]), qspec(q_maps[1]), kspec(k_maps[0]), kspec(k_maps[1]),
                  kspec(v_maps[0]), kspec(v_maps[1])] + side_specs,
        out_specs=[pl.BlockSpec((bq, LANES), lambda j, i: (i, j))] + side_specs,
        out_shape=[jax.ShapeDtypeStruct((n_q, n_pairs * LANES), BF16)]
                  + [jax.ShapeDtypeStruct(a.shape, BF16) for a in side],
        scratch_shapes=[pltpu.VMEM((2, 2, bq, bk), F32), pltpu.VMEM((2, 2, bq, bk), BF16),
                        pltpu.VMEM((2, 2, bq, LANES), F32), pltpu.VMEM((2, bq, LANES), F32),
                        pltpu.VMEM((2, bq, LANES), F32)],
        compiler_params=_cparams(("parallel", "parallel")),
    )(q, q, kmat, kmat, vmat[0], vmat[1], *side)
    return outs if side else outs[0]


def _kv_block(n_kv):
    for bk in (1280, 1024, 512, 256):
        if n_kv % bk == 0:
            return bk
    raise ValueError(f"key count {n_kv} has no supported block")


def _l0_out_kernel(x_ref, mod_ref, of_ref, ob_ref, rg_ref, ao_ref, seg_ref, wo_ref, o_ref, *, tm, n_x, d):
    i = pl.program_id(0)
    seg = seg_ref[...]
    rw = RET_HEADS * HEAD
    acc = _dot(ao_ref[...], wo_ref[rw:, :])
    for g in range(rw // LANES):
        sl = slice(g * LANES, (g + 1) * LANES)
        o = of_ref[:, sl].astype(F32) + ob_ref[:, sl].astype(F32)
        dv = o - _seg_mean(o, seg)
        nrm = dv * lax.rsqrt(_seg_mean(dv * dv, seg) + EPS)
        ra = (nrm * _silu(rg_ref[:, sl].astype(F32))).astype(BF16)
        acc = acc + _dot(ra, wo_ref[g * LANES:(g + 1) * LANES, :])
    o_ref[...] = x_ref[...] + _row_gate(mod_ref, 2, i, tm, n_x, d) * acc


def _l0_out(xa, mod, o_f, o_b, rg, ao, seg, wo, *, tm, n_x):
    t, d = xa.shape
    rw = o_f.shape[1]
    row = lambda i: (i, 0)
    return pl.pallas_call(
        functools.partial(_l0_out_kernel, tm=tm, n_x=n_x, d=d),
        grid=(t // tm,),
        in_specs=[pl.BlockSpec((tm, d), row), _resident(mod.shape)] + [pl.BlockSpec((tm, rw), row)] * 4
                 + [_resident(seg.shape), _resident(wo.shape)],
        out_specs=pl.BlockSpec((tm, d), row),
        out_shape=jax.ShapeDtypeStruct((t, d), F32),
        compiler_params=_cparams(("parallel",)),
    )(xa, mod, o_f, o_b, rg, ao, seg, wo)


def _ffn_kernel(x_ref, mod_ref, g_ref, w1_ref, w3_ref, w2_ref, o_ref, *, tm, n_x, d):
    i = pl.program_id(0)
    x = x_ref[...]
    h = _modulated(x, mod_ref, g_ref, 1, i, tm, n_x, d).astype(BF16)
    a = _dot(h, w1_ref[...])
    u = (_silu(a) * _dot(h, w3_ref[...])).astype(BF16)
    o_ref[...] = x + _row_gate(mod_ref, 5, i, tm, n_x, d) * _dot(u, w2_ref[...])


def _ffn(xa, mod, g, w1, w3, w2, *, tm, n_x):
    t, d = xa.shape
    row = lambda i: (i, 0)
    return pl.pallas_call(
        functools.partial(_ffn_kernel, tm=tm, n_x=n_x, d=d),
        grid=(t // tm,),
        in_specs=[pl.BlockSpec((tm, d), row), _resident(mod.shape), _resident(g.shape),
                  _resident(w1.shape), _resident(w3.shape), _resident(w2.shape)],
        out_specs=pl.BlockSpec((tm, d), row),
        out_shape=jax.ShapeDtypeStruct((t, d), F32),
        compiler_params=_cparams(("parallel",)),
    )(xa, mod, g, w1, w3, w2)


def _l1_proj_kernel(x_ref, mod_ref, g_ref, wq_ref, wkv_ref, wkr_ref, gql_ref, gkvl_ref, wuq_ref, wuk_ref,
                    wuv_ref, seg_ref, gq_ref, gk_ref, gkr_ref, c_ref, s_ref,
                    q_ref, k_ref, vlo_ref, vhi_ref, *, tm, n_x, d):
    i = pl.program_id(0)
    h = _modulated(x_ref[...], mod_ref, g_ref, 0, i, tm, n_x, d).astype(BF16)
    seg = seg_ref[...]
    cos, sin = c_ref[...], s_ref[...]

    def lora_norm(v, g):
        return (v * lax.rsqrt(jnp.mean(v * v, axis=-1, keepdims=True) + EPS) * g).astype(BF16)

    cq = lora_norm(_dot(h, wq_ref[...]), gql_ref[...])
    ckv = lora_norm(_dot(h, wkv_ref[...]), gkvl_ref[...])
    kr = _dot(h, wkr_ref[...])
    kr = kr * lax.rsqrt(_seg_mean(kr * kr, seg) + EPS) * gkr_ref[...]
    kr = _rope128(kr, cos, sin, MLA_ROPE // 2)
    qa = _dot(cq, wuq_ref[...])
    ka = _dot(ckv, wuk_ref[...])
    for hd in range(MLA_HEADS):
        sl = slice(hd * LANES, (hd + 1) * LANES)
        v = qa[:, sl]
        vn = v * lax.rsqrt(_seg_mean(v * v, seg) + EPS) * gq_ref[...]
        q_ref[:, sl] = _rope128(vn, cos, sin, MLA_ROPE // 2).astype(BF16)
        v = ka[:, sl]
        k_ref[:, sl] = (v * lax.rsqrt(_seg_mean(v * v, seg) + EPS) * gk_ref[...] + kr).astype(BF16)
    va = _dot(ckv, wuv_ref[...])
    lane = _lane(va.shape) % LANES
    low = lane < HEAD
    vlo_ref[...] = jnp.where(low, va, jnp.where(lane == LOW_ONE, 1.0, 0.0)).astype(BF16)
    vhi_ref[...] = jnp.where(low, jnp.where(lane == HIGH_ONE, 1.0, 0.0), va).astype(BF16)


def _l1_proj(xa, mod, g, wq, wkv, wkr, gql, gkvl, wuq, wuk, wuv, seg, gq, gk, gkr, cos, sin, *, tm, n_x):
    t, d = xa.shape
    row = lambda i: (i, 0)
    hw = MLA_HEADS * LANES
    vw = MLA_HEADS * HEAD
    consts = (mod, g, wq, wkv, wkr, gql, gkvl, wuq, wuk, wuv, seg, gq, gk, gkr)
    return pl.pallas_call(
        functools.partial(_l1_proj_kernel, tm=tm, n_x=n_x, d=d),
        grid=(t // tm,),
        in_specs=[pl.BlockSpec((tm, d), row)] + [_resident(a.shape) for a in consts]
                 + [pl.BlockSpec((tm, LANES), row)] * 2,
        out_specs=[pl.BlockSpec((tm, hw), row), pl.BlockSpec((tm, hw), row),
                   pl.BlockSpec((tm, vw), row), pl.BlockSpec((tm, vw), row)],
        out_shape=[jax.ShapeDtypeStruct((t, hw), BF16), jax.ShapeDtypeStruct((t, hw), BF16),
                   jax.ShapeDtypeStruct((t, vw), BF16), jax.ShapeDtypeStruct((t, vw), BF16)],
        compiler_params=_cparams(("parallel",)),
    )(xa, *consts, cos, sin)


def _l1_out_kernel(x_ref, mod_ref, g_ref, o_ref, wo_ref, rhi_ref, rlo_ref, x3_ref, h_ref, ei_ref, ew_ref, *, d):
    x3 = x_ref[...] + mod_ref[0:1, 2 * d:3 * d] * _dot(o_ref[...], wo_ref[...])
    x3_ref[...] = x3
    ms = jnp.mean(x3 * x3, axis=-1, keepdims=True)
    h = x3 * lax.rsqrt(ms + EPS) * (g_ref[...] * (1.0 + mod_ref[0:1, 4 * d:5 * d])) + mod_ref[0:1, 3 * d:4 * d]
    hi = h.astype(BF16)
    bits = lax.bitcast_convert_type(hi.astype(F32), jnp.uint32)
    words = (bits[:, :d // 2] >> 16) | (bits[:, d // 2:] & jnp.uint32(0xFFFF0000))
    h_ref[...] = words
    lo = (h - hi.astype(F32)).astype(BF16)
    logits = _dot(hi, rhi_ref[...]) + (_dot(hi, rlo_ref[...]) + _dot(lo, rhi_ref[...]))
    lane_i = _lane(logits.shape)
    lane = lane_i.astype(F32)
    logits = jnp.where(lane_i < N_EXPERTS, logits, -jnp.inf)
    v1 = jnp.max(logits, axis=-1, keepdims=True)
    i1 = jnp.min(jnp.where(logits == v1, lane, float(LANES)), axis=-1, keepdims=True)
    rest = jnp.where(lane == i1, -jnp.inf, logits)
    v2 = jnp.max(rest, axis=-1, keepdims=True)
    i2 = jnp.min(jnp.where(rest == v2, lane, float(LANES)), axis=-1, keepdims=True)
    e2 = jnp.exp(v2 - v1)
    den = 1.0 + e2
    ei_ref[...] = jnp.where(lane_i == 0, i1, jnp.where(lane_i == 1, i2, 0.0)).astype(jnp.int32)
    ew_ref[...] = jnp.where(lane_i == 0, 1.0 / den, jnp.where(lane_i == 1, e2 / den, 0.0))


def _l1_out(xa, mod, g, o, wo, rhi, rlo, *, tm):
    n, d = o.shape[0], xa.shape[1]
    row = lambda i: (i, 0)
    return pl.pallas_call(
        functools.partial(_l1_out_kernel, d=d),
        grid=(n // tm,),
        in_specs=[pl.BlockSpec((tm, d), row), _resident(mod.shape), _resident(g.shape),
                  pl.BlockSpec((tm, o.shape[1]), row), _resident(wo.shape), _resident(rhi.shape),
                  _resident(rlo.shape)],
        out_specs=[pl.BlockSpec((tm, d), row), pl.BlockSpec((tm, d // 2), row),
                   pl.BlockSpec((tm, LANES), row), pl.BlockSpec((tm, LANES), row)],
        out_shape=[jax.ShapeDtypeStruct((n, d), F32), jax.ShapeDtypeStruct((n, d // 2), jnp.uint32),
                   jax.ShapeDtypeStruct((n, LANES), jnp.int32), jax.ShapeDtypeStruct((n, LANES), F32)],
        compiler_params=_cparams(("parallel",)),
    )(xa, mod, g, o, wo, rhi, rlo)


def _moe_kernel(te_ref, nv_ref, x_ref, w1_ref, w3_ref, w2_ref, y_ref, acc_ref, *, nf):
    i, f = pl.program_id(0), pl.program_id(1)

    @pl.when(f == 0)
    def _zero():
        acc_ref[...] = jnp.zeros_like(acc_ref)

    @pl.when(i < nv_ref[0])
    def _compute():
        words = x_ref[...]
        lo = lax.bitcast_convert_type(words << 16, F32)
        hi = lax.bitcast_convert_type(words & jnp.uint32(0xFFFF0000), F32)
        x = jnp.concatenate([lo, hi], axis=1).astype(BF16)
        a = _dot(x, w1_ref[0])
        u = (_silu(a) * _dot(x, w3_ref[0])).astype(BF16)
        acc_ref[...] += _dot(u, w2_ref[0])

    @pl.when(f == nf - 1)
    def _store():
        y_ref[...] = acc_ref[...].astype(y_ref.dtype)


def _moe(tile_expert, n_valid, xs, w1, w3, w2, *, tm, tf):
    p, d = xs.shape[0], 2 * xs.shape[1]
    fdim = w1.shape[2]
    nf = fdim // tf
    fi = lambda i, f, te, nv: jnp.where(i < nv[0], f, nf - 1)
    grid_spec = pltpu.PrefetchScalarGridSpec(
        num_scalar_prefetch=2,
        grid=(p // tm, nf),
        in_specs=[pl.BlockSpec((tm, d // 2), lambda i, f, te, nv: (i, 0)),
                  pl.BlockSpec((1, d, tf), lambda i, f, te, nv: (te[i], 0, fi(i, f, te, nv))),
                  pl.BlockSpec((1, d, tf), lambda i, f, te, nv: (te[i], 0, fi(i, f, te, nv))),
                  pl.BlockSpec((1, tf, d), lambda i, f, te, nv: (te[i], fi(i, f, te, nv), 0))],
        out_specs=pl.BlockSpec((tm, d), lambda i, f, te, nv: (i, 0)),
        scratch_shapes=[pltpu.VMEM((tm, d), F32)],
    )
    return pl.pallas_call(
        functools.partial(_moe_kernel, nf=nf),
        grid_spec=grid_spec,
        out_shape=jax.ShapeDtypeStruct((p, d), BF16),
        compiler_params=_cparams(("arbitrary", "arbitrary")),
    )(tile_expert, n_valid, xs, w1, w3, w2)


SC_GATHER_WINDOW = 128
SC_LANES = 16


def _gather_rows(x, idx):
    n, d = idx.shape[0], x.shape[1]
    w = SC_GATHER_WINDOW
    assert n % w == 0
    mesh = plsc.VectorSubcoreMesh(core_axis_name="core", subcore_axis_name="subcore")

    @pl.kernel(out_type=jax.ShapeDtypeStruct((n, d), x.dtype), mesh=mesh,
               scratch_types=[pltpu.SemaphoreType.DMA])
    def gather_kernel(x_hbm, i_hbm, o_hbm, sem):
        def body(i_vmem, o_vmem):
            copies = []
            for k in range(w // SC_LANES):
                grp = pl.ds(k * SC_LANES, SC_LANES)
                copies.append(pltpu.async_copy(x_hbm.at[i_vmem[0, grp]], o_vmem.at[grp], sem))
            for cp in copies:
                cp.wait()

        pltpu.emit_pipeline(
            body,
            grid=(n // w,),
            in_specs=[pl.BlockSpec((1, w), lambda i: (0, i))],
            out_specs=[pl.BlockSpec((w, d), lambda i: (i, 0), pipeline_mode=pl.Buffered(1))],
            core_axis_name=("core", "subcore"),
            dimension_semantics=(pltpu.PARALLEL,),
        )(i_hbm, o_hbm)

    return gather_kernel(x, idx.reshape(1, n))


def _combine_kernel(x_ref, mod_ref, ew_ref, ya_ref, yb_ref, o_ref, *, d):
    ew = ew_ref[...]
    y = ew[:, 0:1] * ya_ref[...].astype(F32) + ew[:, 1:2] * yb_ref[...].astype(F32)
    o_ref[...] = x_ref[...] + mod_ref[0:1, 5 * d:6 * d] * y


def _combine(x3, mod, ew, ya, yb, *, tm):
    n, d = x3.shape
    row = lambda i: (i, 0)
    return pl.pallas_call(
        functools.partial(_combine_kernel, d=d),
        grid=(n // tm,),
        in_specs=[pl.BlockSpec((tm, d), row), _resident(mod.shape), pl.BlockSpec((tm, LANES), row),
                  pl.BlockSpec((tm, d), row), pl.BlockSpec((tm, d), row)],
        out_specs=pl.BlockSpec((tm, d), row),
        out_shape=jax.ShapeDtypeStruct((n, d), F32),
        compiler_params=_cparams(("parallel",)),
    )(x3, mod, ew, ya, yb)


def _deinterleave(width):
    return np.concatenate([np.arange(0, width, 2), np.arange(1, width, 2)])


def _rope_tables(n_x, n_ctx, rot_dim, seg_start, seg_repeat):
    rows = n_x // GRID_W
    row = jnp.broadcast_to(jnp.arange(rows)[:, None], (rows, GRID_W)).reshape(n_x).astype(F32)
    col = jnp.broadcast_to(jnp.arange(GRID_W)[None, :], (rows, GRID_W)).reshape(n_x).astype(F32)
    axis_dim = rot_dim // 2
    inv_freq = ROPE_THETA ** (-jnp.arange(0, axis_dim, 2, dtype=F32) / axis_dim)
    ang = jnp.concatenate([row[:, None] * inv_freq, col[:, None] * inv_freq], axis=-1)
    cos, sin = jnp.cos(ang), jnp.sin(ang)
    tail = LANES - seg_start - seg_repeat * rot_dim
    c = jnp.concatenate([jnp.ones((n_x, seg_start), F32)] + [cos, cos] * seg_repeat + [jnp.ones((n_x, tail), F32)],
                        axis=-1)
    s = jnp.concatenate([jnp.zeros((n_x, seg_start), F32)] + [-sin, sin] * seg_repeat
                        + [jnp.zeros((n_x, tail), F32)], axis=-1)
    c = jnp.concatenate([c, jnp.ones((n_ctx, LANES), F32)], axis=0)
    s = jnp.concatenate([s, jnp.zeros((n_ctx, LANES), F32)], axis=0)
    return c, s


def _segment_matrix(bounds):
    m = np.zeros((LANES, LANES), np.float32)
    for lo, hi in bounds:
        m[lo:hi, lo:hi] = 1.0 / (hi - lo)
    return jnp.asarray(m, BF16)


def _token_tile(t):
    for tm in (640, 512, 256, 128):
        if t % tm == 0:
            return tm
    raise ValueError(f"token count {t} has no supported tile")


def kernel(x, c, ctx, c_ctx, l0_ada_w, l0_ada_b, l0_norm1_g, l0_norm2_g, l0_w_in, l0_ret_log_decay, l0_q_norm_g, l0_k_norm_g, l0_w_out, l0_ffn_w1, l0_ffn_w3, l0_ffn_w2, l1_ada_w, l1_ada_b, l1_norm1_g, l1_norm2_g, l1_w_in, l1_q_lora_g, l1_kv_lora_g, l1_w_uq, l1_w_ukv, l1_q_nope_g, l1_q_rope_g, l1_k_nope_g, l1_k_rope_g, l1_w_out, l1_router, l1_exp_w1, l1_exp_w3, l1_exp_w2):
    b, n_x, d = x.shape
    n_ctx = ctx.shape[1]
    assert b == 1 and n_x % 256 == 0 and n_ctx % 256 == 0 and n_x % GRID_W == 0
    t = n_x + n_ctx
    tm = _token_tile(t)
    tmx = _token_tile(n_x)
    xa = jnp.concatenate([x[0], ctx[0]], axis=0)
    row1 = lambda v: v.reshape(1, -1).astype(F32)

    cvec = jnp.zeros((8, d), F32).at[0].set(c[0]).at[1].set(c_ctx)
    mod0 = _ada(cvec, l0_ada_w, l0_ada_b)
    mod1 = _ada(cvec, l1_ada_w, l1_ada_b)

    rw = RET_HEADS * HEAD
    perm = _deinterleave(HEAD)
    n_qk = GQA_HEADS + GQA_KV_HEADS
    qk_cols = l0_w_in[:, 4 * rw:4 * rw + n_qk * HEAD].reshape(d, n_qk, HEAD // 2, 2)
    qk_cols = jnp.swapaxes(qk_cols, 2, 3).reshape(d, n_qk * HEAD)
    w_in0 = jnp.concatenate([l0_w_in[:, :rw], l0_w_in[:, rw:2 * rw] * (HEAD ** -0.5), l0_w_in[:, 2 * rw:4 * rw],
                             qk_cols, l0_w_in[:, 4 * rw + n_qk * HEAD:]], axis=1).astype(BF16)
    seg64 = _segment_matrix([(0, HEAD), (HEAD, 2 * HEAD)])
    gq0 = row1(jnp.tile(l0_q_norm_g[perm], 2) * (HEAD ** -0.5 * LOG2E))
    gk0 = row1(jnp.tile(l0_k_norm_g[perm], 2))
    cos0, sin0 = _rope_tables(n_x, n_ctx, HEAD, 0, 2)

    rq, rk, rv, rg, gq, gkx, gvx = _l0_proj(xa, mod0, row1(l0_norm1_g), w_in0, seg64, gq0, gk0, cos0, sin0,
                                             tm=tm, n_x=n_x)
    o_f, o_b = _retention(l0_ret_log_decay.astype(F32), rq, rk, rv, n_x=n_x)

    gqa_maps = dict(q_maps=(lambda j: j, lambda j: j),
                    k_maps=(lambda j: 2 * (j // 2), lambda j: 2 * (j // 2) + 1),
                    v_maps=(lambda j: 2 * (j // 2), lambda j: 2 * (j // 2) + 1), n_pairs=GQA_HEADS // 2)
    bq = 512 if n_x % 512 == 0 else 256
    ao_x = _flash(gq, gkx, (gvx, gvx), n_q=n_x, q_row0=0, kv_row0=0, n_kv=t, bq=bq, bk=_kv_block(t), rs=32,
                  **gqa_maps)
    ao_c = _flash(gq, gkx, (gvx, gvx), n_q=n_ctx, q_row0=n_x, kv_row0=n_x, n_kv=n_ctx, bq=n_ctx,
                  bk=_kv_block(n_ctx), rs=32, **gqa_maps)
    ao = jnp.concatenate([ao_x, ao_c], axis=0)

    xa = _l0_out(xa, mod0, o_f, o_b, rg, ao, seg64, l0_w_out.astype(BF16), tm=tm, n_x=n_x)
    xa = _ffn(xa, mod0, row1(l0_norm2_g), l0_ffn_w1.astype(BF16), l0_ffn_w3.astype(BF16),
              l0_ffn_w2.astype(BF16), tm=tm, n_x=n_x)

    rperm = _deinterleave(MLA_ROPE)
    qk_w = MLA_NOPE + MLA_ROPE
    wuq = jnp.zeros((MLA_Q_RANK, MLA_HEADS * LANES), F32)
    wuk = jnp.zeros((MLA_KV_RANK, MLA_HEADS * LANES), F32)
    wuv = []
    for hd in range(MLA_HEADS):
        src = l1_w_uq[:, hd * qk_w:(hd + 1) * qk_w]
        wuq = wuq.at[:, hd * LANES:hd * LANES + MLA_NOPE].set(src[:, :MLA_NOPE])
        wuq = wuq.at[:, hd * LANES + MLA_NOPE:hd * LANES + qk_w].set(src[:, MLA_NOPE:][:, rperm])
        kvsrc = l1_w_ukv[:, hd * 2 * HEAD:(hd + 1) * 2 * HEAD]
        wuk = wuk.at[:, hd * LANES:hd * LANES + MLA_NOPE].set(kvsrc[:, :MLA_NOPE])
        wuv.append(kvsrc[:, MLA_NOPE:])
    wuv = jnp.concatenate(wuv, axis=1)
    wkr = jnp.zeros((d, LANES), F32).at[:, MLA_NOPE:qk_w].set(l1_w_in[:, MLA_Q_RANK + MLA_KV_RANK:][:, rperm])
    pad = jnp.zeros((LANES - qk_w,), F32)
    zn = jnp.zeros((MLA_NOPE,), F32)
    gq1 = row1(jnp.concatenate([l1_q_nope_g, l1_q_rope_g[rperm], pad]) * (qk_w ** -0.5 * LOG2E))
    gk1 = row1(jnp.concatenate([l1_k_nope_g, jnp.zeros((LANES - MLA_NOPE,), F32)]))
    gkr1 = row1(jnp.concatenate([zn, l1_k_rope_g[rperm], pad]))
    seg_mla = _segment_matrix([(0, MLA_NOPE), (MLA_NOPE, qk_w)])
    cos1, sin1 = _rope_tables(n_x, n_ctx, MLA_ROPE, MLA_NOPE, 1)

    mq, mk, mvlo, mvhi = _l1_proj(
        xa, mod1, row1(l1_norm1_g), l1_w_in[:, :MLA_Q_RANK].astype(BF16),
        l1_w_in[:, MLA_Q_RANK:MLA_Q_RANK + MLA_KV_RANK].astype(BF16), wkr.astype(BF16),
        row1(l1_q_lora_g), row1(l1_kv_lora_g), wuq.astype(BF16), wuk.astype(BF16), wuv.astype(BF16),
        seg_mla, gq1, gk1, gkr1, cos1, sin1, tm=tm, n_x=n_x)
    n_e, _, fdim = l1_exp_w1.shape
    mo, ew1, ew3, ew2 = _flash(
        mq, mk, (mvlo, mvhi), q_maps=(lambda j: 2 * j, lambda j: 2 * j + 1),
        k_maps=(lambda j: 2 * j, lambda j: 2 * j + 1), v_maps=(lambda j: j, lambda j: j),
        n_pairs=MLA_HEADS // 2, n_q=n_x, q_row0=0, kv_row0=0, n_kv=t, bq=bq, bk=_kv_block(t), rs=32,
        side=(l1_exp_w1.reshape(n_e * d, fdim), l1_exp_w3.reshape(n_e * d, fdim),
              l1_exp_w2.reshape(n_e * fdim, d)))
    ew1, ew3, ew2 = ew1.reshape(n_e, d, fdim), ew3.reshape(n_e, d, fdim), ew2.reshape(n_e, fdim, d)

    router = jnp.zeros((d, LANES), F32).at[:, :N_EXPERTS].set(l1_router)
    r_hi = router.astype(BF16)
    r_lo = (router - r_hi.astype(F32)).astype(BF16)
    x3, hmoe, ei, ew = _l1_out(xa, mod1, row1(l1_norm2_g), mo, l1_w_out.astype(BF16), r_hi, r_lo, tm=tmx)

    tme = 512 if n_x >= 4096 else 128
    e_flat = ei[:, :TOP_K].reshape(-1)
    onehot = (e_flat[:, None] == jnp.arange(N_EXPERTS)[None, :]).astype(jnp.int32)
    csum = jnp.cumsum(onehot, axis=0)
    rank = jnp.sum((csum - onehot) * onehot, axis=1)
    counts = csum[-1]
    padded = ((counts + tme - 1) // tme) * tme
    ends = jnp.cumsum(padded)
    pos = (ends - padded)[e_flat] + rank
    n_tiles = (TOP_K * n_x) // tme + N_EXPERTS
    p_rows = n_tiles * tme
    src = (jnp.arange(p_rows, dtype=jnp.int32) % n_x).at[pos].set(jnp.arange(TOP_K * n_x, dtype=jnp.int32) // TOP_K)
    n_valid = (ends[-1] // tme).astype(jnp.int32)
    tile_ids = jnp.minimum(jnp.arange(n_tiles, dtype=jnp.int32), n_valid - 1)
    tile_expert = jnp.sum((ends[None, :] <= (tile_ids * tme)[:, None]).astype(jnp.int32), axis=1)
    tile_expert = jnp.minimum(tile_expert, N_EXPERTS - 1)
    xs = _gather_rows(hmoe, src)
    tf = fdim // 2 if (fdim // 2) % LANES == 0 else fdim
    ys = _moe(tile_expert, n_valid.reshape(1), xs, ew1, ew3, ew2, tm=tme, tf=tf)
    pos2 = pos.reshape(n_x, TOP_K)
    out = _combine(x3, mod1, ew, jnp.take(ys, pos2[:, 0], axis=0, mode="clip"),
                   jnp.take(ys, pos2[:, 1], axis=0, mode="clip"), tm=tmx)
    return out[None]
```

```python
import functools
import math

import numpy as np
import jax
import jax.numpy as jnp
from jax import lax
from jax.experimental import pallas as pl
from jax.experimental.pallas import tpu as pltpu
from jax.experimental.pallas import tpu_sc as plsc

F32 = jnp.float32
BF16 = jnp.bfloat16

EPS = 1e-6
ROPE_THETA = 10000.0
GRID_W = 64
LANES = 128
HEAD = 64
RET_CHUNK = 128
RET_HEADS = 8
GQA_HEADS = 8
GQA_KV_HEADS = 2
MLA_HEADS = 8
MLA_Q_RANK = 384
MLA_KV_RANK = 256
MLA_NOPE = 64
MLA_ROPE = 32
N_EXPERTS = 8
TOP_K = 2
LOW_ONE = HEAD
HIGH_ONE = 0
LOG2E = math.log2(math.e)
VMEM_LIMIT = 56 * 1024 * 1024


def _cparams(sem, vmem=VMEM_LIMIT):
    return pltpu.CompilerParams(dimension_semantics=sem, vmem_limit_bytes=vmem)


def _resident(shape):
    nd = len(shape)
    return pl.BlockSpec(shape, lambda *_: (0,) * nd, pipeline_mode=pl.Buffered(1))


def _dot(a, b):
    return jnp.dot(a, b, preferred_element_type=F32)


def _dot_nt(a, b):
    return lax.dot_general(a, b, (((1,), (1,)), ((), ())), preferred_element_type=F32)


def _seg_mean(v, seg):
    hi = v.astype(BF16)
    lo = (v - hi.astype(F32)).astype(BF16)
    return _dot(hi, seg) + _dot(lo, seg)


def _silu(x):
    return x * jax.nn.sigmoid(x)


def _modulated(x, mod_ref, g_ref, which, tile, tm, n_x, d):
    ms = jnp.mean(x * x, axis=-1, keepdims=True)
    xn = x * lax.rsqrt(ms + EPS)
    g = g_ref[...]
    sh, sc = 3 * which, 3 * which + 1
    a_x = g * (1.0 + mod_ref[0:1, sc * d:(sc + 1) * d])
    a_c = g * (1.0 + mod_ref[1:2, sc * d:(sc + 1) * d])
    b_x = mod_ref[0:1, sh * d:(sh + 1) * d]
    b_c = mod_ref[1:2, sh * d:(sh + 1) * d]
    row = tile * tm + lax.broadcasted_iota(jnp.int32, (tm, 1), 0)
    is_ctx = row >= n_x
    return xn * jnp.where(is_ctx, a_c, a_x) + jnp.where(is_ctx, b_c, b_x)


def _row_gate(mod_ref, idx, tile, tm, n_x, d):
    row = tile * tm + lax.broadcasted_iota(jnp.int32, (tm, 1), 0)
    return jnp.where(row >= n_x, mod_ref[1:2, idx * d:(idx + 1) * d], mod_ref[0:1, idx * d:(idx + 1) * d])


def _lane(shape):
    return lax.broadcasted_iota(jnp.int32, shape, len(shape) - 1)


def _ada_kernel(c_ref, w_ref, b_ref, o_ref):
    c = c_ref[...]
    o_ref[...] = jnp.dot(_silu(c), w_ref[...], preferred_element_type=F32,
                         precision=lax.Precision.HIGHEST) + b_ref[...]


def _ada(cvec8, w, b):
    d, n = w.shape
    tn = n // 4
    return pl.pallas_call(
        _ada_kernel,
        grid=(n // tn,),
        in_specs=[pl.BlockSpec((8, d), lambda j: (0, 0)),
                  pl.BlockSpec((d, tn), lambda j: (0, j)),
                  pl.BlockSpec((1, tn), lambda j: (0, j))],
        out_specs=pl.BlockSpec((8, tn), lambda j: (0, j)),
        out_shape=jax.ShapeDtypeStruct((8, n), F32),
        compiler_params=_cparams(("arbitrary",)),
    )(cvec8, w, b.reshape(1, n))


def _rope128(v, c, s, half):
    lane = _lane(v.shape)
    swapped = jnp.where(lane % (2 * half) < half, pltpu.roll(v, LANES - half, 1), pltpu.roll(v, half, 1))
    return v * c + swapped * s


def _l0_proj_kernel(x_ref, mod_ref, g_ref, w_ref, seg_ref, gq_ref, gk_ref, c_ref, s_ref,
                    rq_ref, rk_ref, rv_ref, rg_ref, q_ref, k_ref, v_ref, *, tm, n_x, d):
    i = pl.program_id(0)
    h = _modulated(x_ref[...], mod_ref, g_ref, 0, i, tm, n_x, d).astype(BF16)
    rw = RET_HEADS * HEAD
    for idx, ref in enumerate((rq_ref, rk_ref, rv_ref, rg_ref)):
        ref[...] = _dot(h, w_ref[:, idx * rw:(idx + 1) * rw]).astype(BF16)
    seg = seg_ref[...]
    cos, sin = c_ref[...], s_ref[...]
    base = 4 * rw
    qw = GQA_HEADS * HEAD
    qa = _dot(h, w_ref[:, base:base + qw])
    for g in range(qw // LANES):
        v = qa[:, g * LANES:(g + 1) * LANES]
        vn = v * lax.rsqrt(_seg_mean(v * v, seg) + EPS) * gq_ref[...]
        q_ref[:, g * LANES:(g + 1) * LANES] = _rope128(vn, cos, sin, HEAD // 2).astype(BF16)
    kv = _dot(h, w_ref[:, base + qw:base + qw + 2 * LANES])
    kk = kv[:, :LANES]
    kk = kk * lax.rsqrt(_seg_mean(kk * kk, seg) + EPS) * gk_ref[...]
    kk = _rope128(kk, cos, sin, HEAD // 2)
    vv = kv[:, LANES:]
    lane = _lane(kk.shape)
    low = lane < HEAD
    for src, ref, one in ((kk, k_ref, 0.0), (vv, v_ref, 1.0)):
        sw = pltpu.roll(src, HEAD, 1)
        lo_fill = jnp.where(lane == LOW_ONE, one, 0.0)
        hi_fill = jnp.where(lane == HIGH_ONE, one, 0.0)
        ref[:, 0 * LANES:1 * LANES] = jnp.where(low, src, lo_fill).astype(BF16)
        ref[:, 1 * LANES:2 * LANES] = jnp.where(low, hi_fill, sw).astype(BF16)
        ref[:, 2 * LANES:3 * LANES] = jnp.where(low, sw, lo_fill).astype(BF16)
        ref[:, 3 * LANES:4 * LANES] = jnp.where(low, hi_fill, src).astype(BF16)


def _l0_proj(xa, mod, g, w, seg, gq, gk, cos, sin, *, tm, n_x):
    t, d = xa.shape
    rw = RET_HEADS * HEAD
    row = lambda i: (i, 0)
    outs = [jax.ShapeDtypeStruct((t, rw), BF16)] * 7
    return pl.pallas_call(
        functools.partial(_l0_proj_kernel, tm=tm, n_x=n_x, d=d),
        grid=(t // tm,),
        in_specs=[pl.BlockSpec((tm, d), row), _resident(mod.shape), _resident(g.shape), _resident(w.shape),
                  _resident(seg.shape), _resident(gq.shape), _resident(gk.shape),
                  pl.BlockSpec((tm, LANES), row), pl.BlockSpec((tm, LANES), row)],
        out_specs=[pl.BlockSpec((tm, rw), row)] * 7,
        out_shape=outs,
        compiler_params=_cparams(("parallel",)),
    )(xa, mod, g, w, seg, gq, gk, cos, sin)


def _retention_kernel(lg_ref, qf_ref, kf_ref, vf_ref, qb_ref, kb_ref, vb_ref, of_ref, ob_ref,
                      state_ref, decay_ref, xi_ref, zeta_ref, gl_ref):
    c = RET_CHUNK
    npairs = RET_HEADS * HEAD // LANES
    step = pl.program_id(0)

    @pl.when(step == 0)
    def _init():
        state_ref[...] = jnp.zeros_like(state_ref)
        ci = lax.broadcasted_iota(jnp.int32, (c, c), 0).astype(F32)
        mi = lax.broadcasted_iota(jnp.int32, (c, c), 1).astype(F32)
        pos = lax.broadcasted_iota(jnp.int32, (c, RET_HEADS * HEAD), 0).astype(F32)
        lane_head = _lane((1, RET_HEADS * HEAD)) // HEAD
        for dr in range(2):
            lgv = jnp.zeros((1, RET_HEADS * HEAD), F32)
            for hd in range(RET_HEADS):
                lg = lg_ref[dr, hd]
                rel = (ci - mi) if dr == 0 else (mi - ci)
                half = slice((hd % 2) * c, (hd % 2 + 1) * c)
                decay_ref[dr, hd // 2, :, half] = jnp.where(rel >= 0, jnp.exp(jnp.maximum(rel, 0.0) * lg), 0.0)
                lgv = jnp.where(lane_head == hd, lg, lgv)
            p = pos if dr == 0 else (c - 1.0 - pos)
            xi_ref[dr] = jnp.exp((p + 1.0) * lgv)
            zeta_ref[dr] = jnp.exp((c - 1.0 - p) * lgv)
            gl_ref[dr] = jnp.exp(float(c) * lgv)

    low = _lane((c, LANES)) < HEAD
    r_i = lax.broadcasted_iota(jnp.int32, (LANES, LANES), 0) // HEAD
    c_i = lax.broadcasted_iota(jnp.int32, (LANES, LANES), 1) // HEAD
    blockdiag = r_i == c_i
    for dr, (q_ref, k_ref, v_ref, o_ref) in enumerate(((qf_ref, kf_ref, vf_ref, of_ref),
                                                       (qb_ref, kb_ref, vb_ref, ob_ref))):
        for j in range(npairs):
            sl = slice(j * LANES, (j + 1) * LANES)
            q, k, v = q_ref[:, sl], k_ref[:, sl], v_ref[:, sl]
            zero = jnp.zeros_like(k)
            k2 = jnp.concatenate([jnp.where(low, k, zero), jnp.where(low, zero, k)], axis=0)
            v2 = jnp.concatenate([jnp.where(low, v, zero), jnp.where(low, zero, v)], axis=0)
            s = _dot_nt(q, k2) * decay_ref[dr, j]
            o = _dot(s.astype(BF16), v2)
            st = state_ref[dr, j]
            qx = (q.astype(F32) * xi_ref[dr, :, sl]).astype(BF16)
            o = o + _dot(qx, st.astype(BF16))
            o_ref[:, sl] = o.astype(BF16)
            kz = (k.astype(F32) * zeta_ref[dr, :, sl]).T.astype(BF16)
            u = _dot(kz, v)
            state_ref[dr, j] = st * gl_ref[dr, :, sl] + jnp.where(blockdiag, u, 0.0)


def _retention(lg, rq, rk, rv, *, n_x):
    t, w = rq.shape
    c = RET_CHUNK
    nc, ncx = t // c, n_x // c
    fwd = lambda i: ((i + ncx) % nc, 0)
    bwd = lambda i: (nc - 1 - i, 0)
    blk = lambda m: pl.BlockSpec((c, w), m)
    npairs = w // LANES
    return pl.pallas_call(
        _retention_kernel,
        grid=(nc,),
        in_specs=[pl.BlockSpec(memory_space=pltpu.SMEM)] + [blk(fwd)] * 3 + [blk(bwd)] * 3,
        out_specs=[blk(fwd), blk(bwd)],
        out_shape=[jax.ShapeDtypeStruct((t, w), BF16)] * 2,
        scratch_shapes=[pltpu.VMEM((2, npairs, LANES, LANES), F32),
                        pltpu.VMEM((2, npairs, c, 2 * c), F32),
                        pltpu.VMEM((2, c, w), F32), pltpu.VMEM((2, c, w), F32),
                        pltpu.VMEM((2, 1, w), F32)],
        compiler_params=_cparams(("arbitrary",)),
    )(lg, rq, rk, rv, rq, rk, rv)


def _flash_kernel(q0_ref, q1_ref, k0_ref, k1_ref, v0_ref, v1_ref, *rest, bk, nkv, rs, n_side):
    side_in, o_ref, side_out = rest[:n_side], rest[n_side], rest[n_side + 1:2 * n_side + 1]
    s_ref, p_ref, a_ref, m_ref, acc_ref = rest[2 * n_side + 1:]
    for src_ref, dst_ref in zip(side_in, side_out):
        dst_ref[...] = src_ref[...].astype(dst_ref.dtype)
    bq = q0_ref.shape[0]
    q_refs, k_refs, v_refs = (q0_ref, q1_ref), (k0_ref, k1_ref), (v0_ref, v1_ref)
    m_ref[...] = jnp.full(m_ref.shape, -jnp.inf, F32)
    acc_ref[...] = jnp.zeros(acc_ref.shape, F32)

    def keys(t):
        return pl.ds(t * bk if isinstance(t, int) else pl.multiple_of(t * bk, bk), bk)

    def scores(t, slot):
        for h in range(2):
            s_ref[slot, h] = _dot_nt(q_refs[h][...], k_refs[h][keys(t), :])

    def softmax(slot):
        col = lambda c: slice(c * LANES, (c + 1) * LANES)
        for r in range(bq // rs):
            rows = slice(r * rs, (r + 1) * rs)
            for h in range(2):
                mx = s_ref[slot, h, rows, col(0)]
                for c in range(1, bk // LANES):
                    mx = jnp.maximum(mx, s_ref[slot, h, rows, col(c)])
                m_old = m_ref[h, rows, :]
                m_new = jnp.maximum(m_old, jnp.max(mx, axis=-1, keepdims=True))
                a_ref[slot, h, rows, :] = jnp.exp2(m_old - m_new)
                m_ref[h, rows, :] = m_new
                for c in range(bk // LANES):
                    p_ref[slot, h, rows, col(c)] = jnp.exp2(s_ref[slot, h, rows, col(c)] - m_new).astype(BF16)

    def values(t, slot):
        for h in range(2):
            acc_ref[h] = acc_ref[h] * a_ref[slot, h] + _dot(p_ref[slot, h], v_refs[h][keys(t), :])

    scores(0, 0)

    def body(i, carry):
        t = 2 * i
        scores(t + 1, 1)
        softmax(0)
        values(t, 0)
        scores(t + 2, 0)
        softmax(1)
        values(t + 1, 1)
        return carry

    n_loop = (nkv - 1) // 2
    lax.fori_loop(0, n_loop, body, 0)
    last = 2 * n_loop
    if last + 1 < nkv:
        scores(last + 1, 1)
    softmax(0)
    values(last, 0)
    if last + 1 < nkv:
        softmax(1)
        values(last + 1, 1)
    low = _lane((bq, LANES)) < HEAD
    acc0, acc1 = acc_ref[0], acc_ref[1]
    out = jnp.where(low, acc0 / acc0[:, LOW_ONE:LOW_ONE + 1], acc1 / acc1[:, HIGH_ONE:HIGH_ONE + 1])
    o_ref[...] = out.astype(o_ref.dtype)


def _flash(q, kmat, vmat, *, q_maps, k_maps, v_maps, n_q, q_row0, kv_row0, n_kv, n_pairs, bq, bk, rs, side=()):
    assert q_row0 % bq == 0 and n_q % bq == 0 and n_kv % bk == 0 and kv_row0 % n_kv == 0 and bq % rs == 0
    qb0, kb0 = q_row0 // bq, kv_row0 // n_kv
    n_i = n_q // bq
    steps = n_pairs * n_i
    assert all(a.shape[0] % (16 * steps) == 0 for a in side)
    qspec = lambda m: pl.BlockSpec((bq, LANES), lambda j, i: (i + qb0, m(j)))
    kspec = lambda m: pl.BlockSpec((n_kv, LANES), lambda j, i: (kb0, m(j)), pipeline_mode=pl.Buffered(1))
    side_specs = [pl.BlockSpec((a.shape[0] // steps, a.shape[1]), lambda j, i: (j * n_i + i, 0)) for a in side]
    outs = pl.pallas_call(
        functools.partial(_flash_kernel, bk=bk, nkv=n_kv // bk, rs=rs, n_side=len(side)),
        grid=(n_pairs, n_i),
        in_specs=[qspec(q_maps[0]), qspec(q_maps[1]), kspec(k_maps[0]), kspec(k_maps[1]),
                  kspec(v_maps[0]), kspec(v_maps[1])] + side_specs,
        out_specs=[pl.BlockSpec((bq, LANES), lambda j, i: (i, j))] + side_specs,
        out_shape=[jax.ShapeDtypeStruct((n_q, n_pairs * LANES), BF16)]
                  + [jax.ShapeDtypeStruct(a.shape, BF16) for a in side],
        scratch_shapes=[pltpu.VMEM((2, 2, bq, bk), F32), pltpu.VMEM((2, 2, bq, bk), BF16),
                        pltpu.VMEM((2, 2, bq, LANES), F32), pltpu.VMEM((2, bq, LANES), F32),
                        pltpu.VMEM((2, bq, LANES), F32)],
        compiler_params=_cparams(("parallel", "parallel")),
    )(q, q, kmat, kmat, vmat[0], vmat[1], *side)
    return outs if side else outs[0]


def _kv_block(n_kv):
    for bk in (1280, 1024, 512, 256):
        if n_kv % bk == 0:
            return bk
    raise ValueError(f"key count {n_kv} has no supported block")


def _l0_out_kernel(x_ref, mod_ref, of_ref, ob_ref, rg_ref, ao_ref, seg_ref, wo_ref, o_ref, *, tm, n_x, d):
    i = pl.program_id(0)
    seg = seg_ref[...]
    rw = RET_HEADS * HEAD
    acc = _dot(ao_ref[...], wo_ref[rw:, :])
    for g in range(rw // LANES):
        sl = slice(g * LANES, (g + 1) * LANES)
        o = of_ref[:, sl].astype(F32) + ob_ref[:, sl].astype(F32)
        dv = o - _seg_mean(o, seg)
        nrm = dv * lax.rsqrt(_seg_mean(dv * dv, seg) + EPS)
        ra = (nrm * _silu(rg_ref[:, sl].astype(F32))).astype(BF16)
        acc = acc + _dot(ra, wo_ref[g * LANES:(g + 1) * LANES, :])
    o_ref[...] = x_ref[...] + _row_gate(mod_ref, 2, i, tm, n_x, d) * acc


def _l0_out(xa, mod, o_f, o_b, rg, ao, seg, wo, *, tm, n_x):
    t, d = xa.shape
    rw = o_f.shape[1]
    row = lambda i: (i, 0)
    return pl.pallas_call(
        functools.partial(_l0_out_kernel, tm=tm, n_x=n_x, d=d),
        grid=(t // tm,),
        in_specs=[pl.BlockSpec((tm, d), row), _resident(mod.shape)] + [pl.BlockSpec((tm, rw), row)] * 4
                 + [_resident(seg.shape), _resident(wo.shape)],
        out_specs=pl.BlockSpec((tm, d), row),
        out_shape=jax.ShapeDtypeStruct((t, d), F32),
        compiler_params=_cparams(("parallel",)),
    )(xa, mod, o_f, o_b, rg, ao, seg, wo)


def _ffn_kernel(x_ref, mod_ref, g_ref, w1_ref, w3_ref, w2_ref, o_ref, *, tm, n_x, d):
    i = pl.program_id(0)
    x = x_ref[...]
    h = _modulated(x, mod_ref, g_ref, 1, i, tm, n_x, d).astype(BF16)
    a = _dot(h, w1_ref[...])
    u = (_silu(a) * _dot(h, w3_ref[...])).astype(BF16)
    o_ref[...] = x + _row_gate(mod_ref, 5, i, tm, n_x, d) * _dot(u, w2_ref[...])


def _ffn(xa, mod, g, w1, w3, w2, *, tm, n_x):
    t, d = xa.shape
    row = lambda i: (i, 0)
    return pl.pallas_call(
        functools.partial(_ffn_kernel, tm=tm, n_x=n_x, d=d),
        grid=(t // tm,),
        in_specs=[pl.BlockSpec((tm, d), row), _resident(mod.shape), _resident(g.shape),
                  _resident(w1.shape), _resident(w3.shape), _resident(w2.shape)],
        out_specs=pl.BlockSpec((tm, d), row),
        out_shape=jax.ShapeDtypeStruct((t, d), F32),
        compiler_params=_cparams(("parallel",)),
    )(xa, mod, g, w1, w3, w2)


def _l1_proj_kernel(x_ref, mod_ref, g_ref, wq_ref, wkv_ref, wkr_ref, gql_ref, gkvl_ref, wuq_ref, wuk_ref,
                    wuv_ref, seg_ref, gq_ref, gk_ref, gkr_ref, c_ref, s_ref,
                    q_ref, k_ref, vlo_ref, vhi_ref, *, tm, n_x, d):
    i = pl.program_id(0)
    h = _modulated(x_ref[...], mod_ref, g_ref, 0, i, tm, n_x, d).astype(BF16)
    seg = seg_ref[...]
    cos, sin = c_ref[...], s_ref[...]

    def lora_norm(v, g):
        return (v * lax.rsqrt(jnp.mean(v * v, axis=-1, keepdims=True) + EPS) * g).astype(BF16)

    cq = lora_norm(_dot(h, wq_ref[...]), gql_ref[...])
    ckv = lora_norm(_dot(h, wkv_ref[...]), gkvl_ref[...])
    kr = _dot(h, wkr_ref[...])
    kr = kr * lax.rsqrt(_seg_mean(kr * kr, seg) + EPS) * gkr_ref[...]
    kr = _rope128(kr, cos, sin, MLA_ROPE // 2)
    qa = _dot(cq, wuq_ref[...])
    ka = _dot(ckv, wuk_ref[...])
    for hd in range(MLA_HEADS):
        sl = slice(hd * LANES, (hd + 1) * LANES)
        v = qa[:, sl]
        vn = v * lax.rsqrt(_seg_mean(v * v, seg) + EPS) * gq_ref[...]
        q_ref[:, sl] = _rope128(vn, cos, sin, MLA_ROPE // 2).astype(BF16)
        v = ka[:, sl]
        k_ref[:, sl] = (v * lax.rsqrt(_seg_mean(v * v, seg) + EPS) * gk_ref[...] + kr).astype(BF16)
    va = _dot(ckv, wuv_ref[...])
    lane = _lane(va.shape) % LANES
    low = lane < HEAD
    vlo_ref[...] = jnp.where(low, va, jnp.where(lane == LOW_ONE, 1.0, 0.0)).astype(BF16)
    vhi_ref[...] = jnp.where(low, jnp.where(lane == HIGH_ONE, 1.0, 0.0), va).astype(BF16)


def _l1_proj(xa, mod, g, wq, wkv, wkr, gql, gkvl, wuq, wuk, wuv, seg, gq, gk, gkr, cos, sin, *, tm, n_x):
    t, d = xa.shape
    row = lambda i: (i, 0)
    hw = MLA_HEADS * LANES
    vw = MLA_HEADS * HEAD
    consts = (mod, g, wq, wkv, wkr, gql, gkvl, wuq, wuk, wuv, seg, gq, gk, gkr)
    return pl.pallas_call(
        functools.partial(_l1_proj_kernel, tm=tm, n_x=n_x, d=d),
        grid=(t // tm,),
        in_specs=[pl.BlockSpec((tm, d), row)] + [_resident(a.shape) for a in consts]
                 + [pl.BlockSpec((tm, LANES), row)] * 2,
        out_specs=[pl.BlockSpec((tm, hw), row), pl.BlockSpec((tm, hw), row),
                   pl.BlockSpec((tm, vw), row), pl.BlockSpec((tm, vw), row)],
        out_shape=[jax.ShapeDtypeStruct((t, hw), BF16), jax.ShapeDtypeStruct((t, hw), BF16),
                   jax.ShapeDtypeStruct((t, vw), BF16), jax.ShapeDtypeStruct((t, vw), BF16)],
        compiler_params=_cparams(("parallel",)),
    )(xa, *consts, cos, sin)


def _l1_out_kernel(x_ref, mod_ref, g_ref, o_ref, wo_ref, rhi_ref, rlo_ref, x3_ref, h_ref, ei_ref, ew_ref, *, d):
    x3 = x_ref[...] + mod_ref[0:1, 2 * d:3 * d] * _dot(o_ref[...], wo_ref[...])
    x3_ref[...] = x3
    ms = jnp.mean(x3 * x3, axis=-1, keepdims=True)
    h = x3 * lax.rsqrt(ms + EPS) * (g_ref[...] * (1.0 + mod_ref[0:1, 4 * d:5 * d])) + mod_ref[0:1, 3 * d:4 * d]
    hi = h.astype(BF16)
    bits = lax.bitcast_convert_type(hi.astype(F32), jnp.uint32)
    words = (bits[:, :d // 2] >> 16) | (bits[:, d // 2:] & jnp.uint32(0xFFFF0000))
    h_ref[...] = words
    lo = (h - hi.astype(F32)).astype(BF16)
    logits = _dot(hi, rhi_ref[...]) + (_dot(hi, rlo_ref[...]) + _dot(lo, rhi_ref[...]))
    lane_i = _lane(logits.shape)
    lane = lane_i.astype(F32)
    logits = jnp.where(lane_i < N_EXPERTS, logits, -jnp.inf)
    v1 = jnp.max(logits, axis=-1, keepdims=True)
    i1 = jnp.min(jnp.where(logits == v1, lane, float(LANES)), axis=-1, keepdims=True)
    rest = jnp.where(lane == i1, -jnp.inf, logits)
    v2 = jnp.max(rest, axis=-1, keepdims=True)
    i2 = jnp.min(jnp.where(rest == v2, lane, float(LANES)), axis=-1, keepdims=True)
    e2 = jnp.exp(v2 - v1)
    den = 1.0 + e2
    ei_ref[...] = jnp.where(lane_i == 0, i1, jnp.where(lane_i == 1, i2, 0.0)).astype(jnp.int32)
    ew_ref[...] = jnp.where(lane_i == 0, 1.0 / den, jnp.where(lane_i == 1, e2 / den, 0.0))


def _l1_out(xa, mod, g, o, wo, rhi, rlo, *, tm):
    n, d = o.shape[0], xa.shape[1]
    row = lambda i: (i, 0)
    return pl.pallas_call(
        functools.partial(_l1_out_kernel, d=d),
        grid=(n // tm,),
        in_specs=[pl.BlockSpec((tm, d), row), _resident(mod.shape), _resident(g.shape),
                  pl.BlockSpec((tm, o.shape[1]), row), _resident(wo.shape), _resident(rhi.shape),
                  _resident(rlo.shape)],
        out_specs=[pl.BlockSpec((tm, d), row), pl.BlockSpec((tm, d // 2), row),
                   pl.BlockSpec((tm, LANES), row), pl.BlockSpec((tm, LANES), row)],
        out_shape=[jax.ShapeDtypeStruct((n, d), F32), jax.ShapeDtypeStruct((n, d // 2), jnp.uint32),
                   jax.ShapeDtypeStruct((n, LANES), jnp.int32), jax.ShapeDtypeStruct((n, LANES), F32)],
        compiler_params=_cparams(("parallel",)),
    )(xa, mod, g, o, wo, rhi, rlo)


def _moe_kernel(te_ref, nv_ref, x_ref, w1_ref, w3_ref, w2_ref, y_ref, acc_ref, *, nf):
    i, f = pl.program_id(0), pl.program_id(1)

    @pl.when(f == 0)
    def _zero():
        acc_ref[...] = jnp.zeros_like(acc_ref)

    @pl.when(i < nv_ref[0])
    def _compute():
        words = x_ref[...]
        lo = lax.bitcast_convert_type(words << 16, F32)
        hi = lax.bitcast_convert_type(words & jnp.uint32(0xFFFF0000), F32)
        x = jnp.concatenate([lo, hi], axis=1).astype(BF16)
        a = _dot(x, w1_ref[0])
        u = (_silu(a) * _dot(x, w3_ref[0])).astype(BF16)
        acc_ref[...] += _dot(u, w2_ref[0])

    @pl.when(f == nf - 1)
    def _store():
        y_ref[...] = acc_ref[...].astype(y_ref.dtype)


def _moe(tile_expert, n_valid, xs, w1, w3, w2, *, tm, tf):
    p, d = xs.shape[0], 2 * xs.shape[1]
    fdim = w1.shape[2]
    nf = fdim // tf
    fi = lambda i, f, te, nv: jnp.where(i < nv[0], f, nf - 1)
    grid_spec = pltpu.PrefetchScalarGridSpec(
        num_scalar_prefetch=2,
        grid=(p // tm, nf),
        in_specs=[pl.BlockSpec((tm, d // 2), lambda i, f, te, nv: (i, 0)),
                  pl.BlockSpec((1, d, tf), lambda i, f, te, nv: (te[i], 0, fi(i, f, te, nv))),
                  pl.BlockSpec((1, d, tf), lambda i, f, te, nv: (te[i], 0, fi(i, f, te, nv))),
                  pl.BlockSpec((1, tf, d), lambda i, f, te, nv: (te[i], fi(i, f, te, nv), 0))],
        out_specs=pl.BlockSpec((tm, d), lambda i, f, te, nv: (i, 0)),
        scratch_shapes=[pltpu.VMEM((tm, d), F32)],
    )
    return pl.pallas_call(
        functools.partial(_moe_kernel, nf=nf),
        grid_spec=grid_spec,
        out_shape=jax.ShapeDtypeStruct((p, d), BF16),
        compiler_params=_cparams(("arbitrary", "arbitrary")),
    )(tile_expert, n_valid, xs, w1, w3, w2)


SC_GATHER_WINDOW = 128
SC_LANES = 16


def _gather_rows(x, idx):
    n, d = idx.shape[0], x.shape[1]
    w = SC_GATHER_WINDOW
    assert n % w == 0
    mesh = plsc.VectorSubcoreMesh(core_axis_name="core", subcore_axis_name="subcore")

    @pl.kernel(out_type=jax.ShapeDtypeStruct((n, d), x.dtype), mesh=mesh,
               scratch_types=[pltpu.SemaphoreType.DMA])
    def gather_kernel(x_hbm, i_hbm, o_hbm, sem):
        def body(i_vmem, o_vmem):
            copies = []
            for k in range(w // SC_LANES):
                grp = pl.ds(k * SC_LANES, SC_LANES)
                copies.append(pltpu.async_copy(x_hbm.at[i_vmem[0, grp]], o_vmem.at[grp], sem))
            for cp in copies:
                cp.wait()

        pltpu.emit_pipeline(
            body,
            grid=(n // w,),
            in_specs=[pl.BlockSpec((1, w), lambda i: (0, i))],
            out_specs=[pl.BlockSpec((w, d), lambda i: (i, 0), pipeline_mode=pl.Buffered(1))],
            core_axis_name=("core", "subcore"),
            dimension_semantics=(pltpu.PARALLEL,),
        )(i_hbm, o_hbm)

    return gather_kernel(x, idx.reshape(1, n))


def _combine_kernel(x_ref, mod_ref, ew_ref, ya_ref, yb_ref, o_ref, *, d):
    ew = ew_ref[...]
    y = ew[:, 0:1] * ya_ref[...].astype(F32) + ew[:, 1:2] * yb_ref[...].astype(F32)
    o_ref[...] = x_ref[...] + mod_ref[0:1, 5 * d:6 * d] * y


def _combine(x3, mod, ew, ya, yb, *, tm):
    n, d = x3.shape
    row = lambda i: (i, 0)
    return pl.pallas_call(
        functools.partial(_combine_kernel, d=d),
        grid=(n // tm,),
        in_specs=[pl.BlockSpec((tm, d), row), _resident(mod.shape), pl.BlockSpec((tm, LANES), row),
                  pl.BlockSpec((tm, d), row), pl.BlockSpec((tm, d), row)],
        out_specs=pl.BlockSpec((tm, d), row),
        out_shape=jax.ShapeDtypeStruct((n, d), F32),
        compiler_params=_cparams(("parallel",)),
    )(x3, mod, ew, ya, yb)


def _deinterleave(width):
    return np.concatenate([np.arange(0, width, 2), np.arange(1, width, 2)])


def _rope_tables(n_x, n_ctx, rot_dim, seg_start, seg_repeat):
    rows = n_x // GRID_W
    row = jnp.broadcast_to(jnp.arange(rows)[:, None], (rows, GRID_W)).reshape(n_x).astype(F32)
    col = jnp.broadcast_to(jnp.arange(GRID_W)[None, :], (rows, GRID_W)).reshape(n_x).astype(F32)
    axis_dim = rot_dim // 2
    inv_freq = ROPE_THETA ** (-jnp.arange(0, axis_dim, 2, dtype=F32) / axis_dim)
    ang = jnp.concatenate([row[:, None] * inv_freq, col[:, None] * inv_freq], axis=-1)
    cos, sin = jnp.cos(ang), jnp.sin(ang)
    tail = LANES - seg_start - seg_repeat * rot_dim
    c = jnp.concatenate([jnp.ones((n_x, seg_start), F32)] + [cos, cos] * seg_repeat + [jnp.ones((n_x, tail), F32)],
                        axis=-1)
    s = jnp.concatenate([jnp.zeros((n_x, seg_start), F32)] + [-sin, sin] * seg_repeat
                        + [jnp.zeros((n_x, tail), F32)], axis=-1)
    c = jnp.concatenate([c, jnp.ones((n_ctx, LANES), F32)], axis=0)
    s = jnp.concatenate([s, jnp.zeros((n_ctx, LANES), F32)], axis=0)
    return c, s


def _segment_matrix(bounds):
    m = np.zeros((LANES, LANES), np.float32)
    for lo, hi in bounds:
        m[lo:hi, lo:hi] = 1.0 / (hi - lo)
    return jnp.asarray(m, BF16)


def _token_tile(t):
    for tm in (640, 512, 256, 128):
        if t % tm == 0:
            return tm
    raise ValueError(f"token count {t} has no supported tile")


def kernel(x, c, ctx, c_ctx, l0_ada_w, l0_ada_b, l0_norm1_g, l0_norm2_g, l0_w_in, l0_ret_log_decay, l0_q_norm_g, l0_k_norm_g, l0_w_out, l0_ffn_w1, l0_ffn_w3, l0_ffn_w2, l1_ada_w, l1_ada_b, l1_norm1_g, l1_norm2_g, l1_w_in, l1_q_lora_g, l1_kv_lora_g, l1_w_uq, l1_w_ukv, l1_q_nope_g, l1_q_rope_g, l1_k_nope_g, l1_k_rope_g, l1_w_out, l1_router, l1_exp_w1, l1_exp_w3, l1_exp_w2):
    b, n_x, d = x.shape
    n_ctx = ctx.shape[1]
    assert b == 1 and n_x % 256 == 0 and n_ctx % 256 == 0 and n_x % GRID_W == 0
    t = n_x + n_ctx
    tm = _token_tile(t)
    tmx = _token_tile(n_x)
    xa = jnp.concatenate([x[0], ctx[0]], axis=0)
    row1 = lambda v: v.reshape(1, -1).astype(F32)

    cvec = jnp.zeros((8, d), F32).at[0].set(c[0]).at[1].set(c_ctx)
    mod0 = _ada(cvec, l0_ada_w, l0_ada_b)
    mod1 = _ada(cvec, l1_ada_w, l1_ada_b)

    rw = RET_HEADS * HEAD
    perm = _deinterleave(HEAD)
    n_qk = GQA_HEADS + GQA_KV_HEADS
    qk_cols = l0_w_in[:, 4 * rw:4 * rw + n_qk * HEAD].reshape(d, n_qk, HEAD // 2, 2)
    qk_cols = jnp.swapaxes(qk_cols, 2, 3).reshape(d, n_qk * HEAD)
    w_in0 = jnp.concatenate([l0_w_in[:, :rw], l0_w_in[:, rw:2 * rw] * (HEAD ** -0.5), l0_w_in[:, 2 * rw:4 * rw],
                             qk_cols, l0_w_in[:, 4 * rw + n_qk * HEAD:]], axis=1).astype(BF16)
    seg64 = _segment_matrix([(0, HEAD), (HEAD, 2 * HEAD)])
    gq0 = row1(jnp.tile(l0_q_norm_g[perm], 2) * (HEAD ** -0.5 * LOG2E))
    gk0 = row1(jnp.tile(l0_k_norm_g[perm], 2))
    cos0, sin0 = _rope_tables(n_x, n_ctx, HEAD, 0, 2)

    rq, rk, rv, rg, gq, gkx, gvx = _l0_proj(xa, mod0, row1(l0_norm1_g), w_in0, seg64, gq0, gk0, cos0, sin0,
                                             tm=tm, n_x=n_x)
    o_f, o_b = _retention(l0_ret_log_decay.astype(F32), rq, rk, rv, n_x=n_x)

    gqa_maps = dict(q_maps=(lambda j: j, lambda j: j),
                    k_maps=(lambda j: 2 * (j // 2), lambda j: 2 * (j // 2) + 1),
                    v_maps=(lambda j: 2 * (j // 2), lambda j: 2 * (j // 2) + 1), n_pairs=GQA_HEADS // 2)
    bq = 512 if n_x % 512 == 0 else 256
    ao_x = _flash(gq, gkx, (gvx, gvx), n_q=n_x, q_row0=0, kv_row0=0, n_kv=t, bq=bq, bk=_kv_block(t), rs=32,
                  **gqa_maps)
    ao_c = _flash(gq, gkx, (gvx, gvx), n_q=n_ctx, q_row0=n_x, kv_row0=n_x, n_kv=n_ctx, bq=n_ctx,
                  bk=_kv_block(n_ctx), rs=32, **gqa_maps)
    ao = jnp.concatenate([ao_x, ao_c], axis=0)

    xa = _l0_out(xa, mod0, o_f, o_b, rg, ao, seg64, l0_w_out.astype(BF16), tm=tm, n_x=n_x)
    xa = _ffn(xa, mod0, row1(l0_norm2_g), l0_ffn_w1.astype(BF16), l0_ffn_w3.astype(BF16),
              l0_ffn_w2.astype(BF16), tm=tm, n_x=n_x)

    rperm = _deinterleave(MLA_ROPE)
    qk_w = MLA_NOPE + MLA_ROPE
    wuq = jnp.zeros((MLA_Q_RANK, MLA_HEADS * LANES), F32)
    wuk = jnp.zeros((MLA_KV_RANK, MLA_HEADS * LANES), F32)
    wuv = []
    for hd in range(MLA_HEADS):
        src = l1_w_uq[:, hd * qk_w:(hd + 1) * qk_w]
        wuq = wuq.at[:, hd * LANES:hd * LANES + MLA_NOPE].set(src[:, :MLA_NOPE])
        wuq = wuq.at[:, hd * LANES + MLA_NOPE:hd * LANES + qk_w].set(src[:, MLA_NOPE:][:, rperm])
        kvsrc = l1_w_ukv[:, hd * 2 * HEAD:(hd + 1) * 2 * HEAD]
        wuk = wuk.at[:, hd * LANES:hd * LANES + MLA_NOPE].set(kvsrc[:, :MLA_NOPE])
        wuv.append(kvsrc[:, MLA_NOPE:])
    wuv = jnp.concatenate(wuv, axis=1)
    wkr = jnp.zeros((d, LANES), F32).at[:, MLA_NOPE:qk_w].set(l1_w_in[:, MLA_Q_RANK + MLA_KV_RANK:][:, rperm])
    pad = jnp.zeros((LANES - qk_w,), F32)
    zn = jnp.zeros((MLA_NOPE,), F32)
    gq1 = row1(jnp.concatenate([l1_q_nope_g, l1_q_rope_g[rperm], pad]) * (qk_w ** -0.5 * LOG2E))
    gk1 = row1(jnp.concatenate([l1_k_nope_g, jnp.zeros((LANES - MLA_NOPE,), F32)]))
    gkr1 = row1(jnp.concatenate([zn, l1_k_rope_g[rperm], pad]))
    seg_mla = _segment_matrix([(0, MLA_NOPE), (MLA_NOPE, qk_w)])
    cos1, sin1 = _rope_tables(n_x, n_ctx, MLA_ROPE, MLA_NOPE, 1)

    mq, mk, mvlo, mvhi = _l1_proj(
        xa, mod1, row1(l1_norm1_g), l1_w_in[:, :MLA_Q_RANK].astype(BF16),
        l1_w_in[:, MLA_Q_RANK:MLA_Q_RANK + MLA_KV_RANK].astype(BF16), wkr.astype(BF16),
        row1(l1_q_lora_g), row1(l1_kv_lora_g), wuq.astype(BF16), wuk.astype(BF16), wuv.astype(BF16),
        seg_mla, gq1, gk1, gkr1, cos1, sin1, tm=tm, n_x=n_x)
    n_e, _, fdim = l1_exp_w1.shape
    mo, ew1, ew3, ew2 = _flash(
        mq, mk, (mvlo, mvhi), q_maps=(lambda j: 2 * j, lambda j: 2 * j + 1),
        k_maps=(lambda j: 2 * j, lambda j: 2 * j + 1), v_maps=(lambda j: j, lambda j: j),
        n_pairs=MLA_HEADS // 2, n_q=n_x, q_row0=0, kv_row0=0, n_kv=t, bq=bq, bk=_kv_block(t), rs=32,
        side=(l1_exp_w1.reshape(n_e * d, fdim), l1_exp_w3.reshape(n_e * d, fdim),
              l1_exp_w2.reshape(n_e * fdim, d)))
    ew1, ew3, ew2 = ew1.reshape(n_e, d, fdim), ew3.reshape(n_e, d, fdim), ew2.reshape(n_e, fdim, d)

    router = jnp.zeros((d, LANES), F32).at[:, :N_EXPERTS].set(l1_router)
    r_hi = router.astype(BF16)
    r_lo = (router - r_hi.astype(F32)).astype(BF16)
    x3, hmoe, ei, ew = _l1_out(xa, mod1, row1(l1_norm2_g), mo, l1_w_out.astype(BF16), r_hi, r_lo, tm=tmx)

    tme = 512 if n_x >= 4096 else 128
    e_flat = ei[:, :TOP_K].reshape(-1)
    onehot = (e_flat[:, None] == jnp.arange(N_EXPERTS)[None, :]).astype(jnp.int32)
    csum = jnp.cumsum(onehot, axis=0)
    rank = jnp.sum((csum - onehot) * onehot, axis=1)
    counts = csum[-1]
    padded = ((counts + tme - 1) // tme) * tme
    ends = jnp.cumsum(padded)
    pos = (ends - padded)[e_flat] + rank
    n_tiles = (TOP_K * n_x) // tme + N_EXPERTS
    p_rows = n_tiles * tme
    n_valid = (ends[-1] // tme).astype(jnp.int32)
    tile_ids = jnp.minimum(jnp.arange(n_tiles, dtype=jnp.int32), n_valid - 1)
    tile_expert = jnp.sum((ends[None, :] <= (tile_ids * tme)[:, None]).astype(jnp.int32), axis=1)
    tile_expert = jnp.minimum(tile_expert, N_EXPERTS - 1)
    by_expert = jnp.argsort(e_flat, stable=True).astype(jnp.int32) // TOP_K
    slot = jnp.arange(p_rows, dtype=jnp.int32)
    slot_e = jnp.repeat(tile_expert, tme)
    slot_rank = slot - (ends - padded)[slot_e]
    listed = jnp.take(by_expert, (jnp.cumsum(counts) - counts)[slot_e] + slot_rank, mode="clip")
    src = jnp.where((slot_rank < counts[slot_e]) & (slot < ends[-1]), listed, slot % n_x)
    xs = _gather_rows(hmoe, src)
    tf = fdim // 2 if (fdim // 2) % LANES == 0 else fdim
    ys = _moe(tile_expert, n_valid.reshape(1), xs, ew1, ew3, ew2, tm=tme, tf=tf)
    pos2 = pos.reshape(n_x, TOP_K)
    out = _combine(x3, mod1, ew, jnp.take(ys, pos2[:, 0], axis=0, mode="clip"),
                   jnp.take(ys, pos2[:, 1], axis=0, mode="clip"), tm=tmx)
    return out[None]
```

```python
import functools
import math

import numpy as np
import jax
import jax.numpy as jnp
from jax import lax
from jax.experimental import pallas as pl
from jax.experimental.pallas import tpu as pltpu
from jax.experimental.pallas import tpu_sc as plsc

F32 = jnp.float32
BF16 = jnp.bfloat16

EPS = 1e-6
ROPE_THETA = 10000.0
GRID_W = 64
LANES = 128
HEAD = 64
RET_CHUNK = 128
RET_STEP_CHUNKS = 2
RET_HEADS = 8
GQA_HEADS = 8
GQA_KV_HEADS = 2
MLA_HEADS = 8
MLA_Q_RANK = 384
MLA_KV_RANK = 256
MLA_NOPE = 64
MLA_ROPE = 32
N_EXPERTS = 8
TOP_K = 2
LOW_ONE = HEAD
HIGH_ONE = 0
LOG2E = math.log2(math.e)
VMEM_LIMIT = 56 * 1024 * 1024


def _cparams(sem, vmem=VMEM_LIMIT):
    return pltpu.CompilerParams(dimension_semantics=sem, vmem_limit_bytes=vmem)


def _resident(shape):
    nd = len(shape)
    return pl.BlockSpec(shape, lambda *_: (0,) * nd, pipeline_mode=pl.Buffered(1))


def _dot(a, b):
    return jnp.dot(a, b, preferred_element_type=F32)


def _dot_nt(a, b):
    return lax.dot_general(a, b, (((1,), (1,)), ((), ())), preferred_element_type=F32)


def _seg_mean(v, seg):
    hi = v.astype(BF16)
    lo = (v - hi.astype(F32)).astype(BF16)
    return _dot(hi, seg) + _dot(lo, seg)


def _silu(x):
    return x * jax.nn.sigmoid(x)


def _modulated(x, mod_ref, g_ref, which, tile, tm, n_x, d):
    ms = jnp.mean(x * x, axis=-1, keepdims=True)
    xn = x * lax.rsqrt(ms + EPS)
    g = g_ref[...]
    sh, sc = 3 * which, 3 * which + 1
    a_x = g * (1.0 + mod_ref[0:1, sc * d:(sc + 1) * d])
    a_c = g * (1.0 + mod_ref[1:2, sc * d:(sc + 1) * d])
    b_x = mod_ref[0:1, sh * d:(sh + 1) * d]
    b_c = mod_ref[1:2, sh * d:(sh + 1) * d]
    row = tile * tm + lax.broadcasted_iota(jnp.int32, (tm, 1), 0)
    is_ctx = row >= n_x
    return xn * jnp.where(is_ctx, a_c, a_x) + jnp.where(is_ctx, b_c, b_x)


def _row_gate(mod_ref, idx, tile, tm, n_x, d):
    row = tile * tm + lax.broadcasted_iota(jnp.int32, (tm, 1), 0)
    return jnp.where(row >= n_x, mod_ref[1:2, idx * d:(idx + 1) * d], mod_ref[0:1, idx * d:(idx + 1) * d])


def _lane(shape):
    return lax.broadcasted_iota(jnp.int32, shape, len(shape) - 1)


def _ada_kernel(c_ref, w_ref, b_ref, o_ref):
    c = c_ref[...]
    o_ref[...] = jnp.dot(_silu(c), w_ref[...], preferred_element_type=F32,
                         precision=lax.Precision.HIGHEST) + b_ref[...]


def _ada(cvec8, w, b):
    d, n = w.shape
    tn = n // 4
    return pl.pallas_call(
        _ada_kernel,
        grid=(n // tn,),
        in_specs=[pl.BlockSpec((8, d), lambda j: (0, 0)),
                  pl.BlockSpec((d, tn), lambda j: (0, j)),
                  pl.BlockSpec((1, tn), lambda j: (0, j))],
        out_specs=pl.BlockSpec((8, tn), lambda j: (0, j)),
        out_shape=jax.ShapeDtypeStruct((8, n), F32),
        compiler_params=_cparams(("arbitrary",)),
    )(cvec8, w, b.reshape(1, n))


def _rope128(v, c, s, half):
    lane = _lane(v.shape)
    swapped = jnp.where(lane % (2 * half) < half, pltpu.roll(v, LANES - half, 1), pltpu.roll(v, half, 1))
    return v * c + swapped * s


def _l0_proj_kernel(x_ref, mod_ref, g_ref, w_ref, seg_ref, gq_ref, gk_ref, c_ref, s_ref,
                    rq_ref, rk_ref, rv_ref, rg_ref, q_ref, k_ref, v_ref, *, tm, n_x, d):
    i = pl.program_id(0)
    h = _modulated(x_ref[...], mod_ref, g_ref, 0, i, tm, n_x, d).astype(BF16)
    rw = RET_HEADS * HEAD
    for idx, ref in enumerate((rq_ref, rk_ref, rv_ref, rg_ref)):
        ref[...] = _dot(h, w_ref[:, idx * rw:(idx + 1) * rw]).astype(BF16)
    seg = seg_ref[...]
    cos, sin = c_ref[...], s_ref[...]
    base = 4 * rw
    qw = GQA_HEADS * HEAD
    qa = _dot(h, w_ref[:, base:base + qw])
    for g in range(qw // LANES):
        v = qa[:, g * LANES:(g + 1) * LANES]
        vn = v * lax.rsqrt(_seg_mean(v * v, seg) + EPS) * gq_ref[...]
        q_ref[:, g * LANES:(g + 1) * LANES] = _rope128(vn, cos, sin, HEAD // 2).astype(BF16)
    kv = _dot(h, w_ref[:, base + qw:base + qw + 2 * LANES])
    kk = kv[:, :LANES]
    kk = kk * lax.rsqrt(_seg_mean(kk * kk, seg) + EPS) * gk_ref[...]
    kk = _rope128(kk, cos, sin, HEAD // 2)
    vv = kv[:, LANES:]
    lane = _lane(kk.shape)
    low = lane < HEAD
    for src, ref, one in ((kk, k_ref, 0.0), (vv, v_ref, 1.0)):
        sw = pltpu.roll(src, HEAD, 1)
        lo_fill = jnp.where(lane == LOW_ONE, one, 0.0)
        hi_fill = jnp.where(lane == HIGH_ONE, one, 0.0)
        ref[:, 0 * LANES:1 * LANES] = jnp.where(low, src, lo_fill).astype(BF16)
        ref[:, 1 * LANES:2 * LANES] = jnp.where(low, hi_fill, sw).astype(BF16)
        ref[:, 2 * LANES:3 * LANES] = jnp.where(low, sw, lo_fill).astype(BF16)
        ref[:, 3 * LANES:4 * LANES] = jnp.where(low, hi_fill, src).astype(BF16)


def _l0_proj(xa, mod, g, w, seg, gq, gk, cos, sin, *, tm, n_x):
    t, d = xa.shape
    rw = RET_HEADS * HEAD
    row = lambda i: (i, 0)
    outs = [jax.ShapeDtypeStruct((t, rw), BF16)] * 7
    return pl.pallas_call(
        functools.partial(_l0_proj_kernel, tm=tm, n_x=n_x, d=d),
        grid=(t // tm,),
        in_specs=[pl.BlockSpec((tm, d), row), _resident(mod.shape), _resident(g.shape), _resident(w.shape),
                  _resident(seg.shape), _resident(gq.shape), _resident(gk.shape),
                  pl.BlockSpec((tm, LANES), row), pl.BlockSpec((tm, LANES), row)],
        out_specs=[pl.BlockSpec((tm, rw), row)] * 7,
        out_shape=outs,
        compiler_params=_cparams(("parallel",)),
    )(xa, mod, g, w, seg, gq, gk, cos, sin)


def _retention_kernel(lg_ref, qf_ref, kf_ref, vf_ref, qb_ref, kb_ref, vb_ref, of_ref, ob_ref,
                      state_ref, decay_ref, xi_ref, zeta_ref, gl_ref):
    c = RET_CHUNK
    npairs = RET_HEADS * HEAD // LANES
    step = pl.program_id(0)

    @pl.when(step == 0)
    def _init():
        state_ref[...] = jnp.zeros_like(state_ref)
        ci = lax.broadcasted_iota(jnp.int32, (c, c), 0).astype(F32)
        mi = lax.broadcasted_iota(jnp.int32, (c, c), 1).astype(F32)
        pos = lax.broadcasted_iota(jnp.int32, (c, RET_HEADS * HEAD), 0).astype(F32)
        lane_head = _lane((1, RET_HEADS * HEAD)) // HEAD
        for dr in range(2):
            lgv = jnp.zeros((1, RET_HEADS * HEAD), F32)
            for hd in range(RET_HEADS):
                lg = lg_ref[dr, hd]
                rel = (ci - mi) if dr == 0 else (mi - ci)
                half = slice((hd % 2) * c, (hd % 2 + 1) * c)
                decay_ref[dr, hd // 2, :, half] = jnp.where(rel >= 0, jnp.exp(jnp.maximum(rel, 0.0) * lg), 0.0)
                lgv = jnp.where(lane_head == hd, lg, lgv)
            p = pos if dr == 0 else (c - 1.0 - pos)
            xi_ref[dr] = jnp.exp((p + 1.0) * lgv)
            zeta_ref[dr] = jnp.exp((c - 1.0 - p) * lgv)
            gl_ref[dr] = jnp.exp(float(c) * lgv)

    low = _lane((c, LANES)) < HEAD
    r_i = lax.broadcasted_iota(jnp.int32, (LANES, LANES), 0) // HEAD
    c_i = lax.broadcasted_iota(jnp.int32, (LANES, LANES), 1) // HEAD
    blockdiag = r_i == c_i
    n_sub = qf_ref.shape[0] // c
    for dr, (q_ref, k_ref, v_ref, o_ref) in enumerate(((qf_ref, kf_ref, vf_ref, of_ref),
                                                       (qb_ref, kb_ref, vb_ref, ob_ref))):
        for j in range(npairs):
            sl = slice(j * LANES, (j + 1) * LANES)
            st = state_ref[dr, j]
            for sub in (range(n_sub) if dr == 0 else reversed(range(n_sub))):
                rows = slice(sub * c, (sub + 1) * c)
                q, k, v = q_ref[rows, sl], k_ref[rows, sl], v_ref[rows, sl]
                zero = jnp.zeros_like(k)
                k2 = jnp.concatenate([jnp.where(low, k, zero), jnp.where(low, zero, k)], axis=0)
                v2 = jnp.concatenate([jnp.where(low, v, zero), jnp.where(low, zero, v)], axis=0)
                s = _dot_nt(q, k2) * decay_ref[dr, j]
                o = _dot(s.astype(BF16), v2)
                qx = (q.astype(F32) * xi_ref[dr, :, sl]).astype(BF16)
                o = o + _dot(qx, st.astype(BF16))
                o_ref[rows, sl] = o.astype(BF16)
                kz = (k.astype(F32) * zeta_ref[dr, :, sl]).T.astype(BF16)
                u = _dot(kz, v)
                st = st * gl_ref[dr, :, sl] + jnp.where(blockdiag, u, 0.0)
            state_ref[dr, j] = st


def _retention(lg, rq, rk, rv, *, n_x):
    t, w = rq.shape
    c = RET_CHUNK
    rows = RET_STEP_CHUNKS * c
    assert n_x % rows == 0 and t % rows == 0
    nc, ncx = t // rows, n_x // rows
    fwd = lambda i: ((i + ncx) % nc, 0)
    bwd = lambda i: (nc - 1 - i, 0)
    blk = lambda m: pl.BlockSpec((rows, w), m)
    npairs = w // LANES
    return pl.pallas_call(
        _retention_kernel,
        grid=(nc,),
        in_specs=[pl.BlockSpec(memory_space=pltpu.SMEM)] + [blk(fwd)] * 3 + [blk(bwd)] * 3,
        out_specs=[blk(fwd), blk(bwd)],
        out_shape=[jax.ShapeDtypeStruct((t, w), BF16)] * 2,
        scratch_shapes=[pltpu.VMEM((2, npairs, LANES, LANES), F32),
                        pltpu.VMEM((2, npairs, c, 2 * c), F32),
                        pltpu.VMEM((2, c, w), F32), pltpu.VMEM((2, c, w), F32),
                        pltpu.VMEM((2, 1, w), F32)],
        compiler_params=_cparams(("arbitrary",)),
    )(lg, rq, rk, rv, rq, rk, rv)


def _flash_kernel(q0_ref, q1_ref, k0_ref, k1_ref, v0_ref, v1_ref, *rest, bk, nkv, rs, n_side):
    side_in, o_ref, side_out = rest[:n_side], rest[n_side], rest[n_side + 1:2 * n_side + 1]
    s_ref, p_ref, a_ref, m_ref, acc_ref = rest[2 * n_side + 1:]
    for src_ref, dst_ref in zip(side_in, side_out):
        dst_ref[...] = src_ref[...].astype(dst_ref.dtype)
    bq = q0_ref.shape[0]
    q_refs, k_refs, v_refs = (q0_ref, q1_ref), (k0_ref, k1_ref), (v0_ref, v1_ref)
    m_ref[...] = jnp.full(m_ref.shape, -jnp.inf, F32)
    acc_ref[...] = jnp.zeros(acc_ref.shape, F32)

    def keys(t):
        return pl.ds(t * bk if isinstance(t, int) else pl.multiple_of(t * bk, bk), bk)

    def scores(t, slot):
        for h in range(2):
            s_ref[slot, h] = _dot_nt(q_refs[h][...], k_refs[h][keys(t), :])

    def softmax(slot):
        col = lambda c: slice(c * LANES, (c + 1) * LANES)
        for r in range(bq // rs):
            rows = slice(r * rs, (r + 1) * rs)
            for h in range(2):
                mx = s_ref[slot, h, rows, col(0)]
                for c in range(1, bk // LANES):
                    mx = jnp.maximum(mx, s_ref[slot, h, rows, col(c)])
                m_old = m_ref[h, rows, :]
                m_new = jnp.maximum(m_old, jnp.max(mx, axis=-1, keepdims=True))
                a_ref[slot, h, rows, :] = jnp.exp2(m_old - m_new)
                m_ref[h, rows, :] = m_new
                for c in range(bk // LANES):
                    p_ref[slot, h, rows, col(c)] = jnp.exp2(s_ref[slot, h, rows, col(c)] - m_new).astype(BF16)

    def values(t, slot):
        for h in range(2):
            acc_ref[h] = acc_ref[h] * a_ref[slot, h] + _dot(p_ref[slot, h], v_refs[h][keys(t), :])

    scores(0, 0)

    def body(i, carry):
        t = 2 * i
        scores(t + 1, 1)
        softmax(0)
        values(t, 0)
        scores(t + 2, 0)
        softmax(1)
        values(t + 1, 1)
        return carry

    n_loop = (nkv - 1) // 2
    lax.fori_loop(0, n_loop, body, 0)
    last = 2 * n_loop
    if last + 1 < nkv:
        scores(last + 1, 1)
    softmax(0)
    values(last, 0)
    if last + 1 < nkv:
        softmax(1)
        values(last + 1, 1)
    low = _lane((bq, LANES)) < HEAD
    acc0, acc1 = acc_ref[0], acc_ref[1]
    out = jnp.where(low, acc0 / acc0[:, LOW_ONE:LOW_ONE + 1], acc1 / acc1[:, HIGH_ONE:HIGH_ONE + 1])
    o_ref[...] = out.astype(o_ref.dtype)


def _flash(q, kmat, vmat, *, q_maps, k_maps, v_maps, n_q, q_row0, kv_row0, n_kv, n_pairs, bq, bk, rs, side=()):
    assert q_row0 % bq == 0 and n_q % bq == 0 and n_kv % bk == 0 and kv_row0 % n_kv == 0 and bq % rs == 0
    qb0, kb0 = q_row0 // bq, kv_row0 // n_kv
    n_i = n_q // bq
    steps = n_pairs * n_i
    assert all(a.shape[0] % (16 * steps) == 0 for a in side)
    qspec = lambda m: pl.BlockSpec((bq, LANES), lambda j, i: (i + qb0, m(j)))
    kspec = lambda m: pl.BlockSpec((n_kv, LANES), lambda j, i: (kb0, m(j)), pipeline_mode=pl.Buffered(1))
    side_specs = [pl.BlockSpec((a.shape[0] // steps, a.shape[1]), lambda j, i: (j * n_i + i, 0)) for a in side]
    outs = pl.pallas_call(
        functools.partial(_flash_kernel, bk=bk, nkv=n_kv // bk, rs=rs, n_side=len(side)),
        grid=(n_pairs, n_i),
        in_specs=[qspec(q_maps[0]), qspec(q_maps[1]), kspec(k_maps[0]), kspec(k_maps[1]),
                  kspec(v_maps[0]), kspec(v_maps[1])] + side_specs,
        out_specs=[pl.BlockSpec((bq, LANES), lambda j, i: (i, j))] + side_specs,
        out_shape=[jax.ShapeDtypeStruct((n_q, n_pairs * LANES), BF16)]
                  + [jax.ShapeDtypeStruct(a.shape, BF16) for a in side],
        scratch_shapes=[pltpu.VMEM((2, 2, bq, bk), F32), pltpu.VMEM((2, 2, bq, bk), BF16),
                        pltpu.VMEM((2, 2, bq, LANES), F32), pltpu.VMEM((2, bq, LANES), F32),
                        pltpu.VMEM((2, bq, LANES), F32)],
        compiler_params=_cparams(("parallel", "parallel")),
    )(q, q, kmat, kmat, vmat[0], vmat[1], *side)
    return outs if side else outs[0]


def _kv_block(n_kv):
    for bk in (1280, 1024, 512, 256):
        if n_kv % bk == 0:
            return bk
    raise ValueError(f"key count {n_kv} has no supported block")


def _l0_out_kernel(x_ref, mod_ref, of_ref, ob_ref, rg_ref, ao_ref, seg_ref, wo_ref, o_ref, *, tm, n_x, d):
    i = pl.program_id(0)
    seg = seg_ref[...]
    rw = RET_HEADS * HEAD
    acc = _dot(ao_ref[...], wo_ref[rw:, :])
    for g in range(rw // LANES):
        sl = slice(g * LANES, (g + 1) * LANES)
        o = of_ref[:, sl].astype(F32) + ob_ref[:, sl].astype(F32)
        dv = o - _seg_mean(o, seg)
        nrm = dv * lax.rsqrt(_seg_mean(dv * dv, seg) + EPS)
        ra = (nrm * _silu(rg_ref[:, sl].astype(F32))).astype(BF16)
        acc = acc + _dot(ra, wo_ref[g * LANES:(g + 1) * LANES, :])
    o_ref[...] = x_ref[...] + _row_gate(mod_ref, 2, i, tm, n_x, d) * acc


def _l0_out(xa, mod, o_f, o_b, rg, ao, seg, wo, *, tm, n_x):
    t, d = xa.shape
    rw = o_f.shape[1]
    row = lambda i: (i, 0)
    return pl.pallas_call(
        functools.partial(_l0_out_kernel, tm=tm, n_x=n_x, d=d),
        grid=(t // tm,),
        in_specs=[pl.BlockSpec((tm, d), row), _resident(mod.shape)] + [pl.BlockSpec((tm, rw), row)] * 4
                 + [_resident(seg.shape), _resident(wo.shape)],
        out_specs=pl.BlockSpec((tm, d), row),
        out_shape=jax.ShapeDtypeStruct((t, d), F32),
        compiler_params=_cparams(("parallel",)),
    )(xa, mod, o_f, o_b, rg, ao, seg, wo)


def _ffn_kernel(x_ref, mod_ref, g_ref, w1_ref, w3_ref, w2_ref, o_ref, *, tm, n_x, d):
    i = pl.program_id(0)
    x = x_ref[...]
    h = _modulated(x, mod_ref, g_ref, 1, i, tm, n_x, d).astype(BF16)
    a = _dot(h, w1_ref[...])
    u = (_silu(a) * _dot(h, w3_ref[...])).astype(BF16)
    o_ref[...] = x + _row_gate(mod_ref, 5, i, tm, n_x, d) * _dot(u, w2_ref[...])


def _ffn(xa, mod, g, w1, w3, w2, *, tm, n_x):
    t, d = xa.shape
    row = lambda i: (i, 0)
    return pl.pallas_call(
        functools.partial(_ffn_kernel, tm=tm, n_x=n_x, d=d),
        grid=(t // tm,),
        in_specs=[pl.BlockSpec((tm, d), row), _resident(mod.shape), _resident(g.shape),
                  _resident(w1.shape), _resident(w3.shape), _resident(w2.shape)],
        out_specs=pl.BlockSpec((tm, d), row),
        out_shape=jax.ShapeDtypeStruct((t, d), F32),
        compiler_params=_cparams(("parallel",)),
    )(xa, mod, g, w1, w3, w2)


def _l1_proj_kernel(x_ref, mod_ref, g_ref, wq_ref, wkv_ref, wkr_ref, gql_ref, gkvl_ref, wuq_ref, wuk_ref,
                    wuv_ref, seg_ref, gq_ref, gk_ref, gkr_ref, c_ref, s_ref,
                    q_ref, k_ref, vlo_ref, vhi_ref, *, tm, n_x, d):
    i = pl.program_id(0)
    h = _modulated(x_ref[...], mod_ref, g_ref, 0, i, tm, n_x, d).astype(BF16)
    seg = seg_ref[...]
    cos, sin = c_ref[...], s_ref[...]

    def lora_norm(v, g):
        return (v * lax.rsqrt(jnp.mean(v * v, axis=-1, keepdims=True) + EPS) * g).astype(BF16)

    cq = lora_norm(_dot(h, wq_ref[...]), gql_ref[...])
    ckv = lora_norm(_dot(h, wkv_ref[...]), gkvl_ref[...])
    kr = _dot(h, wkr_ref[...])
    kr = kr * lax.rsqrt(_seg_mean(kr * kr, seg) + EPS) * gkr_ref[...]
    kr = _rope128(kr, cos, sin, MLA_ROPE // 2)
    qa = _dot(cq, wuq_ref[...])
    ka = _dot(ckv, wuk_ref[...])
    for hd in range(MLA_HEADS):
        sl = slice(hd * LANES, (hd + 1) * LANES)
        v = qa[:, sl]
        vn = v * lax.rsqrt(_seg_mean(v * v, seg) + EPS) * gq_ref[...]
        q_ref[:, sl] = _rope128(vn, cos, sin, MLA_ROPE // 2).astype(BF16)
        v = ka[:, sl]
        k_ref[:, sl] = (v * lax.rsqrt(_seg_mean(v * v, seg) + EPS) * gk_ref[...] + kr).astype(BF16)
    va = _dot(ckv, wuv_ref[...])
    lane = _lane(va.shape) % LANES
    low = lane < HEAD
    vlo_ref[...] = jnp.where(low, va, jnp.where(lane == LOW_ONE, 1.0, 0.0)).astype(BF16)
    vhi_ref[...] = jnp.where(low, jnp.where(lane == HIGH_ONE, 1.0, 0.0), va).astype(BF16)


def _l1_proj(xa, mod, g, wq, wkv, wkr, gql, gkvl, wuq, wuk, wuv, seg, gq, gk, gkr, cos, sin, *, tm, n_x):
    t, d = xa.shape
    row = lambda i: (i, 0)
    hw = MLA_HEADS * LANES
    vw = MLA_HEADS * HEAD
    consts = (mod, g, wq, wkv, wkr, gql, gkvl, wuq, wuk, wuv, seg, gq, gk, gkr)
    return pl.pallas_call(
        functools.partial(_l1_proj_kernel, tm=tm, n_x=n_x, d=d),
        grid=(t // tm,),
        in_specs=[pl.BlockSpec((tm, d), row)] + [_resident(a.shape) for a in consts]
                 + [pl.BlockSpec((tm, LANES), row)] * 2,
        out_specs=[pl.BlockSpec((tm, hw), row), pl.BlockSpec((tm, hw), row),
                   pl.BlockSpec((tm, vw), row), pl.BlockSpec((tm, vw), row)],
        out_shape=[jax.ShapeDtypeStruct((t, hw), BF16), jax.ShapeDtypeStruct((t, hw), BF16),
                   jax.ShapeDtypeStruct((t, vw), BF16), jax.ShapeDtypeStruct((t, vw), BF16)],
        compiler_params=_cparams(("parallel",)),
    )(xa, *consts, cos, sin)


def _l1_out_kernel(x_ref, mod_ref, g_ref, o_ref, wo_ref, rhi_ref, rlo_ref, x3_ref, h_ref, ei_ref, ew_ref, *, d):
    x3 = x_ref[...] + mod_ref[0:1, 2 * d:3 * d] * _dot(o_ref[...], wo_ref[...])
    x3_ref[...] = x3
    ms = jnp.mean(x3 * x3, axis=-1, keepdims=True)
    h = x3 * lax.rsqrt(ms + EPS) * (g_ref[...] * (1.0 + mod_ref[0:1, 4 * d:5 * d])) + mod_ref[0:1, 3 * d:4 * d]
    hi = h.astype(BF16)
    bits = lax.bitcast_convert_type(hi.astype(F32), jnp.uint32)
    words = (bits[:, :d // 2] >> 16) | (bits[:, d // 2:] & jnp.uint32(0xFFFF0000))
    h_ref[...] = words
    lo = (h - hi.astype(F32)).astype(BF16)
    logits = _dot(hi, rhi_ref[...]) + (_dot(hi, rlo_ref[...]) + _dot(lo, rhi_ref[...]))
    lane_i = _lane(logits.shape)
    lane = lane_i.astype(F32)
    logits = jnp.where(lane_i < N_EXPERTS, logits, -jnp.inf)
    v1 = jnp.max(logits, axis=-1, keepdims=True)
    i1 = jnp.min(jnp.where(logits == v1, lane, float(LANES)), axis=-1, keepdims=True)
    rest = jnp.where(lane == i1, -jnp.inf, logits)
    v2 = jnp.max(rest, axis=-1, keepdims=True)
    i2 = jnp.min(jnp.where(rest == v2, lane, float(LANES)), axis=-1, keepdims=True)
    e2 = jnp.exp(v2 - v1)
    den = 1.0 + e2
    ei_ref[...] = jnp.where(lane_i == 0, i1, jnp.where(lane_i == 1, i2, 0.0)).astype(jnp.int32)
    ew_ref[...] = jnp.where(lane_i == 0, 1.0 / den, jnp.where(lane_i == 1, e2 / den, 0.0))


def _l1_out(xa, mod, g, o, wo, rhi, rlo, *, tm):
    n, d = o.shape[0], xa.shape[1]
    row = lambda i: (i, 0)
    return pl.pallas_call(
        functools.partial(_l1_out_kernel, d=d),
        grid=(n // tm,),
        in_specs=[pl.BlockSpec((tm, d), row), _resident(mod.shape), _resident(g.shape),
                  pl.BlockSpec((tm, o.shape[1]), row), _resident(wo.shape), _resident(rhi.shape),
                  _resident(rlo.shape)],
        out_specs=[pl.BlockSpec((tm, d), row), pl.BlockSpec((tm, d // 2), row),
                   pl.BlockSpec((tm, LANES), row), pl.BlockSpec((tm, LANES), row)],
        out_shape=[jax.ShapeDtypeStruct((n, d), F32), jax.ShapeDtypeStruct((n, d // 2), jnp.uint32),
                   jax.ShapeDtypeStruct((n, LANES), jnp.int32), jax.ShapeDtypeStruct((n, LANES), F32)],
        compiler_params=_cparams(("parallel",)),
    )(xa, mod, g, o, wo, rhi, rlo)


def _moe_kernel(te_ref, nv_ref, x_ref, w1_ref, w3_ref, w2_ref, y_ref, acc_ref, *, nf):
    i, f = pl.program_id(0), pl.program_id(1)

    @pl.when(f == 0)
    def _zero():
        acc_ref[...] = jnp.zeros_like(acc_ref)

    @pl.when(i < nv_ref[0])
    def _compute():
        words = x_ref[...]
        lo = lax.bitcast_convert_type(words << 16, F32)
        hi = lax.bitcast_convert_type(words & jnp.uint32(0xFFFF0000), F32)
        x = jnp.concatenate([lo, hi], axis=1).astype(BF16)
        a = _dot(x, w1_ref[0])
        u = (_silu(a) * _dot(x, w3_ref[0])).astype(BF16)
        acc_ref[...] += _dot(u, w2_ref[0])

    @pl.when(f == nf - 1)
    def _store():
        y_ref[...] = acc_ref[...].astype(y_ref.dtype)


def _moe(tile_expert, n_valid, xs, w1, w3, w2, *, tm, tf):
    p, d = xs.shape[0], 2 * xs.shape[1]
    fdim = w1.shape[2]
    nf = fdim // tf
    fi = lambda i, f, te, nv: jnp.where(i < nv[0], f, nf - 1)
    grid_spec = pltpu.PrefetchScalarGridSpec(
        num_scalar_prefetch=2,
        grid=(p // tm, nf),
        in_specs=[pl.BlockSpec((tm, d // 2), lambda i, f, te, nv: (i, 0)),
                  pl.BlockSpec((1, d, tf), lambda i, f, te, nv: (te[i], 0, fi(i, f, te, nv))),
                  pl.BlockSpec((1, d, tf), lambda i, f, te, nv: (te[i], 0, fi(i, f, te, nv))),
                  pl.BlockSpec((1, tf, d), lambda i, f, te, nv: (te[i], fi(i, f, te, nv), 0))],
        out_specs=pl.BlockSpec((tm, d), lambda i, f, te, nv: (i, 0)),
        scratch_shapes=[pltpu.VMEM((tm, d), F32)],
    )
    return pl.pallas_call(
        functools.partial(_moe_kernel, nf=nf),
        grid_spec=grid_spec,
        out_shape=jax.ShapeDtypeStruct((p, d), BF16),
        compiler_params=_cparams(("arbitrary", "arbitrary")),
    )(tile_expert, n_valid, xs, w1, w3, w2)


SC_GATHER_WINDOW = 128
SC_LANES = 16


def _gather_rows(x, idx):
    n, d = idx.shape[0], x.shape[1]
    w = SC_GATHER_WINDOW
    assert n % w == 0
    mesh = plsc.VectorSubcoreMesh(core_axis_name="core", subcore_axis_name="subcore")

    @pl.kernel(out_type=jax.ShapeDtypeStruct((n, d), x.dtype), mesh=mesh,
               scratch_types=[pltpu.SemaphoreType.DMA])
    def gather_kernel(x_hbm, i_hbm, o_hbm, sem):
        def body(i_vmem, o_vmem):
            copies = []
            for k in range(w // SC_LANES):
                grp = pl.ds(k * SC_LANES, SC_LANES)
                copies.append(pltpu.async_copy(x_hbm.at[i_vmem[0, grp]], o_vmem.at[grp], sem))
            for cp in copies:
                cp.wait()

        pltpu.emit_pipeline(
            body,
            grid=(n // w,),
            in_specs=[pl.BlockSpec((1, w), lambda i: (0, i))],
            out_specs=[pl.BlockSpec((w, d), lambda i: (i, 0), pipeline_mode=pl.Buffered(1))],
            core_axis_name=("core", "subcore"),
            dimension_semantics=(pltpu.PARALLEL,),
        )(i_hbm, o_hbm)

    return gather_kernel(x, idx.reshape(1, n))


def _combine_kernel(x_ref, mod_ref, ew_ref, ya_ref, yb_ref, o_ref, *, d):
    ew = ew_ref[...]
    y = ew[:, 0:1] * ya_ref[...].astype(F32) + ew[:, 1:2] * yb_ref[...].astype(F32)
    o_ref[...] = x_ref[...] + mod_ref[0:1, 5 * d:6 * d] * y


def _combine(x3, mod, ew, ya, yb, *, tm):
    n, d = x3.shape
    row = lambda i: (i, 0)
    return pl.pallas_call(
        functools.partial(_combine_kernel, d=d),
        grid=(n // tm,),
        in_specs=[pl.BlockSpec((tm, d), row), _resident(mod.shape), pl.BlockSpec((tm, LANES), row),
                  pl.BlockSpec((tm, d), row), pl.BlockSpec((tm, d), row)],
        out_specs=pl.BlockSpec((tm, d), row),
        out_shape=jax.ShapeDtypeStruct((n, d), F32),
        compiler_params=_cparams(("parallel",)),
    )(x3, mod, ew, ya, yb)


def _deinterleave(width):
    return np.concatenate([np.arange(0, width, 2), np.arange(1, width, 2)])


def _rope_tables(n_x, n_ctx, rot_dim, seg_start, seg_repeat):
    f32 = np.float32
    rows = n_x // GRID_W
    row = np.repeat(np.arange(rows, dtype=f32), GRID_W)
    col = np.tile(np.arange(GRID_W, dtype=f32), rows)
    axis_dim = rot_dim // 2
    inv_freq = f32(ROPE_THETA) ** (-np.arange(0, axis_dim, 2, dtype=f32) / f32(axis_dim))
    ang = np.concatenate([row[:, None] * inv_freq, col[:, None] * inv_freq], axis=-1)
    cos, sin = np.cos(ang).astype(f32), np.sin(ang).astype(f32)
    c = np.ones((n_x + n_ctx, LANES), f32)
    s = np.zeros((n_x + n_ctx, LANES), f32)
    for r in range(seg_repeat):
        lo = seg_start + r * rot_dim
        c[:n_x, lo:lo + rot_dim] = np.concatenate([cos, cos], axis=-1)
        s[:n_x, lo:lo + rot_dim] = np.concatenate([-sin, sin], axis=-1)
    return jnp.asarray(c), jnp.asarray(s)


def _segment_matrix(bounds):
    m = np.zeros((LANES, LANES), np.float32)
    for lo, hi in bounds:
        m[lo:hi, lo:hi] = 1.0 / (hi - lo)
    return jnp.asarray(m, BF16)


def _token_tile(t):
    for tm in (640, 512, 256, 128):
        if t % tm == 0:
            return tm
    raise ValueError(f"token count {t} has no supported tile")


def kernel(x, c, ctx, c_ctx, l0_ada_w, l0_ada_b, l0_norm1_g, l0_norm2_g, l0_w_in, l0_ret_log_decay, l0_q_norm_g, l0_k_norm_g, l0_w_out, l0_ffn_w1, l0_ffn_w3, l0_ffn_w2, l1_ada_w, l1_ada_b, l1_norm1_g, l1_norm2_g, l1_w_in, l1_q_lora_g, l1_kv_lora_g, l1_w_uq, l1_w_ukv, l1_q_nope_g, l1_q_rope_g, l1_k_nope_g, l1_k_rope_g, l1_w_out, l1_router, l1_exp_w1, l1_exp_w3, l1_exp_w2):
    b, n_x, d = x.shape
    n_ctx = ctx.shape[1]
    assert b == 1 and n_x % 256 == 0 and n_ctx % 256 == 0 and n_x % GRID_W == 0
    t = n_x + n_ctx
    tm = _token_tile(t)
    tmx = _token_tile(n_x)
    xa = jnp.concatenate([x[0], ctx[0]], axis=0)
    row1 = lambda v: v.reshape(1, -1).astype(F32)

    cvec = jnp.zeros((8, d), F32).at[0].set(c[0]).at[1].set(c_ctx)
    mod0 = _ada(cvec, l0_ada_w, l0_ada_b)
    mod1 = _ada(cvec, l1_ada_w, l1_ada_b)

    rw = RET_HEADS * HEAD
    perm = _deinterleave(HEAD)
    n_qk = GQA_HEADS + GQA_KV_HEADS
    qk_cols = l0_w_in[:, 4 * rw:4 * rw + n_qk * HEAD].reshape(d, n_qk, HEAD // 2, 2)
    qk_cols = jnp.swapaxes(qk_cols, 2, 3).reshape(d, n_qk * HEAD)
    w_in0 = jnp.concatenate([l0_w_in[:, :rw], l0_w_in[:, rw:2 * rw] * (HEAD ** -0.5), l0_w_in[:, 2 * rw:4 * rw],
                             qk_cols, l0_w_in[:, 4 * rw + n_qk * HEAD:]], axis=1).astype(BF16)
    seg64 = _segment_matrix([(0, HEAD), (HEAD, 2 * HEAD)])
    gq0 = row1(jnp.tile(l0_q_norm_g[perm], 2) * (HEAD ** -0.5 * LOG2E))
    gk0 = row1(jnp.tile(l0_k_norm_g[perm], 2))
    cos0, sin0 = _rope_tables(n_x, n_ctx, HEAD, 0, 2)

    rq, rk, rv, rg, gq, gkx, gvx = _l0_proj(xa, mod0, row1(l0_norm1_g), w_in0, seg64, gq0, gk0, cos0, sin0,
                                             tm=tm, n_x=n_x)
    o_f, o_b = _retention(l0_ret_log_decay.astype(F32), rq, rk, rv, n_x=n_x)

    gqa_maps = dict(q_maps=(lambda j: j, lambda j: j),
                    k_maps=(lambda j: 2 * (j // 2), lambda j: 2 * (j // 2) + 1),
                    v_maps=(lambda j: 2 * (j // 2), lambda j: 2 * (j // 2) + 1), n_pairs=GQA_HEADS // 2)
    bq = 512 if n_x % 512 == 0 else 256
    ao_x = _flash(gq, gkx, (gvx, gvx), n_q=n_x, q_row0=0, kv_row0=0, n_kv=t, bq=bq, bk=_kv_block(t), rs=32,
                  **gqa_maps)
    ao_c = _flash(gq, gkx, (gvx, gvx), n_q=n_ctx, q_row0=n_x, kv_row0=n_x, n_kv=n_ctx, bq=n_ctx,
                  bk=_kv_block(n_ctx), rs=32, **gqa_maps)
    ao = jnp.concatenate([ao_x, ao_c], axis=0)

    xa = _l0_out(xa, mod0, o_f, o_b, rg, ao, seg64, l0_w_out.astype(BF16), tm=tm, n_x=n_x)
    xa = _ffn(xa, mod0, row1(l0_norm2_g), l0_ffn_w1.astype(BF16), l0_ffn_w3.astype(BF16),
              l0_ffn_w2.astype(BF16), tm=tm, n_x=n_x)

    rperm = _deinterleave(MLA_ROPE)
    qk_w = MLA_NOPE + MLA_ROPE
    wuq = jnp.zeros((MLA_Q_RANK, MLA_HEADS * LANES), F32)
    wuk = jnp.zeros((MLA_KV_RANK, MLA_HEADS * LANES), F32)
    wuv = []
    for hd in range(MLA_HEADS):
        src = l1_w_uq[:, hd * qk_w:(hd + 1) * qk_w]
        wuq = wuq.at[:, hd * LANES:hd * LANES + MLA_NOPE].set(src[:, :MLA_NOPE])
        wuq = wuq.at[:, hd * LANES + MLA_NOPE:hd * LANES + qk_w].set(src[:, MLA_NOPE:][:, rperm])
        kvsrc = l1_w_ukv[:, hd * 2 * HEAD:(hd + 1) * 2 * HEAD]
        wuk = wuk.at[:, hd * LANES:hd * LANES + MLA_NOPE].set(kvsrc[:, :MLA_NOPE])
        wuv.append(kvsrc[:, MLA_NOPE:])
    wuv = jnp.concatenate(wuv, axis=1)
    wkr = jnp.zeros((d, LANES), F32).at[:, MLA_NOPE:qk_w].set(l1_w_in[:, MLA_Q_RANK + MLA_KV_RANK:][:, rperm])
    pad = jnp.zeros((LANES - qk_w,), F32)
    zn = jnp.zeros((MLA_NOPE,), F32)
    gq1 = row1(jnp.concatenate([l1_q_nope_g, l1_q_rope_g[rperm], pad]) * (qk_w ** -0.5 * LOG2E))
    gk1 = row1(jnp.concatenate([l1_k_nope_g, jnp.zeros((LANES - MLA_NOPE,), F32)]))
    gkr1 = row1(jnp.concatenate([zn, l1_k_rope_g[rperm], pad]))
    seg_mla = _segment_matrix([(0, MLA_NOPE), (MLA_NOPE, qk_w)])
    cos1, sin1 = _rope_tables(n_x, n_ctx, MLA_ROPE, MLA_NOPE, 1)

    mq, mk, mvlo, mvhi = _l1_proj(
        xa, mod1, row1(l1_norm1_g), l1_w_in[:, :MLA_Q_RANK].astype(BF16),
        l1_w_in[:, MLA_Q_RANK:MLA_Q_RANK + MLA_KV_RANK].astype(BF16), wkr.astype(BF16),
        row1(l1_q_lora_g), row1(l1_kv_lora_g), wuq.astype(BF16), wuk.astype(BF16), wuv.astype(BF16),
        seg_mla, gq1, gk1, gkr1, cos1, sin1, tm=tm, n_x=n_x)
    n_e, _, fdim = l1_exp_w1.shape
    mo, ew1, ew3, ew2 = _flash(
        mq, mk, (mvlo, mvhi), q_maps=(lambda j: 2 * j, lambda j: 2 * j + 1),
        k_maps=(lambda j: 2 * j, lambda j: 2 * j + 1), v_maps=(lambda j: j, lambda j: j),
        n_pairs=MLA_HEADS // 2, n_q=n_x, q_row0=0, kv_row0=0, n_kv=t, bq=bq, bk=_kv_block(t), rs=32,
        side=(l1_exp_w1.reshape(n_e * d, fdim), l1_exp_w3.reshape(n_e * d, fdim),
              l1_exp_w2.reshape(n_e * fdim, d)))
    ew1, ew3, ew2 = ew1.reshape(n_e, d, fdim), ew3.reshape(n_e, d, fdim), ew2.reshape(n_e, fdim, d)

    router = jnp.zeros((d, LANES), F32).at[:, :N_EXPERTS].set(l1_router)
    r_hi = router.astype(BF16)
    r_lo = (router - r_hi.astype(F32)).astype(BF16)
    x3, hmoe, ei, ew = _l1_out(xa, mod1, row1(l1_norm2_g), mo, l1_w_out.astype(BF16), r_hi, r_lo, tm=tmx)

    tme = 512 if n_x >= 4096 else 128
    e_flat = ei[:, :TOP_K].reshape(-1)
    onehot = (e_flat[:, None] == jnp.arange(N_EXPERTS)[None, :]).astype(jnp.int32)
    csum = jnp.cumsum(onehot, axis=0)
    rank = jnp.sum((csum - onehot) * onehot, axis=1)
    counts = csum[-1]
    padded = ((counts + tme - 1) // tme) * tme
    ends = jnp.cumsum(padded)
    pos = (ends - padded)[e_flat] + rank
    n_tiles = (TOP_K * n_x) // tme + N_EXPERTS
    p_rows = n_tiles * tme
    n_valid = (ends[-1] // tme).astype(jnp.int32)
    tile_ids = jnp.minimum(jnp.arange(n_tiles, dtype=jnp.int32), n_valid - 1)
    tile_expert = jnp.sum((ends[None, :] <= (tile_ids * tme)[:, None]).astype(jnp.int32), axis=1)
    tile_expert = jnp.minimum(tile_expert, N_EXPERTS - 1)
    by_expert = jnp.argsort(e_flat, stable=True).astype(jnp.int32) // TOP_K
    slot = jnp.arange(p_rows, dtype=jnp.int32)
    slot_e = jnp.repeat(tile_expert, tme)
    slot_rank = slot - (ends - padded)[slot_e]
    listed = jnp.take(by_expert, (jnp.cumsum(counts) - counts)[slot_e] + slot_rank, mode="clip")
    src = jnp.where((slot_rank < counts[slot_e]) & (slot < ends[-1]), listed, slot % n_x)
    xs = _gather_rows(hmoe, src)
    tf = fdim // 2 if (fdim // 2) % LANES == 0 else fdim
    ys = _moe(tile_expert, n_valid.reshape(1), xs, ew1, ew3, ew2, tm=tme, tf=tf)
    pos2 = pos.reshape(n_x, TOP_K)
    out = _combine(x3, mod1, ew, jnp.take(ys, pos2[:, 0], axis=0, mode="clip"),
                   jnp.take(ys, pos2[:, 1], axis=0, mode="clip"), tm=tmx)
    return out[None]
```

```python
import functools
import math

import numpy as np
import jax
import jax.numpy as jnp
from jax import lax
from jax.experimental import pallas as pl
from jax.experimental.pallas import tpu as pltpu
from jax.experimental.pallas import tpu_sc as plsc

F32 = jnp.float32
BF16 = jnp.bfloat16

EPS = 1e-6
ROPE_THETA = 10000.0
GRID_W = 64
LANES = 128
HEAD = 64
RET_CHUNK = 128
RET_STEP_CHUNKS = 2
RET_HEADS = 8
GQA_HEADS = 8
GQA_KV_HEADS = 2
MLA_HEADS = 8
MLA_Q_RANK = 384
MLA_KV_RANK = 256
MLA_NOPE = 64
MLA_ROPE = 32
N_EXPERTS = 8
TOP_K = 2
LOW_ONE = HEAD
HIGH_ONE = 0
LOG2E = math.log2(math.e)
VMEM_LIMIT = 56 * 1024 * 1024

TOKEN_TILES = (640, 512, 256, 128)
FLASH_Q_TILES = (512, 256)
FLASH_KEY_BLOCKS = (1280, 1024, 512, 256)
FLASH_ROW_GROUP = 32
ADA_COLUMN_TILES = 4
MOE_TILE_ROWS = 512
MOE_SMALL_TILE_ROWS = 128
MOE_SMALL_BELOW = 4096
MOE_F_BLOCKS = 2


def _first_divisor(n, candidates, what):
    for c in candidates:
        if n % c == 0:
            return c
    raise ValueError(f"{what} {n} has no supported tile among {candidates}")


def _cparams(sem, vmem=VMEM_LIMIT):
    return pltpu.CompilerParams(dimension_semantics=sem, vmem_limit_bytes=vmem)


def _resident(shape):
    nd = len(shape)
    return pl.BlockSpec(shape, lambda *_: (0,) * nd, pipeline_mode=pl.Buffered(1))


def _dot(a, b):
    return jnp.dot(a, b, preferred_element_type=F32)


def _dot_nt(a, b):
    return lax.dot_general(a, b, (((1,), (1,)), ((), ())), preferred_element_type=F32)


def _seg_mean(v, seg):
    hi = v.astype(BF16)
    lo = (v - hi.astype(F32)).astype(BF16)
    return _dot(hi, seg) + _dot(lo, seg)


def _silu(x):
    return x * jax.nn.sigmoid(x)


def _modulated(x, mod_ref, g_ref, which, tile, tm, n_x, d):
    ms = jnp.mean(x * x, axis=-1, keepdims=True)
    xn = x * lax.rsqrt(ms + EPS)
    g = g_ref[...]
    sh, sc = 3 * which, 3 * which + 1
    a_x = g * (1.0 + mod_ref[0:1, sc * d:(sc + 1) * d])
    a_c = g * (1.0 + mod_ref[1:2, sc * d:(sc + 1) * d])
    b_x = mod_ref[0:1, sh * d:(sh + 1) * d]
    b_c = mod_ref[1:2, sh * d:(sh + 1) * d]
    row = tile * tm + lax.broadcasted_iota(jnp.int32, (tm, 1), 0)
    is_ctx = row >= n_x
    return xn * jnp.where(is_ctx, a_c, a_x) + jnp.where(is_ctx, b_c, b_x)


def _row_gate(mod_ref, idx, tile, tm, n_x, d):
    row = tile * tm + lax.broadcasted_iota(jnp.int32, (tm, 1), 0)
    return jnp.where(row >= n_x, mod_ref[1:2, idx * d:(idx + 1) * d], mod_ref[0:1, idx * d:(idx + 1) * d])


def _lane(shape):
    return lax.broadcasted_iota(jnp.int32, shape, len(shape) - 1)


def _ada_kernel(c_ref, w_ref, b_ref, o_ref):
    c = c_ref[...]
    o_ref[...] = jnp.dot(_silu(c), w_ref[...], preferred_element_type=F32,
                         precision=lax.Precision.HIGHEST) + b_ref[...]


def _ada(cvec8, w, b):
    d, n = w.shape
    tn = n // ADA_COLUMN_TILES
    return pl.pallas_call(
        _ada_kernel,
        grid=(n // tn,),
        in_specs=[pl.BlockSpec((8, d), lambda j: (0, 0)),
                  pl.BlockSpec((d, tn), lambda j: (0, j)),
                  pl.BlockSpec((1, tn), lambda j: (0, j))],
        out_specs=pl.BlockSpec((8, tn), lambda j: (0, j)),
        out_shape=jax.ShapeDtypeStruct((8, n), F32),
        compiler_params=_cparams(("arbitrary",)),
    )(cvec8, w, b.reshape(1, n))


def _rope128(v, c, s, half):
    lane = _lane(v.shape)
    swapped = jnp.where(lane % (2 * half) < half, pltpu.roll(v, LANES - half, 1), pltpu.roll(v, half, 1))
    return v * c + swapped * s


def _l0_proj_kernel(x_ref, mod_ref, g_ref, w_ref, seg_ref, gq_ref, gk_ref, c_ref, s_ref,
                    rq_ref, rk_ref, rv_ref, rg_ref, q_ref, k_ref, v_ref, *, tm, n_x, d):
    i = pl.program_id(0)
    h = _modulated(x_ref[...], mod_ref, g_ref, 0, i, tm, n_x, d).astype(BF16)
    rw = RET_HEADS * HEAD
    for idx, ref in enumerate((rq_ref, rk_ref, rv_ref, rg_ref)):
        ref[...] = _dot(h, w_ref[:, idx * rw:(idx + 1) * rw]).astype(BF16)
    seg = seg_ref[...]
    cos, sin = c_ref[...], s_ref[...]
    base = 4 * rw
    qw = GQA_HEADS * HEAD
    qa = _dot(h, w_ref[:, base:base + qw])
    for g in range(qw // LANES):
        v = qa[:, g * LANES:(g + 1) * LANES]
        vn = v * lax.rsqrt(_seg_mean(v * v, seg) + EPS) * gq_ref[...]
        q_ref[:, g * LANES:(g + 1) * LANES] = _rope128(vn, cos, sin, HEAD // 2).astype(BF16)
    kv = _dot(h, w_ref[:, base + qw:base + qw + 2 * LANES])
    kk = kv[:, :LANES]
    kk = kk * lax.rsqrt(_seg_mean(kk * kk, seg) + EPS) * gk_ref[...]
    kk = _rope128(kk, cos, sin, HEAD // 2)
    vv = kv[:, LANES:]
    lane = _lane(kk.shape)
    low = lane < HEAD
    for src, ref, one in ((kk, k_ref, 0.0), (vv, v_ref, 1.0)):
        sw = pltpu.roll(src, HEAD, 1)
        lo_fill = jnp.where(lane == LOW_ONE, one, 0.0)
        hi_fill = jnp.where(lane == HIGH_ONE, one, 0.0)
        ref[:, 0 * LANES:1 * LANES] = jnp.where(low, src, lo_fill).astype(BF16)
        ref[:, 1 * LANES:2 * LANES] = jnp.where(low, hi_fill, sw).astype(BF16)
        ref[:, 2 * LANES:3 * LANES] = jnp.where(low, sw, lo_fill).astype(BF16)
        ref[:, 3 * LANES:4 * LANES] = jnp.where(low, hi_fill, src).astype(BF16)


def _l0_proj(xa, mod, g, w, seg, gq, gk, cos, sin, *, tm, n_x):
    t, d = xa.shape
    rw = RET_HEADS * HEAD
    row = lambda i: (i, 0)
    outs = [jax.ShapeDtypeStruct((t, rw), BF16)] * 7
    return pl.pallas_call(
        functools.partial(_l0_proj_kernel, tm=tm, n_x=n_x, d=d),
        grid=(t // tm,),
        in_specs=[pl.BlockSpec((tm, d), row), _resident(mod.shape), _resident(g.shape), _resident(w.shape),
                  _resident(seg.shape), _resident(gq.shape), _resident(gk.shape),
                  pl.BlockSpec((tm, LANES), row), pl.BlockSpec((tm, LANES), row)],
        out_specs=[pl.BlockSpec((tm, rw), row)] * 7,
        out_shape=outs,
        compiler_params=_cparams(("parallel",)),
    )(xa, mod, g, w, seg, gq, gk, cos, sin)


def _retention_kernel(lg_ref, qf_ref, kf_ref, vf_ref, qb_ref, kb_ref, vb_ref, of_ref, ob_ref,
                      state_ref, decay_ref, xi_ref, zeta_ref, gl_ref):
    c = RET_CHUNK
    npairs = RET_HEADS * HEAD // LANES
    step = pl.program_id(0)

    @pl.when(step == 0)
    def _init():
        state_ref[...] = jnp.zeros_like(state_ref)
        ci = lax.broadcasted_iota(jnp.int32, (c, c), 0).astype(F32)
        mi = lax.broadcasted_iota(jnp.int32, (c, c), 1).astype(F32)
        pos = lax.broadcasted_iota(jnp.int32, (c, RET_HEADS * HEAD), 0).astype(F32)
        lane_head = _lane((1, RET_HEADS * HEAD)) // HEAD
        for dr in range(2):
            lgv = jnp.zeros((1, RET_HEADS * HEAD), F32)
            for hd in range(RET_HEADS):
                lg = lg_ref[dr, hd]
                rel = (ci - mi) if dr == 0 else (mi - ci)
                half = slice((hd % 2) * c, (hd % 2 + 1) * c)
                decay_ref[dr, hd // 2, :, half] = jnp.where(rel >= 0, jnp.exp(jnp.maximum(rel, 0.0) * lg), 0.0)
                lgv = jnp.where(lane_head == hd, lg, lgv)
            p = pos if dr == 0 else (c - 1.0 - pos)
            xi_ref[dr] = jnp.exp((p + 1.0) * lgv)
            zeta_ref[dr] = jnp.exp((c - 1.0 - p) * lgv)
            gl_ref[dr] = jnp.exp(float(c) * lgv)

    low = _lane((c, LANES)) < HEAD
    r_i = lax.broadcasted_iota(jnp.int32, (LANES, LANES), 0) // HEAD
    c_i = lax.broadcasted_iota(jnp.int32, (LANES, LANES), 1) // HEAD
    blockdiag = r_i == c_i
    n_sub = qf_ref.shape[0] // c
    for dr, (q_ref, k_ref, v_ref, o_ref) in enumerate(((qf_ref, kf_ref, vf_ref, of_ref),
                                                       (qb_ref, kb_ref, vb_ref, ob_ref))):
        for j in range(npairs):
            sl = slice(j * LANES, (j + 1) * LANES)
            st = state_ref[dr, j]
            for sub in (range(n_sub) if dr == 0 else reversed(range(n_sub))):
                rows = slice(sub * c, (sub + 1) * c)
                q, k, v = q_ref[rows, sl], k_ref[rows, sl], v_ref[rows, sl]
                zero = jnp.zeros_like(k)
                k2 = jnp.concatenate([jnp.where(low, k, zero), jnp.where(low, zero, k)], axis=0)
                v2 = jnp.concatenate([jnp.where(low, v, zero), jnp.where(low, zero, v)], axis=0)
                s = _dot_nt(q, k2) * decay_ref[dr, j]
                o = _dot(s.astype(BF16), v2)
                qx = (q.astype(F32) * xi_ref[dr, :, sl]).astype(BF16)
                o = o + _dot(qx, st.astype(BF16))
                o_ref[rows, sl] = o.astype(BF16)
                kz = (k.astype(F32) * zeta_ref[dr, :, sl]).T.astype(BF16)
                u = _dot(kz, v)
                st = st * gl_ref[dr, :, sl] + jnp.where(blockdiag, u, 0.0)
            state_ref[dr, j] = st


def _retention(lg, rq, rk, rv, *, n_x):
    t, w = rq.shape
    c = RET_CHUNK
    rows = RET_STEP_CHUNKS * c
    assert n_x % rows == 0 and t % rows == 0
    nc, ncx = t // rows, n_x // rows
    fwd = lambda i: ((i + ncx) % nc, 0)
    bwd = lambda i: (nc - 1 - i, 0)
    blk = lambda m: pl.BlockSpec((rows, w), m)
    npairs = w // LANES
    return pl.pallas_call(
        _retention_kernel,
        grid=(nc,),
        in_specs=[pl.BlockSpec(memory_space=pltpu.SMEM)] + [blk(fwd)] * 3 + [blk(bwd)] * 3,
        out_specs=[blk(fwd), blk(bwd)],
        out_shape=[jax.ShapeDtypeStruct((t, w), BF16)] * 2,
        scratch_shapes=[pltpu.VMEM((2, npairs, LANES, LANES), F32),
                        pltpu.VMEM((2, npairs, c, 2 * c), F32),
                        pltpu.VMEM((2, c, w), F32), pltpu.VMEM((2, c, w), F32),
                        pltpu.VMEM((2, 1, w), F32)],
        compiler_params=_cparams(("arbitrary",)),
    )(lg, rq, rk, rv, rq, rk, rv)


def _flash_kernel(q0_ref, q1_ref, k0_ref, k1_ref, v0_ref, v1_ref, *rest, bk, nkv, rs, n_side):
    side_in, o_ref, side_out = rest[:n_side], rest[n_side], rest[n_side + 1:2 * n_side + 1]
    s_ref, p_ref, a_ref, m_ref, acc_ref = rest[2 * n_side + 1:]
    for src_ref, dst_ref in zip(side_in, side_out):
        dst_ref[...] = src_ref[...].astype(dst_ref.dtype)
    bq = q0_ref.shape[0]
    q_refs, k_refs, v_refs = (q0_ref, q1_ref), (k0_ref, k1_ref), (v0_ref, v1_ref)
    m_ref[...] = jnp.full(m_ref.shape, -jnp.inf, F32)
    acc_ref[...] = jnp.zeros(acc_ref.shape, F32)

    def keys(t):
        return pl.ds(t * bk if isinstance(t, int) else pl.multiple_of(t * bk, bk), bk)

    def scores(t, slot):
        for h in range(2):
            s_ref[slot, h] = _dot_nt(q_refs[h][...], k_refs[h][keys(t), :])

    def softmax(slot):
        col = lambda c: slice(c * LANES, (c + 1) * LANES)
        for r in range(bq // rs):
            rows = slice(r * rs, (r + 1) * rs)
            for h in range(2):
                mx = s_ref[slot, h, rows, col(0)]
                for c in range(1, bk // LANES):
                    mx = jnp.maximum(mx, s_ref[slot, h, rows, col(c)])
                m_old = m_ref[h, rows, :]
                m_new = jnp.maximum(m_old, jnp.max(mx, axis=-1, keepdims=True))
                a_ref[slot, h, rows, :] = jnp.exp2(m_old - m_new)
                m_ref[h, rows, :] = m_new
                for c in range(bk // LANES):
                    p_ref[slot, h, rows, col(c)] = jnp.exp2(s_ref[slot, h, rows, col(c)] - m_new).astype(BF16)

    def values(t, slot):
        for h in range(2):
            acc_ref[h] = acc_ref[h] * a_ref[slot, h] + _dot(p_ref[slot, h], v_refs[h][keys(t), :])

    scores(0, 0)

    def body(i, carry):
        t = 2 * i
        scores(t + 1, 1)
        softmax(0)
        values(t, 0)
        scores(t + 2, 0)
        softmax(1)
        values(t + 1, 1)
        return carry

    n_loop = (nkv - 1) // 2
    lax.fori_loop(0, n_loop, body, 0)
    last = 2 * n_loop
    if last + 1 < nkv:
        scores(last + 1, 1)
    softmax(0)
    values(last, 0)
    if last + 1 < nkv:
        softmax(1)
        values(last + 1, 1)
    low = _lane((bq, LANES)) < HEAD
    acc0, acc1 = acc_ref[0], acc_ref[1]
    out = jnp.where(low, acc0 / acc0[:, LOW_ONE:LOW_ONE + 1], acc1 / acc1[:, HIGH_ONE:HIGH_ONE + 1])
    o_ref[...] = out.astype(o_ref.dtype)


def _flash(q, kmat, vmat, *, q_maps, k_maps, v_maps, n_q, q_row0, kv_row0, n_kv, n_pairs, bq, bk, rs, side=()):
    assert q_row0 % bq == 0 and n_q % bq == 0 and n_kv % bk == 0 and kv_row0 % n_kv == 0 and bq % rs == 0
    qb0, kb0 = q_row0 // bq, kv_row0 // n_kv
    n_i = n_q // bq
    steps = n_pairs * n_i
    assert all(a.shape[0] % (16 * steps) == 0 for a in side)
    qspec = lambda m: pl.BlockSpec((bq, LANES), lambda j, i: (i + qb0, m(j)))
    kspec = lambda m: pl.BlockSpec((n_kv, LANES), lambda j, i: (kb0, m(j)), pipeline_mode=pl.Buffered(1))
    side_specs = [pl.BlockSpec((a.shape[0] // steps, a.shape[1]), lambda j, i: (j * n_i + i, 0)) for a in side]
    outs = pl.pallas_call(
        functools.partial(_flash_kernel, bk=bk, nkv=n_kv // bk, rs=rs, n_side=len(side)),
        grid=(n_pairs, n_i),
        in_specs=[qspec(q_maps[0]), qspec(q_maps[1]), kspec(k_maps[0]), kspec(k_maps[1]),
                  kspec(v_maps[0]), kspec(v_maps[1])] + side_specs,
        out_specs=[pl.BlockSpec((bq, LANES), lambda j, i: (i, j))] + side_specs,
        out_shape=[jax.ShapeDtypeStruct((n_q, n_pairs * LANES), BF16)]
                  + [jax.ShapeDtypeStruct(a.shape, BF16) for a in side],
        scratch_shapes=[pltpu.VMEM((2, 2, bq, bk), F32), pltpu.VMEM((2, 2, bq, bk), BF16),
                        pltpu.VMEM((2, 2, bq, LANES), F32), pltpu.VMEM((2, bq, LANES), F32),
                        pltpu.VMEM((2, bq, LANES), F32)],
        compiler_params=_cparams(("parallel", "parallel")),
    )(q, q, kmat, kmat, vmat[0], vmat[1], *side)
    return outs if side else outs[0]


def _kv_block(n_kv):
    return _first_divisor(n_kv, FLASH_KEY_BLOCKS, "key count")


def _l0_out_kernel(x_ref, mod_ref, of_ref, ob_ref, rg_ref, ao_ref, seg_ref, wo_ref, o_ref, *, tm, n_x, d):
    i = pl.program_id(0)
    seg = seg_ref[...]
    rw = RET_HEADS * HEAD
    acc = _dot(ao_ref[...], wo_ref[rw:, :])
    for g in range(rw // LANES):
        sl = slice(g * LANES, (g + 1) * LANES)
        o = of_ref[:, sl].astype(F32) + ob_ref[:, sl].astype(F32)
        dv = o - _seg_mean(o, seg)
        nrm = dv * lax.rsqrt(_seg_mean(dv * dv, seg) + EPS)
        ra = (nrm * _silu(rg_ref[:, sl].astype(F32))).astype(BF16)
        acc = acc + _dot(ra, wo_ref[g * LANES:(g + 1) * LANES, :])
    o_ref[...] = x_ref[...] + _row_gate(mod_ref, 2, i, tm, n_x, d) * acc


def _l0_out(xa, mod, o_f, o_b, rg, ao, seg, wo, *, tm, n_x):
    t, d = xa.shape
    rw = o_f.shape[1]
    row = lambda i: (i, 0)
    return pl.pallas_call(
        functools.partial(_l0_out_kernel, tm=tm, n_x=n_x, d=d),
        grid=(t // tm,),
        in_specs=[pl.BlockSpec((tm, d), row), _resident(mod.shape)] + [pl.BlockSpec((tm, rw), row)] * 4
                 + [_resident(seg.shape), _resident(wo.shape)],
        out_specs=pl.BlockSpec((tm, d), row),
        out_shape=jax.ShapeDtypeStruct((t, d), F32),
        compiler_params=_cparams(("parallel",)),
    )(xa, mod, o_f, o_b, rg, ao, seg, wo)


def _ffn_kernel(x_ref, mod_ref, g_ref, w1_ref, w3_ref, w2_ref, o_ref, *, tm, n_x, d):
    i = pl.program_id(0)
    x = x_ref[...]
    h = _modulated(x, mod_ref, g_ref, 1, i, tm, n_x, d).astype(BF16)
    a = _dot(h, w1_ref[...])
    u = (_silu(a) * _dot(h, w3_ref[...])).astype(BF16)
    o_ref[...] = x + _row_gate(mod_ref, 5, i, tm, n_x, d) * _dot(u, w2_ref[...])


def _ffn(xa, mod, g, w1, w3, w2, *, tm, n_x):
    t, d = xa.shape
    row = lambda i: (i, 0)
    return pl.pallas_call(
        functools.partial(_ffn_kernel, tm=tm, n_x=n_x, d=d),
        grid=(t // tm,),
        in_specs=[pl.BlockSpec((tm, d), row), _resident(mod.shape), _resident(g.shape),
                  _resident(w1.shape), _resident(w3.shape), _resident(w2.shape)],
        out_specs=pl.BlockSpec((tm, d), row),
        out_shape=jax.ShapeDtypeStruct((t, d), F32),
        compiler_params=_cparams(("parallel",)),
    )(xa, mod, g, w1, w3, w2)


def _l1_proj_kernel(x_ref, mod_ref, g_ref, wq_ref, wkv_ref, wkr_ref, gql_ref, gkvl_ref, wuq_ref, wuk_ref,
                    wuv_ref, seg_ref, gq_ref, gk_ref, gkr_ref, c_ref, s_ref,
                    q_ref, k_ref, vlo_ref, vhi_ref, *, tm, n_x, d):
    i = pl.program_id(0)
    h = _modulated(x_ref[...], mod_ref, g_ref, 0, i, tm, n_x, d).astype(BF16)
    seg = seg_ref[...]
    cos, sin = c_ref[...], s_ref[...]

    def lora_norm(v, g):
        return (v * lax.rsqrt(jnp.mean(v * v, axis=-1, keepdims=True) + EPS) * g).astype(BF16)

    cq = lora_norm(_dot(h, wq_ref[...]), gql_ref[...])
    ckv = lora_norm(_dot(h, wkv_ref[...]), gkvl_ref[...])
    kr = _dot(h, wkr_ref[...])
    kr = kr * lax.rsqrt(_seg_mean(kr * kr, seg) + EPS) * gkr_ref[...]
    kr = _rope128(kr, cos, sin, MLA_ROPE // 2)
    qa = _dot(cq, wuq_ref[...])
    ka = _dot(ckv, wuk_ref[...])
    for hd in range(MLA_HEADS):
        sl = slice(hd * LANES, (hd + 1) * LANES)
        v = qa[:, sl]
        vn = v * lax.rsqrt(_seg_mean(v * v, seg) + EPS) * gq_ref[...]
        q_ref[:, sl] = _rope128(vn, cos, sin, MLA_ROPE // 2).astype(BF16)
        v = ka[:, sl]
        k_ref[:, sl] = (v * lax.rsqrt(_seg_mean(v * v, seg) + EPS) * gk_ref[...] + kr).astype(BF16)
    va = _dot(ckv, wuv_ref[...])
    lane = _lane(va.shape) % LANES
    low = lane < HEAD
    vlo_ref[...] = jnp.where(low, va, jnp.where(lane == LOW_ONE, 1.0, 0.0)).astype(BF16)
    vhi_ref[...] = jnp.where(low, jnp.where(lane == HIGH_ONE, 1.0, 0.0), va).astype(BF16)


def _l1_proj(xa, mod, g, wq, wkv, wkr, gql, gkvl, wuq, wuk, wuv, seg, gq, gk, gkr, cos, sin, *, tm, n_x):
    t, d = xa.shape
    row = lambda i: (i, 0)
    hw = MLA_HEADS * LANES
    vw = MLA_HEADS * HEAD
    consts = (mod, g, wq, wkv, wkr, gql, gkvl, wuq, wuk, wuv, seg, gq, gk, gkr)
    return pl.pallas_call(
        functools.partial(_l1_proj_kernel, tm=tm, n_x=n_x, d=d),
        grid=(t // tm,),
        in_specs=[pl.BlockSpec((tm, d), row)] + [_resident(a.shape) for a in consts]
                 + [pl.BlockSpec((tm, LANES), row)] * 2,
        out_specs=[pl.BlockSpec((tm, hw), row), pl.BlockSpec((tm, hw), row),
                   pl.BlockSpec((tm, vw), row), pl.BlockSpec((tm, vw), row)],
        out_shape=[jax.ShapeDtypeStruct((t, hw), BF16), jax.ShapeDtypeStruct((t, hw), BF16),
                   jax.ShapeDtypeStruct((t, vw), BF16), jax.ShapeDtypeStruct((t, vw), BF16)],
        compiler_params=_cparams(("parallel",)),
    )(xa, *consts, cos, sin)


def _l1_out_kernel(x_ref, mod_ref, g_ref, o_ref, wo_ref, rhi_ref, rlo_ref, x3_ref, h_ref, ei_ref, ew_ref, *, d):
    x3 = x_ref[...] + mod_ref[0:1, 2 * d:3 * d] * _dot(o_ref[...], wo_ref[...])
    x3_ref[...] = x3
    ms = jnp.mean(x3 * x3, axis=-1, keepdims=True)
    h = x3 * lax.rsqrt(ms + EPS) * (g_ref[...] * (1.0 + mod_ref[0:1, 4 * d:5 * d])) + mod_ref[0:1, 3 * d:4 * d]
    hi = h.astype(BF16)
    bits = lax.bitcast_convert_type(hi.astype(F32), jnp.uint32)
    words = (bits[:, :d // 2] >> 16) | (bits[:, d // 2:] & jnp.uint32(0xFFFF0000))
    h_ref[...] = words
    lo = (h - hi.astype(F32)).astype(BF16)
    logits = _dot(hi, rhi_ref[...]) + (_dot(hi, rlo_ref[...]) + _dot(lo, rhi_ref[...]))
    lane_i = _lane(logits.shape)
    lane = lane_i.astype(F32)
    logits = jnp.where(lane_i < N_EXPERTS, logits, -jnp.inf)
    v1 = jnp.max(logits, axis=-1, keepdims=True)
    i1 = jnp.min(jnp.where(logits == v1, lane, float(LANES)), axis=-1, keepdims=True)
    rest = jnp.where(lane == i1, -jnp.inf, logits)
    v2 = jnp.max(rest, axis=-1, keepdims=True)
    i2 = jnp.min(jnp.where(rest == v2, lane, float(LANES)), axis=-1, keepdims=True)
    e2 = jnp.exp(v2 - v1)
    den = 1.0 + e2
    ei_ref[...] = jnp.where(lane_i == 0, i1, jnp.where(lane_i == 1, i2, 0.0)).astype(jnp.int32)
    ew_ref[...] = jnp.where(lane_i == 0, 1.0 / den, jnp.where(lane_i == 1, e2 / den, 0.0))


def _l1_out(xa, mod, g, o, wo, rhi, rlo, *, tm):
    n, d = o.shape[0], xa.shape[1]
    row = lambda i: (i, 0)
    return pl.pallas_call(
        functools.partial(_l1_out_kernel, d=d),
        grid=(n // tm,),
        in_specs=[pl.BlockSpec((tm, d), row), _resident(mod.shape), _resident(g.shape),
                  pl.BlockSpec((tm, o.shape[1]), row), _resident(wo.shape), _resident(rhi.shape),
                  _resident(rlo.shape)],
        out_specs=[pl.BlockSpec((tm, d), row), pl.BlockSpec((tm, d // 2), row),
                   pl.BlockSpec((tm, LANES), row), pl.BlockSpec((tm, LANES), row)],
        out_shape=[jax.ShapeDtypeStruct((n, d), F32), jax.ShapeDtypeStruct((n, d // 2), jnp.uint32),
                   jax.ShapeDtypeStruct((n, LANES), jnp.int32), jax.ShapeDtypeStruct((n, LANES), F32)],
        compiler_params=_cparams(("parallel",)),
    )(xa, mod, g, o, wo, rhi, rlo)


def _moe_kernel(te_ref, nv_ref, x_ref, w1_ref, w3_ref, w2_ref, y_ref, acc_ref, *, nf):
    i, f = pl.program_id(0), pl.program_id(1)

    @pl.when(f == 0)
    def _zero():
        acc_ref[...] = jnp.zeros_like(acc_ref)

    @pl.when(i < nv_ref[0])
    def _compute():
        words = x_ref[...]
        lo = lax.bitcast_convert_type(words << 16, F32)
        hi = lax.bitcast_convert_type(words & jnp.uint32(0xFFFF0000), F32)
        x = jnp.concatenate([lo, hi], axis=1).astype(BF16)
        a = _dot(x, w1_ref[0])
        u = (_silu(a) * _dot(x, w3_ref[0])).astype(BF16)
        acc_ref[...] += _dot(u, w2_ref[0])

    @pl.when(f == nf - 1)
    def _store():
        y_ref[...] = acc_ref[...].astype(y_ref.dtype)


def _moe(tile_expert, n_valid, xs, w1, w3, w2, *, tm, tf):
    p, d = xs.shape[0], 2 * xs.shape[1]
    fdim = w1.shape[2]
    nf = fdim // tf
    fi = lambda i, f, te, nv: jnp.where(i < nv[0], f, nf - 1)
    grid_spec = pltpu.PrefetchScalarGridSpec(
        num_scalar_prefetch=2,
        grid=(p // tm, nf),
        in_specs=[pl.BlockSpec((tm, d // 2), lambda i, f, te, nv: (i, 0)),
                  pl.BlockSpec((1, d, tf), lambda i, f, te, nv: (te[i], 0, fi(i, f, te, nv))),
                  pl.BlockSpec((1, d, tf), lambda i, f, te, nv: (te[i], 0, fi(i, f, te, nv))),
                  pl.BlockSpec((1, tf, d), lambda i, f, te, nv: (te[i], fi(i, f, te, nv), 0))],
        out_specs=pl.BlockSpec((tm, d), lambda i, f, te, nv: (i, 0)),
        scratch_shapes=[pltpu.VMEM((tm, d), F32)],
    )
    return pl.pallas_call(
        functools.partial(_moe_kernel, nf=nf),
        grid_spec=grid_spec,
        out_shape=jax.ShapeDtypeStruct((p, d), BF16),
        compiler_params=_cparams(("arbitrary", "arbitrary")),
    )(tile_expert, n_valid, xs, w1, w3, w2)


SC_GATHER_WINDOW = 128
SC_LANES = 16


def _gather_rows(x, idx):
    n, d = idx.shape[0], x.shape[1]
    w = SC_GATHER_WINDOW
    assert n % w == 0
    mesh = plsc.VectorSubcoreMesh(core_axis_name="core", subcore_axis_name="subcore")

    @pl.kernel(out_type=jax.ShapeDtypeStruct((n, d), x.dtype), mesh=mesh,
               scratch_types=[pltpu.SemaphoreType.DMA])
    def gather_kernel(x_hbm, i_hbm, o_hbm, sem):
        def body(i_vmem, o_vmem):
            copies = []
            for k in range(w // SC_LANES):
                grp = pl.ds(k * SC_LANES, SC_LANES)
                copies.append(pltpu.async_copy(x_hbm.at[i_vmem[0, grp]], o_vmem.at[grp], sem))
            for cp in copies:
                cp.wait()

        pltpu.emit_pipeline(
            body,
            grid=(n // w,),
            in_specs=[pl.BlockSpec((1, w), lambda i: (0, i))],
            out_specs=[pl.BlockSpec((w, d), lambda i: (i, 0), pipeline_mode=pl.Buffered(1))],
            core_axis_name=("core", "subcore"),
            dimension_semantics=(pltpu.PARALLEL,),
        )(i_hbm, o_hbm)

    return gather_kernel(x, idx.reshape(1, n))


def _combine_kernel(x_ref, mod_ref, ew_ref, ya_ref, yb_ref, o_ref, *, d):
    ew = ew_ref[...]
    y = ew[:, 0:1] * ya_ref[...].astype(F32) + ew[:, 1:2] * yb_ref[...].astype(F32)
    o_ref[...] = x_ref[...] + mod_ref[0:1, 5 * d:6 * d] * y


def _combine(x3, mod, ew, ya, yb, *, tm):
    n, d = x3.shape
    row = lambda i: (i, 0)
    return pl.pallas_call(
        functools.partial(_combine_kernel, d=d),
        grid=(n // tm,),
        in_specs=[pl.BlockSpec((tm, d), row), _resident(mod.shape), pl.BlockSpec((tm, LANES), row),
                  pl.BlockSpec((tm, d), row), pl.BlockSpec((tm, d), row)],
        out_specs=pl.BlockSpec((tm, d), row),
        out_shape=jax.ShapeDtypeStruct((n, d), F32),
        compiler_params=_cparams(("parallel",)),
    )(x3, mod, ew, ya, yb)


def _deinterleave(width):
    return np.concatenate([np.arange(0, width, 2), np.arange(1, width, 2)])


def _rope_tables(n_x, n_ctx, rot_dim, seg_start, seg_repeat):
    f32 = np.float32
    rows = n_x // GRID_W
    row = np.repeat(np.arange(rows, dtype=f32), GRID_W)
    col = np.tile(np.arange(GRID_W, dtype=f32), rows)
    axis_dim = rot_dim // 2
    inv_freq = f32(ROPE_THETA) ** (-np.arange(0, axis_dim, 2, dtype=f32) / f32(axis_dim))
    ang = np.concatenate([row[:, None] * inv_freq, col[:, None] * inv_freq], axis=-1)
    cos, sin = np.cos(ang).astype(f32), np.sin(ang).astype(f32)
    c = np.ones((n_x + n_ctx, LANES), f32)
    s = np.zeros((n_x + n_ctx, LANES), f32)
    for r in range(seg_repeat):
        lo = seg_start + r * rot_dim
        c[:n_x, lo:lo + rot_dim] = np.concatenate([cos, cos], axis=-1)
        s[:n_x, lo:lo + rot_dim] = np.concatenate([-sin, sin], axis=-1)
    return jnp.asarray(c), jnp.asarray(s)


def _segment_matrix(bounds):
    m = np.zeros((LANES, LANES), np.float32)
    for lo, hi in bounds:
        m[lo:hi, lo:hi] = 1.0 / (hi - lo)
    return jnp.asarray(m, BF16)


def _token_tile(t):
    return _first_divisor(t, TOKEN_TILES, "token count")


def kernel(x, c, ctx, c_ctx, l0_ada_w, l0_ada_b, l0_norm1_g, l0_norm2_g, l0_w_in, l0_ret_log_decay, l0_q_norm_g, l0_k_norm_g, l0_w_out, l0_ffn_w1, l0_ffn_w3, l0_ffn_w2, l1_ada_w, l1_ada_b, l1_norm1_g, l1_norm2_g, l1_w_in, l1_q_lora_g, l1_kv_lora_g, l1_w_uq, l1_w_ukv, l1_q_nope_g, l1_q_rope_g, l1_k_nope_g, l1_k_rope_g, l1_w_out, l1_router, l1_exp_w1, l1_exp_w3, l1_exp_w2):
    b, n_x, d = x.shape
    n_ctx = ctx.shape[1]
    assert b == 1 and n_x % 256 == 0 and n_ctx % 256 == 0 and n_x % GRID_W == 0
    t = n_x + n_ctx
    tm = _token_tile(t)
    tmx = _token_tile(n_x)
    xa = jnp.concatenate([x[0], ctx[0]], axis=0)
    row1 = lambda v: v.reshape(1, -1).astype(F32)

    cvec = jnp.zeros((8, d), F32).at[0].set(c[0]).at[1].set(c_ctx)
    mod0 = _ada(cvec, l0_ada_w, l0_ada_b)
    mod1 = _ada(cvec, l1_ada_w, l1_ada_b)

    rw = RET_HEADS * HEAD
    perm = _deinterleave(HEAD)
    n_qk = GQA_HEADS + GQA_KV_HEADS
    qk_cols = l0_w_in[:, 4 * rw:4 * rw + n_qk * HEAD].reshape(d, n_qk, HEAD // 2, 2)
    qk_cols = jnp.swapaxes(qk_cols, 2, 3).reshape(d, n_qk * HEAD)
    w_in0 = jnp.concatenate([l0_w_in[:, :rw], l0_w_in[:, rw:2 * rw] * (HEAD ** -0.5), l0_w_in[:, 2 * rw:4 * rw],
                             qk_cols, l0_w_in[:, 4 * rw + n_qk * HEAD:]], axis=1).astype(BF16)
    seg64 = _segment_matrix([(0, HEAD), (HEAD, 2 * HEAD)])
    gq0 = row1(jnp.tile(l0_q_norm_g[perm], 2) * (HEAD ** -0.5 * LOG2E))
    gk0 = row1(jnp.tile(l0_k_norm_g[perm], 2))
    cos0, sin0 = _rope_tables(n_x, n_ctx, HEAD, 0, 2)

    rq, rk, rv, rg, gq, gkx, gvx = _l0_proj(xa, mod0, row1(l0_norm1_g), w_in0, seg64, gq0, gk0, cos0, sin0,
                                             tm=tm, n_x=n_x)
    o_f, o_b = _retention(l0_ret_log_decay.astype(F32), rq, rk, rv, n_x=n_x)

    gqa_maps = dict(q_maps=(lambda j: j, lambda j: j),
                    k_maps=(lambda j: 2 * (j // 2), lambda j: 2 * (j // 2) + 1),
                    v_maps=(lambda j: 2 * (j // 2), lambda j: 2 * (j // 2) + 1), n_pairs=GQA_HEADS // 2)
    bq = _first_divisor(n_x, FLASH_Q_TILES, "query count")
    rs = FLASH_ROW_GROUP
    ao_x = _flash(gq, gkx, (gvx, gvx), n_q=n_x, q_row0=0, kv_row0=0, n_kv=t, bq=bq, bk=_kv_block(t), rs=rs,
                  **gqa_maps)
    ao_c = _flash(gq, gkx, (gvx, gvx), n_q=n_ctx, q_row0=n_x, kv_row0=n_x, n_kv=n_ctx, bq=n_ctx,
                  bk=_kv_block(n_ctx), rs=rs, **gqa_maps)
    ao = jnp.concatenate([ao_x, ao_c], axis=0)

    xa = _l0_out(xa, mod0, o_f, o_b, rg, ao, seg64, l0_w_out.astype(BF16), tm=tm, n_x=n_x)
    xa = _ffn(xa, mod0, row1(l0_norm2_g), l0_ffn_w1.astype(BF16), l0_ffn_w3.astype(BF16),
              l0_ffn_w2.astype(BF16), tm=tm, n_x=n_x)

    rperm = _deinterleave(MLA_ROPE)
    qk_w = MLA_NOPE + MLA_ROPE
    wuq = jnp.zeros((MLA_Q_RANK, MLA_HEADS * LANES), F32)
    wuk = jnp.zeros((MLA_KV_RANK, MLA_HEADS * LANES), F32)
    wuv = []
    for hd in range(MLA_HEADS):
        src = l1_w_uq[:, hd * qk_w:(hd + 1) * qk_w]
        wuq = wuq.at[:, hd * LANES:hd * LANES + MLA_NOPE].set(src[:, :MLA_NOPE])
        wuq = wuq.at[:, hd * LANES + MLA_NOPE:hd * LANES + qk_w].set(src[:, MLA_NOPE:][:, rperm])
        kvsrc = l1_w_ukv[:, hd * 2 * HEAD:(hd + 1) * 2 * HEAD]
        wuk = wuk.at[:, hd * LANES:hd * LANES + MLA_NOPE].set(kvsrc[:, :MLA_NOPE])
        wuv.append(kvsrc[:, MLA_NOPE:])
    wuv = jnp.concatenate(wuv, axis=1)
    wkr = jnp.zeros((d, LANES), F32).at[:, MLA_NOPE:qk_w].set(l1_w_in[:, MLA_Q_RANK + MLA_KV_RANK:][:, rperm])
    pad = jnp.zeros((LANES - qk_w,), F32)
    zn = jnp.zeros((MLA_NOPE,), F32)
    gq1 = row1(jnp.concatenate([l1_q_nope_g, l1_q_rope_g[rperm], pad]) * (qk_w ** -0.5 * LOG2E))
    gk1 = row1(jnp.concatenate([l1_k_nope_g, jnp.zeros((LANES - MLA_NOPE,), F32)]))
    gkr1 = row1(jnp.concatenate([zn, l1_k_rope_g[rperm], pad]))
    seg_mla = _segment_matrix([(0, MLA_NOPE), (MLA_NOPE, qk_w)])
    cos1, sin1 = _rope_tables(n_x, n_ctx, MLA_ROPE, MLA_NOPE, 1)

    mq, mk, mvlo, mvhi = _l1_proj(
        xa, mod1, row1(l1_norm1_g), l1_w_in[:, :MLA_Q_RANK].astype(BF16),
        l1_w_in[:, MLA_Q_RANK:MLA_Q_RANK + MLA_KV_RANK].astype(BF16), wkr.astype(BF16),
        row1(l1_q_lora_g), row1(l1_kv_lora_g), wuq.astype(BF16), wuk.astype(BF16), wuv.astype(BF16),
        seg_mla, gq1, gk1, gkr1, cos1, sin1, tm=tm, n_x=n_x)
    n_e, _, fdim = l1_exp_w1.shape
    mo, ew1, ew3, ew2 = _flash(
        mq, mk, (mvlo, mvhi), q_maps=(lambda j: 2 * j, lambda j: 2 * j + 1),
        k_maps=(lambda j: 2 * j, lambda j: 2 * j + 1), v_maps=(lambda j: j, lambda j: j),
        n_pairs=MLA_HEADS // 2, n_q=n_x, q_row0=0, kv_row0=0, n_kv=t, bq=bq, bk=_kv_block(t), rs=rs,
        side=(l1_exp_w1.reshape(n_e * d, fdim), l1_exp_w3.reshape(n_e * d, fdim),
              l1_exp_w2.reshape(n_e * fdim, d)))
    ew1, ew3, ew2 = ew1.reshape(n_e, d, fdim), ew3.reshape(n_e, d, fdim), ew2.reshape(n_e, fdim, d)

    router = jnp.zeros((d, LANES), F32).at[:, :N_EXPERTS].set(l1_router)
    r_hi = router.astype(BF16)
    r_lo = (router - r_hi.astype(F32)).astype(BF16)
    x3, hmoe, ei, ew = _l1_out(xa, mod1, row1(l1_norm2_g), mo, l1_w_out.astype(BF16), r_hi, r_lo, tm=tmx)

    tme = MOE_TILE_ROWS if n_x >= MOE_SMALL_BELOW else MOE_SMALL_TILE_ROWS
    e_flat = ei[:, :TOP_K].reshape(-1)
    onehot = (e_flat[:, None] == jnp.arange(N_EXPERTS)[None, :]).astype(jnp.int32)
    csum = jnp.cumsum(onehot, axis=0)
    rank = jnp.sum((csum - onehot) * onehot, axis=1)
    counts = csum[-1]
    padded = ((counts + tme - 1) // tme) * tme
    ends = jnp.cumsum(padded)
    pos = (ends - padded)[e_flat] + rank
    n_tiles = (TOP_K * n_x) // tme + N_EXPERTS
    p_rows = n_tiles * tme
    n_valid = (ends[-1] // tme).astype(jnp.int32)
    tile_ids = jnp.minimum(jnp.arange(n_tiles, dtype=jnp.int32), n_valid - 1)
    tile_expert = jnp.sum((ends[None, :] <= (tile_ids * tme)[:, None]).astype(jnp.int32), axis=1)
    tile_expert = jnp.minimum(tile_expert, N_EXPERTS - 1)
    by_expert = jnp.argsort(e_flat, stable=True).astype(jnp.int32) // TOP_K
    slot = jnp.arange(p_rows, dtype=jnp.int32)
    slot_e = jnp.repeat(tile_expert, tme)
    slot_rank = slot - (ends - padded)[slot_e]
    listed = jnp.take(by_expert, (jnp.cumsum(counts) - counts)[slot_e] + slot_rank, mode="clip")
    src = jnp.where((slot_rank < counts[slot_e]) & (slot < ends[-1]), listed, slot % n_x)
    xs = _gather_rows(hmoe, src)
    tf = fdim // MOE_F_BLOCKS if (fdim // MOE_F_BLOCKS) % LANES == 0 else fdim
    ys = _moe(tile_expert, n_valid.reshape(1), xs, ew1, ew3, ew2, tm=tme, tf=tf)
    pos2 = pos.reshape(n_x, TOP_K)
    out = _combine(x3, mod1, ew, jnp.take(ys, pos2[:, 0], axis=0, mode="clip"),
                   jnp.take(ys, pos2[:, 1], axis=0, mode="clip"), tm=tmx)
    return out[None]
```

```python
import functools
import math

import numpy as np
import jax
import jax.numpy as jnp
from jax import lax
from jax.experimental import pallas as pl
from jax.experimental.pallas import tpu as pltpu
from jax.experimental.pallas import tpu_sc as plsc

F32 = jnp.float32
BF16 = jnp.bfloat16

EPS = 1e-6
ROPE_THETA = 10000.0
GRID_W = 64
LANES = 128
HEAD = 64
RET_CHUNK = 128
RET_STEP_CHUNKS = 2
RET_HEADS = 8
GQA_HEADS = 8
GQA_KV_HEADS = 2
MLA_HEADS = 8
MLA_Q_RANK = 384
MLA_KV_RANK = 256
MLA_NOPE = 64
MLA_ROPE = 32
N_EXPERTS = 8
TOP_K = 2
LOW_ONE = HEAD
HIGH_ONE = 0
LOG2E = math.log2(math.e)
VMEM_LIMIT = 56 * 1024 * 1024

TOKEN_TILES = (640, 512, 256, 128)
FLASH_Q_TILES = (512, 256)
FLASH_KEY_BLOCKS = (1280, 1024, 512, 256)
FLASH_ROW_GROUP = 32
ADA_COLUMN_TILES = 4
MOE_TILE_ROWS = 512
MOE_SMALL_TILE_ROWS = 128
MOE_SMALL_BELOW = 4096
MOE_F_BLOCKS = 2


def _first_divisor(n, candidates, what):
    for c in candidates:
        if n % c == 0:
            return c
    raise ValueError(f"{what} {n} has no supported tile among {candidates}")


def _cparams(sem, vmem=VMEM_LIMIT):
    return pltpu.CompilerParams(dimension_semantics=sem, vmem_limit_bytes=vmem)


def _resident(shape):
    nd = len(shape)
    return pl.BlockSpec(shape, lambda *_: (0,) * nd, pipeline_mode=pl.Buffered(1))


def _dot(a, b):
    return jnp.dot(a, b, preferred_element_type=F32)


def _dot_nt(a, b):
    return lax.dot_general(a, b, (((1,), (1,)), ((), ())), preferred_element_type=F32)


def _seg_mean(v, seg):
    hi = v.astype(BF16)
    lo = (v - hi.astype(F32)).astype(BF16)
    return _dot(hi, seg) + _dot(lo, seg)


def _silu(x):
    return x * jax.nn.sigmoid(x)


def _modulated(x, mod_ref, g_ref, which, tile, tm, n_x, d):
    ms = jnp.mean(x * x, axis=-1, keepdims=True)
    xn = x * lax.rsqrt(ms + EPS)
    g = g_ref[...]
    sh, sc = 3 * which, 3 * which + 1
    a_x = g * (1.0 + mod_ref[0:1, sc * d:(sc + 1) * d])
    a_c = g * (1.0 + mod_ref[1:2, sc * d:(sc + 1) * d])
    b_x = mod_ref[0:1, sh * d:(sh + 1) * d]
    b_c = mod_ref[1:2, sh * d:(sh + 1) * d]
    row = tile * tm + lax.broadcasted_iota(jnp.int32, (tm, 1), 0)
    is_ctx = row >= n_x
    return xn * jnp.where(is_ctx, a_c, a_x) + jnp.where(is_ctx, b_c, b_x)


def _row_gate(mod_ref, idx, tile, tm, n_x, d):
    row = tile * tm + lax.broadcasted_iota(jnp.int32, (tm, 1), 0)
    return jnp.where(row >= n_x, mod_ref[1:2, idx * d:(idx + 1) * d], mod_ref[0:1, idx * d:(idx + 1) * d])


def _lane(shape):
    return lax.broadcasted_iota(jnp.int32, shape, len(shape) - 1)


ADA_ROWS = 2


def _ada_kernel(c_ref, w_ref, b_ref, o_ref):
    s_t = _silu(c_ref[...]).T
    w = w_ref[...]
    rows = [jnp.sum(w * s_t[:, r:r + 1], axis=0, keepdims=True) for r in range(ADA_ROWS)]
    pad = jnp.zeros((c_ref.shape[0] - ADA_ROWS, w.shape[1]), F32)
    o_ref[...] = jnp.concatenate(rows + [pad], axis=0) + b_ref[...]


def _ada(cvec8, w, b):
    d, n = w.shape
    tn = n // ADA_COLUMN_TILES
    return pl.pallas_call(
        _ada_kernel,
        grid=(n // tn,),
        in_specs=[pl.BlockSpec((8, d), lambda j: (0, 0)),
                  pl.BlockSpec((d, tn), lambda j: (0, j)),
                  pl.BlockSpec((1, tn), lambda j: (0, j))],
        out_specs=pl.BlockSpec((8, tn), lambda j: (0, j)),
        out_shape=jax.ShapeDtypeStruct((8, n), F32),
        compiler_params=_cparams(("arbitrary",)),
    )(cvec8, w, b.reshape(1, n))


def _rope128(v, c, s, half):
    lane = _lane(v.shape)
    swapped = jnp.where(lane % (2 * half) < half, pltpu.roll(v, LANES - half, 1), pltpu.roll(v, half, 1))
    return v * c + swapped * s


def _l0_proj_kernel(x_ref, mod_ref, g_ref, w_ref, seg_ref, gq_ref, gk_ref, c_ref, s_ref,
                    rq_ref, rk_ref, rv_ref, rg_ref, q_ref, k_ref, v_ref, *, tm, n_x, d):
    i = pl.program_id(0)
    h = _modulated(x_ref[...], mod_ref, g_ref, 0, i, tm, n_x, d).astype(BF16)
    rw = RET_HEADS * HEAD
    for idx, ref in enumerate((rq_ref, rk_ref, rv_ref, rg_ref)):
        ref[...] = _dot(h, w_ref[:, idx * rw:(idx + 1) * rw]).astype(BF16)
    seg = seg_ref[...]
    cos, sin = c_ref[...], s_ref[...]
    base = 4 * rw
    qw = GQA_HEADS * HEAD
    qa = _dot(h, w_ref[:, base:base + qw])
    for g in range(qw // LANES):
        v = qa[:, g * LANES:(g + 1) * LANES]
        vn = v * lax.rsqrt(_seg_mean(v * v, seg) + EPS) * gq_ref[...]
        q_ref[:, g * LANES:(g + 1) * LANES] = _rope128(vn, cos, sin, HEAD // 2).astype(BF16)
    kv = _dot(h, w_ref[:, base + qw:base + qw + 2 * LANES])
    kk = kv[:, :LANES]
    kk = kk * lax.rsqrt(_seg_mean(kk * kk, seg) + EPS) * gk_ref[...]
    kk = _rope128(kk, cos, sin, HEAD // 2)
    vv = kv[:, LANES:]
    lane = _lane(kk.shape)
    low = lane < HEAD
    for src, ref, one in ((kk, k_ref, 0.0), (vv, v_ref, 1.0)):
        sw = pltpu.roll(src, HEAD, 1)
        lo_fill = jnp.where(lane == LOW_ONE, one, 0.0)
        hi_fill = jnp.where(lane == HIGH_ONE, one, 0.0)
        ref[:, 0 * LANES:1 * LANES] = jnp.where(low, src, lo_fill).astype(BF16)
        ref[:, 1 * LANES:2 * LANES] = jnp.where(low, hi_fill, sw).astype(BF16)
        ref[:, 2 * LANES:3 * LANES] = jnp.where(low, sw, lo_fill).astype(BF16)
        ref[:, 3 * LANES:4 * LANES] = jnp.where(low, hi_fill, src).astype(BF16)


def _l0_proj(xa, mod, g, w, seg, gq, gk, cos, sin, *, tm, n_x):
    t, d = xa.shape
    rw = RET_HEADS * HEAD
    row = lambda i: (i, 0)
    outs = [jax.ShapeDtypeStruct((t, rw), BF16)] * 7
    return pl.pallas_call(
        functools.partial(_l0_proj_kernel, tm=tm, n_x=n_x, d=d),
        grid=(t // tm,),
        in_specs=[pl.BlockSpec((tm, d), row), _resident(mod.shape), _resident(g.shape), _resident(w.shape),
                  _resident(seg.shape), _resident(gq.shape), _resident(gk.shape),
                  pl.BlockSpec((tm, LANES), row), pl.BlockSpec((tm, LANES), row)],
        out_specs=[pl.BlockSpec((tm, rw), row)] * 7,
        out_shape=outs,
        compiler_params=_cparams(("parallel",)),
    )(xa, mod, g, w, seg, gq, gk, cos, sin)


def _retention_kernel(lg_ref, qf_ref, kf_ref, vf_ref, qb_ref, kb_ref, vb_ref, of_ref, ob_ref,
                      state_ref, decay_ref, xi_ref, zeta_ref, gl_ref):
    c = RET_CHUNK
    npairs = RET_HEADS * HEAD // LANES
    step = pl.program_id(0)

    @pl.when(step == 0)
    def _init():
        state_ref[...] = jnp.zeros_like(state_ref)
        ci = lax.broadcasted_iota(jnp.int32, (c, c), 0).astype(F32)
        mi = lax.broadcasted_iota(jnp.int32, (c, c), 1).astype(F32)
        pos = lax.broadcasted_iota(jnp.int32, (c, RET_HEADS * HEAD), 0).astype(F32)
        lane_head = _lane((1, RET_HEADS * HEAD)) // HEAD
        for dr in range(2):
            lgv = jnp.zeros((1, RET_HEADS * HEAD), F32)
            for hd in range(RET_HEADS):
                lg = lg_ref[dr, hd]
                rel = (ci - mi) if dr == 0 else (mi - ci)
                half = slice((hd % 2) * c, (hd % 2 + 1) * c)
                decay_ref[dr, hd // 2, :, half] = jnp.where(rel >= 0, jnp.exp(jnp.maximum(rel, 0.0) * lg), 0.0)
                lgv = jnp.where(lane_head == hd, lg, lgv)
            p = pos if dr == 0 else (c - 1.0 - pos)
            xi_ref[dr] = jnp.exp((p + 1.0) * lgv)
            zeta_ref[dr] = jnp.exp((c - 1.0 - p) * lgv)
            gl_ref[dr] = jnp.exp(float(c) * lgv)

    low = _lane((c, LANES)) < HEAD
    r_i = lax.broadcasted_iota(jnp.int32, (LANES, LANES), 0) // HEAD
    c_i = lax.broadcasted_iota(jnp.int32, (LANES, LANES), 1) // HEAD
    blockdiag = r_i == c_i
    n_sub = qf_ref.shape[0] // c
    for dr, (q_ref, k_ref, v_ref, o_ref) in enumerate(((qf_ref, kf_ref, vf_ref, of_ref),
                                                       (qb_ref, kb_ref, vb_ref, ob_ref))):
        for j in range(npairs):
            sl = slice(j * LANES, (j + 1) * LANES)
            st = state_ref[dr, j]
            for sub in (range(n_sub) if dr == 0 else reversed(range(n_sub))):
                rows = slice(sub * c, (sub + 1) * c)
                q, k, v = q_ref[rows, sl], k_ref[rows, sl], v_ref[rows, sl]
                zero = jnp.zeros_like(k)
                k2 = jnp.concatenate([jnp.where(low, k, zero), jnp.where(low, zero, k)], axis=0)
                v2 = jnp.concatenate([jnp.where(low, v, zero), jnp.where(low, zero, v)], axis=0)
                s = _dot_nt(q, k2) * decay_ref[dr, j]
                o = _dot(s.astype(BF16), v2)
                qx = (q.astype(F32) * xi_ref[dr, :, sl]).astype(BF16)
                o = o + _dot(qx, st.astype(BF16))
                o_ref[rows, sl] = o.astype(BF16)
                kz = (k.astype(F32) * zeta_ref[dr, :, sl]).T.astype(BF16)
                u = _dot(kz, v)
                st = st * gl_ref[dr, :, sl] + jnp.where(blockdiag, u, 0.0)
            state_ref[dr, j] = st


def _retention(lg, rq, rk, rv, *, n_x):
    t, w = rq.shape
    c = RET_CHUNK
    rows = RET_STEP_CHUNKS * c
    assert n_x % rows == 0 and t % rows == 0
    nc, ncx = t // rows, n_x // rows
    fwd = lambda i: ((i + ncx) % nc, 0)
    bwd = lambda i: (nc - 1 - i, 0)
    blk = lambda m: pl.BlockSpec((rows, w), m)
    npairs = w // LANES
    return pl.pallas_call(
        _retention_kernel,
        grid=(nc,),
        in_specs=[pl.BlockSpec(memory_space=pltpu.SMEM)] + [blk(fwd)] * 3 + [blk(bwd)] * 3,
        out_specs=[blk(fwd), blk(bwd)],
        out_shape=[jax.ShapeDtypeStruct((t, w), BF16)] * 2,
        scratch_shapes=[pltpu.VMEM((2, npairs, LANES, LANES), F32),
                        pltpu.VMEM((2, npairs, c, 2 * c), F32),
                        pltpu.VMEM((2, c, w), F32), pltpu.VMEM((2, c, w), F32),
                        pltpu.VMEM((2, 1, w), F32)],
        compiler_params=_cparams(("arbitrary",)),
    )(lg, rq, rk, rv, rq, rk, rv)


def _flash_kernel(q0_ref, q1_ref, k0_ref, k1_ref, v0_ref, v1_ref, *rest, bk, nkv, rs, n_side):
    side_in, o_ref, side_out = rest[:n_side], rest[n_side], rest[n_side + 1:2 * n_side + 1]
    s_ref, p_ref, a_ref, m_ref, acc_ref = rest[2 * n_side + 1:]
    for src_ref, dst_ref in zip(side_in, side_out):
        dst_ref[...] = src_ref[...].astype(dst_ref.dtype)
    bq = q0_ref.shape[0]
    q_refs, k_refs, v_refs = (q0_ref, q1_ref), (k0_ref, k1_ref), (v0_ref, v1_ref)
    m_ref[...] = jnp.full(m_ref.shape, -jnp.inf, F32)
    acc_ref[...] = jnp.zeros(acc_ref.shape, F32)

    def keys(t):
        return pl.ds(t * bk if isinstance(t, int) else pl.multiple_of(t * bk, bk), bk)

    def scores(t, slot):
        for h in range(2):
            s_ref[slot, h] = _dot_nt(q_refs[h][...], k_refs[h][keys(t), :])

    def softmax(slot):
        col = lambda c: slice(c * LANES, (c + 1) * LANES)
        for r in range(bq // rs):
            rows = slice(r * rs, (r + 1) * rs)
            for h in range(2):
                mx = s_ref[slot, h, rows, col(0)]
                for c in range(1, bk // LANES):
                    mx = jnp.maximum(mx, s_ref[slot, h, rows, col(c)])
                m_old = m_ref[h, rows, :]
                m_new = jnp.maximum(m_old, jnp.max(mx, axis=-1, keepdims=True))
                a_ref[slot, h, rows, :] = jnp.exp2(m_old - m_new)
                m_ref[h, rows, :] = m_new
                for c in range(bk // LANES):
                    p_ref[slot, h, rows, col(c)] = jnp.exp2(s_ref[slot, h, rows, col(c)] - m_new).astype(BF16)

    def values(t, slot):
        for h in range(2):
            acc_ref[h] = acc_ref[h] * a_ref[slot, h] + _dot(p_ref[slot, h], v_refs[h][keys(t), :])

    scores(0, 0)

    def body(i, carry):
        t = 2 * i
        scores(t + 1, 1)
        softmax(0)
        values(t, 0)
        scores(t + 2, 0)
        softmax(1)
        values(t + 1, 1)
        return carry

    n_loop = (nkv - 1) // 2
    lax.fori_loop(0, n_loop, body, 0)
    last = 2 * n_loop
    if last + 1 < nkv:
        scores(last + 1, 1)
    softmax(0)
    values(last, 0)
    if last + 1 < nkv:
        softmax(1)
        values(last + 1, 1)
    low = _lane((bq, LANES)) < HEAD
    acc0, acc1 = acc_ref[0], acc_ref[1]
    out = jnp.where(low, acc0 / acc0[:, LOW_ONE:LOW_ONE + 1], acc1 / acc1[:, HIGH_ONE:HIGH_ONE + 1])
    o_ref[...] = out.astype(o_ref.dtype)


def _flash(q, kmat, vmat, *, q_maps, k_maps, v_maps, n_q, q_row0, kv_row0, n_kv, n_pairs, bq, bk, rs, side=()):
    assert q_row0 % bq == 0 and n_q % bq == 0 and n_kv % bk == 0 and kv_row0 % n_kv == 0 and bq % rs == 0
    qb0, kb0 = q_row0 // bq, kv_row0 // n_kv
    n_i = n_q // bq
    steps = n_pairs * n_i
    assert all(a.shape[0] % (16 * steps) == 0 for a in side)
    qspec = lambda m: pl.BlockSpec((bq, LANES), lambda j, i: (i + qb0, m(j)))
    kspec = lambda m: pl.BlockSpec((n_kv, LANES), lambda j, i: (kb0, m(j)), pipeline_mode=pl.Buffered(1))
    side_specs = [pl.BlockSpec((a.shape[0] // steps, a.shape[1]), lambda j, i: (j * n_i + i, 0)) for a in side]
    outs = pl.pallas_call(
        functools.partial(_flash_kernel, bk=bk, nkv=n_kv // bk, rs=rs, n_side=len(side)),
        grid=(n_pairs, n_i),
        in_specs=[qspec(q_maps[0]), qspec(q_maps[1]), kspec(k_maps[0]), kspec(k_maps[1]),
                  kspec(v_maps[0]), kspec(v_maps[1])] + side_specs,
        out_specs=[pl.BlockSpec((bq, LANES), lambda j, i: (i, j))] + side_specs,
        out_shape=[jax.ShapeDtypeStruct((n_q, n_pairs * LANES), BF16)]
                  + [jax.ShapeDtypeStruct(a.shape, BF16) for a in side],
        scratch_shapes=[pltpu.VMEM((2, 2, bq, bk), F32), pltpu.VMEM((2, 2, bq, bk), BF16),
                        pltpu.VMEM((2, 2, bq, LANES), F32), pltpu.VMEM((2, bq, LANES), F32),
                        pltpu.VMEM((2, bq, LANES), F32)],
        compiler_params=_cparams(("parallel", "parallel")),
    )(q, q, kmat, kmat, vmat[0], vmat[1], *side)
    return outs if side else outs[0]


def _kv_block(n_kv):
    return _first_divisor(n_kv, FLASH_KEY_BLOCKS, "key count")


def _l0_out_kernel(x_ref, mod_ref, of_ref, ob_ref, rg_ref, ao_ref, seg_ref, wo_ref, o_ref, *, tm, n_x, d):
    i = pl.program_id(0)
    seg = seg_ref[...]
    rw = RET_HEADS * HEAD
    acc = _dot(ao_ref[...], wo_ref[rw:, :])
    for g in range(rw // LANES):
        sl = slice(g * LANES, (g + 1) * LANES)
        o = of_ref[:, sl].astype(F32) + ob_ref[:, sl].astype(F32)
        dv = o - _seg_mean(o, seg)
        nrm = dv * lax.rsqrt(_seg_mean(dv * dv, seg) + EPS)
        ra = (nrm * _silu(rg_ref[:, sl].astype(F32))).astype(BF16)
        acc = acc + _dot(ra, wo_ref[g * LANES:(g + 1) * LANES, :])
    o_ref[...] = x_ref[...] + _row_gate(mod_ref, 2, i, tm, n_x, d) * acc


def _l0_out(xa, mod, o_f, o_b, rg, ao, seg, wo, *, tm, n_x):
    t, d = xa.shape
    rw = o_f.shape[1]
    row = lambda i: (i, 0)
    return pl.pallas_call(
        functools.partial(_l0_out_kernel, tm=tm, n_x=n_x, d=d),
        grid=(t // tm,),
        in_specs=[pl.BlockSpec((tm, d), row), _resident(mod.shape)] + [pl.BlockSpec((tm, rw), row)] * 4
                 + [_resident(seg.shape), _resident(wo.shape)],
        out_specs=pl.BlockSpec((tm, d), row),
        out_shape=jax.ShapeDtypeStruct((t, d), F32),
        compiler_params=_cparams(("parallel",)),
    )(xa, mod, o_f, o_b, rg, ao, seg, wo)


def _ffn_kernel(x_ref, mod_ref, g_ref, w1_ref, w3_ref, w2_ref, o_ref, *, tm, n_x, d):
    i = pl.program_id(0)
    x = x_ref[...]
    h = _modulated(x, mod_ref, g_ref, 1, i, tm, n_x, d).astype(BF16)
    a = _dot(h, w1_ref[...])
    u = (_silu(a) * _dot(h, w3_ref[...])).astype(BF16)
    o_ref[...] = x + _row_gate(mod_ref, 5, i, tm, n_x, d) * _dot(u, w2_ref[...])


def _ffn(xa, mod, g, w1, w3, w2, *, tm, n_x):
    t, d = xa.shape
    row = lambda i: (i, 0)
    return pl.pallas_call(
        functools.partial(_ffn_kernel, tm=tm, n_x=n_x, d=d),
        grid=(t // tm,),
        in_specs=[pl.BlockSpec((tm, d), row), _resident(mod.shape), _resident(g.shape),
                  _resident(w1.shape), _resident(w3.shape), _resident(w2.shape)],
        out_specs=pl.BlockSpec((tm, d), row),
        out_shape=jax.ShapeDtypeStruct((t, d), F32),
        compiler_params=_cparams(("parallel",)),
    )(xa, mod, g, w1, w3, w2)


def _l1_proj_kernel(x_ref, mod_ref, g_ref, wq_ref, wkv_ref, wkr_ref, gql_ref, gkvl_ref, wuq_ref, wuk_ref,
                    wuv_ref, seg_ref, gq_ref, gk_ref, gkr_ref, c_ref, s_ref,
                    q_ref, k_ref, vlo_ref, vhi_ref, *, tm, n_x, d):
    i = pl.program_id(0)
    h = _modulated(x_ref[...], mod_ref, g_ref, 0, i, tm, n_x, d).astype(BF16)
    seg = seg_ref[...]
    cos, sin = c_ref[...], s_ref[...]

    def lora_norm(v, g):
        return (v * lax.rsqrt(jnp.mean(v * v, axis=-1, keepdims=True) + EPS) * g).astype(BF16)

    cq = lora_norm(_dot(h, wq_ref[...]), gql_ref[...])
    ckv = lora_norm(_dot(h, wkv_ref[...]), gkvl_ref[...])
    kr = _dot(h, wkr_ref[...])
    kr = kr * lax.rsqrt(_seg_mean(kr * kr, seg) + EPS) * gkr_ref[...]
    kr = _rope128(kr, cos, sin, MLA_ROPE // 2)
    qa = _dot(cq, wuq_ref[...])
    ka = _dot(ckv, wuk_ref[...])
    for hd in range(MLA_HEADS):
        sl = slice(hd * LANES, (hd + 1) * LANES)
        v = qa[:, sl]
        vn = v * lax.rsqrt(_seg_mean(v * v, seg) + EPS) * gq_ref[...]
        q_ref[:, sl] = _rope128(vn, cos, sin, MLA_ROPE // 2).astype(BF16)
        v = ka[:, sl]
        k_ref[:, sl] = (v * lax.rsqrt(_seg_mean(v * v, seg) + EPS) * gk_ref[...] + kr).astype(BF16)
    va = _dot(ckv, wuv_ref[...])
    lane = _lane(va.shape) % LANES
    low = lane < HEAD
    vlo_ref[...] = jnp.where(low, va, jnp.where(lane == LOW_ONE, 1.0, 0.0)).astype(BF16)
    vhi_ref[...] = jnp.where(low, jnp.where(lane == HIGH_ONE, 1.0, 0.0), va).astype(BF16)


def _l1_proj(xa, mod, g, wq, wkv, wkr, gql, gkvl, wuq, wuk, wuv, seg, gq, gk, gkr, cos, sin, *, tm, n_x):
    t, d = xa.shape
    row = lambda i: (i, 0)
    hw = MLA_HEADS * LANES
    vw = MLA_HEADS * HEAD
    consts = (mod, g, wq, wkv, wkr, gql, gkvl, wuq, wuk, wuv, seg, gq, gk, gkr)
    return pl.pallas_call(
        functools.partial(_l1_proj_kernel, tm=tm, n_x=n_x, d=d),
        grid=(t // tm,),
        in_specs=[pl.BlockSpec((tm, d), row)] + [_resident(a.shape) for a in consts]
                 + [pl.BlockSpec((tm, LANES), row)] * 2,
        out_specs=[pl.BlockSpec((tm, hw), row), pl.BlockSpec((tm, hw), row),
                   pl.BlockSpec((tm, vw), row), pl.BlockSpec((tm, vw), row)],
        out_shape=[jax.ShapeDtypeStruct((t, hw), BF16), jax.ShapeDtypeStruct((t, hw), BF16),
                   jax.ShapeDtypeStruct((t, vw), BF16), jax.ShapeDtypeStruct((t, vw), BF16)],
        compiler_params=_cparams(("parallel",)),
    )(xa, *consts, cos, sin)


def _l1_out_kernel(x_ref, mod_ref, g_ref, o_ref, wo_ref, rhi_ref, rlo_ref, x3_ref, h_ref, ei_ref, ew_ref, *, d):
    x3 = x_ref[...] + mod_ref[0:1, 2 * d:3 * d] * _dot(o_ref[...], wo_ref[...])
    x3_ref[...] = x3
    ms = jnp.mean(x3 * x3, axis=-1, keepdims=True)
    h = x3 * lax.rsqrt(ms + EPS) * (g_ref[...] * (1.0 + mod_ref[0:1, 4 * d:5 * d])) + mod_ref[0:1, 3 * d:4 * d]
    hi = h.astype(BF16)
    bits = lax.bitcast_convert_type(hi.astype(F32), jnp.uint32)
    words = (bits[:, :d // 2] >> 16) | (bits[:, d // 2:] & jnp.uint32(0xFFFF0000))
    h_ref[...] = words
    lo = (h - hi.astype(F32)).astype(BF16)
    logits = _dot(hi, rhi_ref[...]) + (_dot(hi, rlo_ref[...]) + _dot(lo, rhi_ref[...]))
    lane_i = _lane(logits.shape)
    lane = lane_i.astype(F32)
    logits = jnp.where(lane_i < N_EXPERTS, logits, -jnp.inf)
    v1 = jnp.max(logits, axis=-1, keepdims=True)
    i1 = jnp.min(jnp.where(logits == v1, lane, float(LANES)), axis=-1, keepdims=True)
    rest = jnp.where(lane == i1, -jnp.inf, logits)
    v2 = jnp.max(rest, axis=-1, keepdims=True)
    i2 = jnp.min(jnp.where(rest == v2, lane, float(LANES)), axis=-1, keepdims=True)
    e2 = jnp.exp(v2 - v1)
    den = 1.0 + e2
    ei_ref[...] = jnp.where(lane_i == 0, i1, jnp.where(lane_i == 1, i2, 0.0)).astype(jnp.int32)
    ew_ref[...] = jnp.where(lane_i == 0, 1.0 / den, jnp.where(lane_i == 1, e2 / den, 0.0))


def _l1_out(xa, mod, g, o, wo, rhi, rlo, *, tm):
    n, d = o.shape[0], xa.shape[1]
    row = lambda i: (i, 0)
    return pl.pallas_call(
        functools.partial(_l1_out_kernel, d=d),
        grid=(n // tm,),
        in_specs=[pl.BlockSpec((tm, d), row), _resident(mod.shape), _resident(g.shape),
                  pl.BlockSpec((tm, o.shape[1]), row), _resident(wo.shape), _resident(rhi.shape),
                  _resident(rlo.shape)],
        out_specs=[pl.BlockSpec((tm, d), row), pl.BlockSpec((tm, d // 2), row),
                   pl.BlockSpec((tm, LANES), row), pl.BlockSpec((tm, LANES), row)],
        out_shape=[jax.ShapeDtypeStruct((n, d), F32), jax.ShapeDtypeStruct((n, d // 2), jnp.uint32),
                   jax.ShapeDtypeStruct((n, LANES), jnp.int32), jax.ShapeDtypeStruct((n, LANES), F32)],
        compiler_params=_cparams(("parallel",)),
    )(xa, mod, g, o, wo, rhi, rlo)


def _moe_kernel(te_ref, nv_ref, x_ref, w1_ref, w3_ref, w2_ref, y_ref, acc_ref, *, nf):
    i, f = pl.program_id(0), pl.program_id(1)

    @pl.when(f == 0)
    def _zero():
        acc_ref[...] = jnp.zeros_like(acc_ref)

    @pl.when(i < nv_ref[0])
    def _compute():
        words = x_ref[...]
        lo = lax.bitcast_convert_type(words << 16, F32)
        hi = lax.bitcast_convert_type(words & jnp.uint32(0xFFFF0000), F32)
        x = jnp.concatenate([lo, hi], axis=1).astype(BF16)
        a = _dot(x, w1_ref[0])
        u = (_silu(a) * _dot(x, w3_ref[0])).astype(BF16)
        acc_ref[...] += _dot(u, w2_ref[0])

    @pl.when(f == nf - 1)
    def _store():
        y_ref[...] = acc_ref[...].astype(y_ref.dtype)


def _moe(tile_expert, n_valid, xs, w1, w3, w2, *, tm, tf):
    p, d = xs.shape[0], 2 * xs.shape[1]
    fdim = w1.shape[2]
    nf = fdim // tf
    fi = lambda i, f, te, nv: jnp.where(i < nv[0], f, nf - 1)
    grid_spec = pltpu.PrefetchScalarGridSpec(
        num_scalar_prefetch=2,
        grid=(p // tm, nf),
        in_specs=[pl.BlockSpec((tm, d // 2), lambda i, f, te, nv: (i, 0)),
                  pl.BlockSpec((1, d, tf), lambda i, f, te, nv: (te[i], 0, fi(i, f, te, nv))),
                  pl.BlockSpec((1, d, tf), lambda i, f, te, nv: (te[i], 0, fi(i, f, te, nv))),
                  pl.BlockSpec((1, tf, d), lambda i, f, te, nv: (te[i], fi(i, f, te, nv), 0))],
        out_specs=pl.BlockSpec((tm, d), lambda i, f, te, nv: (i, 0)),
        scratch_shapes=[pltpu.VMEM((tm, d), F32)],
    )
    return pl.pallas_call(
        functools.partial(_moe_kernel, nf=nf),
        grid_spec=grid_spec,
        out_shape=jax.ShapeDtypeStruct((p, d), BF16),
        compiler_params=_cparams(("arbitrary", "arbitrary")),
    )(tile_expert, n_valid, xs, w1, w3, w2)


SC_GATHER_WINDOW = 128
SC_LANES = 16


def _gather_rows(x, idx):
    n, d = idx.shape[0], x.shape[1]
    w = SC_GATHER_WINDOW
    assert n % w == 0
    mesh = plsc.VectorSubcoreMesh(core_axis_name="core", subcore_axis_name="subcore")

    @pl.kernel(out_type=jax.ShapeDtypeStruct((n, d), x.dtype), mesh=mesh,
               scratch_types=[pltpu.SemaphoreType.DMA])
    def gather_kernel(x_hbm, i_hbm, o_hbm, sem):
        def body(i_vmem, o_vmem):
            copies = []
            for k in range(w // SC_LANES):
                grp = pl.ds(k * SC_LANES, SC_LANES)
                copies.append(pltpu.async_copy(x_hbm.at[i_vmem[0, grp]], o_vmem.at[grp], sem))
            for cp in copies:
                cp.wait()

        pltpu.emit_pipeline(
            body,
            grid=(n // w,),
            in_specs=[pl.BlockSpec((1, w), lambda i: (0, i))],
            out_specs=[pl.BlockSpec((w, d), lambda i: (i, 0), pipeline_mode=pl.Buffered(1))],
            core_axis_name=("core", "subcore"),
            dimension_semantics=(pltpu.PARALLEL,),
        )(i_hbm, o_hbm)

    return gather_kernel(x, idx.reshape(1, n))


def _combine_kernel(x_ref, mod_ref, ew_ref, ya_ref, yb_ref, o_ref, *, d):
    ew = ew_ref[...]
    y = ew[:, 0:1] * ya_ref[...].astype(F32) + ew[:, 1:2] * yb_ref[...].astype(F32)
    o_ref[...] = x_ref[...] + mod_ref[0:1, 5 * d:6 * d] * y


def _combine(x3, mod, ew, ya, yb, *, tm):
    n, d = x3.shape
    row = lambda i: (i, 0)
    return pl.pallas_call(
        functools.partial(_combine_kernel, d=d),
        grid=(n // tm,),
        in_specs=[pl.BlockSpec((tm, d), row), _resident(mod.shape), pl.BlockSpec((tm, LANES), row),
                  pl.BlockSpec((tm, d), row), pl.BlockSpec((tm, d), row)],
        out_specs=pl.BlockSpec((tm, d), row),
        out_shape=jax.ShapeDtypeStruct((n, d), F32),
        compiler_params=_cparams(("parallel",)),
    )(x3, mod, ew, ya, yb)


def _deinterleave(width):
    return np.concatenate([np.arange(0, width, 2), np.arange(1, width, 2)])


def _rope_tables(n_x, n_ctx, rot_dim, seg_start, seg_repeat):
    f32 = np.float32
    rows = n_x // GRID_W
    row = np.repeat(np.arange(rows, dtype=f32), GRID_W)
    col = np.tile(np.arange(GRID_W, dtype=f32), rows)
    axis_dim = rot_dim // 2
    inv_freq = f32(ROPE_THETA) ** (-np.arange(0, axis_dim, 2, dtype=f32) / f32(axis_dim))
    ang = np.concatenate([row[:, None] * inv_freq, col[:, None] * inv_freq], axis=-1)
    cos, sin = np.cos(ang).astype(f32), np.sin(ang).astype(f32)
    c = np.ones((n_x + n_ctx, LANES), f32)
    s = np.zeros((n_x + n_ctx, LANES), f32)
    for r in range(seg_repeat):
        lo = seg_start + r * rot_dim
        c[:n_x, lo:lo + rot_dim] = np.concatenate([cos, cos], axis=-1)
        s[:n_x, lo:lo + rot_dim] = np.concatenate([-sin, sin], axis=-1)
    return jnp.asarray(c), jnp.asarray(s)


def _segment_matrix(bounds):
    m = np.zeros((LANES, LANES), np.float32)
    for lo, hi in bounds:
        m[lo:hi, lo:hi] = 1.0 / (hi - lo)
    return jnp.asarray(m, BF16)


def _token_tile(t):
    return _first_divisor(t, TOKEN_TILES, "token count")


def kernel(x, c, ctx, c_ctx, l0_ada_w, l0_ada_b, l0_norm1_g, l0_norm2_g, l0_w_in, l0_ret_log_decay, l0_q_norm_g, l0_k_norm_g, l0_w_out, l0_ffn_w1, l0_ffn_w3, l0_ffn_w2, l1_ada_w, l1_ada_b, l1_norm1_g, l1_norm2_g, l1_w_in, l1_q_lora_g, l1_kv_lora_g, l1_w_uq, l1_w_ukv, l1_q_nope_g, l1_q_rope_g, l1_k_nope_g, l1_k_rope_g, l1_w_out, l1_router, l1_exp_w1, l1_exp_w3, l1_exp_w2):
    b, n_x, d = x.shape
    n_ctx = ctx.shape[1]
    assert b == 1 and n_x % 256 == 0 and n_ctx % 256 == 0 and n_x % GRID_W == 0
    t = n_x + n_ctx
    tm = _token_tile(t)
    tmx = _token_tile(n_x)
    xa = jnp.concatenate([x[0], ctx[0]], axis=0)
    row1 = lambda v: v.reshape(1, -1).astype(F32)

    cvec = jnp.zeros((8, d), F32).at[0].set(c[0]).at[1].set(c_ctx)
    mod0 = _ada(cvec, l0_ada_w, l0_ada_b)
    mod1 = _ada(cvec, l1_ada_w, l1_ada_b)

    rw = RET_HEADS * HEAD
    perm = _deinterleave(HEAD)
    n_qk = GQA_HEADS + GQA_KV_HEADS
    qk_cols = l0_w_in[:, 4 * rw:4 * rw + n_qk * HEAD].reshape(d, n_qk, HEAD // 2, 2)
    qk_cols = jnp.swapaxes(qk_cols, 2, 3).reshape(d, n_qk * HEAD)
    w_in0 = jnp.concatenate([l0_w_in[:, :rw], l0_w_in[:, rw:2 * rw] * (HEAD ** -0.5), l0_w_in[:, 2 * rw:4 * rw],
                             qk_cols, l0_w_in[:, 4 * rw + n_qk * HEAD:]], axis=1).astype(BF16)
    seg64 = _segment_matrix([(0, HEAD), (HEAD, 2 * HEAD)])
    gq0 = row1(jnp.tile(l0_q_norm_g[perm], 2) * (HEAD ** -0.5 * LOG2E))
    gk0 = row1(jnp.tile(l0_k_norm_g[perm], 2))
    cos0, sin0 = _rope_tables(n_x, n_ctx, HEAD, 0, 2)

    rq, rk, rv, rg, gq, gkx, gvx = _l0_proj(xa, mod0, row1(l0_norm1_g), w_in0, seg64, gq0, gk0, cos0, sin0,
                                             tm=tm, n_x=n_x)
    o_f, o_b = _retention(l0_ret_log_decay.astype(F32), rq, rk, rv, n_x=n_x)

    gqa_maps = dict(q_maps=(lambda j: j, lambda j: j),
                    k_maps=(lambda j: 2 * (j // 2), lambda j: 2 * (j // 2) + 1),
                    v_maps=(lambda j: 2 * (j // 2), lambda j: 2 * (j // 2) + 1), n_pairs=GQA_HEADS // 2)
    bq = _first_divisor(n_x, FLASH_Q_TILES, "query count")
    rs = FLASH_ROW_GROUP
    ao_x = _flash(gq, gkx, (gvx, gvx), n_q=n_x, q_row0=0, kv_row0=0, n_kv=t, bq=bq, bk=_kv_block(t), rs=rs,
                  **gqa_maps)
    ao_c = _flash(gq, gkx, (gvx, gvx), n_q=n_ctx, q_row0=n_x, kv_row0=n_x, n_kv=n_ctx, bq=n_ctx,
                  bk=_kv_block(n_ctx), rs=rs, **gqa_maps)
    ao = jnp.concatenate([ao_x, ao_c], axis=0)

    xa = _l0_out(xa, mod0, o_f, o_b, rg, ao, seg64, l0_w_out.astype(BF16), tm=tm, n_x=n_x)
    xa = _ffn(xa, mod0, row1(l0_norm2_g), l0_ffn_w1.astype(BF16), l0_ffn_w3.astype(BF16),
              l0_ffn_w2.astype(BF16), tm=tm, n_x=n_x)

    rperm = _deinterleave(MLA_ROPE)
    qk_w = MLA_NOPE + MLA_ROPE
    wuq = jnp.zeros((MLA_Q_RANK, MLA_HEADS * LANES), F32)
    wuk = jnp.zeros((MLA_KV_RANK, MLA_HEADS * LANES), F32)
    wuv = []
    for hd in range(MLA_HEADS):
        src = l1_w_uq[:, hd * qk_w:(hd + 1) * qk_w]
        wuq = wuq.at[:, hd * LANES:hd * LANES + MLA_NOPE].set(src[:, :MLA_NOPE])
        wuq = wuq.at[:, hd * LANES + MLA_NOPE:hd * LANES + qk_w].set(src[:, MLA_NOPE:][:, rperm])
        kvsrc = l1_w_ukv[:, hd * 2 * HEAD:(hd + 1) * 2 * HEAD]
        wuk = wuk.at[:, hd * LANES:hd * LANES + MLA_NOPE].set(kvsrc[:, :MLA_NOPE])
        wuv.append(kvsrc[:, MLA_NOPE:])
    wuv = jnp.concatenate(wuv, axis=1)
    wkr = jnp.zeros((d, LANES), F32).at[:, MLA_NOPE:qk_w].set(l1_w_in[:, MLA_Q_RANK + MLA_KV_RANK:][:, rperm])
    pad = jnp.zeros((LANES - qk_w,), F32)
    zn = jnp.zeros((MLA_NOPE,), F32)
    gq1 = row1(jnp.concatenate([l1_q_nope_g, l1_q_rope_g[rperm], pad]) * (qk_w ** -0.5 * LOG2E))
    gk1 = row1(jnp.concatenate([l1_k_nope_g, jnp.zeros((LANES - MLA_NOPE,), F32)]))
    gkr1 = row1(jnp.concatenate([zn, l1_k_rope_g[rperm], pad]))
    seg_mla = _segment_matrix([(0, MLA_NOPE), (MLA_NOPE, qk_w)])
    cos1, sin1 = _rope_tables(n_x, n_ctx, MLA_ROPE, MLA_NOPE, 1)

    mq, mk, mvlo, mvhi = _l1_proj(
        xa, mod1, row1(l1_norm1_g), l1_w_in[:, :MLA_Q_RANK].astype(BF16),
        l1_w_in[:, MLA_Q_RANK:MLA_Q_RANK + MLA_KV_RANK].astype(BF16), wkr.astype(BF16),
        row1(l1_q_lora_g), row1(l1_kv_lora_g), wuq.astype(BF16), wuk.astype(BF16), wuv.astype(BF16),
        seg_mla, gq1, gk1, gkr1, cos1, sin1, tm=tm, n_x=n_x)
    n_e, _, fdim = l1_exp_w1.shape
    mo, ew1, ew3, ew2 = _flash(
        mq, mk, (mvlo, mvhi), q_maps=(lambda j: 2 * j, lambda j: 2 * j + 1),
        k_maps=(lambda j: 2 * j, lambda j: 2 * j + 1), v_maps=(lambda j: j, lambda j: j),
        n_pairs=MLA_HEADS // 2, n_q=n_x, q_row0=0, kv_row0=0, n_kv=t, bq=bq, bk=_kv_block(t), rs=rs,
        side=(l1_exp_w1.reshape(n_e * d, fdim), l1_exp_w3.reshape(n_e * d, fdim),
              l1_exp_w2.reshape(n_e * fdim, d)))
    ew1, ew3, ew2 = ew1.reshape(n_e, d, fdim), ew3.reshape(n_e, d, fdim), ew2.reshape(n_e, fdim, d)

    router = jnp.zeros((d, LANES), F32).at[:, :N_EXPERTS].set(l1_router)
    r_hi = router.astype(BF16)
    r_lo = (router - r_hi.astype(F32)).astype(BF16)
    x3, hmoe, ei, ew = _l1_out(xa, mod1, row1(l1_norm2_g), mo, l1_w_out.astype(BF16), r_hi, r_lo, tm=tmx)

    tme = MOE_TILE_ROWS if n_x >= MOE_SMALL_BELOW else MOE_SMALL_TILE_ROWS
    e_flat = ei[:, :TOP_K].reshape(-1)
    onehot = (e_flat[:, None] == jnp.arange(N_EXPERTS)[None, :]).astype(jnp.int32)
    csum = jnp.cumsum(onehot, axis=0)
    rank = jnp.sum((csum - onehot) * onehot, axis=1)
    counts = csum[-1]
    padded = ((counts + tme - 1) // tme) * tme
    ends = jnp.cumsum(padded)
    pos = (ends - padded)[e_flat] + rank
    n_tiles = (TOP_K * n_x) // tme + N_EXPERTS
    p_rows = n_tiles * tme
    n_valid = (ends[-1] // tme).astype(jnp.int32)
    tile_ids = jnp.minimum(jnp.arange(n_tiles, dtype=jnp.int32), n_valid - 1)
    tile_expert = jnp.sum((ends[None, :] <= (tile_ids * tme)[:, None]).astype(jnp.int32), axis=1)
    tile_expert = jnp.minimum(tile_expert, N_EXPERTS - 1)
    by_expert = jnp.argsort(e_flat, stable=True).astype(jnp.int32) // TOP_K
    slot = jnp.arange(p_rows, dtype=jnp.int32)
    slot_e = jnp.repeat(tile_expert, tme)
    slot_rank = slot - (ends - padded)[slot_e]
    listed = jnp.take(by_expert, (jnp.cumsum(counts) - counts)[slot_e] + slot_rank, mode="clip")
    src = jnp.where((slot_rank < counts[slot_e]) & (slot < ends[-1]), listed, slot % n_x)
    xs = _gather_rows(hmoe, src)
    tf = fdim // MOE_F_BLOCKS if (fdim // MOE_F_BLOCKS) % LANES == 0 else fdim
    ys = _moe(tile_expert, n_valid.reshape(1), xs, ew1, ew3, ew2, tm=tme, tf=tf)
    pos2 = pos.reshape(n_x, TOP_K)
    out = _combine(x3, mod1, ew, jnp.take(ys, pos2[:, 0], axis=0, mode="clip"),
                   jnp.take(ys, pos2[:, 1], axis=0, mode="clip"), tm=tmx)
    return out[None]
```

```python
import functools
import math

import numpy as np
import jax
import jax.numpy as jnp
from jax import lax
from jax.experimental import pallas as pl
from jax.experimental.pallas import tpu as pltpu
from jax.experimental.pallas import tpu_sc as plsc

F32 = jnp.float32
BF16 = jnp.bfloat16

EPS = 1e-6
ROPE_THETA = 10000.0
GRID_W = 64
LANES = 128
HEAD = 64
RET_CHUNK = 128
RET_STEP_CHUNKS = 2
RET_HEADS = 8
GQA_HEADS = 8
GQA_KV_HEADS = 2
MLA_HEADS = 8
MLA_Q_RANK = 384
MLA_KV_RANK = 256
MLA_NOPE = 64
MLA_ROPE = 32
N_EXPERTS = 8
TOP_K = 2
LOW_ONE = HEAD
HIGH_ONE = 0
LOG2E = math.log2(math.e)
VMEM_LIMIT = 56 * 1024 * 1024

TOKEN_TILES = (640, 512, 256, 128)
FLASH_Q_TILES = (512, 256)
FLASH_KEY_BLOCKS = (1280, 1024, 512, 256)
FLASH_ROW_GROUP = 32
ADA_COLUMN_TILES = 4
MOE_TILE_ROWS = 512
MOE_SMALL_TILE_ROWS = 128
MOE_SMALL_BELOW = 4096
MOE_F_BLOCKS = 2


def _first_divisor(n, candidates, what):
    for c in candidates:
        if n % c == 0:
            return c
    raise ValueError(f"{what} {n} has no supported tile among {candidates}")


def _cparams(sem, vmem=VMEM_LIMIT):
    return pltpu.CompilerParams(dimension_semantics=sem, vmem_limit_bytes=vmem)


def _resident(shape):
    nd = len(shape)
    return pl.BlockSpec(shape, lambda *_: (0,) * nd, pipeline_mode=pl.Buffered(1))


def _dot(a, b):
    return jnp.dot(a, b, preferred_element_type=F32)


def _dot_nt(a, b):
    return lax.dot_general(a, b, (((1,), (1,)), ((), ())), preferred_element_type=F32)


def _seg_mean(v, seg):
    hi = v.astype(BF16)
    lo = (v - hi.astype(F32)).astype(BF16)
    return _dot(hi, seg) + _dot(lo, seg)


def _silu(x):
    return x * jax.nn.sigmoid(x)


def _modulated(x, mod_ref, g_ref, which, tile, tm, n_x, d):
    ms = jnp.mean(x * x, axis=-1, keepdims=True)
    xn = x * lax.rsqrt(ms + EPS)
    g = g_ref[...]
    sh, sc = 3 * which, 3 * which + 1
    a_x = g * (1.0 + mod_ref[0:1, sc * d:(sc + 1) * d])
    a_c = g * (1.0 + mod_ref[1:2, sc * d:(sc + 1) * d])
    b_x = mod_ref[0:1, sh * d:(sh + 1) * d]
    b_c = mod_ref[1:2, sh * d:(sh + 1) * d]
    row = tile * tm + lax.broadcasted_iota(jnp.int32, (tm, 1), 0)
    is_ctx = row >= n_x
    return xn * jnp.where(is_ctx, a_c, a_x) + jnp.where(is_ctx, b_c, b_x)


def _row_gate(mod_ref, idx, tile, tm, n_x, d):
    row = tile * tm + lax.broadcasted_iota(jnp.int32, (tm, 1), 0)
    return jnp.where(row >= n_x, mod_ref[1:2, idx * d:(idx + 1) * d], mod_ref[0:1, idx * d:(idx + 1) * d])


def _lane(shape):
    return lax.broadcasted_iota(jnp.int32, shape, len(shape) - 1)


ADA_ROWS = 2


def _ada_kernel(c_ref, w_ref, b_ref, o_ref):
    s_t = _silu(c_ref[...]).T
    w = w_ref[...]
    rows = [jnp.sum(w * s_t[:, r:r + 1], axis=0, keepdims=True) for r in range(ADA_ROWS)]
    pad = jnp.zeros((c_ref.shape[0] - ADA_ROWS, w.shape[1]), F32)
    o_ref[...] = jnp.concatenate(rows + [pad], axis=0) + b_ref[...]


def _ada(cvec8, w, b):
    d, n = w.shape
    tn = n // ADA_COLUMN_TILES
    return pl.pallas_call(
        _ada_kernel,
        grid=(n // tn,),
        in_specs=[pl.BlockSpec((8, d), lambda j: (0, 0)),
                  pl.BlockSpec((d, tn), lambda j: (0, j)),
                  pl.BlockSpec((1, tn), lambda j: (0, j))],
        out_specs=pl.BlockSpec((8, tn), lambda j: (0, j)),
        out_shape=jax.ShapeDtypeStruct((8, n), F32),
        compiler_params=_cparams(("arbitrary",)),
    )(cvec8, w, b.reshape(1, n))


def _rope128(v, c, s, half):
    lane = _lane(v.shape)
    swapped = jnp.where(lane % (2 * half) < half, pltpu.roll(v, LANES - half, 1), pltpu.roll(v, half, 1))
    return v * c + swapped * s


def _l0_proj_kernel(x_ref, mod_ref, g_ref, w_ref, seg_ref, gq_ref, gk_ref, c_ref, s_ref,
                    rq_ref, rk_ref, rv_ref, rg_ref, q_ref, k_ref, v_ref, *, tm, n_x, d):
    i = pl.program_id(0)
    h = _modulated(x_ref[...], mod_ref, g_ref, 0, i, tm, n_x, d).astype(BF16)
    rw = RET_HEADS * HEAD
    for idx, ref in enumerate((rq_ref, rk_ref, rv_ref, rg_ref)):
        ref[...] = _dot(h, w_ref[:, idx * rw:(idx + 1) * rw]).astype(BF16)
    seg = seg_ref[...]
    cos, sin = c_ref[...], s_ref[...]
    base = 4 * rw
    qw = GQA_HEADS * HEAD
    qa = _dot(h, w_ref[:, base:base + qw])
    for g in range(qw // LANES):
        v = qa[:, g * LANES:(g + 1) * LANES]
        vn = v * lax.rsqrt(_seg_mean(v * v, seg) + EPS) * gq_ref[...]
        q_ref[:, g * LANES:(g + 1) * LANES] = _rope128(vn, cos, sin, HEAD // 2).astype(BF16)
    kv = _dot(h, w_ref[:, base + qw:base + qw + 2 * LANES])
    kk = kv[:, :LANES]
    kk = kk * lax.rsqrt(_seg_mean(kk * kk, seg) + EPS) * gk_ref[...]
    kk = _rope128(kk, cos, sin, HEAD // 2)
    vv = kv[:, LANES:]
    lane = _lane(kk.shape)
    low = lane < HEAD
    for src, ref, one in ((kk, k_ref, 0.0), (vv, v_ref, 1.0)):
        sw = pltpu.roll(src, HEAD, 1)
        lo_fill = jnp.where(lane == LOW_ONE, one, 0.0)
        hi_fill = jnp.where(lane == HIGH_ONE, one, 0.0)
        ref[:, 0 * LANES:1 * LANES] = jnp.where(low, src, lo_fill).astype(BF16)
        ref[:, 1 * LANES:2 * LANES] = jnp.where(low, hi_fill, sw).astype(BF16)
        ref[:, 2 * LANES:3 * LANES] = jnp.where(low, sw, lo_fill).astype(BF16)
        ref[:, 3 * LANES:4 * LANES] = jnp.where(low, hi_fill, src).astype(BF16)


def _l0_proj(xa, mod, g, w, seg, gq, gk, cos, sin, *, tm, n_x):
    t, d = xa.shape
    rw = RET_HEADS * HEAD
    row = lambda i: (i, 0)
    outs = [jax.ShapeDtypeStruct((t, rw), BF16)] * 7
    return pl.pallas_call(
        functools.partial(_l0_proj_kernel, tm=tm, n_x=n_x, d=d),
        grid=(t // tm,),
        in_specs=[pl.BlockSpec((tm, d), row), _resident(mod.shape), _resident(g.shape), _resident(w.shape),
                  _resident(seg.shape), _resident(gq.shape), _resident(gk.shape),
                  pl.BlockSpec((tm, LANES), row), pl.BlockSpec((tm, LANES), row)],
        out_specs=[pl.BlockSpec((tm, rw), row)] * 7,
        out_shape=outs,
        compiler_params=_cparams(("parallel",)),
    )(xa, mod, g, w, seg, gq, gk, cos, sin)


def _retention_kernel(lg_ref, qf_ref, kf_ref, vf_ref, qb_ref, kb_ref, vb_ref, of_ref, ob_ref,
                      state_ref, decay_ref, xi_ref, zeta_ref, gl_ref):
    c = RET_CHUNK
    npairs = RET_HEADS * HEAD // LANES
    step = pl.program_id(0)

    @pl.when(step == 0)
    def _init():
        state_ref[...] = jnp.zeros_like(state_ref)
        ci = lax.broadcasted_iota(jnp.int32, (c, c), 0).astype(F32)
        mi = lax.broadcasted_iota(jnp.int32, (c, c), 1).astype(F32)
        pos = lax.broadcasted_iota(jnp.int32, (c, RET_HEADS * HEAD), 0).astype(F32)
        lane_head = _lane((1, RET_HEADS * HEAD)) // HEAD
        for dr in range(2):
            lgv = jnp.zeros((1, RET_HEADS * HEAD), F32)
            for hd in range(RET_HEADS):
                lg = lg_ref[dr, hd]
                rel = (ci - mi) if dr == 0 else (mi - ci)
                half = slice((hd % 2) * c, (hd % 2 + 1) * c)
                decay_ref[dr, hd // 2, :, half] = jnp.where(rel >= 0, jnp.exp(jnp.maximum(rel, 0.0) * lg), 0.0)
                lgv = jnp.where(lane_head == hd, lg, lgv)
            p = pos if dr == 0 else (c - 1.0 - pos)
            xi_ref[dr] = jnp.exp((p + 1.0) * lgv)
            zeta_ref[dr] = jnp.exp((c - 1.0 - p) * lgv)
            gl_ref[dr] = jnp.exp(float(c) * lgv)

    low = _lane((c, LANES)) < HEAD
    r_i = lax.broadcasted_iota(jnp.int32, (LANES, LANES), 0) // HEAD
    c_i = lax.broadcasted_iota(jnp.int32, (LANES, LANES), 1) // HEAD
    blockdiag = r_i == c_i
    n_sub = qf_ref.shape[0] // c
    for dr, (q_ref, k_ref, v_ref, o_ref) in enumerate(((qf_ref, kf_ref, vf_ref, of_ref),
                                                       (qb_ref, kb_ref, vb_ref, ob_ref))):
        for j in range(npairs):
            sl = slice(j * LANES, (j + 1) * LANES)
            st = state_ref[dr, j]
            for sub in (range(n_sub) if dr == 0 else reversed(range(n_sub))):
                rows = slice(sub * c, (sub + 1) * c)
                q, k, v = q_ref[rows, sl], k_ref[rows, sl], v_ref[rows, sl]
                zero = jnp.zeros_like(k)
                k2 = jnp.concatenate([jnp.where(low, k, zero), jnp.where(low, zero, k)], axis=0)
                v2 = jnp.concatenate([jnp.where(low, v, zero), jnp.where(low, zero, v)], axis=0)
                s = _dot_nt(q, k2) * decay_ref[dr, j]
                o = _dot(s.astype(BF16), v2)
                qx = (q.astype(F32) * xi_ref[dr, :, sl]).astype(BF16)
                o = o + _dot(qx, st.astype(BF16))
                o_ref[rows, sl] = o.astype(BF16)
                kz = (k.astype(F32) * zeta_ref[dr, :, sl]).T.astype(BF16)
                u = _dot(kz, v)
                st = st * gl_ref[dr, :, sl] + jnp.where(blockdiag, u, 0.0)
            state_ref[dr, j] = st


def _retention(lg, rq, rk, rv, *, n_x):
    t, w = rq.shape
    c = RET_CHUNK
    rows = RET_STEP_CHUNKS * c
    assert n_x % rows == 0 and t % rows == 0
    nc, ncx = t // rows, n_x // rows
    fwd = lambda i: ((i + ncx) % nc, 0)
    bwd = lambda i: (nc - 1 - i, 0)
    blk = lambda m: pl.BlockSpec((rows, w), m)
    npairs = w // LANES
    return pl.pallas_call(
        _retention_kernel,
        grid=(nc,),
        in_specs=[pl.BlockSpec(memory_space=pltpu.SMEM)] + [blk(fwd)] * 3 + [blk(bwd)] * 3,
        out_specs=[blk(fwd), blk(bwd)],
        out_shape=[jax.ShapeDtypeStruct((t, w), BF16)] * 2,
        scratch_shapes=[pltpu.VMEM((2, npairs, LANES, LANES), F32),
                        pltpu.VMEM((2, npairs, c, 2 * c), F32),
                        pltpu.VMEM((2, c, w), F32), pltpu.VMEM((2, c, w), F32),
                        pltpu.VMEM((2, 1, w), F32)],
        compiler_params=_cparams(("arbitrary",)),
    )(lg, rq, rk, rv, rq, rk, rv)


def _flash_kernel(q0_ref, q1_ref, k0_ref, k1_ref, v0_ref, v1_ref, *rest, bk, nkv, rs, n_side):
    side_in, o_ref, side_out = rest[:n_side], rest[n_side], rest[n_side + 1:2 * n_side + 1]
    s_ref, p_ref, a_ref, m_ref, acc_ref = rest[2 * n_side + 1:]
    for src_ref, dst_ref in zip(side_in, side_out):
        dst_ref[...] = src_ref[...].astype(dst_ref.dtype)
    bq = q0_ref.shape[0]
    q_refs, k_refs, v_refs = (q0_ref, q1_ref), (k0_ref, k1_ref), (v0_ref, v1_ref)
    m_ref[...] = jnp.full(m_ref.shape, -jnp.inf, F32)
    acc_ref[...] = jnp.zeros(acc_ref.shape, F32)

    def keys(t):
        return pl.ds(t * bk if isinstance(t, int) else pl.multiple_of(t * bk, bk), bk)

    def scores(t, slot):
        for h in range(2):
            s_ref[slot, h] = _dot_nt(q_refs[h][...], k_refs[h][keys(t), :])

    def softmax(slot):
        col = lambda c: slice(c * LANES, (c + 1) * LANES)
        for r in range(bq // rs):
            rows = slice(r * rs, (r + 1) * rs)
            for h in range(2):
                mx = s_ref[slot, h, rows, col(0)]
                for c in range(1, bk // LANES):
                    mx = jnp.maximum(mx, s_ref[slot, h, rows, col(c)])
                m_old = m_ref[h, rows, :]
                m_new = jnp.maximum(m_old, jnp.max(mx, axis=-1, keepdims=True))
                a_ref[slot, h, rows, :] = jnp.exp2(m_old - m_new)
                m_ref[h, rows, :] = m_new
                for c in range(bk // LANES):
                    p_ref[slot, h, rows, col(c)] = jnp.exp2(s_ref[slot, h, rows, col(c)] - m_new).astype(BF16)

    def values(t, slot):
        for h in range(2):
            acc_ref[h] = acc_ref[h] * a_ref[slot, h] + _dot(p_ref[slot, h], v_refs[h][keys(t), :])

    scores(0, 0)

    def body(i, carry):
        t = 2 * i
        scores(t + 1, 1)
        softmax(0)
        values(t, 0)
        scores(t + 2, 0)
        softmax(1)
        values(t + 1, 1)
        return carry

    n_loop = (nkv - 1) // 2
    lax.fori_loop(0, n_loop, body, 0)
    last = 2 * n_loop
    if last + 1 < nkv:
        scores(last + 1, 1)
    softmax(0)
    values(last, 0)
    if last + 1 < nkv:
        softmax(1)
        values(last + 1, 1)
    low = _lane((bq, LANES)) < HEAD
    acc0, acc1 = acc_ref[0], acc_ref[1]
    out = jnp.where(low, acc0 / acc0[:, LOW_ONE:LOW_ONE + 1], acc1 / acc1[:, HIGH_ONE:HIGH_ONE + 1])
    o_ref[...] = out.astype(o_ref.dtype)


def _flash(q, kmat, vmat, *, q_maps, k_maps, v_maps, n_q, q_row0, kv_row0, n_kv, n_pairs, bq, bk, rs, side=()):
    assert q_row0 % bq == 0 and n_q % bq == 0 and n_kv % bk == 0 and kv_row0 % n_kv == 0 and bq % rs == 0
    qb0, kb0 = q_row0 // bq, kv_row0 // n_kv
    n_i = n_q // bq
    steps = n_pairs * n_i
    assert all(a.shape[0] % (16 * steps) == 0 for a in side)
    qspec = lambda m: pl.BlockSpec((bq, LANES), lambda j, i: (i + qb0, m(j)))
    kspec = lambda m: pl.BlockSpec((n_kv, LANES), lambda j, i: (kb0, m(j)), pipeline_mode=pl.Buffered(1))
    side_specs = [pl.BlockSpec((a.shape[0] // steps, a.shape[1]), lambda j, i: (j * n_i + i, 0)) for a in side]
    outs = pl.pallas_call(
        functools.partial(_flash_kernel, bk=bk, nkv=n_kv // bk, rs=rs, n_side=len(side)),
        grid=(n_pairs, n_i),
        in_specs=[qspec(q_maps[0]), qspec(q_maps[1]), kspec(k_maps[0]), kspec(k_maps[1]),
                  kspec(v_maps[0]), kspec(v_maps[1])] + side_specs,
        out_specs=[pl.BlockSpec((bq, LANES), lambda j, i: (i, j))] + side_specs,
        out_shape=[jax.ShapeDtypeStruct((n_q, n_pairs * LANES), BF16)]
                  + [jax.ShapeDtypeStruct(a.shape, BF16) for a in side],
        scratch_shapes=[pltpu.VMEM((2, 2, bq, bk), F32), pltpu.VMEM((2, 2, bq, bk), BF16),
                        pltpu.VMEM((2, 2, bq, LANES), F32), pltpu.VMEM((2, bq, LANES), F32),
                        pltpu.VMEM((2, bq, LANES), F32)],
        compiler_params=_cparams(("parallel", "parallel")),
    )(q, q, kmat, kmat, vmat[0], vmat[1], *side)
    return outs if side else outs[0]


def _kv_block(n_kv):
    return _first_divisor(n_kv, FLASH_KEY_BLOCKS, "key count")


def _l0_out_kernel(x_ref, mod_ref, of_ref, ob_ref, rg_ref, ao_ref, seg_ref, wo_ref, o_ref, *, tm, n_x, d):
    i = pl.program_id(0)
    seg = seg_ref[...]
    rw = RET_HEADS * HEAD
    acc = _dot(ao_ref[...], wo_ref[rw:, :])
    for g in range(rw // LANES):
        sl = slice(g * LANES, (g + 1) * LANES)
        o = of_ref[:, sl].astype(F32) + ob_ref[:, sl].astype(F32)
        dv = o - _seg_mean(o, seg)
        nrm = dv * lax.rsqrt(_seg_mean(dv * dv, seg) + EPS)
        ra = (nrm * _silu(rg_ref[:, sl].astype(F32))).astype(BF16)
        acc = acc + _dot(ra, wo_ref[g * LANES:(g + 1) * LANES, :])
    o_ref[...] = x_ref[...] + _row_gate(mod_ref, 2, i, tm, n_x, d) * acc


def _l0_out(xa, mod, o_f, o_b, rg, ao, seg, wo, *, tm, n_x):
    t, d = xa.shape
    rw = o_f.shape[1]
    row = lambda i: (i, 0)
    return pl.pallas_call(
        functools.partial(_l0_out_kernel, tm=tm, n_x=n_x, d=d),
        grid=(t // tm,),
        in_specs=[pl.BlockSpec((tm, d), row), _resident(mod.shape)] + [pl.BlockSpec((tm, rw), row)] * 4
                 + [_resident(seg.shape), _resident(wo.shape)],
        out_specs=pl.BlockSpec((tm, d), row),
        out_shape=jax.ShapeDtypeStruct((t, d), F32),
        compiler_params=_cparams(("parallel",)),
    )(xa, mod, o_f, o_b, rg, ao, seg, wo)


def _ffn_kernel(x_ref, mod_ref, g_ref, w1_ref, w3_ref, w2_ref, o_ref, *, tm, n_x, d):
    i = pl.program_id(0)
    x = x_ref[...]
    h = _modulated(x, mod_ref, g_ref, 1, i, tm, n_x, d).astype(BF16)
    a = _dot(h, w1_ref[...])
    u = (_silu(a) * _dot(h, w3_ref[...])).astype(BF16)
    o_ref[...] = x + _row_gate(mod_ref, 5, i, tm, n_x, d) * _dot(u, w2_ref[...])


def _ffn(xa, mod, g, w1, w3, w2, *, tm, n_x):
    t, d = xa.shape
    row = lambda i: (i, 0)
    return pl.pallas_call(
        functools.partial(_ffn_kernel, tm=tm, n_x=n_x, d=d),
        grid=(t // tm,),
        in_specs=[pl.BlockSpec((tm, d), row), _resident(mod.shape), _resident(g.shape),
                  _resident(w1.shape), _resident(w3.shape), _resident(w2.shape)],
        out_specs=pl.BlockSpec((tm, d), row),
        out_shape=jax.ShapeDtypeStruct((t, d), F32),
        compiler_params=_cparams(("parallel",)),
    )(xa, mod, g, w1, w3, w2)


def _l1_proj_kernel(x_ref, mod_ref, g_ref, wq_ref, wkv_ref, wkr_ref, gql_ref, gkvl_ref, wuq_ref, wuk_ref,
                    wuv_ref, seg_ref, gq_ref, gk_ref, gkr_ref, c_ref, s_ref,
                    q_ref, k_ref, vlo_ref, vhi_ref, *, tm, n_x, d):
    i = pl.program_id(0)
    h = _modulated(x_ref[...], mod_ref, g_ref, 0, i, tm, n_x, d).astype(BF16)
    seg = seg_ref[...]
    cos, sin = c_ref[...], s_ref[...]

    def lora_norm(v, g):
        return (v * lax.rsqrt(jnp.mean(v * v, axis=-1, keepdims=True) + EPS) * g).astype(BF16)

    cq = lora_norm(_dot(h, wq_ref[...]), gql_ref[...])
    ckv = lora_norm(_dot(h, wkv_ref[...]), gkvl_ref[...])
    kr = _dot(h, wkr_ref[...])
    kr = kr * lax.rsqrt(_seg_mean(kr * kr, seg) + EPS) * gkr_ref[...]
    kr = _rope128(kr, cos, sin, MLA_ROPE // 2)
    qa = _dot(cq, wuq_ref[...])
    ka = _dot(ckv, wuk_ref[...])
    for hd in range(MLA_HEADS):
        sl = slice(hd * LANES, (hd + 1) * LANES)
        v = qa[:, sl]
        vn = v * lax.rsqrt(_seg_mean(v * v, seg) + EPS) * gq_ref[...]
        q_ref[:, sl] = _rope128(vn, cos, sin, MLA_ROPE // 2).astype(BF16)
        v = ka[:, sl]
        k_ref[:, sl] = (v * lax.rsqrt(_seg_mean(v * v, seg) + EPS) * gk_ref[...] + kr).astype(BF16)
    va = _dot(ckv, wuv_ref[...])
    lane = _lane(va.shape) % LANES
    low = lane < HEAD
    vlo_ref[...] = jnp.where(low, va, jnp.where(lane == LOW_ONE, 1.0, 0.0)).astype(BF16)
    vhi_ref[...] = jnp.where(low, jnp.where(lane == HIGH_ONE, 1.0, 0.0), va).astype(BF16)


def _l1_proj(xa, mod, g, wq, wkv, wkr, gql, gkvl, wuq, wuk, wuv, seg, gq, gk, gkr, cos, sin, *, tm, n_x):
    t, d = xa.shape
    row = lambda i: (i, 0)
    hw = MLA_HEADS * LANES
    vw = MLA_HEADS * HEAD
    consts = (mod, g, wq, wkv, wkr, gql, gkvl, wuq, wuk, wuv, seg, gq, gk, gkr)
    return pl.pallas_call(
        functools.partial(_l1_proj_kernel, tm=tm, n_x=n_x, d=d),
        grid=(t // tm,),
        in_specs=[pl.BlockSpec((tm, d), row)] + [_resident(a.shape) for a in consts]
                 + [pl.BlockSpec((tm, LANES), row)] * 2,
        out_specs=[pl.BlockSpec((tm, hw), row), pl.BlockSpec((tm, hw), row),
                   pl.BlockSpec((tm, vw), row), pl.BlockSpec((tm, vw), row)],
        out_shape=[jax.ShapeDtypeStruct((t, hw), BF16), jax.ShapeDtypeStruct((t, hw), BF16),
                   jax.ShapeDtypeStruct((t, vw), BF16), jax.ShapeDtypeStruct((t, vw), BF16)],
        compiler_params=_cparams(("parallel",)),
    )(xa, *consts, cos, sin)


def _l1_out_kernel(x_ref, mod_ref, g_ref, o_ref, wo_ref, rhi_ref, rlo_ref, x3_ref, h_ref, ei_ref, ew_ref, *, d):
    x3 = x_ref[...] + mod_ref[0:1, 2 * d:3 * d] * _dot(o_ref[...], wo_ref[...])
    x3_ref[...] = x3
    ms = jnp.mean(x3 * x3, axis=-1, keepdims=True)
    h = x3 * lax.rsqrt(ms + EPS) * (g_ref[...] * (1.0 + mod_ref[0:1, 4 * d:5 * d])) + mod_ref[0:1, 3 * d:4 * d]
    hi = h.astype(BF16)
    bits = lax.bitcast_convert_type(hi.astype(F32), jnp.uint32)
    words = (bits[:, :d // 2] >> 16) | (bits[:, d // 2:] & jnp.uint32(0xFFFF0000))
    h_ref[...] = words
    lo = (h - hi.astype(F32)).astype(BF16)
    logits = _dot(hi, rhi_ref[...]) + (_dot(hi, rlo_ref[...]) + _dot(lo, rhi_ref[...]))
    lane_i = _lane(logits.shape)
    lane = lane_i.astype(F32)
    logits = jnp.where(lane_i < N_EXPERTS, logits, -jnp.inf)
    v1 = jnp.max(logits, axis=-1, keepdims=True)
    i1 = jnp.min(jnp.where(logits == v1, lane, float(LANES)), axis=-1, keepdims=True)
    rest = jnp.where(lane == i1, -jnp.inf, logits)
    v2 = jnp.max(rest, axis=-1, keepdims=True)
    i2 = jnp.min(jnp.where(rest == v2, lane, float(LANES)), axis=-1, keepdims=True)
    e2 = jnp.exp(v2 - v1)
    den = 1.0 + e2
    ei_ref[...] = jnp.where(lane_i == 0, i1, jnp.where(lane_i == 1, i2, 0.0)).astype(jnp.int32)
    ew_ref[...] = jnp.where(lane_i == 0, 1.0 / den, jnp.where(lane_i == 1, e2 / den, 0.0))


def _l1_out(xa, mod, g, o, wo, rhi, rlo, *, tm):
    n, d = o.shape[0], xa.shape[1]
    row = lambda i: (i, 0)
    return pl.pallas_call(
        functools.partial(_l1_out_kernel, d=d),
        grid=(n // tm,),
        in_specs=[pl.BlockSpec((tm, d), row), _resident(mod.shape), _resident(g.shape),
                  pl.BlockSpec((tm, o.shape[1]), row), _resident(wo.shape), _resident(rhi.shape),
                  _resident(rlo.shape)],
        out_specs=[pl.BlockSpec((tm, d), row), pl.BlockSpec((tm, d // 2), row),
                   pl.BlockSpec((tm, LANES), row), pl.BlockSpec((tm, LANES), row)],
        out_shape=[jax.ShapeDtypeStruct((n, d), F32), jax.ShapeDtypeStruct((n, d // 2), jnp.uint32),
                   jax.ShapeDtypeStruct((n, LANES), jnp.int32), jax.ShapeDtypeStruct((n, LANES), F32)],
        compiler_params=_cparams(("parallel",)),
    )(xa, mod, g, o, wo, rhi, rlo)


def _moe_kernel(te_ref, nv_ref, x_ref, w1_ref, w3_ref, w2_ref, y_ref, acc_ref, *, nf):
    i, f = pl.program_id(0), pl.program_id(1)

    @pl.when(f == 0)
    def _zero():
        acc_ref[...] = jnp.zeros_like(acc_ref)

    @pl.when(i < nv_ref[0])
    def _compute():
        words = x_ref[...]
        lo = lax.bitcast_convert_type(words << 16, F32)
        hi = lax.bitcast_convert_type(words & jnp.uint32(0xFFFF0000), F32)
        x = jnp.concatenate([lo, hi], axis=1).astype(BF16)
        a = _dot(x, w1_ref[0])
        u = (_silu(a) * _dot(x, w3_ref[0])).astype(BF16)
        acc_ref[...] += _dot(u, w2_ref[0])

    @pl.when(f == nf - 1)
    def _store():
        y_ref[...] = acc_ref[...].astype(y_ref.dtype)


def _moe(tile_expert, n_valid, xs, w1, w3, w2, *, tm, tf):
    p, d = xs.shape[0], 2 * xs.shape[1]
    fdim = w1.shape[2]
    nf = fdim // tf
    fi = lambda i, f, te, nv: jnp.where(i < nv[0], f, nf - 1)
    grid_spec = pltpu.PrefetchScalarGridSpec(
        num_scalar_prefetch=2,
        grid=(p // tm, nf),
        in_specs=[pl.BlockSpec((tm, d // 2), lambda i, f, te, nv: (i, 0)),
                  pl.BlockSpec((1, d, tf), lambda i, f, te, nv: (te[i], 0, fi(i, f, te, nv))),
                  pl.BlockSpec((1, d, tf), lambda i, f, te, nv: (te[i], 0, fi(i, f, te, nv))),
                  pl.BlockSpec((1, tf, d), lambda i, f, te, nv: (te[i], fi(i, f, te, nv), 0))],
        out_specs=pl.BlockSpec((tm, d), lambda i, f, te, nv: (i, 0)),
        scratch_shapes=[pltpu.VMEM((tm, d), F32)],
    )
    return pl.pallas_call(
        functools.partial(_moe_kernel, nf=nf),
        grid_spec=grid_spec,
        out_shape=jax.ShapeDtypeStruct((p, d), BF16),
        compiler_params=_cparams(("arbitrary", "arbitrary")),
    )(tile_expert, n_valid, xs, w1, w3, w2)


SC_GATHER_WINDOW = 128
SC_LANES = 16


def _gather_rows(x, idx):
    n, d = idx.shape[0], x.shape[1]
    w = SC_GATHER_WINDOW
    assert n % w == 0
    mesh = plsc.VectorSubcoreMesh(core_axis_name="core", subcore_axis_name="subcore")

    @pl.kernel(out_type=jax.ShapeDtypeStruct((n, d), x.dtype), mesh=mesh,
               scratch_types=[pltpu.SemaphoreType.DMA])
    def gather_kernel(x_hbm, i_hbm, o_hbm, sem):
        def body(i_vmem, o_vmem):
            copies = []
            for k in range(w // SC_LANES):
                grp = pl.ds(k * SC_LANES, SC_LANES)
                copies.append(pltpu.async_copy(x_hbm.at[i_vmem[0, grp]], o_vmem.at[grp], sem))
            for cp in copies:
                cp.wait()

        pltpu.emit_pipeline(
            body,
            grid=(n // w,),
            in_specs=[pl.BlockSpec((1, w), lambda i: (0, i))],
            out_specs=[pl.BlockSpec((w, d), lambda i: (i, 0), pipeline_mode=pl.Buffered(1))],
            core_axis_name=("core", "subcore"),
            dimension_semantics=(pltpu.PARALLEL,),
        )(i_hbm, o_hbm)

    return gather_kernel(x, idx.reshape(1, n))


def _combine_kernel(x_ref, mod_ref, ew_ref, ya_ref, yb_ref, o_ref, *, d):
    ew = ew_ref[...]
    y = ew[:, 0:1] * ya_ref[...].astype(F32) + ew[:, 1:2] * yb_ref[...].astype(F32)
    o_ref[...] = x_ref[...] + mod_ref[0:1, 5 * d:6 * d] * y


def _combine(x3, mod, ew, ya, yb, *, tm):
    n, d = x3.shape
    row = lambda i: (i, 0)
    return pl.pallas_call(
        functools.partial(_combine_kernel, d=d),
        grid=(n // tm,),
        in_specs=[pl.BlockSpec((tm, d), row), _resident(mod.shape), pl.BlockSpec((tm, LANES), row),
                  pl.BlockSpec((tm, d), row), pl.BlockSpec((tm, d), row)],
        out_specs=pl.BlockSpec((tm, d), row),
        out_shape=jax.ShapeDtypeStruct((n, d), F32),
        compiler_params=_cparams(("parallel",)),
    )(x3, mod, ew, ya, yb)


def _rope_tables(n_x, n_ctx, rot_dim, seg_start, seg_repeat):
    f32 = np.float32
    rows = n_x // GRID_W
    row = np.repeat(np.arange(rows, dtype=f32), GRID_W)
    col = np.tile(np.arange(GRID_W, dtype=f32), rows)
    axis_dim = rot_dim // 2
    inv_freq = f32(ROPE_THETA) ** (-np.arange(0, axis_dim, 2, dtype=f32) / f32(axis_dim))
    ang = np.concatenate([row[:, None] * inv_freq, col[:, None] * inv_freq], axis=-1)
    cos, sin = np.cos(ang).astype(f32), np.sin(ang).astype(f32)
    c = np.ones((n_x + n_ctx, LANES), f32)
    s = np.zeros((n_x + n_ctx, LANES), f32)
    for r in range(seg_repeat):
        lo = seg_start + r * rot_dim
        c[:n_x, lo:lo + rot_dim] = np.concatenate([cos, cos], axis=-1)
        s[:n_x, lo:lo + rot_dim] = np.concatenate([-sin, sin], axis=-1)
    return jnp.asarray(c), jnp.asarray(s)


def _segment_matrix(bounds):
    m = np.zeros((LANES, LANES), np.float32)
    for lo, hi in bounds:
        m[lo:hi, lo:hi] = 1.0 / (hi - lo)
    return jnp.asarray(m, BF16)


def _token_tile(t):
    return _first_divisor(t, TOKEN_TILES, "token count")


def kernel(x, c, ctx, c_ctx, l0_ada_w, l0_ada_b, l0_norm1_g, l0_norm2_g, l0_w_in, l0_ret_log_decay, l0_q_norm_g, l0_k_norm_g, l0_w_out, l0_ffn_w1, l0_ffn_w3, l0_ffn_w2, l1_ada_w, l1_ada_b, l1_norm1_g, l1_norm2_g, l1_w_in, l1_q_lora_g, l1_kv_lora_g, l1_w_uq, l1_w_ukv, l1_q_nope_g, l1_q_rope_g, l1_k_nope_g, l1_k_rope_g, l1_w_out, l1_router, l1_exp_w1, l1_exp_w3, l1_exp_w2):
    b, n_x, d = x.shape
    n_ctx = ctx.shape[1]
    assert b == 1 and n_x % 256 == 0 and n_ctx % 256 == 0 and n_x % GRID_W == 0
    t = n_x + n_ctx
    tm = _token_tile(t)
    tmx = _token_tile(n_x)
    xa = jnp.concatenate([x[0], ctx[0]], axis=0)
    row1 = lambda v: v.reshape(1, -1).astype(F32)

    cvec = jnp.zeros((8, d), F32).at[0].set(c[0]).at[1].set(c_ctx)
    mod0 = _ada(cvec, l0_ada_w, l0_ada_b)
    mod1 = _ada(cvec, l1_ada_w, l1_ada_b)

    rw = RET_HEADS * HEAD
    n_qk = GQA_HEADS + GQA_KV_HEADS
    qk_cols = l0_w_in[:, 4 * rw:4 * rw + n_qk * HEAD].reshape(d, n_qk, HEAD // 2, 2)
    qk_cols = jnp.swapaxes(qk_cols, 2, 3).reshape(d, n_qk * HEAD)
    w_in0 = jnp.concatenate([l0_w_in[:, :rw], l0_w_in[:, rw:2 * rw] * (HEAD ** -0.5), l0_w_in[:, 2 * rw:4 * rw],
                             qk_cols, l0_w_in[:, 4 * rw + n_qk * HEAD:]], axis=1).astype(BF16)
    seg64 = _segment_matrix([(0, HEAD), (HEAD, 2 * HEAD)])
    halves = lambda g: jnp.swapaxes(g.reshape(HEAD // 2, 2), 0, 1).reshape(HEAD)
    gq0 = row1(jnp.tile(halves(l0_q_norm_g), 2) * (HEAD ** -0.5 * LOG2E))
    gk0 = row1(jnp.tile(halves(l0_k_norm_g), 2))
    cos0, sin0 = _rope_tables(n_x, n_ctx, HEAD, 0, 2)

    rq, rk, rv, rg, gq, gkx, gvx = _l0_proj(xa, mod0, row1(l0_norm1_g), w_in0, seg64, gq0, gk0, cos0, sin0,
                                             tm=tm, n_x=n_x)
    o_f, o_b = _retention(l0_ret_log_decay.astype(F32), rq, rk, rv, n_x=n_x)

    gqa_maps = dict(q_maps=(lambda j: j, lambda j: j),
                    k_maps=(lambda j: 2 * (j // 2), lambda j: 2 * (j // 2) + 1),
                    v_maps=(lambda j: 2 * (j // 2), lambda j: 2 * (j // 2) + 1), n_pairs=GQA_HEADS // 2)
    bq = _first_divisor(n_x, FLASH_Q_TILES, "query count")
    rs = FLASH_ROW_GROUP
    ao_x = _flash(gq, gkx, (gvx, gvx), n_q=n_x, q_row0=0, kv_row0=0, n_kv=t, bq=bq, bk=_kv_block(t), rs=rs,
                  **gqa_maps)
    ao_c = _flash(gq, gkx, (gvx, gvx), n_q=n_ctx, q_row0=n_x, kv_row0=n_x, n_kv=n_ctx, bq=n_ctx,
                  bk=_kv_block(n_ctx), rs=rs, **gqa_maps)
    ao = jnp.concatenate([ao_x, ao_c], axis=0)

    xa = _l0_out(xa, mod0, o_f, o_b, rg, ao, seg64, l0_w_out.astype(BF16), tm=tm, n_x=n_x)
    xa = _ffn(xa, mod0, row1(l0_norm2_g), l0_ffn_w1.astype(BF16), l0_ffn_w3.astype(BF16),
              l0_ffn_w2.astype(BF16), tm=tm, n_x=n_x)

    qk_w = MLA_NOPE + MLA_ROPE
    pad_w = LANES - qk_w

    def even_odd(a):
        return jnp.swapaxes(a.reshape(a.shape[:-1] + (a.shape[-1] // 2, 2)), -1, -2).reshape(a.shape)

    uq = l1_w_uq.reshape(MLA_Q_RANK, MLA_HEADS, qk_w)
    wuq = jnp.concatenate([uq[..., :MLA_NOPE], even_odd(uq[..., MLA_NOPE:]),
                           jnp.zeros((MLA_Q_RANK, MLA_HEADS, pad_w), F32)], axis=-1).reshape(MLA_Q_RANK, -1)
    ukv = l1_w_ukv.reshape(MLA_KV_RANK, MLA_HEADS, 2 * HEAD)
    wuk = jnp.concatenate([ukv[..., :MLA_NOPE], jnp.zeros((MLA_KV_RANK, MLA_HEADS, LANES - MLA_NOPE), F32)],
                          axis=-1).reshape(MLA_KV_RANK, -1)
    wuv = ukv[..., MLA_NOPE:].reshape(MLA_KV_RANK, -1)
    wkr = jnp.concatenate([jnp.zeros((d, MLA_NOPE), F32), even_odd(l1_w_in[:, MLA_Q_RANK + MLA_KV_RANK:]),
                           jnp.zeros((d, pad_w), F32)], axis=-1)
    pad = jnp.zeros((pad_w,), F32)
    zn = jnp.zeros((MLA_NOPE,), F32)
    gq1 = row1(jnp.concatenate([l1_q_nope_g, even_odd(l1_q_rope_g), pad]) * (qk_w ** -0.5 * LOG2E))
    gk1 = row1(jnp.concatenate([l1_k_nope_g, jnp.zeros((LANES - MLA_NOPE,), F32)]))
    gkr1 = row1(jnp.concatenate([zn, even_odd(l1_k_rope_g), pad]))
    seg_mla = _segment_matrix([(0, MLA_NOPE), (MLA_NOPE, qk_w)])
    cos1, sin1 = _rope_tables(n_x, n_ctx, MLA_ROPE, MLA_NOPE, 1)

    mq, mk, mvlo, mvhi = _l1_proj(
        xa, mod1, row1(l1_norm1_g), l1_w_in[:, :MLA_Q_RANK].astype(BF16),
        l1_w_in[:, MLA_Q_RANK:MLA_Q_RANK + MLA_KV_RANK].astype(BF16), wkr.astype(BF16),
        row1(l1_q_lora_g), row1(l1_kv_lora_g), wuq.astype(BF16), wuk.astype(BF16), wuv.astype(BF16),
        seg_mla, gq1, gk1, gkr1, cos1, sin1, tm=tm, n_x=n_x)
    n_e, _, fdim = l1_exp_w1.shape
    mo, ew1, ew3, ew2 = _flash(
        mq, mk, (mvlo, mvhi), q_maps=(lambda j: 2 * j, lambda j: 2 * j + 1),
        k_maps=(lambda j: 2 * j, lambda j: 2 * j + 1), v_maps=(lambda j: j, lambda j: j),
        n_pairs=MLA_HEADS // 2, n_q=n_x, q_row0=0, kv_row0=0, n_kv=t, bq=bq, bk=_kv_block(t), rs=rs,
        side=(l1_exp_w1.reshape(n_e * d, fdim), l1_exp_w3.reshape(n_e * d, fdim),
              l1_exp_w2.reshape(n_e * fdim, d)))
    ew1, ew3, ew2 = ew1.reshape(n_e, d, fdim), ew3.reshape(n_e, d, fdim), ew2.reshape(n_e, fdim, d)

    router = jnp.zeros((d, LANES), F32).at[:, :N_EXPERTS].set(l1_router)
    r_hi = router.astype(BF16)
    r_lo = (router - r_hi.astype(F32)).astype(BF16)
    x3, hmoe, ei, ew = _l1_out(xa, mod1, row1(l1_norm2_g), mo, l1_w_out.astype(BF16), r_hi, r_lo, tm=tmx)

    tme = MOE_TILE_ROWS if n_x >= MOE_SMALL_BELOW else MOE_SMALL_TILE_ROWS
    e_flat = ei[:, :TOP_K].reshape(-1)
    onehot = (e_flat[:, None] == jnp.arange(N_EXPERTS)[None, :]).astype(jnp.int32)
    csum = jnp.cumsum(onehot, axis=0)
    rank = jnp.sum((csum - onehot) * onehot, axis=1)
    counts = csum[-1]
    padded = ((counts + tme - 1) // tme) * tme
    ends = jnp.cumsum(padded)
    pos = (ends - padded)[e_flat] + rank
    n_tiles = (TOP_K * n_x) // tme + N_EXPERTS
    p_rows = n_tiles * tme
    n_valid = (ends[-1] // tme).astype(jnp.int32)
    tile_ids = jnp.minimum(jnp.arange(n_tiles, dtype=jnp.int32), n_valid - 1)
    tile_expert = jnp.sum((ends[None, :] <= (tile_ids * tme)[:, None]).astype(jnp.int32), axis=1)
    tile_expert = jnp.minimum(tile_expert, N_EXPERTS - 1)
    by_expert = jnp.argsort(e_flat, stable=True).astype(jnp.int32) // TOP_K
    slot = jnp.arange(p_rows, dtype=jnp.int32)
    slot_e = jnp.repeat(tile_expert, tme)
    slot_rank = slot - (ends - padded)[slot_e]
    listed = jnp.take(by_expert, (jnp.cumsum(counts) - counts)[slot_e] + slot_rank, mode="clip")
    src = jnp.where((slot_rank < counts[slot_e]) & (slot < ends[-1]), listed, slot % n_x)
    xs = _gather_rows(hmoe, src)
    tf = fdim // MOE_F_BLOCKS if (fdim // MOE_F_BLOCKS) % LANES == 0 else fdim
    ys = _moe(tile_expert, n_valid.reshape(1), xs, ew1, ew3, ew2, tm=tme, tf=tf)
    pos2 = pos.reshape(n_x, TOP_K)
    out = _combine(x3, mod1, ew, jnp.take(ys, pos2[:, 0], axis=0, mode="clip"),
                   jnp.take(ys, pos2[:, 1], axis=0, mode="clip"), tm=tmx)
    return out[None]
```

```python
import functools
import math

import numpy as np
import jax
import jax.numpy as jnp
from jax import lax
from jax.experimental import pallas as pl
from jax.experimental.pallas import tpu as pltpu
from jax.experimental.pallas import tpu_sc as plsc

F32 = jnp.float32
BF16 = jnp.bfloat16

EPS = 1e-6
ROPE_THETA = 10000.0
GRID_W = 64
LANES = 128
HEAD = 64
RET_CHUNK = 128
RET_STEP_CHUNKS = 2
RET_HEADS = 8
GQA_HEADS = 8
GQA_KV_HEADS = 2
MLA_HEADS = 8
MLA_Q_RANK = 384
MLA_KV_RANK = 256
MLA_NOPE = 64
MLA_ROPE = 32
N_EXPERTS = 8
TOP_K = 2
LOW_ONE = HEAD
HIGH_ONE = 0
LOG2E = math.log2(math.e)
VMEM_LIMIT = 56 * 1024 * 1024

TOKEN_TILES = (640, 512, 256, 128)
FLASH_Q_TILES = (512, 256)
FLASH_KEY_BLOCKS = (1280, 1024, 512, 256)
FLASH_ROW_GROUP = 32
ADA_COLUMN_TILES = 4
MOE_TILE_ROWS = 512
MOE_SMALL_TILE_ROWS = 128
MOE_SMALL_BELOW = 4096
MOE_F_BLOCKS = 2


def _first_divisor(n, candidates, what):
    for c in candidates:
        if n % c == 0:
            return c
    raise ValueError(f"{what} {n} has no supported tile among {candidates}")


def _cparams(sem, vmem=VMEM_LIMIT):
    return pltpu.CompilerParams(dimension_semantics=sem, vmem_limit_bytes=vmem)


def _resident(shape):
    nd = len(shape)
    return pl.BlockSpec(shape, lambda *_: (0,) * nd, pipeline_mode=pl.Buffered(1))


def _dot(a, b):
    return jnp.dot(a, b, preferred_element_type=F32)


def _dot_nt(a, b):
    return lax.dot_general(a, b, (((1,), (1,)), ((), ())), preferred_element_type=F32)


def _seg_mean(v, seg):
    hi = v.astype(BF16)
    lo = (v - hi.astype(F32)).astype(BF16)
    return _dot(hi, seg) + _dot(lo, seg)


def _silu(x):
    return x * jax.nn.sigmoid(x)


def _modulated(x, mod_ref, g_ref, which, tile, tm, n_x, d):
    ms = jnp.mean(x * x, axis=-1, keepdims=True)
    xn = x * lax.rsqrt(ms + EPS)
    g = g_ref[...]
    sh, sc = 3 * which, 3 * which + 1
    a_x = g * (1.0 + mod_ref[0:1, sc * d:(sc + 1) * d])
    a_c = g * (1.0 + mod_ref[1:2, sc * d:(sc + 1) * d])
    b_x = mod_ref[0:1, sh * d:(sh + 1) * d]
    b_c = mod_ref[1:2, sh * d:(sh + 1) * d]
    row = tile * tm + lax.broadcasted_iota(jnp.int32, (tm, 1), 0)
    is_ctx = row >= n_x
    return xn * jnp.where(is_ctx, a_c, a_x) + jnp.where(is_ctx, b_c, b_x)


def _row_gate(mod_ref, idx, tile, tm, n_x, d):
    row = tile * tm + lax.broadcasted_iota(jnp.int32, (tm, 1), 0)
    return jnp.where(row >= n_x, mod_ref[1:2, idx * d:(idx + 1) * d], mod_ref[0:1, idx * d:(idx + 1) * d])


def _lane(shape):
    return lax.broadcasted_iota(jnp.int32, shape, len(shape) - 1)


ADA_ROWS = 2


def _ada_kernel(c_ref, w_ref, b_ref, o_ref):
    s_t = _silu(c_ref[...]).T
    w = w_ref[...]
    rows = [jnp.sum(w * s_t[:, r:r + 1], axis=0, keepdims=True) for r in range(ADA_ROWS)]
    pad = jnp.zeros((c_ref.shape[0] - ADA_ROWS, w.shape[1]), F32)
    o_ref[...] = jnp.concatenate(rows + [pad], axis=0) + b_ref[...]


def _ada(cvec8, w, b):
    d, n = w.shape
    tn = n // ADA_COLUMN_TILES
    return pl.pallas_call(
        _ada_kernel,
        grid=(n // tn,),
        in_specs=[pl.BlockSpec((8, d), lambda j: (0, 0)),
                  pl.BlockSpec((d, tn), lambda j: (0, j)),
                  pl.BlockSpec((1, tn), lambda j: (0, j))],
        out_specs=pl.BlockSpec((8, tn), lambda j: (0, j)),
        out_shape=jax.ShapeDtypeStruct((8, n), F32),
        compiler_params=_cparams(("arbitrary",)),
    )(cvec8, w, b.reshape(1, n))


def _rope128(v, c, s, half):
    lane = _lane(v.shape)
    swapped = jnp.where(lane % (2 * half) < half, pltpu.roll(v, LANES - half, 1), pltpu.roll(v, half, 1))
    return v * c + swapped * s


def _l0_proj_kernel(x_ref, mod_ref, g_ref, w_ref, seg_ref, gq_ref, gk_ref, c_ref, s_ref,
                    rq_ref, rk_ref, rv_ref, rg_ref, q_ref, k_ref, v_ref, *, tm, n_x, d):
    i = pl.program_id(0)
    h = _modulated(x_ref[...], mod_ref, g_ref, 0, i, tm, n_x, d).astype(BF16)
    rw = RET_HEADS * HEAD
    for idx, ref in enumerate((rq_ref, rk_ref, rv_ref, rg_ref)):
        ref[...] = _dot(h, w_ref[:, idx * rw:(idx + 1) * rw]).astype(BF16)
    seg = seg_ref[...]
    cos, sin = c_ref[...], s_ref[...]
    base = 4 * rw
    qw = GQA_HEADS * HEAD
    qa = _dot(h, w_ref[:, base:base + qw])
    for g in range(qw // LANES):
        v = qa[:, g * LANES:(g + 1) * LANES]
        vn = v * lax.rsqrt(_seg_mean(v * v, seg) + EPS) * gq_ref[...]
        q_ref[:, g * LANES:(g + 1) * LANES] = _rope128(vn, cos, sin, HEAD // 2).astype(BF16)
    kv = _dot(h, w_ref[:, base + qw:base + qw + 2 * LANES])
    kk = kv[:, :LANES]
    kk = kk * lax.rsqrt(_seg_mean(kk * kk, seg) + EPS) * gk_ref[...]
    kk = _rope128(kk, cos, sin, HEAD // 2)
    vv = kv[:, LANES:]
    lane = _lane(kk.shape)
    low = lane < HEAD
    for src, ref, one in ((kk, k_ref, 0.0), (vv, v_ref, 1.0)):
        sw = pltpu.roll(src, HEAD, 1)
        lo_fill = jnp.where(lane == LOW_ONE, one, 0.0)
        hi_fill = jnp.where(lane == HIGH_ONE, one, 0.0)
        ref[:, 0 * LANES:1 * LANES] = jnp.where(low, src, lo_fill).astype(BF16)
        ref[:, 1 * LANES:2 * LANES] = jnp.where(low, hi_fill, sw).astype(BF16)
        ref[:, 2 * LANES:3 * LANES] = jnp.where(low, sw, lo_fill).astype(BF16)
        ref[:, 3 * LANES:4 * LANES] = jnp.where(low, hi_fill, src).astype(BF16)


def _l0_proj(xa, mod, g, w, seg, gq, gk, cos, sin, *, tm, n_x):
    t, d = xa.shape
    rw = RET_HEADS * HEAD
    row = lambda i: (i, 0)
    outs = [jax.ShapeDtypeStruct((t, rw), BF16)] * 7
    return pl.pallas_call(
        functools.partial(_l0_proj_kernel, tm=tm, n_x=n_x, d=d),
        grid=(t // tm,),
        in_specs=[pl.BlockSpec((tm, d), row), _resident(mod.shape), _resident(g.shape), _resident(w.shape),
                  _resident(seg.shape), _resident(gq.shape), _resident(gk.shape),
                  pl.BlockSpec((tm, LANES), row), pl.BlockSpec((tm, LANES), row)],
        out_specs=[pl.BlockSpec((tm, rw), row)] * 7,
        out_shape=outs,
        compiler_params=_cparams(("parallel",)),
    )(xa, mod, g, w, seg, gq, gk, cos, sin)


def _retention_kernel(lg_ref, qf_ref, kf_ref, vf_ref, qb_ref, kb_ref, vb_ref, of_ref, ob_ref,
                      state_ref, decay_ref, xi_ref, zeta_ref, gl_ref):
    c = RET_CHUNK
    npairs = RET_HEADS * HEAD // LANES
    step = pl.program_id(0)

    @pl.when(step == 0)
    def _init():
        state_ref[...] = jnp.zeros_like(state_ref)
        ci = lax.broadcasted_iota(jnp.int32, (c, c), 0).astype(F32)
        mi = lax.broadcasted_iota(jnp.int32, (c, c), 1).astype(F32)
        pos = lax.broadcasted_iota(jnp.int32, (c, RET_HEADS * HEAD), 0).astype(F32)
        lane_head = _lane((1, RET_HEADS * HEAD)) // HEAD
        for dr in range(2):
            lgv = jnp.zeros((1, RET_HEADS * HEAD), F32)
            for hd in range(RET_HEADS):
                lg = lg_ref[dr, hd]
                rel = (ci - mi) if dr == 0 else (mi - ci)
                half = slice((hd % 2) * c, (hd % 2 + 1) * c)
                decay_ref[dr, hd // 2, :, half] = jnp.where(rel >= 0, jnp.exp(jnp.maximum(rel, 0.0) * lg), 0.0)
                lgv = jnp.where(lane_head == hd, lg, lgv)
            p = pos if dr == 0 else (c - 1.0 - pos)
            xi_ref[dr] = jnp.exp((p + 1.0) * lgv)
            zeta_ref[dr] = jnp.exp((c - 1.0 - p) * lgv)
            gl_ref[dr] = jnp.exp(float(c) * lgv)

    low = _lane((c, LANES)) < HEAD
    r_i = lax.broadcasted_iota(jnp.int32, (LANES, LANES), 0) // HEAD
    c_i = lax.broadcasted_iota(jnp.int32, (LANES, LANES), 1) // HEAD
    blockdiag = r_i == c_i
    n_sub = qf_ref.shape[0] // c
    for dr, (q_ref, k_ref, v_ref, o_ref) in enumerate(((qf_ref, kf_ref, vf_ref, of_ref),
                                                       (qb_ref, kb_ref, vb_ref, ob_ref))):
        for j in range(npairs):
            sl = slice(j * LANES, (j + 1) * LANES)
            st = state_ref[dr, j]
            for sub in (range(n_sub) if dr == 0 else reversed(range(n_sub))):
                rows = slice(sub * c, (sub + 1) * c)
                q, k, v = q_ref[rows, sl], k_ref[rows, sl], v_ref[rows, sl]
                zero = jnp.zeros_like(k)
                k2 = jnp.concatenate([jnp.where(low, k, zero), jnp.where(low, zero, k)], axis=0)
                v2 = jnp.concatenate([jnp.where(low, v, zero), jnp.where(low, zero, v)], axis=0)
                s = _dot_nt(q, k2) * decay_ref[dr, j]
                o = _dot(s.astype(BF16), v2)
                qx = (q.astype(F32) * xi_ref[dr, :, sl]).astype(BF16)
                o = o + _dot(qx, st.astype(BF16))
                o_ref[rows, sl] = o.astype(BF16)
                kz = (k.astype(F32) * zeta_ref[dr, :, sl]).T.astype(BF16)
                u = _dot(kz, v)
                st = st * gl_ref[dr, :, sl] + jnp.where(blockdiag, u, 0.0)
            state_ref[dr, j] = st


def _retention(lg, rq, rk, rv, *, n_x):
    t, w = rq.shape
    c = RET_CHUNK
    rows = RET_STEP_CHUNKS * c
    assert n_x % rows == 0 and t % rows == 0
    nc, ncx = t // rows, n_x // rows
    fwd = lambda i: ((i + ncx) % nc, 0)
    bwd = lambda i: (nc - 1 - i, 0)
    blk = lambda m: pl.BlockSpec((rows, w), m)
    npairs = w // LANES
    return pl.pallas_call(
        _retention_kernel,
        grid=(nc,),
        in_specs=[pl.BlockSpec(memory_space=pltpu.SMEM)] + [blk(fwd)] * 3 + [blk(bwd)] * 3,
        out_specs=[blk(fwd), blk(bwd)],
        out_shape=[jax.ShapeDtypeStruct((t, w), BF16)] * 2,
        scratch_shapes=[pltpu.VMEM((2, npairs, LANES, LANES), F32),
                        pltpu.VMEM((2, npairs, c, 2 * c), F32),
                        pltpu.VMEM((2, c, w), F32), pltpu.VMEM((2, c, w), F32),
                        pltpu.VMEM((2, 1, w), F32)],
        compiler_params=_cparams(("arbitrary",)),
    )(lg, rq, rk, rv, rq, rk, rv)


def _flash_kernel(q0_ref, q1_ref, k0_ref, k1_ref, v0_ref, v1_ref, *rest, bk, nkv, rs, n_side):
    side_in, o_ref, side_out = rest[:n_side], rest[n_side], rest[n_side + 1:2 * n_side + 1]
    s_ref, p_ref, a_ref, m_ref, acc_ref = rest[2 * n_side + 1:]
    for src_ref, dst_ref in zip(side_in, side_out):
        dst_ref[...] = src_ref[...].astype(dst_ref.dtype)
    bq = q0_ref.shape[0]
    q_refs, k_refs, v_refs = (q0_ref, q1_ref), (k0_ref, k1_ref), (v0_ref, v1_ref)
    m_ref[...] = jnp.full(m_ref.shape, -jnp.inf, F32)
    acc_ref[...] = jnp.zeros(acc_ref.shape, F32)

    def keys(t):
        return pl.ds(t * bk if isinstance(t, int) else pl.multiple_of(t * bk, bk), bk)

    def scores(t, slot):
        for h in range(2):
            s_ref[slot, h] = _dot_nt(q_refs[h][...], k_refs[h][keys(t), :])

    def softmax(slot):
        col = lambda c: slice(c * LANES, (c + 1) * LANES)
        for r in range(bq // rs):
            rows = slice(r * rs, (r + 1) * rs)
            for h in range(2):
                mx = s_ref[slot, h, rows, col(0)]
                for c in range(1, bk // LANES):
                    mx = jnp.maximum(mx, s_ref[slot, h, rows, col(c)])
                m_old = m_ref[h, rows, :]
                m_new = jnp.maximum(m_old, jnp.max(mx, axis=-1, keepdims=True))
                a_ref[slot, h, rows, :] = jnp.exp2(m_old - m_new)
                m_ref[h, rows, :] = m_new
                for c in range(bk // LANES):
                    p_ref[slot, h, rows, col(c)] = jnp.exp2(s_ref[slot, h, rows, col(c)] - m_new).astype(BF16)

    def values(t, slot):
        for h in range(2):
            acc_ref[h] = acc_ref[h] * a_ref[slot, h] + _dot(p_ref[slot, h], v_refs[h][keys(t), :])

    scores(0, 0)

    def body(i, carry):
        t = 2 * i
        scores(t + 1, 1)
        softmax(0)
        values(t, 0)
        scores(t + 2, 0)
        softmax(1)
        values(t + 1, 1)
        return carry

    n_loop = (nkv - 1) // 2
    lax.fori_loop(0, n_loop, body, 0)
    last = 2 * n_loop
    if last + 1 < nkv:
        scores(last + 1, 1)
    softmax(0)
    values(last, 0)
    if last + 1 < nkv:
        softmax(1)
        values(last + 1, 1)
    low = _lane((bq, LANES)) < HEAD
    acc0, acc1 = acc_ref[0], acc_ref[1]
    out = jnp.where(low, acc0 / acc0[:, LOW_ONE:LOW_ONE + 1], acc1 / acc1[:, HIGH_ONE:HIGH_ONE + 1])
    o_ref[...] = out.astype(o_ref.dtype)


def _flash(q, kmat, vmat, *, q_maps, k_maps, v_maps, n_q, q_row0, kv_row0, n_kv, n_pairs, bq, bk, rs, side=()):
    assert q_row0 % bq == 0 and n_q % bq == 0 and n_kv % bk == 0 and kv_row0 % n_kv == 0 and bq % rs == 0
    qb0, kb0 = q_row0 // bq, kv_row0 // n_kv
    n_i = n_q // bq
    steps = n_pairs * n_i
    assert all(a.shape[0] % (16 * steps) == 0 for a in side)
    qspec = lambda m: pl.BlockSpec((bq, LANES), lambda j, i: (i + qb0, m(j)))
    kspec = lambda m: pl.BlockSpec((n_kv, LANES), lambda j, i: (kb0, m(j)), pipeline_mode=pl.Buffered(1))
    side_specs = [pl.BlockSpec((a.shape[0] // steps, a.shape[1]), lambda j, i: (j * n_i + i, 0)) for a in side]
    outs = pl.pallas_call(
        functools.partial(_flash_kernel, bk=bk, nkv=n_kv // bk, rs=rs, n_side=len(side)),
        grid=(n_pairs, n_i),
        in_specs=[qspec(q_maps[0]), qspec(q_maps[1]), kspec(k_maps[0]), kspec(k_maps[1]),
                  kspec(v_maps[0]), kspec(v_maps[1])] + side_specs,
        out_specs=[pl.BlockSpec((bq, LANES), lambda j, i: (i, j))] + side_specs,
        out_shape=[jax.ShapeDtypeStruct((n_q, n_pairs * LANES), BF16)]
                  + [jax.ShapeDtypeStruct(a.shape, BF16) for a in side],
        scratch_shapes=[pltpu.VMEM((2, 2, bq, bk), F32), pltpu.VMEM((2, 2, bq, bk), BF16),
                        pltpu.VMEM((2, 2, bq, LANES), F32), pltpu.VMEM((2, bq, LANES), F32),
                        pltpu.VMEM((2, bq, LANES), F32)],
        compiler_params=_cparams(("parallel", "parallel")),
    )(q, q, kmat, kmat, vmat[0], vmat[1], *side)
    return outs if side else outs[0]


def _kv_block(n_kv):
    return _first_divisor(n_kv, FLASH_KEY_BLOCKS, "key count")


def _l0_out_kernel(x_ref, mod_ref, of_ref, ob_ref, rg_ref, ao_ref, seg_ref, wo_ref, o_ref, *, tm, n_x, d):
    i = pl.program_id(0)
    seg = seg_ref[...]
    rw = RET_HEADS * HEAD
    acc = _dot(ao_ref[...], wo_ref[rw:, :])
    for g in range(rw // LANES):
        sl = slice(g * LANES, (g + 1) * LANES)
        o = of_ref[:, sl].astype(F32) + ob_ref[:, sl].astype(F32)
        dv = o - _seg_mean(o, seg)
        nrm = dv * lax.rsqrt(_seg_mean(dv * dv, seg) + EPS)
        ra = (nrm * _silu(rg_ref[:, sl].astype(F32))).astype(BF16)
        acc = acc + _dot(ra, wo_ref[g * LANES:(g + 1) * LANES, :])
    o_ref[...] = x_ref[...] + _row_gate(mod_ref, 2, i, tm, n_x, d) * acc


def _l0_out(xa, mod, o_f, o_b, rg, ao, seg, wo, *, tm, n_x):
    t, d = xa.shape
    rw = o_f.shape[1]
    row = lambda i: (i, 0)
    return pl.pallas_call(
        functools.partial(_l0_out_kernel, tm=tm, n_x=n_x, d=d),
        grid=(t // tm,),
        in_specs=[pl.BlockSpec((tm, d), row), _resident(mod.shape)] + [pl.BlockSpec((tm, rw), row)] * 4
                 + [_resident(seg.shape), _resident(wo.shape)],
        out_specs=pl.BlockSpec((tm, d), row),
        out_shape=jax.ShapeDtypeStruct((t, d), F32),
        compiler_params=_cparams(("parallel",)),
    )(xa, mod, o_f, o_b, rg, ao, seg, wo)


def _ffn_kernel(x_ref, mod_ref, g_ref, w1_ref, w3_ref, w2_ref, o_ref, *, tm, n_x, d):
    i = pl.program_id(0)
    x = x_ref[...]
    h = _modulated(x, mod_ref, g_ref, 1, i, tm, n_x, d).astype(BF16)
    a = _dot(h, w1_ref[...])
    u = (_silu(a) * _dot(h, w3_ref[...])).astype(BF16)
    o_ref[...] = x + _row_gate(mod_ref, 5, i, tm, n_x, d) * _dot(u, w2_ref[...])


def _ffn(xa, mod, g, w1, w3, w2, *, tm, n_x):
    t, d = xa.shape
    row = lambda i: (i, 0)
    return pl.pallas_call(
        functools.partial(_ffn_kernel, tm=tm, n_x=n_x, d=d),
        grid=(t // tm,),
        in_specs=[pl.BlockSpec((tm, d), row), _resident(mod.shape), _resident(g.shape),
                  _resident(w1.shape), _resident(w3.shape), _resident(w2.shape)],
        out_specs=pl.BlockSpec((tm, d), row),
        out_shape=jax.ShapeDtypeStruct((t, d), F32),
        compiler_params=_cparams(("parallel",)),
    )(xa, mod, g, w1, w3, w2)


def _l1_proj_kernel(x_ref, mod_ref, g_ref, wq_ref, wkv_ref, wkr_ref, gql_ref, gkvl_ref, wuq_ref, wuk_ref,
                    wuv_ref, seg_ref, gq_ref, gk_ref, gkr_ref, c_ref, s_ref,
                    q_ref, k_ref, vlo_ref, vhi_ref, *, tm, n_x, d):
    i = pl.program_id(0)
    h = _modulated(x_ref[...], mod_ref, g_ref, 0, i, tm, n_x, d).astype(BF16)
    seg = seg_ref[...]
    cos, sin = c_ref[...], s_ref[...]

    def lora_norm(v, g):
        return (v * lax.rsqrt(jnp.mean(v * v, axis=-1, keepdims=True) + EPS) * g).astype(BF16)

    cq = lora_norm(_dot(h, wq_ref[...]), gql_ref[...])
    ckv = lora_norm(_dot(h, wkv_ref[...]), gkvl_ref[...])
    kr = _dot(h, wkr_ref[...])
    kr = kr * lax.rsqrt(_seg_mean(kr * kr, seg) + EPS) * gkr_ref[...]
    kr = _rope128(kr, cos, sin, MLA_ROPE // 2)
    qa = _dot(cq, wuq_ref[...])
    ka = _dot(ckv, wuk_ref[...])
    for hd in range(MLA_HEADS):
        sl = slice(hd * LANES, (hd + 1) * LANES)
        v = qa[:, sl]
        vn = v * lax.rsqrt(_seg_mean(v * v, seg) + EPS) * gq_ref[...]
        q_ref[:, sl] = _rope128(vn, cos, sin, MLA_ROPE // 2).astype(BF16)
        v = ka[:, sl]
        k_ref[:, sl] = (v * lax.rsqrt(_seg_mean(v * v, seg) + EPS) * gk_ref[...] + kr).astype(BF16)
    va = _dot(ckv, wuv_ref[...])
    lane = _lane(va.shape) % LANES
    low = lane < HEAD
    vlo_ref[...] = jnp.where(low, va, jnp.where(lane == LOW_ONE, 1.0, 0.0)).astype(BF16)
    vhi_ref[...] = jnp.where(low, jnp.where(lane == HIGH_ONE, 1.0, 0.0), va).astype(BF16)


def _l1_proj(xa, mod, g, wq, wkv, wkr, gql, gkvl, wuq, wuk, wuv, seg, gq, gk, gkr, cos, sin, *, tm, n_x):
    t, d = xa.shape
    row = lambda i: (i, 0)
    hw = MLA_HEADS * LANES
    vw = MLA_HEADS * HEAD
    consts = (mod, g, wq, wkv, wkr, gql, gkvl, wuq, wuk, wuv, seg, gq, gk, gkr)
    return pl.pallas_call(
        functools.partial(_l1_proj_kernel, tm=tm, n_x=n_x, d=d),
        grid=(t // tm,),
        in_specs=[pl.BlockSpec((tm, d), row)] + [_resident(a.shape) for a in consts]
                 + [pl.BlockSpec((tm, LANES), row)] * 2,
        out_specs=[pl.BlockSpec((tm, hw), row), pl.BlockSpec((tm, hw), row),
                   pl.BlockSpec((tm, vw), row), pl.BlockSpec((tm, vw), row)],
        out_shape=[jax.ShapeDtypeStruct((t, hw), BF16), jax.ShapeDtypeStruct((t, hw), BF16),
                   jax.ShapeDtypeStruct((t, vw), BF16), jax.ShapeDtypeStruct((t, vw), BF16)],
        compiler_params=_cparams(("parallel",)),
    )(xa, *consts, cos, sin)


def _l1_out_kernel(x_ref, mod_ref, g_ref, o_ref, wo_ref, rhi_ref, rlo_ref, x3_ref, h_ref, ei_ref, ew_ref, *, d):
    x3 = x_ref[...] + mod_ref[0:1, 2 * d:3 * d] * _dot(o_ref[...], wo_ref[...])
    x3_ref[...] = x3
    ms = jnp.mean(x3 * x3, axis=-1, keepdims=True)
    h = x3 * lax.rsqrt(ms + EPS) * (g_ref[...] * (1.0 + mod_ref[0:1, 4 * d:5 * d])) + mod_ref[0:1, 3 * d:4 * d]
    hi = h.astype(BF16)
    bits = lax.bitcast_convert_type(hi.astype(F32), jnp.uint32)
    words = (bits[:, :d // 2] >> 16) | (bits[:, d // 2:] & jnp.uint32(0xFFFF0000))
    h_ref[...] = words
    lo = (h - hi.astype(F32)).astype(BF16)
    logits = _dot(hi, rhi_ref[...]) + (_dot(hi, rlo_ref[...]) + _dot(lo, rhi_ref[...]))
    lane_i = _lane(logits.shape)
    lane = lane_i.astype(F32)
    logits = jnp.where(lane_i < N_EXPERTS, logits, -jnp.inf)
    v1 = jnp.max(logits, axis=-1, keepdims=True)
    i1 = jnp.min(jnp.where(logits == v1, lane, float(LANES)), axis=-1, keepdims=True)
    rest = jnp.where(lane == i1, -jnp.inf, logits)
    v2 = jnp.max(rest, axis=-1, keepdims=True)
    i2 = jnp.min(jnp.where(rest == v2, lane, float(LANES)), axis=-1, keepdims=True)
    e2 = jnp.exp(v2 - v1)
    den = 1.0 + e2
    ei_ref[...] = jnp.where(lane_i == 0, i1, jnp.where(lane_i == 1, i2, 0.0)).astype(jnp.int32)
    ew_ref[...] = jnp.where(lane_i == 0, 1.0 / den, jnp.where(lane_i == 1, e2 / den, 0.0))


def _l1_out(xa, mod, g, o, wo, rhi, rlo, *, tm):
    n, d = o.shape[0], xa.shape[1]
    row = lambda i: (i, 0)
    return pl.pallas_call(
        functools.partial(_l1_out_kernel, d=d),
        grid=(n // tm,),
        in_specs=[pl.BlockSpec((tm, d), row), _resident(mod.shape), _resident(g.shape),
                  pl.BlockSpec((tm, o.shape[1]), row), _resident(wo.shape), _resident(rhi.shape),
                  _resident(rlo.shape)],
        out_specs=[pl.BlockSpec((tm, d), row), pl.BlockSpec((tm, d // 2), row),
                   pl.BlockSpec((tm, LANES), row), pl.BlockSpec((tm, LANES), row)],
        out_shape=[jax.ShapeDtypeStruct((n, d), F32), jax.ShapeDtypeStruct((n, d // 2), jnp.uint32),
                   jax.ShapeDtypeStruct((n, LANES), jnp.int32), jax.ShapeDtypeStruct((n, LANES), F32)],
        compiler_params=_cparams(("parallel",)),
    )(xa, mod, g, o, wo, rhi, rlo)


def _moe_kernel(te_ref, nv_ref, x_ref, w1_ref, w3_ref, w2_ref, y_ref, acc_ref, *, nf):
    i, f = pl.program_id(0), pl.program_id(1)

    @pl.when(f == 0)
    def _zero():
        acc_ref[...] = jnp.zeros_like(acc_ref)

    @pl.when(i < nv_ref[0])
    def _compute():
        words = x_ref[...]
        lo = lax.bitcast_convert_type(words << 16, F32)
        hi = lax.bitcast_convert_type(words & jnp.uint32(0xFFFF0000), F32)
        x = jnp.concatenate([lo, hi], axis=1).astype(BF16)
        a = _dot(x, w1_ref[0])
        u = (_silu(a) * _dot(x, w3_ref[0])).astype(BF16)
        acc_ref[...] += _dot(u, w2_ref[0])

    @pl.when(f == nf - 1)
    def _store():
        y_ref[...] = acc_ref[...].astype(y_ref.dtype)


def _moe(tile_expert, n_valid, xs, w1, w3, w2, *, tm, tf):
    p, d = xs.shape[0], 2 * xs.shape[1]
    fdim = w1.shape[2]
    nf = fdim // tf
    fi = lambda i, f, te, nv: jnp.where(i < nv[0], f, nf - 1)
    grid_spec = pltpu.PrefetchScalarGridSpec(
        num_scalar_prefetch=2,
        grid=(p // tm, nf),
        in_specs=[pl.BlockSpec((tm, d // 2), lambda i, f, te, nv: (i, 0)),
                  pl.BlockSpec((1, d, tf), lambda i, f, te, nv: (te[i], 0, fi(i, f, te, nv))),
                  pl.BlockSpec((1, d, tf), lambda i, f, te, nv: (te[i], 0, fi(i, f, te, nv))),
                  pl.BlockSpec((1, tf, d), lambda i, f, te, nv: (te[i], fi(i, f, te, nv), 0))],
        out_specs=pl.BlockSpec((tm, d), lambda i, f, te, nv: (i, 0)),
        scratch_shapes=[pltpu.VMEM((tm, d), F32)],
    )
    return pl.pallas_call(
        functools.partial(_moe_kernel, nf=nf),
        grid_spec=grid_spec,
        out_shape=jax.ShapeDtypeStruct((p, d), BF16),
        compiler_params=_cparams(("arbitrary", "arbitrary")),
    )(tile_expert, n_valid, xs, w1, w3, w2)


SC_GATHER_WINDOW = 128
SC_LANES = 16


def _gather_rows(x, idx):
    n, d = idx.shape[0], x.shape[1]
    w = SC_GATHER_WINDOW
    assert n % w == 0
    mesh = plsc.VectorSubcoreMesh(core_axis_name="core", subcore_axis_name="subcore")

    @pl.kernel(out_type=jax.ShapeDtypeStruct((n, d), x.dtype), mesh=mesh,
               scratch_types=[pltpu.SemaphoreType.DMA])
    def gather_kernel(x_hbm, i_hbm, o_hbm, sem):
        def body(i_vmem, o_vmem):
            copies = []
            for k in range(w // SC_LANES):
                grp = pl.ds(k * SC_LANES, SC_LANES)
                copies.append(pltpu.async_copy(x_hbm.at[i_vmem[0, grp]], o_vmem.at[grp], sem))
            for cp in copies:
                cp.wait()

        pltpu.emit_pipeline(
            body,
            grid=(n // w,),
            in_specs=[pl.BlockSpec((1, w), lambda i: (0, i))],
            out_specs=[pl.BlockSpec((w, d), lambda i: (i, 0), pipeline_mode=pl.Buffered(1))],
            core_axis_name=("core", "subcore"),
            dimension_semantics=(pltpu.PARALLEL,),
        )(i_hbm, o_hbm)

    return gather_kernel(x, idx.reshape(1, n))


def _combine_kernel(x_ref, mod_ref, ew_ref, ya_ref, yb_ref, o_ref, *, d):
    ew = ew_ref[...]
    y = ew[:, 0:1] * ya_ref[...].astype(F32) + ew[:, 1:2] * yb_ref[...].astype(F32)
    o_ref[...] = x_ref[...] + mod_ref[0:1, 5 * d:6 * d] * y


def _combine(x3, mod, ew, ya, yb, *, tm):
    n, d = x3.shape
    row = lambda i: (i, 0)
    return pl.pallas_call(
        functools.partial(_combine_kernel, d=d),
        grid=(n // tm,),
        in_specs=[pl.BlockSpec((tm, d), row), _resident(mod.shape), pl.BlockSpec((tm, LANES), row),
                  pl.BlockSpec((tm, d), row), pl.BlockSpec((tm, d), row)],
        out_specs=pl.BlockSpec((tm, d), row),
        out_shape=jax.ShapeDtypeStruct((n, d), F32),
        compiler_params=_cparams(("parallel",)),
    )(x3, mod, ew, ya, yb)


def _rope_tables(n_x, n_ctx, rot_dim, seg_start, seg_repeat):
    f32 = np.float32
    rows = n_x // GRID_W
    row = np.repeat(np.arange(rows, dtype=f32), GRID_W)
    col = np.tile(np.arange(GRID_W, dtype=f32), rows)
    axis_dim = rot_dim // 2
    inv_freq = f32(ROPE_THETA) ** (-np.arange(0, axis_dim, 2, dtype=f32) / f32(axis_dim))
    ang = np.concatenate([row[:, None] * inv_freq, col[:, None] * inv_freq], axis=-1)
    cos, sin = np.cos(ang).astype(f32), np.sin(ang).astype(f32)
    c = np.ones((n_x + n_ctx, LANES), f32)
    s = np.zeros((n_x + n_ctx, LANES), f32)
    for r in range(seg_repeat):
        lo = seg_start + r * rot_dim
        c[:n_x, lo:lo + rot_dim] = np.concatenate([cos, cos], axis=-1)
        s[:n_x, lo:lo + rot_dim] = np.concatenate([-sin, sin], axis=-1)
    return jnp.asarray(c), jnp.asarray(s)


def _segment_matrix(bounds):
    m = np.zeros((LANES, LANES), np.float32)
    for lo, hi in bounds:
        m[lo:hi, lo:hi] = 1.0 / (hi - lo)
    return jnp.asarray(m, BF16)


def _token_tile(t):
    return _first_divisor(t, TOKEN_TILES, "token count")


def kernel(x, c, ctx, c_ctx, l0_ada_w, l0_ada_b, l0_norm1_g, l0_norm2_g, l0_w_in, l0_ret_log_decay, l0_q_norm_g, l0_k_norm_g, l0_w_out, l0_ffn_w1, l0_ffn_w3, l0_ffn_w2, l1_ada_w, l1_ada_b, l1_norm1_g, l1_norm2_g, l1_w_in, l1_q_lora_g, l1_kv_lora_g, l1_w_uq, l1_w_ukv, l1_q_nope_g, l1_q_rope_g, l1_k_nope_g, l1_k_rope_g, l1_w_out, l1_router, l1_exp_w1, l1_exp_w3, l1_exp_w2):
    b, n_x, d = x.shape
    n_ctx = ctx.shape[1]
    assert b == 1 and n_x % 256 == 0 and n_ctx % 256 == 0 and n_x % GRID_W == 0
    t = n_x + n_ctx
    tm = _token_tile(t)
    tmx = _token_tile(n_x)
    xa = jnp.concatenate([x[0], ctx[0]], axis=0)
    row1 = lambda v: v.reshape(1, -1).astype(F32)

    cvec = jnp.zeros((8, d), F32).at[0].set(c[0]).at[1].set(c_ctx)
    mod0 = _ada(cvec, l0_ada_w, l0_ada_b)
    mod1 = _ada(cvec, l1_ada_w, l1_ada_b)

    rw = RET_HEADS * HEAD
    n_qk = GQA_HEADS + GQA_KV_HEADS
    qk_cols = l0_w_in[:, 4 * rw:4 * rw + n_qk * HEAD].reshape(d, n_qk, HEAD // 2, 2)
    qk_cols = jnp.swapaxes(qk_cols, 2, 3).reshape(d, n_qk * HEAD)
    w_in0 = jnp.concatenate([l0_w_in[:, :rw], l0_w_in[:, rw:2 * rw] * (HEAD ** -0.5), l0_w_in[:, 2 * rw:4 * rw],
                             qk_cols, l0_w_in[:, 4 * rw + n_qk * HEAD:]], axis=1).astype(BF16)
    seg64 = _segment_matrix([(0, HEAD), (HEAD, 2 * HEAD)])
    halves = lambda g: jnp.swapaxes(g.reshape(HEAD // 2, 2), 0, 1).reshape(HEAD)
    gq0 = row1(jnp.tile(halves(l0_q_norm_g), 2) * (HEAD ** -0.5 * LOG2E))
    gk0 = row1(jnp.tile(halves(l0_k_norm_g), 2))
    cos0, sin0 = _rope_tables(n_x, n_ctx, HEAD, 0, 2)

    rq, rk, rv, rg, gq, gkx, gvx = _l0_proj(xa, mod0, row1(l0_norm1_g), w_in0, seg64, gq0, gk0, cos0, sin0,
                                             tm=tm, n_x=n_x)
    o_f, o_b = _retention(l0_ret_log_decay.astype(F32), rq, rk, rv, n_x=n_x)

    gqa_maps = dict(q_maps=(lambda j: j, lambda j: j),
                    k_maps=(lambda j: 2 * (j // 2), lambda j: 2 * (j // 2) + 1),
                    v_maps=(lambda j: 2 * (j // 2), lambda j: 2 * (j // 2) + 1), n_pairs=GQA_HEADS // 2)
    bq = _first_divisor(n_x, FLASH_Q_TILES, "query count")
    rs = FLASH_ROW_GROUP
    ao_x = _flash(gq, gkx, (gvx, gvx), n_q=n_x, q_row0=0, kv_row0=0, n_kv=t, bq=bq, bk=_kv_block(t), rs=rs,
                  **gqa_maps)
    ao_c = _flash(gq, gkx, (gvx, gvx), n_q=n_ctx, q_row0=n_x, kv_row0=n_x, n_kv=n_ctx, bq=n_ctx,
                  bk=_kv_block(n_ctx), rs=rs, **gqa_maps)
    ao = jnp.concatenate([ao_x, ao_c], axis=0)

    xa = _l0_out(xa, mod0, o_f, o_b, rg, ao, seg64, l0_w_out.astype(BF16), tm=tm, n_x=n_x)
    xa = _ffn(xa, mod0, row1(l0_norm2_g), l0_ffn_w1.astype(BF16), l0_ffn_w3.astype(BF16),
              l0_ffn_w2.astype(BF16), tm=tm, n_x=n_x)

    qk_w = MLA_NOPE + MLA_ROPE
    pad_w = LANES - qk_w

    def even_odd(a):
        return jnp.swapaxes(a.reshape(a.shape[:-1] + (a.shape[-1] // 2, 2)), -1, -2).reshape(a.shape)

    uq = l1_w_uq.reshape(MLA_Q_RANK, MLA_HEADS, qk_w)
    wuq = jnp.concatenate([uq[..., :MLA_NOPE], even_odd(uq[..., MLA_NOPE:]),
                           jnp.zeros((MLA_Q_RANK, MLA_HEADS, pad_w), F32)], axis=-1).reshape(MLA_Q_RANK, -1)
    ukv = l1_w_ukv.reshape(MLA_KV_RANK, MLA_HEADS, 2 * HEAD)
    wuk = jnp.concatenate([ukv[..., :MLA_NOPE], jnp.zeros((MLA_KV_RANK, MLA_HEADS, LANES - MLA_NOPE), F32)],
                          axis=-1).reshape(MLA_KV_RANK, -1)
    wuv = ukv[..., MLA_NOPE:].reshape(MLA_KV_RANK, -1)
    wkr = jnp.concatenate([jnp.zeros((d, MLA_NOPE), F32), even_odd(l1_w_in[:, MLA_Q_RANK + MLA_KV_RANK:]),
                           jnp.zeros((d, pad_w), F32)], axis=-1)
    pad = jnp.zeros((pad_w,), F32)
    zn = jnp.zeros((MLA_NOPE,), F32)
    gq1 = row1(jnp.concatenate([l1_q_nope_g, even_odd(l1_q_rope_g), pad]) * (qk_w ** -0.5 * LOG2E))
    gk1 = row1(jnp.concatenate([l1_k_nope_g, jnp.zeros((LANES - MLA_NOPE,), F32)]))
    gkr1 = row1(jnp.concatenate([zn, even_odd(l1_k_rope_g), pad]))
    seg_mla = _segment_matrix([(0, MLA_NOPE), (MLA_NOPE, qk_w)])
    cos1, sin1 = _rope_tables(n_x, n_ctx, MLA_ROPE, MLA_NOPE, 1)

    mq, mk, mvlo, mvhi = _l1_proj(
        xa, mod1, row1(l1_norm1_g), l1_w_in[:, :MLA_Q_RANK].astype(BF16),
        l1_w_in[:, MLA_Q_RANK:MLA_Q_RANK + MLA_KV_RANK].astype(BF16), wkr.astype(BF16),
        row1(l1_q_lora_g), row1(l1_kv_lora_g), wuq.astype(BF16), wuk.astype(BF16), wuv.astype(BF16),
        seg_mla, gq1, gk1, gkr1, cos1, sin1, tm=tm, n_x=n_x)
    n_e, _, fdim = l1_exp_w1.shape
    mo, ew1, ew3, ew2 = _flash(
        mq, mk, (mvlo, mvhi), q_maps=(lambda j: 2 * j, lambda j: 2 * j + 1),
        k_maps=(lambda j: 2 * j, lambda j: 2 * j + 1), v_maps=(lambda j: j, lambda j: j),
        n_pairs=MLA_HEADS // 2, n_q=n_x, q_row0=0, kv_row0=0, n_kv=t, bq=bq, bk=_kv_block(t), rs=rs,
        side=(l1_exp_w1.reshape(n_e * d, fdim), l1_exp_w3.reshape(n_e * d, fdim),
              l1_exp_w2.reshape(n_e * fdim, d)))
    ew1, ew3, ew2 = ew1.reshape(n_e, d, fdim), ew3.reshape(n_e, d, fdim), ew2.reshape(n_e, fdim, d)

    router = jnp.zeros((d, LANES), F32).at[:, :N_EXPERTS].set(l1_router)
    r_hi = router.astype(BF16)
    r_lo = (router - r_hi.astype(F32)).astype(BF16)
    x3, hmoe, ei, ew = _l1_out(xa, mod1, row1(l1_norm2_g), mo, l1_w_out.astype(BF16), r_hi, r_lo, tm=tmx)

    tme = MOE_TILE_ROWS if n_x >= MOE_SMALL_BELOW else MOE_SMALL_TILE_ROWS
    e_flat = jnp.concatenate([ei[:, k] for k in range(TOP_K)])
    onehot = (e_flat[:, None] == jnp.arange(N_EXPERTS)[None, :]).astype(jnp.int32)
    csum = jnp.cumsum(onehot, axis=0)
    rank = jnp.sum((csum - onehot) * onehot, axis=1)
    counts = csum[-1]
    padded = ((counts + tme - 1) // tme) * tme
    ends = jnp.cumsum(padded)
    pos = (ends - padded)[e_flat] + rank
    n_tiles = (TOP_K * n_x) // tme + N_EXPERTS
    p_rows = n_tiles * tme
    n_valid = (ends[-1] // tme).astype(jnp.int32)
    tile_ids = jnp.minimum(jnp.arange(n_tiles, dtype=jnp.int32), n_valid - 1)
    tile_expert = jnp.sum((ends[None, :] <= (tile_ids * tme)[:, None]).astype(jnp.int32), axis=1)
    tile_expert = jnp.minimum(tile_expert, N_EXPERTS - 1)
    by_expert = jnp.argsort(e_flat, stable=True).astype(jnp.int32) % n_x
    slot = jnp.arange(p_rows, dtype=jnp.int32)
    slot_e = jnp.repeat(tile_expert, tme)
    slot_rank = slot - (ends - padded)[slot_e]
    listed = jnp.take(by_expert, (jnp.cumsum(counts) - counts)[slot_e] + slot_rank, mode="clip")
    src = jnp.where((slot_rank < counts[slot_e]) & (slot < ends[-1]), listed, slot % n_x)
    xs = _gather_rows(hmoe, src)
    tf = fdim // MOE_F_BLOCKS if (fdim // MOE_F_BLOCKS) % LANES == 0 else fdim
    ys = _moe(tile_expert, n_valid.reshape(1), xs, ew1, ew3, ew2, tm=tme, tf=tf)
    out = _combine(x3, mod1, ew, jnp.take(ys, pos[:n_x], axis=0, mode="clip"),
                   jnp.take(ys, pos[n_x:], axis=0, mode="clip"), tm=tmx)
    return out[None]
```
